```python
import math
import jax, jax.numpy as jnp
from jax import lax
import numpy as np

D_MODEL = 1024
BATCH = 4
SEQ = 4096
DEPTH = 2

CHUNK = 64
Q_BLOCK = 128
EPS = 1e-6

N_A = DEPTH // 2
N_B = DEPTH - N_A
N_DENSE = (DEPTH + 1) // 2
N_MOE = DEPTH // 2

RET_QK_DIM = 256
RET_HEADS = D_MODEL // RET_QK_DIM
RET_V_DIM = 2 * RET_QK_DIM
RET_THETA = 10000.0

DIFF_HEAD_DIM = 64
DIFF_HEADS = D_MODEL // (2 * DIFF_HEAD_DIM)
DIFF_V_DIM = 2 * DIFF_HEAD_DIM
ROPE_THETA = 500000.0
ROPE_DIM = DIFF_HEAD_DIM // 4

FFN_DIM = 256 * math.ceil(8 * D_MODEL / 3 / 256)
N_EXPERTS = 8
TOP_K = 2
EXPERT_DIM = 7 * D_MODEL // 2

kernel_name = "yoco_retention_diffattn_moe"


def rms_norm(x, g):
    xf = x.astype(jnp.float32)
    y = xf * lax.rsqrt(jnp.mean(xf * xf, axis=-1, keepdims=True) + EPS)
    return (y * g.astype(jnp.float32)).astype(x.dtype)


def rope_tables(seq, rot_dim, theta):
    inv = 1.0 / (theta ** (jnp.arange(0, rot_dim, 2, dtype=jnp.float32) / rot_dim))
    ang = jnp.arange(seq, dtype=jnp.float32)[:, None] * inv[None, :]
    return jnp.cos(ang), jnp.sin(ang)


def apply_rope(x, cos, sin, rot_dim):
    half = rot_dim // 2
    shape = (1, x.shape[1]) + (1,) * (x.ndim - 3) + (half,)
    c = cos.reshape(shape).astype(x.dtype)
    s = sin.reshape(shape).astype(x.dtype)
    x1 = x[..., :half]
    x2 = x[..., half:rot_dim]
    return jnp.concatenate([x1 * c - x2 * s, x2 * c + x1 * s, x[..., rot_dim:]], axis=-1)


def to_chunks(t):
    b, s, h, d = t.shape
    return t.reshape(b, s // CHUNK, CHUNK, h, d).transpose(1, 0, 3, 2, 4)


def retention_mixer(h, w_in, w_o):
    b, s, _ = h.shape
    dt = h.dtype
    proj = (h @ w_in).astype(jnp.float32)
    d_qk = RET_HEADS * RET_QK_DIM
    d_v = RET_HEADS * RET_V_DIM
    q = proj[..., :d_qk].reshape(b, s, RET_HEADS, RET_QK_DIM)
    k = proj[..., d_qk:2 * d_qk].reshape(b, s, RET_HEADS, RET_QK_DIM)
    v = proj[..., 2 * d_qk:2 * d_qk + d_v].reshape(b, s, RET_HEADS, RET_V_DIM)
    g = proj[..., 2 * d_qk + d_v:]
    cos, sin = rope_tables(s, RET_QK_DIM, RET_THETA)
    q = apply_rope(q, cos, sin, RET_QK_DIM)
    k = apply_rope(k, cos, sin, RET_QK_DIM) * (RET_QK_DIM ** -0.5)

    log_gamma = jnp.log(1.0 - 2.0 ** (-5.0 - jnp.arange(RET_HEADS, dtype=jnp.float32)))
    idx = jnp.arange(CHUNK, dtype=jnp.float32)
    rel = idx[:, None] - idx[None, :]
    inner_decay = jnp.where(rel[None] >= 0,
                            jnp.exp(jnp.maximum(rel, 0.0)[None] * log_gamma[:, None, None]), 0.0)
    q_decay = jnp.exp((idx + 1.0)[None, :] * log_gamma[:, None])[None, :, :, None]
    k_decay = jnp.exp((CHUNK - 1.0 - idx)[None, :] * log_gamma[:, None])[None, :, :, None]
    chunk_decay = jnp.exp(CHUNK * log_gamma)[None, :, None, None]

    def step(state, xs):
        qc, kc, vc = xs
        scores = jnp.einsum('bhnd,bhmd->bhnm', qc, kc) * inner_decay[None]
        out = (jnp.einsum('bhnm,bhmv->bhnv', scores, vc)
               + jnp.einsum('bhnd,bhdv->bhnv', qc, state) * q_decay)
        state = state * chunk_decay + jnp.einsum('bhmd,bhmv->bhdv', kc * k_decay, vc)
        return state, out

    state0 = jnp.zeros((b, RET_HEADS, RET_QK_DIM, RET_V_DIM), jnp.float32)
    _, o = lax.scan(step, state0, (to_chunks(q), to_chunks(k), to_chunks(v)))
    o = o.transpose(1, 0, 3, 2, 4).reshape(b, s, RET_HEADS, RET_V_DIM)
    mu = jnp.mean(o, axis=-1, keepdims=True)
    var = jnp.mean(jnp.square(o - mu), axis=-1, keepdims=True)
    o = ((o - mu) * lax.rsqrt(var + EPS)).reshape(b, s, d_v)
    o = (o * jax.nn.silu(g)).astype(dt)
    return o @ w_o


def shared_kv(x, kv_norm, w_kv, cos, sin):
    b, s, _ = x.shape
    kv = rms_norm(x, kv_norm) @ w_kv
    d_k = DIFF_HEADS * 2 * DIFF_HEAD_DIM
    k = kv[..., :d_k].reshape(b, s, DIFF_HEADS, 2, DIFF_HEAD_DIM)
    k = apply_rope(k, cos, sin, ROPE_DIM).transpose(0, 2, 3, 1, 4)
    v = kv[..., d_k:].reshape(b, s, DIFF_HEADS, DIFF_V_DIM).transpose(0, 2, 1, 3)
    return k, v


def diff_attention(h, k_sh, v_sh, w_q, lq1, lk1, lq2, lk2, subln, w_o, lambda_init, cos, sin):
    b, s, _ = h.shape
    q = (h @ w_q).reshape(b, s, DIFF_HEADS, 2, DIFF_HEAD_DIM)
    q = apply_rope(q, cos, sin, ROPE_DIM)
    n_qb = s // Q_BLOCK
    qb = q.reshape(b, n_qb, Q_BLOCK, DIFF_HEADS, 2, DIFF_HEAD_DIM).transpose(1, 0, 3, 4, 2, 5)
    lam = (jnp.exp(jnp.sum(lq1.astype(jnp.float32) * lk1.astype(jnp.float32)))
           - jnp.exp(jnp.sum(lq2.astype(jnp.float32) * lk2.astype(jnp.float32))) + lambda_init)
    key_chunk = jnp.arange(s) // CHUNK
    scale = DIFF_HEAD_DIM ** -0.5
    neg = jnp.finfo(jnp.float32).min

    def block(args):
        q_blk, bi = args
        q_chunk = (bi * Q_BLOCK + jnp.arange(Q_BLOCK)) // CHUNK
        mask = key_chunk[None, :] <= q_chunk[:, None]
        sc = jnp.einsum('bhcqd,bhckd->bhcqk', q_blk, k_sh).astype(jnp.float32) * scale
        p = jax.nn.softmax(jnp.where(mask, sc, neg), axis=-1)
        a = p[:, :, 0] - lam * p[:, :, 1]
        return jnp.einsum('bhqk,bhkv->bhqv', a.astype(v_sh.dtype), v_sh)

    o = lax.map(block, (qb, jnp.arange(n_qb)))
    o = o.transpose(1, 0, 3, 2, 4).reshape(b, s, DIFF_HEADS, DIFF_V_DIM)
    o = rms_norm(o, subln) * (1.0 - lambda_init)
    return o.reshape(b, s, DIFF_HEADS * DIFF_V_DIM) @ w_o


def swiglu(h, w_gu, w_down):
    f = w_down.shape[0]
    gu = h @ w_gu
    return (jax.nn.silu(gu[..., :f]) * gu[..., f:]) @ w_down


def moe_swiglu(h, router, w_gu, w_down):
    b, s, d = h.shape
    t = h.reshape(b * s, d)
    logits = (t @ router).astype(jnp.float32)
    top_val, top_idx = lax.top_k(logits, TOP_K)
    top_w = jax.nn.softmax(top_val, axis=-1)
    gates = jnp.sum(jax.nn.one_hot(top_idx, N_EXPERTS, dtype=jnp.float32) * top_w[..., None], axis=1)
    gates = gates.astype(t.dtype)
    y = jnp.zeros_like(t)
    for e in range(N_EXPERTS):
        y = y + gates[:, e:e + 1] * swiglu(t, w_gu[e], w_down[e])
    return y.reshape(b, s, d)


def setup_inputs(seed: int = 0) -> dict:
    key = jax.random.key(seed)
    ks = jax.random.split(key, 24)
    D = D_MODEL
    def w(k, shape, fan_in):
        return jax.random.normal(k, shape, jnp.float32) * (fan_in ** -0.5)
    def gain(k, shape):
        return 1.0 + 0.01 * jax.random.normal(k, shape, jnp.float32)
    ret_in_cols = 2 * RET_HEADS * RET_QK_DIM + 2 * RET_HEADS * RET_V_DIM
    ret_v = RET_HEADS * RET_V_DIM
    diff_w = DIFF_HEADS * DIFF_V_DIM
    return {
        "x": jax.random.normal(ks[0], (BATCH, SEQ, D), jnp.float32),
        "ln_mix": gain(ks[1], (DEPTH, D)),
        "ln_ffn": gain(ks[2], (DEPTH, D)),
        "ret_w_in": w(ks[3], (N_A, D, ret_in_cols), D),
        "ret_w_o": w(ks[4], (N_A, ret_v, D), ret_v),
        "kv_norm": gain(ks[5], (D,)),
        "w_kv": w(ks[6], (D, DIFF_HEADS * 2 * DIFF_HEAD_DIM + diff_w), D),
        "diff_w_q": w(ks[7], (N_B, D, DIFF_HEADS * 2 * DIFF_HEAD_DIM), D),
        "lam_q1": 0.1 * jax.random.normal(ks[8], (N_B, DIFF_HEAD_DIM), jnp.float32),
        "lam_k1": 0.1 * jax.random.normal(ks[9], (N_B, DIFF_HEAD_DIM), jnp.float32),
        "lam_q2": 0.1 * jax.random.normal(ks[10], (N_B, DIFF_HEAD_DIM), jnp.float32),
        "lam_k2": 0.1 * jax.random.normal(ks[11], (N_B, DIFF_HEAD_DIM), jnp.float32),
        "diff_subln": gain(ks[12], (N_B, DIFF_V_DIM)),
        "diff_w_o": w(ks[13], (N_B, diff_w, D), diff_w),
        "ffn_w_gu": w(ks[14], (N_DENSE, D, 2 * FFN_DIM), D),
        "ffn_w_down": w(ks[15], (N_DENSE, FFN_DIM, D), FFN_DIM),
        "moe_router": w(ks[16], (N_MOE, D, N_EXPERTS), D),
        "moe_w_gu": w(ks[17], (N_MOE, N_EXPERTS, D, 2 * EXPERT_DIM), D),
        "moe_w_down": w(ks[18], (N_MOE, N_EXPERTS, EXPERT_DIM, D), EXPERT_DIM),
        "final_norm": gain(ks[19], (D,)),
    }


def reference(x, ln_mix, ln_ffn, ret_w_in, ret_w_o, kv_norm, w_kv, diff_w_q, lam_q1, lam_k1,
              lam_q2, lam_k2, diff_subln, diff_w_o, ffn_w_gu, ffn_w_down, moe_router, moe_w_gu,
              moe_w_down, final_norm):
    s = x.shape[1]
    cos, sin = rope_tables(s, ROPE_DIM, ROPE_THETA)
    k_sh, v_sh = (shared_kv(x, kv_norm, w_kv, cos, sin) if N_A == 0 else (None, None))
    for layer in range(DEPTH):
        h = rms_norm(x, ln_mix[layer])
        if layer < N_A:
            x = x + retention_mixer(h, ret_w_in[layer], ret_w_o[layer])
        else:
            bl = layer - N_A
            lambda_init = 0.8 - 0.6 * math.exp(-0.3 * layer)
            x = x + diff_attention(h, k_sh, v_sh, diff_w_q[bl], lam_q1[bl], lam_k1[bl],
                                   lam_q2[bl], lam_k2[bl], diff_subln[bl], diff_w_o[bl],
                                   lambda_init, cos, sin)
        h = rms_norm(x, ln_ffn[layer])
        if layer % 2 == 0:
            x = x + swiglu(h, ffn_w_gu[layer // 2], ffn_w_down[layer // 2])
        else:
            x = x + moe_swiglu(h, moe_router[layer // 2], moe_w_gu[layer // 2], moe_w_down[layer // 2])
        if layer == N_A - 1:
            k_sh, v_sh = shared_kv(x, kv_norm, w_kv, cos, sin)
    return rms_norm(x, final_norm)
```

```python
import functools
import math

import jax
import jax.numpy as jnp
from jax import lax
from jax.experimental import pallas as pl
from jax.experimental.pallas import tpu as pltpu

F32 = jnp.float32
BF16 = jnp.bfloat16

EPS = 1e-6
CHUNK = 64
RET_QK = 256
RET_V = 512
RET_HEADS = 4
RET_THETA = 10000.0
DIFF_HD = 64
DIFF_HEADS = 8
ROPE_THETA = 500000.0
ROPE_DIM = 16
N_EXPERTS = 8

LANES = 128
SUBLANES = 8
VMEM_LIMIT = 56 * 1024 * 1024

RET_CHUNK = 256
TOK_TILE = 512
ATT_TQ = 256
MOE_TILE = 512
MOE_FC = 1792
COPY_CHUNK = 256


def _cparams(sem, vmem=VMEM_LIMIT):
    return pltpu.CompilerParams(dimension_semantics=sem, vmem_limit_bytes=vmem)


def _const_spec(shape):
    nd = len(shape)
    return pl.BlockSpec(shape, lambda *_: (0,) * nd, pipeline_mode=pl.Buffered(1))


def _rms(x, g):
    return x * lax.rsqrt(jnp.mean(x * x, axis=-1, keepdims=True) + EPS) * g


def _dot(a, b):
    return jnp.dot(a, b, preferred_element_type=F32)


def _ret_in_kernel(x_ref, g_ref, w_ref, cos_ref, sin_ref, o_ref):
    h = _rms(x_ref[...], g_ref[...]).astype(BF16)
    cos = cos_ref[...]
    sin = sin_ref[...]
    d_qk = RET_HEADS * RET_QK
    half = RET_QK // 2
    for c in range(2 * RET_HEADS):
        c0 = c * RET_QK
        acc = _dot(h, w_ref[:, c0:c0 + RET_QK])
        x1 = acc[:, :half]
        x2 = acc[:, half:]
        scale = 1.0 if c < RET_HEADS else RET_QK ** -0.5
        o_ref[:, c0:c0 + half] = ((x1 * cos - x2 * sin) * scale).astype(BF16)
        o_ref[:, c0 + half:c0 + RET_QK] = ((x2 * cos + x1 * sin) * scale).astype(BF16)
    n_rest = (w_ref.shape[1] - 2 * d_qk) // RET_V
    for c in range(n_rest):
        c0 = 2 * d_qk + c * RET_V
        o_ref[:, c0:c0 + RET_V] = _dot(h, w_ref[:, c0:c0 + RET_V]).astype(BF16)


def _ret_in_proj(x2d, gain, w, cos, sin, seq):
    t, d = x2d.shape
    n = w.shape[1]
    tm = TOK_TILE
    n_pos = seq // tm
    return pl.pallas_call(
        _ret_in_kernel,
        out_shape=jax.ShapeDtypeStruct((t, n), BF16),
        grid=(t // tm,),
        in_specs=[
            pl.BlockSpec((tm, d), lambda i: (i, 0)),
            _const_spec((1, d)),
            _const_spec((d, n)),
            pl.BlockSpec((tm, RET_QK // 2), lambda i: (i % n_pos, 0)),
            pl.BlockSpec((tm, RET_QK // 2), lambda i: (i % n_pos, 0)),
        ],
        out_specs=pl.BlockSpec((tm, n), lambda i: (i, 0)),
        compiler_params=_cparams(("parallel",)),
        name="ret_in_proj",
    )(x2d, gain, w, cos, sin)


def _retention_kernel(q_ref, k_ref, v_ref, g_ref, dmat_ref, qd_ref, kd_ref, cd_ref, o_ref, state_ref):
    @pl.when(pl.program_id(1) == 0)
    def _():
        state_ref[...] = jnp.zeros_like(state_ref)

    n_chunks = q_ref.shape[0] // RET_CHUNK
    for h in range(RET_HEADS):
        for c in range(n_chunks):
            rows = slice(c * RET_CHUNK, (c + 1) * RET_CHUNK)
            q = q_ref[rows, h * RET_QK:(h + 1) * RET_QK]
            k = k_ref[rows, h * RET_QK:(h + 1) * RET_QK]
            v = v_ref[rows, h * RET_V:(h + 1) * RET_V]
            state = state_ref[h]
            s = lax.dot_general(q, k, (((1,), (1,)), ((), ())), preferred_element_type=F32)
            s = s * dmat_ref[h]
            o = _dot(s.astype(BF16), v)
            qs = (q.astype(F32) * qd_ref[h]).astype(BF16)
            o = o + _dot(qs, state.astype(BF16))
            ks = (k.astype(F32) * kd_ref[h]).astype(BF16)
            state_ref[h] = state * cd_ref[h] + lax.dot_general(
                ks, v, (((0,), (0,)), ((), ())), preferred_element_type=F32)
            mu = jnp.mean(o, axis=-1, keepdims=True)
            oc = o - mu
            var = jnp.mean(oc * oc, axis=-1, keepdims=True)
            on = oc * lax.rsqrt(var + EPS)
            gate = g_ref[rows, h * RET_V:(h + 1) * RET_V].astype(F32)
            o_ref[rows, h * RET_V:(h + 1) * RET_V] = (on * (gate * jax.nn.sigmoid(gate))).astype(BF16)


def _retention(proj, dmat, qd, kd, cd, batch, seq):
    t = proj.shape[0]
    tb = TOK_TILE
    nj = seq // tb
    d_qk = RET_HEADS * RET_QK
    d_v = RET_HEADS * RET_V
    row = lambda b, j: b * nj + j
    return pl.pallas_call(
        _retention_kernel,
        out_shape=jax.ShapeDtypeStruct((t, d_v), BF16),
        grid=(batch, nj),
        in_specs=[
            pl.BlockSpec((tb, d_qk), lambda b, j: (row(b, j), 0)),
            pl.BlockSpec((tb, d_qk), lambda b, j: (row(b, j), 1)),
            pl.BlockSpec((tb, d_v), lambda b, j: (row(b, j), 1)),
            pl.BlockSpec((tb, d_v), lambda b, j: (row(b, j), 2)),
            _const_spec(dmat.shape),
            _const_spec(qd.shape),
            _const_spec(kd.shape),
            _const_spec(cd.shape),
        ],
        out_specs=pl.BlockSpec((tb, d_v), lambda b, j: (row(b, j), 0)),
        scratch_shapes=[pltpu.VMEM((RET_HEADS, RET_QK, RET_V), F32)],
        compiler_params=_cparams(("parallel", "arbitrary")),
        name="retention",
    )(proj, proj, proj, proj, dmat, qd, kd, cd)


def _ret_out_ffn_kernel(x_ref, o_ref, wo_ref, g_ref, wgu_ref, wd_ref, out_ref, act_ref, *, fc):
    x1 = x_ref[...] + _dot(o_ref[...], wo_ref[...])
    h = _rms(x1, g_ref[...]).astype(BF16)
    f = wd_ref.shape[0]
    for c in range(f // fc):
        gt = _dot(h, wgu_ref[:, c * fc:(c + 1) * fc])
        up = _dot(h, wgu_ref[:, f + c * fc:f + (c + 1) * fc])
        act_ref[:, c * fc:(c + 1) * fc] = (gt * jax.nn.sigmoid(gt) * up).astype(BF16)
    out_ref[...] = x1 + _dot(act_ref[...], wd_ref[...])


def _ret_out_ffn(x2d, o, w_o, gain, w_gu, w_down):
    t, d = x2d.shape
    tm = TOK_TILE
    f = w_down.shape[0]
    return pl.pallas_call(
        functools.partial(_ret_out_ffn_kernel, fc=256),
        out_shape=jax.ShapeDtypeStruct((t, d), F32),
        grid=(t // tm,),
        in_specs=[
            pl.BlockSpec((tm, d), lambda i: (i, 0)),
            pl.BlockSpec((tm, o.shape[1]), lambda i: (i, 0)),
            _const_spec(w_o.shape),
            _const_spec((1, d)),
            _const_spec(w_gu.shape),
            _const_spec(w_down.shape),
        ],
        out_specs=pl.BlockSpec((tm, d), lambda i: (i, 0)),
        scratch_shapes=[pltpu.VMEM((tm, f), BF16)],
        compiler_params=_cparams(("parallel",)),
        name="ret_out_ffn",
    )(x2d, o, w_o, gain, w_gu, w_down)


def _rope16(x, ctab, s1tab, s2tab):
    half = ROPE_DIM // 2
    return (x * ctab + pltpu.roll(x, LANES - half, 1) * s1tab + pltpu.roll(x, half, 1) * s2tab)


def _qkv_kernel(x_ref, gq_ref, gkv_ref, wq_ref, wkv_ref, c_ref, s1_ref, s2_ref, q_ref, k_ref, v_ref):
    x = x_ref[...]
    xn = x * lax.rsqrt(jnp.mean(x * x, axis=-1, keepdims=True) + EPS)
    hq = (xn * gq_ref[...]).astype(BF16)
    hkv = (xn * gkv_ref[...]).astype(BF16)
    ctab, s1tab, s2tab = c_ref[...], s1_ref[...], s2_ref[...]
    d = q_ref.shape[1]
    q_scale = DIFF_HD ** -0.5
    for c in range(d // LANES):
        cols = slice(c * LANES, (c + 1) * LANES)
        q_ref[:, cols] = (_rope16(_dot(hq, wq_ref[:, cols]), ctab, s1tab, s2tab) * q_scale).astype(BF16)
        k_ref[:, cols] = _rope16(_dot(hkv, wkv_ref[:, cols]), ctab, s1tab, s2tab).astype(BF16)
    v_ref[...] = _dot(hkv, wkv_ref[:, d:]).astype(BF16)


def _qkv_proj(x2d, gq, gkv, w_q, w_kv, ctab, s1tab, s2tab, seq):
    t, d = x2d.shape
    tm = TOK_TILE
    n_pos = seq // tm
    tok = pl.BlockSpec((tm, d), lambda i: (i, 0))
    tab = pl.BlockSpec((tm, LANES), lambda i: (i % n_pos, 0))
    return pl.pallas_call(
        _qkv_kernel,
        out_shape=[jax.ShapeDtypeStruct((t, d), BF16)] * 3,
        grid=(t // tm,),
        in_specs=[tok, _const_spec((1, d)), _const_spec((1, d)), _const_spec(w_q.shape),
                  _const_spec(w_kv.shape), tab, tab, tab],
        out_specs=[tok, tok, tok],
        compiler_params=_cparams(("parallel",)),
        name="qkv_proj",
    )(x2d, gq, gkv, w_q, w_kv, ctab, s1tab, s2tab)


def _attn_kernel(lam_ref, q_ref, k_ref, v_ref, sub_ref, o_ref, m_ref, l_ref, acc_ref, *, lambda_init):
    i = pl.program_id(2)
    tq = q_ref.shape[0]
    q = q_ref[...]
    lane = lax.broadcasted_iota(jnp.int32, q.shape, 1)
    zero = jnp.zeros_like(q)
    qc = (jnp.where(lane < DIFF_HD, q, zero), jnp.where(lane >= DIFF_HD, q, zero))
    m_ref[...] = jnp.full(m_ref.shape, -jnp.inf, F32)
    l_ref[...] = jnp.zeros_like(l_ref)
    acc_ref[...] = jnp.zeros_like(acc_ref)

    def step(j, masked):
        start = pl.multiple_of(j * tq, tq)
        kt = k_ref[pl.ds(start, tq), :]
        vt = v_ref[pl.ds(start, tq), :]
        if masked:
            r = lax.broadcasted_iota(jnp.int32, (tq, tq), 0) // CHUNK
            cidx = lax.broadcasted_iota(jnp.int32, (tq, tq), 1) // CHUNK
            visible = cidx <= r
        for c in range(2):
            s = lax.dot_general(qc[c], kt, (((1,), (1,)), ((), ())), preferred_element_type=F32)
            if masked:
                s = jnp.where(visible, s, -1e30)
            m_old = m_ref[c]
            m_new = jnp.maximum(m_old, jnp.max(s, axis=-1, keepdims=True))
            alpha = jnp.exp(m_old - m_new)
            p = jnp.exp(s - m_new)
            l_ref[c] = alpha * l_ref[c] + jnp.sum(p, axis=-1, keepdims=True)
            acc_ref[c] = alpha * acc_ref[c] + _dot(p.astype(BF16), vt)
            m_ref[c] = m_new

    def body(j, carry):
        step(j, False)
        return carry

    lax.fori_loop(0, i, body, 0)
    step(i, True)

    lam_v = lam_ref[...]
    lam = (jnp.exp(jnp.sum(lam_v[0:1] * lam_v[1:2], axis=-1, keepdims=True))
           - jnp.exp(jnp.sum(lam_v[2:3] * lam_v[3:4], axis=-1, keepdims=True)) + lambda_init)
    o = acc_ref[0] / l_ref[0] - lam * (acc_ref[1] / l_ref[1])
    o = _rms(o, sub_ref[...]) * (1.0 - lambda_init)
    o_ref[...] = o.astype(BF16)


def _diff_attention(q, k, v, lam_vecs, subln, batch, seq, lambda_init):
    t, d = q.shape
    tq = ATT_TQ
    nq = seq // tq
    dv = 2 * DIFF_HD
    return pl.pallas_call(
        functools.partial(_attn_kernel, lambda_init=lambda_init),
        out_shape=jax.ShapeDtypeStruct((t, d), BF16),
        grid=(batch, DIFF_HEADS, nq),
        in_specs=[
            _const_spec(lam_vecs.shape),
            pl.BlockSpec((tq, dv), lambda b, h, i: (b * nq + i, h)),
            pl.BlockSpec((seq, dv), lambda b, h, i: (b, h)),
            pl.BlockSpec((seq, dv), lambda b, h, i: (b, h)),
            _const_spec((1, dv)),
        ],
        out_specs=pl.BlockSpec((tq, dv), lambda b, h, i: (b * nq + i, h)),
        scratch_shapes=[pltpu.VMEM((2, tq, 1), F32), pltpu.VMEM((2, tq, 1), F32),
                        pltpu.VMEM((2, tq, dv), F32)],
        compiler_params=_cparams(("parallel", "parallel", "arbitrary")),
        name="diff_attn",
    )(lam_vecs, q, k, v, subln)


def _attn_out_kernel(x_ref, o_ref, wo_ref, g_ref, r_ref, x3_ref, h3_ref, route_ref, cnt_ref, carry_ref):
    @pl.when(pl.program_id(0) == 0)
    def _():
        carry_ref[...] = jnp.zeros_like(carry_ref)

    tm = x_ref.shape[0]
    x3 = x_ref[...] + _dot(o_ref[...], wo_ref[...])
    x3_ref[...] = x3
    h3 = _rms(x3, g_ref[...])
    for s in range(h3.shape[1] // LANES):
        h3_ref[pl.ds(s, tm, stride=SUBLANES), :] = h3[:, s * LANES:(s + 1) * LANES]

    logits = jnp.dot(h3, r_ref[...], preferred_element_type=F32, precision=lax.Precision.HIGHEST)
    lane = lax.broadcasted_iota(jnp.int32, logits.shape, 1).astype(F32)
    lg = jnp.where(lane < N_EXPERTS, logits, -jnp.inf)
    v1 = jnp.max(lg, axis=-1, keepdims=True)
    i1 = jnp.min(jnp.where(lg == v1, lane, float(LANES)), axis=-1, keepdims=True)
    lg2 = jnp.where(lane == i1, -jnp.inf, lg)
    v2 = jnp.max(lg2, axis=-1, keepdims=True)
    i2 = jnp.min(jnp.where(lg2 == v2, lane, float(LANES)), axis=-1, keepdims=True)
    e = jnp.exp(v2 - v1)
    w1 = 1.0 / (1.0 + e)
    w2 = e / (1.0 + e)
    oh1 = lane == i1
    oh2 = lane == i2
    assign = jnp.where(oh1 | oh2, 1.0, 0.0)
    rr = lax.broadcasted_iota(jnp.int32, (tm, tm), 0)
    cc = lax.broadcasted_iota(jnp.int32, (tm, tm), 1)
    lower = jnp.where(cc < rr, 1.0, 0.0).astype(BF16)
    excl = _dot(lower, assign.astype(BF16)) + carry_ref[0:1, :]
    r1 = jnp.sum(jnp.where(oh1, excl, 0.0), axis=-1, keepdims=True)
    r2 = jnp.sum(jnp.where(oh2, excl, 0.0), axis=-1, keepdims=True)
    route = jnp.zeros_like(logits)
    for col, val in enumerate((i1, i2, r1, r2, w1, w2)):
        route = jnp.where(lane == float(col), val, route)
    route_ref[...] = route
    total = carry_ref[0:1, :] + jnp.sum(assign, axis=0, keepdims=True)
    carry_ref[...] = jnp.broadcast_to(total, carry_ref.shape)
    cnt_ref[...] = jnp.broadcast_to(total, cnt_ref.shape)


def _attn_out(x2d, o, w_o, gain, router_pad):
    t, d = x2d.shape
    tm = TOK_TILE
    n_sub = d // LANES
    return pl.pallas_call(
        _attn_out_kernel,
        out_shape=[jax.ShapeDtypeStruct((t, d), F32),
                   jax.ShapeDtypeStruct((t * n_sub, LANES), F32),
                   jax.ShapeDtypeStruct((t, LANES), F32),
                   jax.ShapeDtypeStruct((SUBLANES, LANES), F32)],
        grid=(t // tm,),
        in_specs=[pl.BlockSpec((tm, d), lambda i: (i, 0)),
                  pl.BlockSpec((tm, d), lambda i: (i, 0)),
                  _const_spec(w_o.shape), _const_spec((1, d)), _const_spec(router_pad.shape)],
        out_specs=[pl.BlockSpec((tm, d), lambda i: (i, 0)),
                   pl.BlockSpec((tm * n_sub, LANES), lambda i: (i, 0)),
                   pl.BlockSpec((tm, LANES), lambda i: (i, 0)),
                   _const_spec((SUBLANES, LANES))],
        scratch_shapes=[pltpu.VMEM((SUBLANES, LANES), F32)],
        compiler_params=_cparams(("arbitrary",)),
        name="attn_out_route",
    )(x2d, o, w_o, gain, router_pad)


def _row_copy_body(sidx_ref, didx_ref, src_ref, dst_ref, sem, n):
    def chunk_wait():
        pltpu.make_async_copy(src_ref.at[pl.ds(0, COPY_CHUNK)], dst_ref.at[pl.ds(0, COPY_CHUNK)], sem).wait()

    def outer(c, carry):
        def inner(r, carry2):
            i = c * COPY_CHUNK + r
            pltpu.make_async_copy(src_ref.at[sidx_ref[i]], dst_ref.at[didx_ref[i]], sem).start()
            return carry2

        lax.fori_loop(0, COPY_CHUNK, inner, 0, unroll=8)

        @pl.when(c > 0)
        def _():
            chunk_wait()

        return carry

    lax.fori_loop(0, n // COPY_CHUNK, outer, 0)
    chunk_wait()


def _dispatch_kernel(sidx_ref, didx_ref, zpos_ref, zflag_ref, src_ref, dst_ref, zbuf_ref, zsem, sem, *, n):
    zbuf_ref[...] = jnp.zeros_like(zbuf_ref)

    def zcopy(e):
        return pltpu.make_async_copy(zbuf_ref, dst_ref.at[pl.ds(zpos_ref[e], MOE_TILE)], zsem.at[e])

    for e in range(2 * N_EXPERTS):
        @pl.when(zflag_ref[e] == 1)
        def _():
            zcopy(e).start()
    for e in range(2 * N_EXPERTS):
        @pl.when(zflag_ref[e] == 1)
        def _():
            zcopy(e).wait()
    _row_copy_body(sidx_ref, didx_ref, src_ref, dst_ref, sem, n)


def _unpermute_kernel(sidx_ref, didx_ref, src_ref, dst_ref, sem, *, n):
    _row_copy_body(sidx_ref, didx_ref, src_ref, dst_ref, sem, n)


def _dispatch(h3_rows, sidx, didx, zpos, zflag, n_dst):
    n = sidx.shape[0]
    return pl.pallas_call(
        functools.partial(_dispatch_kernel, n=n),
        out_shape=jax.ShapeDtypeStruct((n_dst,) + h3_rows.shape[1:], F32),
        grid_spec=pltpu.PrefetchScalarGridSpec(
            num_scalar_prefetch=4, grid=(1,),
            in_specs=[pl.BlockSpec(memory_space=pl.ANY)],
            out_specs=pl.BlockSpec(memory_space=pl.ANY),
            scratch_shapes=[pltpu.VMEM((MOE_TILE,) + h3_rows.shape[1:], F32),
                            pltpu.SemaphoreType.DMA((2 * N_EXPERTS,)), pltpu.SemaphoreType.DMA(())]),
        compiler_params=_cparams(("arbitrary",)),
        name="moe_dispatch",
    )(sidx, didx, zpos, zflag, h3_rows)


def _unpermute(ys_rows, sidx, didx, n_dst):
    n = sidx.shape[0]
    return pl.pallas_call(
        functools.partial(_unpermute_kernel, n=n),
        out_shape=jax.ShapeDtypeStruct((n_dst,) + ys_rows.shape[1:], F32),
        grid_spec=pltpu.PrefetchScalarGridSpec(
            num_scalar_prefetch=2, grid=(1,),
            in_specs=[pl.BlockSpec(memory_space=pl.ANY)],
            out_specs=pl.BlockSpec(memory_space=pl.ANY),
            scratch_shapes=[pltpu.SemaphoreType.DMA(())]),
        compiler_params=_cparams(("arbitrary",)),
        name="moe_unpermute",
    )(sidx, didx, ys_rows)


def _moe_kernel(te_ref, tv_ref, ts_ref, xs_ref, wg_ref, wu_ref, wd_ref, ys_ref, h_ref, acc_ref):
    i = pl.program_id(0)
    c = pl.program_id(1)
    last = pl.num_programs(1) - 1
    valid = tv_ref[i] == 1
    tile = h_ref.shape[0]
    n_sub = h_ref.shape[1] // LANES

    @pl.when(valid & (c == 0))
    def _():
        for s in range(n_sub):
            h_ref[:, s * LANES:(s + 1) * LANES] = xs_ref[pl.ds(s, tile, stride=SUBLANES), :].astype(BF16)
        acc_ref[...] = jnp.zeros_like(acc_ref)

    @pl.when(valid)
    def _():
        h = h_ref[...]
        gt = _dot(h, wg_ref[...])
        up = _dot(h, wu_ref[...])
        act = (gt * jax.nn.sigmoid(gt) * up).astype(BF16)
        acc_ref[...] += _dot(act, wd_ref[...])

    @pl.when(valid & (c == last))
    def _():
        for s in range(n_sub):
            ys_ref[pl.ds(s, tile, stride=SUBLANES), :] = acc_ref[:, s * LANES:(s + 1) * LANES]

    @pl.when(jnp.logical_not(valid) & (c == last))
    def _():
        ys_ref[...] = jnp.zeros_like(ys_ref)


def _moe_experts(xs2d, w_gu, w_down, tile_expert, tile_valid, tile_src, d):
    n_sub = d // LANES
    n_tiles = xs2d.shape[0] // (MOE_TILE * n_sub)
    f = w_down.shape[1]
    nfc = f // MOE_FC
    blk = MOE_TILE * n_sub

    def ceff(i, c, tv):
        return jnp.where(tv[i] == 1, c, nfc - 1)

    return pl.pallas_call(
        _moe_kernel,
        out_shape=jax.ShapeDtypeStruct(xs2d.shape, F32),
        grid_spec=pltpu.PrefetchScalarGridSpec(
            num_scalar_prefetch=3, grid=(n_tiles, nfc),
            in_specs=[
                pl.BlockSpec((blk, LANES), lambda i, c, te, tv, ts: (ts[i], 0)),
                pl.BlockSpec((None, d, MOE_FC), lambda i, c, te, tv, ts: (te[i], 0, ceff(i, c, tv))),
                pl.BlockSpec((None, d, MOE_FC), lambda i, c, te, tv, ts: (te[i], 0, nfc + ceff(i, c, tv))),
                pl.BlockSpec((None, MOE_FC, d), lambda i, c, te, tv, ts: (te[i], ceff(i, c, tv), 0)),
            ],
            out_specs=pl.BlockSpec((blk, LANES), lambda i, c, te, tv, ts: (i, 0)),
            scratch_shapes=[pltpu.VMEM((MOE_TILE, d), BF16), pltpu.VMEM((MOE_TILE, d), F32)]),
        compiler_params=_cparams(("arbitrary", "arbitrary")),
        name="moe_experts",
    )(tile_expert, tile_valid, tile_src, xs2d, w_gu, w_gu, w_down)


def _combine_kernel(x_ref, y_ref, route_ref, g_ref, o_ref):
    tm, d = x_ref.shape
    n_sub = d // LANES
    route = route_ref[...]
    w1 = route[:, 4:5]
    w2 = route[:, 5:6]
    for s in range(n_sub):
        cols = slice(s * LANES, (s + 1) * LANES)
        y1 = y_ref[pl.ds(s, tm, stride=2 * n_sub), :]
        y2 = y_ref[pl.ds(n_sub + s, tm, stride=2 * n_sub), :]
        o_ref[:, cols] = x_ref[:, cols] + w1 * y1 + w2 * y2
    o_ref[...] = _rms(o_ref[...], g_ref[...])


def _combine(x2d, y2, route, gain):
    t, d = x2d.shape
    tm = TOK_TILE
    n_sub = d // LANES
    return pl.pallas_call(
        _combine_kernel,
        out_shape=jax.ShapeDtypeStruct((t, d), F32),
        grid=(t // tm,),
        in_specs=[pl.BlockSpec((tm, d), lambda i: (i, 0)),
                  pl.BlockSpec((tm * 2 * n_sub, LANES), lambda i: (i, 0)),
                  pl.BlockSpec((tm, LANES), lambda i: (i, 0)),
                  _const_spec((1, d))],
        out_specs=pl.BlockSpec((tm, d), lambda i: (i, 0)),
        compiler_params=_cparams(("parallel",)),
        name="moe_combine",
    )(x2d, y2, route, gain)


def _rope_tables(seq, rot_dim, theta):
    inv = 1.0 / (theta ** (jnp.arange(0, rot_dim, 2, dtype=F32) / rot_dim))
    ang = jnp.arange(seq, dtype=F32)[:, None] * inv[None, :]
    return jnp.cos(ang), jnp.sin(ang)


def _retention_tables():
    c = RET_CHUNK
    log_gamma = jnp.log(1.0 - 2.0 ** (-5.0 - jnp.arange(RET_HEADS, dtype=F32)))
    idx = jnp.arange(c, dtype=F32)
    rel = idx[:, None] - idx[None, :]
    dmat = jnp.where(rel[None] >= 0, jnp.exp(jnp.maximum(rel, 0.0)[None] * log_gamma[:, None, None]), 0.0)
    qd = jnp.exp((idx + 1.0)[None, :] * log_gamma[:, None])[:, :, None]
    kd = jnp.exp((c - 1.0 - idx)[None, :] * log_gamma[:, None])[:, :, None]
    cd = jnp.exp(c * log_gamma)[:, None, None]
    return dmat, qd, kd, cd


def _attn_rope_tables(seq):
    cos, sin = _rope_tables(seq, ROPE_DIM, ROPE_THETA)
    half = ROPE_DIM // 2
    pad = DIFF_HD - ROPE_DIM
    ones = jnp.ones((seq, pad), F32)
    zeros = jnp.zeros((seq, pad), F32)
    zh = jnp.zeros((seq, half), F32)
    ctab = jnp.concatenate([cos, cos, ones], axis=1)
    s1tab = jnp.concatenate([-sin, zh, zeros], axis=1)
    s2tab = jnp.concatenate([zh, sin, zeros], axis=1)
    rep = LANES // DIFF_HD
    return tuple(jnp.tile(tb, (1, rep)) for tb in (ctab, s1tab, s2tab))


def kernel(x, ln_mix, ln_ffn, ret_w_in, ret_w_o, kv_norm, w_kv, diff_w_q, lam_q1, lam_k1, lam_q2, lam_k2,
           diff_subln, diff_w_o, ffn_w_gu, ffn_w_down, moe_router, moe_w_gu, moe_w_down, final_norm):
    batch, seq, d = x.shape
    t = batch * seq
    assert ln_mix.shape[0] == 2 and ret_w_in.shape[0] == 1 and diff_w_q.shape[0] == 1
    assert seq % TOK_TILE == 0 and TOK_TILE % RET_CHUNK == 0 and seq % ATT_TQ == 0 and ATT_TQ % CHUNK == 0
    x2d = x.reshape(t, d)
    row = lambda g: g.reshape(1, -1)

    cos_r, sin_r = _rope_tables(seq, RET_QK, RET_THETA)
    proj = _ret_in_proj(x2d, row(ln_mix[0]), ret_w_in[0].astype(BF16), cos_r, sin_r, seq)
    ret_o = _retention(proj, *_retention_tables(), batch, seq)
    x2 = _ret_out_ffn(x2d, ret_o, ret_w_o[0].astype(BF16), row(ln_ffn[0]),
                      ffn_w_gu[0].astype(BF16), ffn_w_down[0].astype(BF16))

    lambda_init = 0.8 - 0.6 * math.exp(-0.3 * 1)
    q, k, v = _qkv_proj(x2, row(ln_mix[1]), row(kv_norm), diff_w_q[0].astype(BF16), w_kv.astype(BF16),
                        *_attn_rope_tables(seq), seq)
    lam_vecs = jnp.stack([lam_q1[0], lam_k1[0], lam_q2[0], lam_k2[0]]).astype(F32)
    att = _diff_attention(q, k, v, lam_vecs, row(diff_subln[0]), batch, seq, lambda_init)
    router_pad = jnp.pad(moe_router[0], ((0, 0), (0, LANES - N_EXPERTS)))
    x3, h3_rows, route, cnt = _attn_out(x2, att, diff_w_o[0].astype(BF16), row(ln_ffn[1]), router_pad)

    n_sub = d // LANES
    n_rows = 2 * t + N_EXPERTS * MOE_TILE
    n_tiles = n_rows // MOE_TILE
    expert = route[:, 0:2].astype(jnp.int32)
    rank = route[:, 2:4].astype(jnp.int32)
    counts = cnt[0, :N_EXPERTS].astype(jnp.int32)
    padded = (counts + MOE_TILE - 1) // MOE_TILE * MOE_TILE
    seg_end = jnp.cumsum(padded)
    seg_start = seg_end - padded
    dest = (jnp.sum(jnp.where(expert[..., None] == jnp.arange(N_EXPERTS), seg_start, 0), axis=-1)
            + rank).reshape(-1)
    tok = jnp.arange(2 * t, dtype=jnp.int32) // 2
    tile_row = jnp.arange(n_tiles, dtype=jnp.int32) * MOE_TILE
    tile_valid = (tile_row < seg_end[-1]).astype(jnp.int32)
    n_valid = seg_end[-1] // MOE_TILE
    tile_src = jnp.minimum(jnp.arange(n_tiles, dtype=jnp.int32), jnp.maximum(n_valid - 1, 0))
    tile_expert = jnp.minimum(jnp.sum(tile_src[:, None] * MOE_TILE >= seg_end[None, :], axis=1),
                              N_EXPERTS - 1).astype(jnp.int32)
    zflag = jnp.concatenate([(padded > 0).astype(jnp.int32), 1 - tile_valid[-N_EXPERTS:]])
    zpos = jnp.concatenate([jnp.maximum(seg_end - MOE_TILE, 0), tile_row[-N_EXPERTS:]]).astype(jnp.int32)

    xs = _dispatch(h3_rows.reshape(t, n_sub, LANES), tok, dest, zpos, zflag, n_rows)
    ys = _moe_experts(xs.reshape(n_rows * n_sub, LANES), moe_w_gu[0].astype(BF16),
                      moe_w_down[0].astype(BF16), tile_expert, tile_valid, tile_src, d)
    y2 = _unpermute(ys.reshape(n_rows, n_sub, LANES), dest, jnp.arange(2 * t, dtype=jnp.int32), 2 * t)
    out = _combine(x3, y2.reshape(2 * t * n_sub, LANES), route, row(final_norm))
    return out.reshape(batch, seq, d)
```

```python
import functools
import math

import jax
import jax.numpy as jnp
from jax import lax
from jax.experimental import pallas as pl
from jax.experimental.pallas import tpu as pltpu

F32 = jnp.float32
BF16 = jnp.bfloat16

EPS = 1e-6
CHUNK = 64
RET_QK = 256
RET_V = 512
RET_HEADS = 4
RET_THETA = 10000.0
DIFF_HD = 64
DIFF_HEADS = 8
ROPE_THETA = 500000.0
ROPE_DIM = 16
N_EXPERTS = 8

LANES = 128
SUBLANES = 8
VMEM_LIMIT = 56 * 1024 * 1024

RET_CHUNK = 256
TOK_TILE = 512
ATT_TQ = 256
ATT_TK = 512
ATT_ONES = 16
MOE_TILE = 512
MOE_FC = 1792


def _cparams(sem, vmem=VMEM_LIMIT):
    return pltpu.CompilerParams(dimension_semantics=sem, vmem_limit_bytes=vmem)


def _const_spec(shape):
    nd = len(shape)
    return pl.BlockSpec(shape, lambda *_: (0,) * nd, pipeline_mode=pl.Buffered(1))


def _rms(x, g):
    return x * lax.rsqrt(jnp.mean(x * x, axis=-1, keepdims=True) + EPS) * g


def _dot(a, b):
    return jnp.dot(a, b, preferred_element_type=F32)


def _ret_in_kernel(x_ref, g_ref, w_ref, cos_ref, sin_ref, o_ref):
    h = _rms(x_ref[...], g_ref[...]).astype(BF16)
    cos = cos_ref[...]
    sin = sin_ref[...]
    d_qk = RET_HEADS * RET_QK
    half = RET_QK // 2
    for c in range(2 * RET_HEADS):
        c0 = c * RET_QK
        acc = _dot(h, w_ref[:, c0:c0 + RET_QK])
        x1 = acc[:, :half]
        x2 = acc[:, half:]
        scale = 1.0 if c < RET_HEADS else RET_QK ** -0.5
        o_ref[:, c0:c0 + half] = ((x1 * cos - x2 * sin) * scale).astype(BF16)
        o_ref[:, c0 + half:c0 + RET_QK] = ((x2 * cos + x1 * sin) * scale).astype(BF16)
    n_rest = (w_ref.shape[1] - 2 * d_qk) // RET_V
    for c in range(n_rest):
        c0 = 2 * d_qk + c * RET_V
        o_ref[:, c0:c0 + RET_V] = _dot(h, w_ref[:, c0:c0 + RET_V]).astype(BF16)


def _ret_in_proj(x2d, gain, w, cos, sin, seq):
    t, d = x2d.shape
    n = w.shape[1]
    tm = TOK_TILE
    n_pos = seq // tm
    return pl.pallas_call(
        _ret_in_kernel,
        out_shape=jax.ShapeDtypeStruct((t, n), BF16),
        grid=(t // tm,),
        in_specs=[
            pl.BlockSpec((tm, d), lambda i: (i, 0)),
            _const_spec((1, d)),
            _const_spec((d, n)),
            pl.BlockSpec((tm, RET_QK // 2), lambda i: (i % n_pos, 0)),
            pl.BlockSpec((tm, RET_QK // 2), lambda i: (i % n_pos, 0)),
        ],
        out_specs=pl.BlockSpec((tm, n), lambda i: (i, 0)),
        compiler_params=_cparams(("parallel",)),
        name="ret_in_proj",
    )(x2d, gain, w, cos, sin)


def _retention_kernel(q_ref, k_ref, v_ref, g_ref, dmat_ref, qd_ref, kd_ref, cd_ref, o_ref, state_ref):
    @pl.when(pl.program_id(1) == 0)
    def _():
        state_ref[...] = jnp.zeros_like(state_ref)

    n_chunks = q_ref.shape[0] // RET_CHUNK
    for h in range(RET_HEADS):
        for c in range(n_chunks):
            rows = slice(c * RET_CHUNK, (c + 1) * RET_CHUNK)
            q = q_ref[rows, h * RET_QK:(h + 1) * RET_QK]
            k = k_ref[rows, h * RET_QK:(h + 1) * RET_QK]
            v = v_ref[rows, h * RET_V:(h + 1) * RET_V]
            state = state_ref[h]
            s = lax.dot_general(q, k, (((1,), (1,)), ((), ())), preferred_element_type=F32)
            s = s * dmat_ref[h]
            o = _dot(s.astype(BF16), v)
            qs = (q.astype(F32) * qd_ref[h]).astype(BF16)
            o = o + _dot(qs, state.astype(BF16))
            ks = (k.astype(F32) * kd_ref[h]).astype(BF16)
            state_ref[h] = state * cd_ref[h] + lax.dot_general(
                ks, v, (((0,), (0,)), ((), ())), preferred_element_type=F32)
            mu = jnp.mean(o, axis=-1, keepdims=True)
            oc = o - mu
            var = jnp.mean(oc * oc, axis=-1, keepdims=True)
            on = oc * lax.rsqrt(var + EPS)
            gate = g_ref[rows, h * RET_V:(h + 1) * RET_V].astype(F32)
            o_ref[rows, h * RET_V:(h + 1) * RET_V] = (on * (gate * jax.nn.sigmoid(gate))).astype(BF16)


def _retention(proj, dmat, qd, kd, cd, batch, seq):
    t = proj.shape[0]
    tb = TOK_TILE
    nj = seq // tb
    d_qk = RET_HEADS * RET_QK
    d_v = RET_HEADS * RET_V
    row = lambda b, j: b * nj + j
    return pl.pallas_call(
        _retention_kernel,
        out_shape=jax.ShapeDtypeStruct((t, d_v), BF16),
        grid=(batch, nj),
        in_specs=[
            pl.BlockSpec((tb, d_qk), lambda b, j: (row(b, j), 0)),
            pl.BlockSpec((tb, d_qk), lambda b, j: (row(b, j), 1)),
            pl.BlockSpec((tb, d_v), lambda b, j: (row(b, j), 1)),
            pl.BlockSpec((tb, d_v), lambda b, j: (row(b, j), 2)),
            _const_spec(dmat.shape),
            _const_spec(qd.shape),
            _const_spec(kd.shape),
            _const_spec(cd.shape),
        ],
        out_specs=pl.BlockSpec((tb, d_v), lambda b, j: (row(b, j), 0)),
        scratch_shapes=[pltpu.VMEM((RET_HEADS, RET_QK, RET_V), F32)],
        compiler_params=_cparams(("parallel", "arbitrary")),
        name="retention",
    )(proj, proj, proj, proj, dmat, qd, kd, cd)


def _ret_out_ffn_kernel(x_ref, o_ref, wo_ref, g_ref, wgu_ref, wd_ref, out_ref, act_ref, *, fc):
    x1 = x_ref[...] + _dot(o_ref[...], wo_ref[...])
    h = _rms(x1, g_ref[...]).astype(BF16)
    f = wd_ref.shape[0]
    for c in range(f // fc):
        gt = _dot(h, wgu_ref[:, c * fc:(c + 1) * fc])
        up = _dot(h, wgu_ref[:, f + c * fc:f + (c + 1) * fc])
        act_ref[:, c * fc:(c + 1) * fc] = (gt * jax.nn.sigmoid(gt) * up).astype(BF16)
    out_ref[...] = x1 + _dot(act_ref[...], wd_ref[...])


def _ret_out_ffn(x2d, o, w_o, gain, w_gu, w_down):
    t, d = x2d.shape
    tm = TOK_TILE
    f = w_down.shape[0]
    return pl.pallas_call(
        functools.partial(_ret_out_ffn_kernel, fc=256),
        out_shape=jax.ShapeDtypeStruct((t, d), F32),
        grid=(t // tm,),
        in_specs=[
            pl.BlockSpec((tm, d), lambda i: (i, 0)),
            pl.BlockSpec((tm, o.shape[1]), lambda i: (i, 0)),
            _const_spec(w_o.shape),
            _const_spec((1, d)),
            _const_spec(w_gu.shape),
            _const_spec(w_down.shape),
        ],
        out_specs=pl.BlockSpec((tm, d), lambda i: (i, 0)),
        scratch_shapes=[pltpu.VMEM((tm, f), BF16)],
        compiler_params=_cparams(("parallel",)),
        name="ret_out_ffn",
    )(x2d, o, w_o, gain, w_gu, w_down)


def _rope16(x, ctab, s1tab, s2tab):
    half = ROPE_DIM // 2
    return (x * ctab + pltpu.roll(x, LANES - half, 1) * s1tab + pltpu.roll(x, half, 1) * s2tab)


def _qkv_kernel(x_ref, gq_ref, gkv_ref, wq_ref, wk_ref, wvt_ref, c_ref, s1_ref, s2_ref, q_ref, k_ref, vt_ref):
    x = x_ref[...]
    xn = x * lax.rsqrt(jnp.mean(x * x, axis=-1, keepdims=True) + EPS)
    hq = (xn * gq_ref[...]).astype(BF16)
    hkv = (xn * gkv_ref[...]).astype(BF16)
    ctab, s1tab, s2tab = c_ref[...], s1_ref[...], s2_ref[...]
    d = q_ref.shape[1]
    q_scale = DIFF_HD ** -0.5 * math.log2(math.e)
    for c in range(d // LANES):
        cols = slice(c * LANES, (c + 1) * LANES)
        q_ref[:, cols] = (_rope16(_dot(hq, wq_ref[:, cols]), ctab, s1tab, s2tab) * q_scale).astype(BF16)
        k_ref[:, cols] = _rope16(_dot(hkv, wk_ref[:, cols]), ctab, s1tab, s2tab).astype(BF16)
    vt_ref[...] = lax.dot_general(wvt_ref[...], hkv, (((1,), (1,)), ((), ())),
                                  preferred_element_type=F32).astype(BF16)


def _qkv_proj(x2d, gq, gkv, w_q, w_k, w_vt, ctab, s1tab, s2tab, batch, seq):
    t, d = x2d.shape
    tm = TOK_TILE
    n_pos = seq // tm
    tok = pl.BlockSpec((tm, d), lambda i: (i, 0))
    tab = pl.BlockSpec((tm, LANES), lambda i: (i % n_pos, 0))
    return pl.pallas_call(
        _qkv_kernel,
        out_shape=[jax.ShapeDtypeStruct((t, d), BF16), jax.ShapeDtypeStruct((t, d), BF16),
                   jax.ShapeDtypeStruct((batch * d, seq), BF16)],
        grid=(t // tm,),
        in_specs=[tok, _const_spec((1, d)), _const_spec((1, d)), _const_spec(w_q.shape),
                  _const_spec(w_k.shape), _const_spec(w_vt.shape), tab, tab, tab],
        out_specs=[tok, tok, pl.BlockSpec((d, tm), lambda i: (i // n_pos, i % n_pos))],
        compiler_params=_cparams(("parallel",)),
        name="qkv_proj",
    )(x2d, gq, gkv, w_q, w_k, w_vt, ctab, s1tab, s2tab)


def _attn_kernel(lam_ref, q_ref, k_ref, vt_ref, sub_ref, o_ref, vta_ref, acc_ref, *, lambda_init):
    i = pl.program_id(2)
    tq = q_ref.shape[0]
    dv = vt_ref.shape[0]
    n_kv, _, tk = vta_ref.shape

    @pl.when(i == 0)
    def _():
        for j in range(n_kv):
            vta_ref[j, 0:dv, :] = vt_ref[:, j * tk:(j + 1) * tk]
            vta_ref[j, dv:, :] = jnp.ones((ATT_ONES, tk), BF16)

    q = q_ref[...]
    lane = lax.broadcasted_iota(jnp.int32, q.shape, 1)
    zero = jnp.zeros_like(q)
    qc = (jnp.where(lane < DIFF_HD, q, zero), jnp.where(lane >= DIFF_HD, q, zero))
    n_full = (i * tq) // tk

    def scores(j, masked):
        kt = k_ref[pl.ds(pl.multiple_of(j * tk, tk), tk), :]
        out = []
        for c in range(2):
            s = lax.dot_general(kt, qc[c], (((1,), (1,)), ((), ())), preferred_element_type=F32)
            if masked:
                key = j * tk + lax.broadcasted_iota(jnp.int32, (tk, tq), 0)
                qry = i * tq + lax.broadcasted_iota(jnp.int32, (tk, tq), 1)
                s = jnp.where(key // CHUNK <= qry // CHUNK, s, -1e30)
            out.append(s)
        return out

    def tile_max(s):
        return jnp.max(s.reshape(tk // SUBLANES, SUBLANES, tq), axis=0)

    def max_step(j, m8, masked):
        return tuple(jnp.maximum(mc, tile_max(s)) for mc, s in zip(m8, scores(j, masked)))

    neg = jnp.full((SUBLANES, tq), -jnp.inf, F32)
    m8 = lax.fori_loop(0, n_full, lambda j, m8: max_step(j, m8, False), (neg, neg))
    m = tuple(jnp.max(mc, axis=0, keepdims=True) for mc in max_step(n_full, m8, True))

    acc_ref[...] = jnp.zeros_like(acc_ref)

    def pv_step(j, masked):
        vta = vta_ref[j]
        for c, s in enumerate(scores(j, masked)):
            acc_ref[c] += _dot(vta, jnp.exp2(s - m[c]).astype(BF16))

    def pv_body(j, carry):
        pv_step(j, False)
        return carry

    lax.fori_loop(0, n_full, pv_body, 0)
    pv_step(n_full, True)

    lam_v = lam_ref[...]
    lam = (jnp.exp(jnp.sum(lam_v[0:1] * lam_v[1:2], axis=-1, keepdims=True))
           - jnp.exp(jnp.sum(lam_v[2:3] * lam_v[3:4], axis=-1, keepdims=True)) + lambda_init)
    a0 = acc_ref[0]
    a1 = acc_ref[1]
    ot = a0[0:dv] / a0[dv:dv + 1] - lam * (a1[0:dv] / a1[dv:dv + 1])
    o = _rms(ot.T, sub_ref[...]) * (1.0 - lambda_init)
    o_ref[...] = o.astype(BF16)


def _diff_attention(q, k, vt, lam_vecs, subln, batch, seq, lambda_init):
    t, d = q.shape
    tq = ATT_TQ
    nq = seq // tq
    dv = 2 * DIFF_HD
    return pl.pallas_call(
        functools.partial(_attn_kernel, lambda_init=lambda_init),
        out_shape=jax.ShapeDtypeStruct((t, d), BF16),
        grid=(batch, DIFF_HEADS, nq),
        in_specs=[
            _const_spec(lam_vecs.shape),
            pl.BlockSpec((tq, dv), lambda b, h, i: (b * nq + i, h)),
            pl.BlockSpec((seq, dv), lambda b, h, i: (b, h)),
            pl.BlockSpec((dv, seq), lambda b, h, i: (b * DIFF_HEADS + h, 0)),
            _const_spec((1, dv)),
        ],
        out_specs=pl.BlockSpec((tq, dv), lambda b, h, i: (b * nq + i, h)),
        scratch_shapes=[pltpu.VMEM((seq // ATT_TK, dv + ATT_ONES, ATT_TK), BF16),
                        pltpu.VMEM((2, dv + ATT_ONES, tq), F32)],
        compiler_params=_cparams(("parallel", "parallel", "arbitrary")),
        name="diff_attn",
    )(lam_vecs, q, k, vt, subln)


def _attn_out_kernel(x_ref, o_ref, wo_ref, g_ref, r_ref, x3_ref, h3_ref, route_ref, cnt_ref, carry_ref):
    @pl.when(pl.program_id(0) == 0)
    def _():
        carry_ref[...] = jnp.zeros_like(carry_ref)

    tm = x_ref.shape[0]
    x3 = x_ref[...] + _dot(o_ref[...], wo_ref[...])
    x3_ref[...] = x3
    h3 = _rms(x3, g_ref[...])
    for s in range(h3.shape[1] // LANES):
        h3_ref[pl.ds(s, tm, stride=SUBLANES), :] = h3[:, s * LANES:(s + 1) * LANES]

    logits = jnp.dot(h3, r_ref[...], preferred_element_type=F32, precision=lax.Precision.HIGHEST)
    lane = lax.broadcasted_iota(jnp.int32, logits.shape, 1).astype(F32)
    lg = jnp.where(lane < N_EXPERTS, logits, -jnp.inf)
    v1 = jnp.max(lg, axis=-1, keepdims=True)
    i1 = jnp.min(jnp.where(lg == v1, lane, float(LANES)), axis=-1, keepdims=True)
    lg2 = jnp.where(lane == i1, -jnp.inf, lg)
    v2 = jnp.max(lg2, axis=-1, keepdims=True)
    i2 = jnp.min(jnp.where(lg2 == v2, lane, float(LANES)), axis=-1, keepdims=True)
    e = jnp.exp(v2 - v1)
    w1 = 1.0 / (1.0 + e)
    w2 = e / (1.0 + e)
    oh1 = lane == i1
    oh2 = lane == i2
    assign = jnp.where(oh1 | oh2, 1.0, 0.0)
    rr = lax.broadcasted_iota(jnp.int32, (tm, tm), 0)
    cc = lax.broadcasted_iota(jnp.int32, (tm, tm), 1)
    lower = jnp.where(cc < rr, 1.0, 0.0).astype(BF16)
    excl = _dot(lower, assign.astype(BF16)) + carry_ref[0:1, :]
    r1 = jnp.sum(jnp.where(oh1, excl, 0.0), axis=-1, keepdims=True)
    r2 = jnp.sum(jnp.where(oh2, excl, 0.0), axis=-1, keepdims=True)
    route = jnp.zeros_like(logits)
    for col, val in enumerate((i1, i2, r1, r2, w1, w2)):
        route = jnp.where(lane == float(col), val, route)
    route_ref[...] = route
    total = carry_ref[0:1, :] + jnp.sum(assign, axis=0, keepdims=True)
    carry_ref[...] = jnp.broadcast_to(total, carry_ref.shape)
    cnt_ref[...] = jnp.broadcast_to(total, cnt_ref.shape)


def _attn_out(x2d, o, w_o, gain, router_pad):
    t, d = x2d.shape
    tm = TOK_TILE
    n_sub = d // LANES
    return pl.pallas_call(
        _attn_out_kernel,
        out_shape=[jax.ShapeDtypeStruct((t, d), F32),
                   jax.ShapeDtypeStruct((t * n_sub, LANES), F32),
                   jax.ShapeDtypeStruct((t, LANES), F32),
                   jax.ShapeDtypeStruct((SUBLANES, LANES), F32)],
        grid=(t // tm,),
        in_specs=[pl.BlockSpec((tm, d), lambda i: (i, 0)),
                  pl.BlockSpec((tm, d), lambda i: (i, 0)),
                  _const_spec(w_o.shape), _const_spec((1, d)), _const_spec(router_pad.shape)],
        out_specs=[pl.BlockSpec((tm, d), lambda i: (i, 0)),
                   pl.BlockSpec((tm * n_sub, LANES), lambda i: (i, 0)),
                   pl.BlockSpec((tm, LANES), lambda i: (i, 0)),
                   _const_spec((SUBLANES, LANES))],
        scratch_shapes=[pltpu.VMEM((SUBLANES, LANES), F32)],
        compiler_params=_cparams(("arbitrary",)),
        name="attn_out_route",
    )(x2d, o, w_o, gain, router_pad)


def _dispatch_kernel(dest_ref, zpos_ref, zflag_ref, h_ref, dst_ref, zbuf_ref, zsem, sem):
    i = pl.program_id(0)
    tm = h_ref.shape[0]

    @pl.when(i == 0)
    def _():
        zbuf_ref[...] = jnp.zeros_like(zbuf_ref)

        def zcopy(e):
            return pltpu.make_async_copy(zbuf_ref, dst_ref.at[pl.ds(zpos_ref[e], MOE_TILE)], zsem.at[e])

        for e in range(2 * N_EXPERTS):
            @pl.when(zflag_ref[e] == 1)
            def _():
                zcopy(e).start()
        for e in range(2 * N_EXPERTS):
            @pl.when(zflag_ref[e] == 1)
            def _():
                zcopy(e).wait()

    def body(r, carry):
        base = 2 * (i * tm + r)
        for k in range(2):
            pltpu.make_async_copy(h_ref.at[r], dst_ref.at[dest_ref[base + k]], sem).start()
        return carry

    lax.fori_loop(0, tm, body, 0, unroll=8)
    for k in range(2):
        pltpu.make_async_copy(h_ref, dst_ref.at[pl.ds(0, tm)], sem).wait()


def _dispatch(h3_rows, dest, zpos, zflag, n_dst):
    t = h3_rows.shape[0]
    tm = TOK_TILE
    tail = h3_rows.shape[1:]
    return pl.pallas_call(
        _dispatch_kernel,
        out_shape=jax.ShapeDtypeStruct((n_dst,) + tail, F32),
        grid_spec=pltpu.PrefetchScalarGridSpec(
            num_scalar_prefetch=3, grid=(t // tm,),
            in_specs=[pl.BlockSpec((tm,) + tail, lambda i, *_: (i, 0, 0))],
            out_specs=pl.BlockSpec(memory_space=pl.ANY),
            scratch_shapes=[pltpu.VMEM((MOE_TILE,) + tail, F32),
                            pltpu.SemaphoreType.DMA((2 * N_EXPERTS,)), pltpu.SemaphoreType.DMA(())]),
        compiler_params=_cparams(("arbitrary",)),
        name="moe_dispatch",
    )(dest, zpos, zflag, h3_rows)


def _moe_kernel(te_ref, tv_ref, ts_ref, xs_ref, wg_ref, wu_ref, wd_ref, ys_ref, h_ref, acc_ref):
    i = pl.program_id(0)
    c = pl.program_id(1)
    last = pl.num_programs(1) - 1
    valid = tv_ref[i] == 1
    tile = h_ref.shape[0]
    n_sub = h_ref.shape[1] // LANES

    @pl.when(valid & (c == 0))
    def _():
        for s in range(n_sub):
            h_ref[:, s * LANES:(s + 1) * LANES] = xs_ref[pl.ds(s, tile, stride=SUBLANES), :].astype(BF16)
        acc_ref[...] = jnp.zeros_like(acc_ref)

    @pl.when(valid)
    def _():
        h = h_ref[...]
        gt = _dot(h, wg_ref[...])
        up = _dot(h, wu_ref[...])
        act = (gt * jax.nn.sigmoid(gt) * up).astype(BF16)
        acc_ref[...] += _dot(act, wd_ref[...])

    @pl.when(valid & (c == last))
    def _():
        for s in range(n_sub):
            ys_ref[pl.ds(s, tile, stride=SUBLANES), :] = acc_ref[:, s * LANES:(s + 1) * LANES]

    @pl.when(jnp.logical_not(valid) & (c == last))
    def _():
        ys_ref[...] = jnp.zeros_like(ys_ref)


def _moe_experts(xs2d, w_gu, w_down, tile_expert, tile_valid, tile_src, d):
    n_sub = d // LANES
    n_tiles = xs2d.shape[0] // (MOE_TILE * n_sub)
    f = w_down.shape[1]
    nfc = f // MOE_FC
    blk = MOE_TILE * n_sub

    def ceff(i, c, tv):
        return jnp.where(tv[i] == 1, c, nfc - 1)

    return pl.pallas_call(
        _moe_kernel,
        out_shape=jax.ShapeDtypeStruct(xs2d.shape, F32),
        grid_spec=pltpu.PrefetchScalarGridSpec(
            num_scalar_prefetch=3, grid=(n_tiles, nfc),
            in_specs=[
                pl.BlockSpec((blk, LANES), lambda i, c, te, tv, ts: (ts[i], 0)),
                pl.BlockSpec((None, d, MOE_FC), lambda i, c, te, tv, ts: (te[i], 0, ceff(i, c, tv))),
                pl.BlockSpec((None, d, MOE_FC), lambda i, c, te, tv, ts: (te[i], 0, nfc + ceff(i, c, tv))),
                pl.BlockSpec((None, MOE_FC, d), lambda i, c, te, tv, ts: (te[i], ceff(i, c, tv), 0)),
            ],
            out_specs=pl.BlockSpec((blk, LANES), lambda i, c, te, tv, ts: (i, 0)),
            scratch_shapes=[pltpu.VMEM((MOE_TILE, d), BF16), pltpu.VMEM((MOE_TILE, d), F32)]),
        compiler_params=_cparams(("arbitrary", "arbitrary")),
        name="moe_experts",
    )(tile_expert, tile_valid, tile_src, xs2d, w_gu, w_gu, w_down)


def _combine_kernel(dest_ref, x_ref, ys_ref, route_ref, g_ref, o_ref, ybuf_ref, sem):
    i = pl.program_id(0)
    n = pl.num_programs(0)
    tm, d = x_ref.shape
    n_sub = d // LANES

    def gather(tile, slot):
        def body(r, carry):
            base = 2 * (tile * tm + r)
            for k in range(2):
                src = ys_ref.at[pl.ds(pl.multiple_of(dest_ref[base + k] * n_sub, n_sub), n_sub)]
                dst = ybuf_ref.at[slot, pl.ds(pl.multiple_of((2 * r + k) * n_sub, n_sub), n_sub)]
                pltpu.make_async_copy(src, dst, sem.at[slot]).start()
            return carry

        lax.fori_loop(0, tm, body, 0, unroll=8)

    @pl.when(i == 0)
    def _():
        gather(0, 0)

    @pl.when(i + 1 < n)
    def _():
        gather(i + 1, (i + 1) % 2)

    slot = i % 2
    pltpu.make_async_copy(ys_ref.at[pl.ds(0, 2 * tm * n_sub)], ybuf_ref.at[slot], sem.at[slot]).wait()

    y_ref = ybuf_ref.at[slot]
    route = route_ref[...]
    w1 = route[:, 4:5]
    w2 = route[:, 5:6]
    for s in range(n_sub):
        cols = slice(s * LANES, (s + 1) * LANES)
        y1 = y_ref[pl.ds(s, tm, stride=2 * n_sub), :]
        y2 = y_ref[pl.ds(n_sub + s, tm, stride=2 * n_sub), :]
        o_ref[:, cols] = x_ref[:, cols] + w1 * y1 + w2 * y2
    o_ref[...] = _rms(o_ref[...], g_ref[...])


def _combine(x2d, ys2d, dest, route, gain):
    t, d = x2d.shape
    tm = TOK_TILE
    n_sub = d // LANES
    return pl.pallas_call(
        _combine_kernel,
        out_shape=jax.ShapeDtypeStruct((t, d), F32),
        grid_spec=pltpu.PrefetchScalarGridSpec(
            num_scalar_prefetch=1, grid=(t // tm,),
            in_specs=[pl.BlockSpec((tm, d), lambda i, *_: (i, 0)),
                      pl.BlockSpec(memory_space=pl.ANY),
                      pl.BlockSpec((tm, LANES), lambda i, *_: (i, 0)),
                      pl.BlockSpec((1, d), lambda i, *_: (0, 0), pipeline_mode=pl.Buffered(1))],
            out_specs=pl.BlockSpec((tm, d), lambda i, *_: (i, 0)),
            scratch_shapes=[pltpu.VMEM((2, 2 * tm * n_sub, LANES), F32), pltpu.SemaphoreType.DMA((2,))]),
        compiler_params=_cparams(("arbitrary",)),
        name="moe_combine",
    )(dest, x2d, ys2d, route, gain)


def _rope_tables(seq, rot_dim, theta):
    inv = 1.0 / (theta ** (jnp.arange(0, rot_dim, 2, dtype=F32) / rot_dim))
    ang = jnp.arange(seq, dtype=F32)[:, None] * inv[None, :]
    return jnp.cos(ang), jnp.sin(ang)


def _retention_tables():
    c = RET_CHUNK
    log_gamma = jnp.log(1.0 - 2.0 ** (-5.0 - jnp.arange(RET_HEADS, dtype=F32)))
    idx = jnp.arange(c, dtype=F32)
    rel = idx[:, None] - idx[None, :]
    dmat = jnp.where(rel[None] >= 0, jnp.exp(jnp.maximum(rel, 0.0)[None] * log_gamma[:, None, None]), 0.0)
    qd = jnp.exp((idx + 1.0)[None, :] * log_gamma[:, None])[:, :, None]
    kd = jnp.exp((c - 1.0 - idx)[None, :] * log_gamma[:, None])[:, :, None]
    cd = jnp.exp(c * log_gamma)[:, None, None]
    return dmat, qd, kd, cd


def _attn_rope_tables(seq):
    cos, sin = _rope_tables(seq, ROPE_DIM, ROPE_THETA)
    half = ROPE_DIM // 2
    pad = DIFF_HD - ROPE_DIM
    ones = jnp.ones((seq, pad), F32)
    zeros = jnp.zeros((seq, pad), F32)
    zh = jnp.zeros((seq, half), F32)
    ctab = jnp.concatenate([cos, cos, ones], axis=1)
    s1tab = jnp.concatenate([-sin, zh, zeros], axis=1)
    s2tab = jnp.concatenate([zh, sin, zeros], axis=1)
    rep = LANES // DIFF_HD
    return tuple(jnp.tile(tb, (1, rep)) for tb in (ctab, s1tab, s2tab))


def kernel(x, ln_mix, ln_ffn, ret_w_in, ret_w_o, kv_norm, w_kv, diff_w_q, lam_q1, lam_k1, lam_q2, lam_k2,
           diff_subln, diff_w_o, ffn_w_gu, ffn_w_down, moe_router, moe_w_gu, moe_w_down, final_norm):
    batch, seq, d = x.shape
    t = batch * seq
    assert ln_mix.shape[0] == 2 and ret_w_in.shape[0] == 1 and diff_w_q.shape[0] == 1
    assert seq % TOK_TILE == 0 and TOK_TILE % RET_CHUNK == 0 and ATT_TQ % CHUNK == 0
    assert seq % ATT_TK == 0 and ATT_TK % ATT_TQ == 0
    x2d = x.reshape(t, d)
    row = lambda g: g.reshape(1, -1)

    cos_r, sin_r = _rope_tables(seq, RET_QK, RET_THETA)
    proj = _ret_in_proj(x2d, row(ln_mix[0]), ret_w_in[0].astype(BF16), cos_r, sin_r, seq)
    ret_o = _retention(proj, *_retention_tables(), batch, seq)
    x2 = _ret_out_ffn(x2d, ret_o, ret_w_o[0].astype(BF16), row(ln_ffn[0]),
                      ffn_w_gu[0].astype(BF16), ffn_w_down[0].astype(BF16))

    lambda_init = 0.8 - 0.6 * math.exp(-0.3 * 1)
    q, k, vt = _qkv_proj(x2, row(ln_mix[1]), row(kv_norm), diff_w_q[0].astype(BF16),
                         w_kv[:, :d].astype(BF16), w_kv[:, d:].T.astype(BF16),
                         *_attn_rope_tables(seq), batch, seq)
    lam_vecs = jnp.stack([lam_q1[0], lam_k1[0], lam_q2[0], lam_k2[0]]).astype(F32)
    att = _diff_attention(q, k, vt, lam_vecs, row(diff_subln[0]), batch, seq, lambda_init)
    router_pad = jnp.pad(moe_router[0], ((0, 0), (0, LANES - N_EXPERTS)))
    x3, h3_rows, route, cnt = _attn_out(x2, att, diff_w_o[0].astype(BF16), row(ln_ffn[1]), router_pad)

    n_sub = d // LANES
    n_rows = 2 * t + N_EXPERTS * MOE_TILE
    n_tiles = n_rows // MOE_TILE
    expert = route[:, 0:2].astype(jnp.int32)
    rank = route[:, 2:4].astype(jnp.int32)
    counts = cnt[0, :N_EXPERTS].astype(jnp.int32)
    padded = (counts + MOE_TILE - 1) // MOE_TILE * MOE_TILE
    seg_end = jnp.cumsum(padded)
    seg_start = seg_end - padded
    dest = (jnp.sum(jnp.where(expert[..., None] == jnp.arange(N_EXPERTS), seg_start, 0), axis=-1)
            + rank).reshape(-1)
    tile_row = jnp.arange(n_tiles, dtype=jnp.int32) * MOE_TILE
    tile_valid = (tile_row < seg_end[-1]).astype(jnp.int32)
    n_valid = seg_end[-1] // MOE_TILE
    tile_src = jnp.minimum(jnp.arange(n_tiles, dtype=jnp.int32), jnp.maximum(n_valid - 1, 0))
    tile_expert = jnp.minimum(jnp.sum(tile_src[:, None] * MOE_TILE >= seg_end[None, :], axis=1),
                              N_EXPERTS - 1).astype(jnp.int32)
    zflag = jnp.concatenate([(padded > 0).astype(jnp.int32), 1 - tile_valid[-N_EXPERTS:]])
    zpos = jnp.concatenate([jnp.maximum(seg_end - MOE_TILE, 0), tile_row[-N_EXPERTS:]]).astype(jnp.int32)

    xs = _dispatch(h3_rows.reshape(t, n_sub, LANES), dest, zpos, zflag, n_rows)
    ys = _moe_experts(xs.reshape(n_rows * n_sub, LANES), moe_w_gu[0].astype(BF16),
                      moe_w_down[0].astype(BF16), tile_expert, tile_valid, tile_src, d)
    out = _combine(x3, ys, dest, route, row(final_norm))
    return out.reshape(batch, seq, d)
```

```python
import functools
import math

import jax
import jax.numpy as jnp
from jax import lax
from jax.experimental import pallas as pl
from jax.experimental.pallas import tpu as pltpu

F32 = jnp.float32
BF16 = jnp.bfloat16

EPS = 1e-6
CHUNK = 64
RET_QK = 256
RET_V = 512
RET_HEADS = 4
RET_THETA = 10000.0
DIFF_HD = 64
DIFF_HEADS = 8
ROPE_THETA = 500000.0
ROPE_DIM = 16
N_EXPERTS = 8

LANES = 128
SUBLANES = 8
VMEM_LIMIT = 56 * 1024 * 1024

RET_CHUNK = 256
TOK_TILE = 512
ATT_TQ = 256
ATT_TK = 512
ATT_HEADS = 4
ATT_ONES = 16
MOE_TILE = 512
MOE_FC = 1792


def _cparams(sem, vmem=VMEM_LIMIT):
    return pltpu.CompilerParams(dimension_semantics=sem, vmem_limit_bytes=vmem)


def _const_spec(shape):
    nd = len(shape)
    return pl.BlockSpec(shape, lambda *_: (0,) * nd, pipeline_mode=pl.Buffered(1))


def _rms(x, g):
    return x * lax.rsqrt(jnp.mean(x * x, axis=-1, keepdims=True) + EPS) * g


def _dot(a, b):
    return jnp.dot(a, b, preferred_element_type=F32)


def _ret_in_kernel(x_ref, g_ref, w_ref, cos_ref, sin_ref, o_ref):
    h = _rms(x_ref[...], g_ref[...]).astype(BF16)
    cos = cos_ref[...]
    sin = sin_ref[...]
    d_qk = RET_HEADS * RET_QK
    half = RET_QK // 2
    for c in range(2 * RET_HEADS):
        c0 = c * RET_QK
        acc = _dot(h, w_ref[:, c0:c0 + RET_QK])
        x1 = acc[:, :half]
        x2 = acc[:, half:]
        scale = 1.0 if c < RET_HEADS else RET_QK ** -0.5
        o_ref[:, c0:c0 + half] = ((x1 * cos - x2 * sin) * scale).astype(BF16)
        o_ref[:, c0 + half:c0 + RET_QK] = ((x2 * cos + x1 * sin) * scale).astype(BF16)
    n_rest = (w_ref.shape[1] - 2 * d_qk) // RET_V
    for c in range(n_rest):
        c0 = 2 * d_qk + c * RET_V
        o_ref[:, c0:c0 + RET_V] = _dot(h, w_ref[:, c0:c0 + RET_V]).astype(BF16)


def _ret_in_proj(x2d, gain, w, cos, sin, seq):
    t, d = x2d.shape
    n = w.shape[1]
    tm = TOK_TILE
    n_pos = seq // tm
    return pl.pallas_call(
        _ret_in_kernel,
        out_shape=jax.ShapeDtypeStruct((t, n), BF16),
        grid=(t // tm,),
        in_specs=[
            pl.BlockSpec((tm, d), lambda i: (i, 0)),
            _const_spec((1, d)),
            _const_spec((d, n)),
            pl.BlockSpec((tm, RET_QK // 2), lambda i: (i % n_pos, 0)),
            pl.BlockSpec((tm, RET_QK // 2), lambda i: (i % n_pos, 0)),
        ],
        out_specs=pl.BlockSpec((tm, n), lambda i: (i, 0)),
        compiler_params=_cparams(("parallel",)),
        name="ret_in_proj",
    )(x2d, gain, w, cos, sin)


def _retention_kernel(q_ref, k_ref, v_ref, g_ref, dmat_ref, qd_ref, kd_ref, cd_ref, o_ref, state_ref):
    @pl.when(pl.program_id(1) == 0)
    def _():
        state_ref[...] = jnp.zeros_like(state_ref)

    n_chunks = q_ref.shape[0] // RET_CHUNK
    for h in range(RET_HEADS):
        for c in range(n_chunks):
            rows = slice(c * RET_CHUNK, (c + 1) * RET_CHUNK)
            q = q_ref[rows, h * RET_QK:(h + 1) * RET_QK]
            k = k_ref[rows, h * RET_QK:(h + 1) * RET_QK]
            v = v_ref[rows, h * RET_V:(h + 1) * RET_V]
            state = state_ref[h]
            s = lax.dot_general(q, k, (((1,), (1,)), ((), ())), preferred_element_type=F32)
            s = s * dmat_ref[h]
            o = _dot(s.astype(BF16), v)
            qs = (q.astype(F32) * qd_ref[h]).astype(BF16)
            o = o + _dot(qs, state.astype(BF16))
            ks = (k.astype(F32) * kd_ref[h]).astype(BF16)
            state_ref[h] = state * cd_ref[h] + lax.dot_general(
                ks, v, (((0,), (0,)), ((), ())), preferred_element_type=F32)
            mu = jnp.mean(o, axis=-1, keepdims=True)
            oc = o - mu
            var = jnp.mean(oc * oc, axis=-1, keepdims=True)
            on = oc * lax.rsqrt(var + EPS)
            gate = g_ref[rows, h * RET_V:(h + 1) * RET_V].astype(F32)
            o_ref[rows, h * RET_V:(h + 1) * RET_V] = (on * (gate * jax.nn.sigmoid(gate))).astype(BF16)


def _retention(proj, dmat, qd, kd, cd, batch, seq):
    t = proj.shape[0]
    tb = TOK_TILE
    nj = seq // tb
    d_qk = RET_HEADS * RET_QK
    d_v = RET_HEADS * RET_V
    row = lambda b, j: b * nj + j
    return pl.pallas_call(
        _retention_kernel,
        out_shape=jax.ShapeDtypeStruct((t, d_v), BF16),
        grid=(batch, nj),
        in_specs=[
            pl.BlockSpec((tb, d_qk), lambda b, j: (row(b, j), 0)),
            pl.BlockSpec((tb, d_qk), lambda b, j: (row(b, j), 1)),
            pl.BlockSpec((tb, d_v), lambda b, j: (row(b, j), 1)),
            pl.BlockSpec((tb, d_v), lambda b, j: (row(b, j), 2)),
            _const_spec(dmat.shape),
            _const_spec(qd.shape),
            _const_spec(kd.shape),
            _const_spec(cd.shape),
        ],
        out_specs=pl.BlockSpec((tb, d_v), lambda b, j: (row(b, j), 0)),
        scratch_shapes=[pltpu.VMEM((RET_HEADS, RET_QK, RET_V), F32)],
        compiler_params=_cparams(("parallel", "arbitrary")),
        name="retention",
    )(proj, proj, proj, proj, dmat, qd, kd, cd)


def _ret_out_ffn_kernel(x_ref, o_ref, wo_ref, g_ref, wgu_ref, wd_ref, out_ref, act_ref, *, fc):
    x1 = x_ref[...] + _dot(o_ref[...], wo_ref[...])
    h = _rms(x1, g_ref[...]).astype(BF16)
    f = wd_ref.shape[0]
    for c in range(f // fc):
        gt = _dot(h, wgu_ref[:, c * fc:(c + 1) * fc])
        up = _dot(h, wgu_ref[:, f + c * fc:f + (c + 1) * fc])
        act_ref[:, c * fc:(c + 1) * fc] = (gt * jax.nn.sigmoid(gt) * up).astype(BF16)
    out_ref[...] = x1 + _dot(act_ref[...], wd_ref[...])


def _ret_out_ffn(x2d, o, w_o, gain, w_gu, w_down):
    t, d = x2d.shape
    tm = TOK_TILE
    f = w_down.shape[0]
    return pl.pallas_call(
        functools.partial(_ret_out_ffn_kernel, fc=256),
        out_shape=jax.ShapeDtypeStruct((t, d), F32),
        grid=(t // tm,),
        in_specs=[
            pl.BlockSpec((tm, d), lambda i: (i, 0)),
            pl.BlockSpec((tm, o.shape[1]), lambda i: (i, 0)),
            _const_spec(w_o.shape),
            _const_spec((1, d)),
            _const_spec(w_gu.shape),
            _const_spec(w_down.shape),
        ],
        out_specs=pl.BlockSpec((tm, d), lambda i: (i, 0)),
        scratch_shapes=[pltpu.VMEM((tm, f), BF16)],
        compiler_params=_cparams(("parallel",)),
        name="ret_out_ffn",
    )(x2d, o, w_o, gain, w_gu, w_down)


def _rope16(x, ctab, s1tab, s2tab):
    half = ROPE_DIM // 2
    return (x * ctab + pltpu.roll(x, LANES - half, 1) * s1tab + pltpu.roll(x, half, 1) * s2tab)


def _qkv_kernel(x_ref, gq_ref, gkv_ref, wq_ref, wk_ref, wvt_ref, c_ref, s1_ref, s2_ref, q_ref, k_ref, vt_ref):
    x = x_ref[...]
    xn = x * lax.rsqrt(jnp.mean(x * x, axis=-1, keepdims=True) + EPS)
    hq = (xn * gq_ref[...]).astype(BF16)
    hkv = (xn * gkv_ref[...]).astype(BF16)
    ctab, s1tab, s2tab = c_ref[...], s1_ref[...], s2_ref[...]
    d = q_ref.shape[1]
    q_scale = DIFF_HD ** -0.5 * math.log2(math.e)
    for c in range(d // LANES):
        cols = slice(c * LANES, (c + 1) * LANES)
        q_ref[:, cols] = (_rope16(_dot(hq, wq_ref[:, cols]), ctab, s1tab, s2tab) * q_scale).astype(BF16)
        k_ref[:, cols] = _rope16(_dot(hkv, wk_ref[:, cols]), ctab, s1tab, s2tab).astype(BF16)
    vt_ref[...] = lax.dot_general(wvt_ref[...], hkv, (((1,), (1,)), ((), ())),
                                  preferred_element_type=F32).astype(BF16)


def _qkv_proj(x2d, gq, gkv, w_q, w_k, w_vt, ctab, s1tab, s2tab, batch, seq):
    t, d = x2d.shape
    tm = TOK_TILE
    n_pos = seq // tm
    tok = pl.BlockSpec((tm, d), lambda i: (i, 0))
    tab = pl.BlockSpec((tm, LANES), lambda i: (i % n_pos, 0))
    return pl.pallas_call(
        _qkv_kernel,
        out_shape=[jax.ShapeDtypeStruct((t, d), BF16), jax.ShapeDtypeStruct((t, d), BF16),
                   jax.ShapeDtypeStruct((batch * d, seq), BF16)],
        grid=(t // tm,),
        in_specs=[tok, _const_spec((1, d)), _const_spec((1, d)), _const_spec(w_q.shape),
                  _const_spec(w_k.shape), _const_spec(w_vt.shape), tab, tab, tab],
        out_specs=[tok, tok, pl.BlockSpec((d, tm), lambda i: (i // n_pos, i % n_pos))],
        compiler_params=_cparams(("parallel",)),
        name="qkv_proj",
    )(x2d, gq, gkv, w_q, w_k, w_vt, ctab, s1tab, s2tab)


def _attn_kernel(lam_ref, bias_ref, q_ref, k_ref, vt_ref, sub_ref, o_ref, vta_ref, s_ref, acc_ref, *,
                 lambda_init):
    i = pl.program_id(2)
    tq = q_ref.shape[0]
    dv = 2 * DIFF_HD
    n_heads, n_kv, _, tk = vta_ref.shape
    chains = [(h, c) for h in range(n_heads) for c in range(2)]

    @pl.when(i == 0)
    def _():
        for h in range(n_heads):
            for j in range(n_kv):
                vta_ref[h, j, 0:dv, :] = vt_ref[h * dv:(h + 1) * dv, j * tk:(j + 1) * tk]
                vta_ref[h, j, dv:, :] = jnp.ones((ATT_ONES, tk), BF16)

    lane = lax.broadcasted_iota(jnp.int32, (tq, dv), 1)
    qc = []
    for h in range(n_heads):
        q = q_ref[:, h * dv:(h + 1) * dv]
        zero = jnp.zeros_like(q)
        qc.append((jnp.where(lane < DIFF_HD, q, zero), jnp.where(lane >= DIFF_HD, q, zero)))
    n_full = (i * tq) // tk
    tail_bias = 1 + (i * tq - n_full * tk) // tq

    def tile_max(s):
        m8 = jnp.max(s.reshape(tk // SUBLANES, SUBLANES, tq), axis=0)
        return jnp.max(m8, axis=0, keepdims=True)

    def score_tile(j):
        bias = bias_ref[jnp.where(j == n_full, tail_bias, 0)]
        rows = pl.ds(pl.multiple_of(j * tk, tk), tk)
        maxima = []
        for x, (h, c) in enumerate(chains):
            kt = k_ref[rows, h * dv:(h + 1) * dv]
            s = lax.dot_general(kt, qc[h][c], (((1,), (1,)), ((), ())), preferred_element_type=F32) + bias
            s_ref[x] = s
            maxima.append(tile_max(s))
        return tuple(maxima)

    acc_ref[...] = jnp.zeros_like(acc_ref)
    m_init = tuple(jnp.full((1, tq), -jnp.inf, F32) for _ in chains)

    def body(j, carry):
        m_run, m_tile = carry
        m_new = tuple(jnp.maximum(a, b) for a, b in zip(m_run, m_tile))
        for x, (h, c) in enumerate(chains):
            alpha = jnp.exp2(m_run[x] - m_new[x])
            p = jnp.exp2(s_ref[x] - m_new[x]).astype(BF16)
            acc_ref[x] = alpha * acc_ref[x] + _dot(vta_ref[h, j], p)
        return m_new, score_tile(jnp.minimum(j + 1, n_full))

    lax.fori_loop(0, n_full + 1, body, (m_init, score_tile(0)))

    lam_v = lam_ref[...]
    lam = (jnp.exp(jnp.sum(lam_v[0:1] * lam_v[1:2], axis=-1, keepdims=True))
           - jnp.exp(jnp.sum(lam_v[2:3] * lam_v[3:4], axis=-1, keepdims=True)) + lambda_init)
    for h in range(n_heads):
        a0 = acc_ref[2 * h]
        a1 = acc_ref[2 * h + 1]
        ot = a0[0:dv] / a0[dv:dv + 1] - lam * (a1[0:dv] / a1[dv:dv + 1])
        o = _rms(ot.T, sub_ref[...]) * (1.0 - lambda_init)
        o_ref[:, h * dv:(h + 1) * dv] = o.astype(BF16)


def _attn_bias(tq, tk):
    key = jnp.arange(tk)[:, None] // CHUNK
    tiles = [jnp.zeros((tk, tq), F32)]
    for r in range(tk // tq):
        qry = (r * tq + jnp.arange(tq))[None, :] // CHUNK
        tiles.append(jnp.where(key <= qry, 0.0, -1e30).astype(F32))
    return jnp.stack(tiles)


def _diff_attention(q, k, vt, lam_vecs, subln, batch, seq, lambda_init):
    t, d = q.shape
    tq = ATT_TQ
    nq = seq // tq
    dv = 2 * DIFF_HD
    hp = ATT_HEADS
    n_groups = DIFF_HEADS // hp
    bias = _attn_bias(tq, ATT_TK)
    return pl.pallas_call(
        functools.partial(_attn_kernel, lambda_init=lambda_init),
        out_shape=jax.ShapeDtypeStruct((t, d), BF16),
        grid=(batch, n_groups, nq),
        in_specs=[
            _const_spec(lam_vecs.shape),
            _const_spec(bias.shape),
            pl.BlockSpec((tq, hp * dv), lambda b, g, i: (b * nq + i, g)),
            pl.BlockSpec((seq, hp * dv), lambda b, g, i: (b, g)),
            pl.BlockSpec((hp * dv, seq), lambda b, g, i: (b * n_groups + g, 0)),
            _const_spec((1, dv)),
        ],
        out_specs=pl.BlockSpec((tq, hp * dv), lambda b, g, i: (b * nq + i, g)),
        scratch_shapes=[pltpu.VMEM((hp, seq // ATT_TK, dv + ATT_ONES, ATT_TK), BF16),
                        pltpu.VMEM((2 * hp, ATT_TK, tq), F32),
                        pltpu.VMEM((2 * hp, dv + ATT_ONES, tq), F32)],
        compiler_params=_cparams(("parallel", "parallel", "arbitrary")),
        name="diff_attn",
    )(lam_vecs, bias, q, k, vt, subln)


def _attn_out_kernel(x_ref, o_ref, wo_ref, g_ref, r_ref, x3_ref, h3_ref, route_ref, cnt_ref, carry_ref):
    @pl.when(pl.program_id(0) == 0)
    def _():
        carry_ref[...] = jnp.zeros_like(carry_ref)

    tm = x_ref.shape[0]
    x3 = x_ref[...] + _dot(o_ref[...], wo_ref[...])
    x3_ref[...] = x3
    h3 = _rms(x3, g_ref[...])
    for s in range(h3.shape[1] // LANES):
        h3_ref[pl.ds(s, tm, stride=SUBLANES), :] = h3[:, s * LANES:(s + 1) * LANES]

    logits = jnp.dot(h3, r_ref[...], preferred_element_type=F32, precision=lax.Precision.HIGHEST)
    lane = lax.broadcasted_iota(jnp.int32, logits.shape, 1).astype(F32)
    lg = jnp.where(lane < N_EXPERTS, logits, -jnp.inf)
    v1 = jnp.max(lg, axis=-1, keepdims=True)
    i1 = jnp.min(jnp.where(lg == v1, lane, float(LANES)), axis=-1, keepdims=True)
    lg2 = jnp.where(lane == i1, -jnp.inf, lg)
    v2 = jnp.max(lg2, axis=-1, keepdims=True)
    i2 = jnp.min(jnp.where(lg2 == v2, lane, float(LANES)), axis=-1, keepdims=True)
    e = jnp.exp(v2 - v1)
    w1 = 1.0 / (1.0 + e)
    w2 = e / (1.0 + e)
    oh1 = lane == i1
    oh2 = lane == i2
    assign = jnp.where(oh1 | oh2, 1.0, 0.0)
    rr = lax.broadcasted_iota(jnp.int32, (tm, tm), 0)
    cc = lax.broadcasted_iota(jnp.int32, (tm, tm), 1)
    lower = jnp.where(cc < rr, 1.0, 0.0).astype(BF16)
    excl = _dot(lower, assign.astype(BF16)) + carry_ref[0:1, :]
    r1 = jnp.sum(jnp.where(oh1, excl, 0.0), axis=-1, keepdims=True)
    r2 = jnp.sum(jnp.where(oh2, excl, 0.0), axis=-1, keepdims=True)
    route = jnp.zeros_like(logits)
    for col, val in enumerate((i1, i2, r1, r2, w1, w2)):
        route = jnp.where(lane == float(col), val, route)
    route_ref[...] = route
    total = carry_ref[0:1, :] + jnp.sum(assign, axis=0, keepdims=True)
    carry_ref[...] = jnp.broadcast_to(total, carry_ref.shape)
    cnt_ref[...] = jnp.broadcast_to(total, cnt_ref.shape)


def _attn_out(x2d, o, w_o, gain, router_pad):
    t, d = x2d.shape
    tm = TOK_TILE
    n_sub = d // LANES
    return pl.pallas_call(
        _attn_out_kernel,
        out_shape=[jax.ShapeDtypeStruct((t, d), F32),
                   jax.ShapeDtypeStruct((t * n_sub, LANES), F32),
                   jax.ShapeDtypeStruct((t, LANES), F32),
                   jax.ShapeDtypeStruct((SUBLANES, LANES), F32)],
        grid=(t // tm,),
        in_specs=[pl.BlockSpec((tm, d), lambda i: (i, 0)),
                  pl.BlockSpec((tm, d), lambda i: (i, 0)),
                  _const_spec(w_o.shape), _const_spec((1, d)), _const_spec(router_pad.shape)],
        out_specs=[pl.BlockSpec((tm, d), lambda i: (i, 0)),
                   pl.BlockSpec((tm * n_sub, LANES), lambda i: (i, 0)),
                   pl.BlockSpec((tm, LANES), lambda i: (i, 0)),
                   _const_spec((SUBLANES, LANES))],
        scratch_shapes=[pltpu.VMEM((SUBLANES, LANES), F32)],
        compiler_params=_cparams(("arbitrary",)),
        name="attn_out_route",
    )(x2d, o, w_o, gain, router_pad)


def _dispatch_kernel(dest_ref, zpos_ref, zflag_ref, h_ref, dst_ref, zbuf_ref, zsem, sem):
    i = pl.program_id(0)
    tm = h_ref.shape[0]

    @pl.when(i == 0)
    def _():
        zbuf_ref[...] = jnp.zeros_like(zbuf_ref)

        def zcopy(e):
            return pltpu.make_async_copy(zbuf_ref, dst_ref.at[pl.ds(zpos_ref[e], MOE_TILE)], zsem.at[e])

        for e in range(2 * N_EXPERTS):
            @pl.when(zflag_ref[e] == 1)
            def _():
                zcopy(e).start()
        for e in range(2 * N_EXPERTS):
            @pl.when(zflag_ref[e] == 1)
            def _():
                zcopy(e).wait()

    def body(r, carry):
        base = 2 * (i * tm + r)
        for k in range(2):
            pltpu.make_async_copy(h_ref.at[r], dst_ref.at[dest_ref[base + k]], sem).start()
        return carry

    lax.fori_loop(0, tm, body, 0, unroll=8)
    for k in range(2):
        pltpu.make_async_copy(h_ref, dst_ref.at[pl.ds(0, tm)], sem).wait()


def _dispatch(h3_rows, dest, zpos, zflag, n_dst):
    t = h3_rows.shape[0]
    tm = TOK_TILE
    tail = h3_rows.shape[1:]
    return pl.pallas_call(
        _dispatch_kernel,
        out_shape=jax.ShapeDtypeStruct((n_dst,) + tail, F32),
        grid_spec=pltpu.PrefetchScalarGridSpec(
            num_scalar_prefetch=3, grid=(t // tm,),
            in_specs=[pl.BlockSpec((tm,) + tail, lambda i, *_: (i, 0, 0))],
            out_specs=pl.BlockSpec(memory_space=pl.ANY),
            scratch_shapes=[pltpu.VMEM((MOE_TILE,) + tail, F32),
                            pltpu.SemaphoreType.DMA((2 * N_EXPERTS,)), pltpu.SemaphoreType.DMA(())]),
        compiler_params=_cparams(("arbitrary",)),
        name="moe_dispatch",
    )(dest, zpos, zflag, h3_rows)


def _moe_kernel(te_ref, tv_ref, ts_ref, xs_ref, wg_ref, wu_ref, wd_ref, ys_ref, h_ref, acc_ref):
    i = pl.program_id(0)
    c = pl.program_id(1)
    last = pl.num_programs(1) - 1
    valid = tv_ref[i] == 1
    tile = h_ref.shape[0]
    n_sub = h_ref.shape[1] // LANES

    @pl.when(valid & (c == 0))
    def _():
        for s in range(n_sub):
            h_ref[:, s * LANES:(s + 1) * LANES] = xs_ref[pl.ds(s, tile, stride=SUBLANES), :].astype(BF16)
        acc_ref[...] = jnp.zeros_like(acc_ref)

    @pl.when(valid)
    def _():
        h = h_ref[...]
        gt = _dot(h, wg_ref[...])
        up = _dot(h, wu_ref[...])
        act = (gt * jax.nn.sigmoid(gt) * up).astype(BF16)
        acc_ref[...] += _dot(act, wd_ref[...])

    @pl.when(valid & (c == last))
    def _():
        for s in range(n_sub):
            ys_ref[pl.ds(s, tile, stride=SUBLANES), :] = acc_ref[:, s * LANES:(s + 1) * LANES]

    @pl.when(jnp.logical_not(valid) & (c == last))
    def _():
        ys_ref[...] = jnp.zeros_like(ys_ref)


def _moe_experts(xs2d, w_gu, w_down, tile_expert, tile_valid, tile_src, d):
    n_sub = d // LANES
    n_tiles = xs2d.shape[0] // (MOE_TILE * n_sub)
    f = w_down.shape[1]
    nfc = f // MOE_FC
    blk = MOE_TILE * n_sub

    def ceff(i, c, tv):
        return jnp.where(tv[i] == 1, c, nfc - 1)

    return pl.pallas_call(
        _moe_kernel,
        out_shape=jax.ShapeDtypeStruct(xs2d.shape, F32),
        grid_spec=pltpu.PrefetchScalarGridSpec(
            num_scalar_prefetch=3, grid=(n_tiles, nfc),
            in_specs=[
                pl.BlockSpec((blk, LANES), lambda i, c, te, tv, ts: (ts[i], 0)),
                pl.BlockSpec((None, d, MOE_FC), lambda i, c, te, tv, ts: (te[i], 0, ceff(i, c, tv))),
                pl.BlockSpec((None, d, MOE_FC), lambda i, c, te, tv, ts: (te[i], 0, nfc + ceff(i, c, tv))),
                pl.BlockSpec((None, MOE_FC, d), lambda i, c, te, tv, ts: (te[i], ceff(i, c, tv), 0)),
            ],
            out_specs=pl.BlockSpec((blk, LANES), lambda i, c, te, tv, ts: (i, 0)),
            scratch_shapes=[pltpu.VMEM((MOE_TILE, d), BF16), pltpu.VMEM((MOE_TILE, d), F32)]),
        compiler_params=_cparams(("arbitrary", "arbitrary")),
        name="moe_experts",
    )(tile_expert, tile_valid, tile_src, xs2d, w_gu, w_gu, w_down)


def _combine_kernel(dest_ref, x_ref, ys_ref, route_ref, g_ref, o_ref, ybuf_ref, sem):
    i = pl.program_id(0)
    n = pl.num_programs(0)
    tm, d = x_ref.shape
    n_sub = d // LANES

    def gather(tile, slot):
        def body(r, carry):
            base = 2 * (tile * tm + r)
            for k in range(2):
                src = ys_ref.at[pl.ds(pl.multiple_of(dest_ref[base + k] * n_sub, n_sub), n_sub)]
                dst = ybuf_ref.at[slot, pl.ds(pl.multiple_of((2 * r + k) * n_sub, n_sub), n_sub)]
                pltpu.make_async_copy(src, dst, sem.at[slot]).start()
            return carry

        lax.fori_loop(0, tm, body, 0, unroll=8)

    @pl.when(i == 0)
    def _():
        gather(0, 0)

    @pl.when(i + 1 < n)
    def _():
        gather(i + 1, (i + 1) % 2)

    slot = i % 2
    pltpu.make_async_copy(ys_ref.at[pl.ds(0, 2 * tm * n_sub)], ybuf_ref.at[slot], sem.at[slot]).wait()

    y_ref = ybuf_ref.at[slot]
    route = route_ref[...]
    w1 = route[:, 4:5]
    w2 = route[:, 5:6]
    for s in range(n_sub):
        cols = slice(s * LANES, (s + 1) * LANES)
        y1 = y_ref[pl.ds(s, tm, stride=2 * n_sub), :]
        y2 = y_ref[pl.ds(n_sub + s, tm, stride=2 * n_sub), :]
        o_ref[:, cols] = x_ref[:, cols] + w1 * y1 + w2 * y2
    o_ref[...] = _rms(o_ref[...], g_ref[...])


def _combine(x2d, ys2d, dest, route, gain):
    t, d = x2d.shape
    tm = TOK_TILE
    n_sub = d // LANES
    return pl.pallas_call(
        _combine_kernel,
        out_shape=jax.ShapeDtypeStruct((t, d), F32),
        grid_spec=pltpu.PrefetchScalarGridSpec(
            num_scalar_prefetch=1, grid=(t // tm,),
            in_specs=[pl.BlockSpec((tm, d), lambda i, *_: (i, 0)),
                      pl.BlockSpec(memory_space=pl.ANY),
                      pl.BlockSpec((tm, LANES), lambda i, *_: (i, 0)),
                      pl.BlockSpec((1, d), lambda i, *_: (0, 0), pipeline_mode=pl.Buffered(1))],
            out_specs=pl.BlockSpec((tm, d), lambda i, *_: (i, 0)),
            scratch_shapes=[pltpu.VMEM((2, 2 * tm * n_sub, LANES), F32), pltpu.SemaphoreType.DMA((2,))]),
        compiler_params=_cparams(("arbitrary",)),
        name="moe_combine",
    )(dest, x2d, ys2d, route, gain)


def _rope_tables(seq, rot_dim, theta):
    inv = 1.0 / (theta ** (jnp.arange(0, rot_dim, 2, dtype=F32) / rot_dim))
    ang = jnp.arange(seq, dtype=F32)[:, None] * inv[None, :]
    return jnp.cos(ang), jnp.sin(ang)


def _retention_tables():
    c = RET_CHUNK
    log_gamma = jnp.log(1.0 - 2.0 ** (-5.0 - jnp.arange(RET_HEADS, dtype=F32)))
    idx = jnp.arange(c, dtype=F32)
    rel = idx[:, None] - idx[None, :]
    dmat = jnp.where(rel[None] >= 0, jnp.exp(jnp.maximum(rel, 0.0)[None] * log_gamma[:, None, None]), 0.0)
    qd = jnp.exp((idx + 1.0)[None, :] * log_gamma[:, None])[:, :, None]
    kd = jnp.exp((c - 1.0 - idx)[None, :] * log_gamma[:, None])[:, :, None]
    cd = jnp.exp(c * log_gamma)[:, None, None]
    return dmat, qd, kd, cd


def _attn_rope_tables(seq):
    cos, sin = _rope_tables(seq, ROPE_DIM, ROPE_THETA)
    half = ROPE_DIM // 2
    pad = DIFF_HD - ROPE_DIM
    ones = jnp.ones((seq, pad), F32)
    zeros = jnp.zeros((seq, pad), F32)
    zh = jnp.zeros((seq, half), F32)
    ctab = jnp.concatenate([cos, cos, ones], axis=1)
    s1tab = jnp.concatenate([-sin, zh, zeros], axis=1)
    s2tab = jnp.concatenate([zh, sin, zeros], axis=1)
    rep = LANES // DIFF_HD
    return tuple(jnp.tile(tb, (1, rep)) for tb in (ctab, s1tab, s2tab))


def kernel(x, ln_mix, ln_ffn, ret_w_in, ret_w_o, kv_norm, w_kv, diff_w_q, lam_q1, lam_k1, lam_q2, lam_k2,
           diff_subln, diff_w_o, ffn_w_gu, ffn_w_down, moe_router, moe_w_gu, moe_w_down, final_norm):
    batch, seq, d = x.shape
    t = batch * seq
    assert ln_mix.shape[0] == 2 and ret_w_in.shape[0] == 1 and diff_w_q.shape[0] == 1
    assert seq % TOK_TILE == 0 and TOK_TILE % RET_CHUNK == 0 and ATT_TQ % CHUNK == 0
    assert seq % ATT_TK == 0 and ATT_TK % ATT_TQ == 0
    x2d = x.reshape(t, d)
    row = lambda g: g.reshape(1, -1)

    cos_r, sin_r = _rope_tables(seq, RET_QK, RET_THETA)
    proj = _ret_in_proj(x2d, row(ln_mix[0]), ret_w_in[0].astype(BF16), cos_r, sin_r, seq)
    ret_o = _retention(proj, *_retention_tables(), batch, seq)
    x2 = _ret_out_ffn(x2d, ret_o, ret_w_o[0].astype(BF16), row(ln_ffn[0]),
                      ffn_w_gu[0].astype(BF16), ffn_w_down[0].astype(BF16))

    lambda_init = 0.8 - 0.6 * math.exp(-0.3 * 1)
    q, k, vt = _qkv_proj(x2, row(ln_mix[1]), row(kv_norm), diff_w_q[0].astype(BF16),
                         w_kv[:, :d].astype(BF16), w_kv[:, d:].T.astype(BF16),
                         *_attn_rope_tables(seq), batch, seq)
    lam_vecs = jnp.stack([lam_q1[0], lam_k1[0], lam_q2[0], lam_k2[0]]).astype(F32)
    att = _diff_attention(q, k, vt, lam_vecs, row(diff_subln[0]), batch, seq, lambda_init)
    router_pad = jnp.pad(moe_router[0], ((0, 0), (0, LANES - N_EXPERTS)))
    x3, h3_rows, route, cnt = _attn_out(x2, att, diff_w_o[0].astype(BF16), row(ln_ffn[1]), router_pad)

    n_sub = d // LANES
    n_rows = 2 * t + N_EXPERTS * MOE_TILE
    n_tiles = n_rows // MOE_TILE
    expert = route[:, 0:2].astype(jnp.int32)
    rank = route[:, 2:4].astype(jnp.int32)
    counts = cnt[0, :N_EXPERTS].astype(jnp.int32)
    padded = (counts + MOE_TILE - 1) // MOE_TILE * MOE_TILE
    seg_end = jnp.cumsum(padded)
    seg_start = seg_end - padded
    dest = (jnp.sum(jnp.where(expert[..., None] == jnp.arange(N_EXPERTS), seg_start, 0), axis=-1)
            + rank).reshape(-1)
    tile_row = jnp.arange(n_tiles, dtype=jnp.int32) * MOE_TILE
    tile_valid = (tile_row < seg_end[-1]).astype(jnp.int32)
    n_valid = seg_end[-1] // MOE_TILE
    tile_src = jnp.minimum(jnp.arange(n_tiles, dtype=jnp.int32), jnp.maximum(n_valid - 1, 0))
    tile_expert = jnp.minimum(jnp.sum(tile_src[:, None] * MOE_TILE >= seg_end[None, :], axis=1),
                              N_EXPERTS - 1).astype(jnp.int32)
    zflag = jnp.concatenate([(padded > 0).astype(jnp.int32), 1 - tile_valid[-N_EXPERTS:]])
    zpos = jnp.concatenate([jnp.maximum(seg_end - MOE_TILE, 0), tile_row[-N_EXPERTS:]]).astype(jnp.int32)

    xs = _dispatch(h3_rows.reshape(t, n_sub, LANES), dest, zpos, zflag, n_rows)
    ys = _moe_experts(xs.reshape(n_rows * n_sub, LANES), moe_w_gu[0].astype(BF16),
                      moe_w_down[0].astype(BF16), tile_expert, tile_valid, tile_src, d)
    out = _combine(x3, ys, dest, route, row(final_norm))
    return out.reshape(batch, seq, d)
```

```python
import functools
import math

import jax
import jax.numpy as jnp
from jax import lax
from jax.experimental import pallas as pl
from jax.experimental.pallas import tpu as pltpu

F32 = jnp.float32
BF16 = jnp.bfloat16

EPS = 1e-6
CHUNK = 64
RET_QK = 256
RET_V = 512
RET_HEADS = 4
RET_THETA = 10000.0
DIFF_HD = 64
DIFF_HEADS = 8
ROPE_THETA = 500000.0
ROPE_DIM = 16
N_EXPERTS = 8

LANES = 128
SUBLANES = 8
MXU_COLS = 256
VMEM_LIMIT = 56 * 1024 * 1024

RET_CHUNK = 256
TOK_TILE = 512
ATT_TQ = 512
ATT_TK = 512
ATT_HEADS = 4
ATT_ONES = 16
MOE_TILE = 512
MOE_FC = 1792


def _cparams(sem, vmem=VMEM_LIMIT):
    return pltpu.CompilerParams(dimension_semantics=sem, vmem_limit_bytes=vmem)


def _const_spec(shape):
    nd = len(shape)
    return pl.BlockSpec(shape, lambda *_: (0,) * nd, pipeline_mode=pl.Buffered(1))


def _rms(x, g):
    return x * lax.rsqrt(jnp.mean(x * x, axis=-1, keepdims=True) + EPS) * g


def _dot(a, b):
    return jnp.dot(a, b, preferred_element_type=F32)


def _ret_in_kernel(x_ref, g_ref, w_ref, cos_ref, sin_ref, o_ref):
    h = _rms(x_ref[...], g_ref[...]).astype(BF16)
    cos = cos_ref[...]
    sin = sin_ref[...]
    d_qk = RET_HEADS * RET_QK
    half = RET_QK // 2
    for c in range(2 * RET_HEADS):
        c0 = c * RET_QK
        acc = _dot(h, w_ref[:, c0:c0 + RET_QK])
        x1 = acc[:, :half]
        x2 = acc[:, half:]
        scale = 1.0 if c < RET_HEADS else RET_QK ** -0.5
        o_ref[:, c0:c0 + half] = ((x1 * cos - x2 * sin) * scale).astype(BF16)
        o_ref[:, c0 + half:c0 + RET_QK] = ((x2 * cos + x1 * sin) * scale).astype(BF16)
    n_rest = (w_ref.shape[1] - 2 * d_qk) // RET_V
    for c in range(n_rest):
        c0 = 2 * d_qk + c * RET_V
        o_ref[:, c0:c0 + RET_V] = _dot(h, w_ref[:, c0:c0 + RET_V]).astype(BF16)


def _ret_in_proj(x2d, gain, w, cos, sin, seq):
    t, d = x2d.shape
    n = w.shape[1]
    tm = TOK_TILE
    n_pos = seq // tm
    return pl.pallas_call(
        _ret_in_kernel,
        out_shape=jax.ShapeDtypeStruct((t, n), BF16),
        grid=(t // tm,),
        in_specs=[
            pl.BlockSpec((tm, d), lambda i: (i, 0)),
            _const_spec((1, d)),
            _const_spec((d, n)),
            pl.BlockSpec((tm, RET_QK // 2), lambda i: (i % n_pos, 0)),
            pl.BlockSpec((tm, RET_QK // 2), lambda i: (i % n_pos, 0)),
        ],
        out_specs=pl.BlockSpec((tm, n), lambda i: (i, 0)),
        compiler_params=_cparams(("parallel",)),
        name="ret_in_proj",
    )(x2d, gain, w, cos, sin)


def _retention_kernel(q_ref, k_ref, v_ref, g_ref, dmat_ref, qd_ref, kd_ref, cd_ref, o_ref, state_ref):
    @pl.when(pl.program_id(1) == 0)
    def _():
        state_ref[...] = jnp.zeros_like(state_ref)

    n_chunks = q_ref.shape[0] // RET_CHUNK
    for h in range(RET_HEADS):
        for c in range(n_chunks):
            rows = slice(c * RET_CHUNK, (c + 1) * RET_CHUNK)
            q = q_ref[rows, h * RET_QK:(h + 1) * RET_QK]
            k = k_ref[rows, h * RET_QK:(h + 1) * RET_QK]
            v = v_ref[rows, h * RET_V:(h + 1) * RET_V]
            state = state_ref[h]
            s = lax.dot_general(q, k, (((1,), (1,)), ((), ())), preferred_element_type=F32)
            s = s * dmat_ref[h]
            o = _dot(s.astype(BF16), v)
            qs = (q.astype(F32) * qd_ref[h]).astype(BF16)
            o = o + _dot(qs, state.astype(BF16))
            ks = (k.astype(F32) * kd_ref[h]).astype(BF16)
            state_ref[h] = state * cd_ref[h] + lax.dot_general(
                ks, v, (((0,), (0,)), ((), ())), preferred_element_type=F32)
            mu = jnp.mean(o, axis=-1, keepdims=True)
            oc = o - mu
            var = jnp.mean(oc * oc, axis=-1, keepdims=True)
            on = oc * lax.rsqrt(var + EPS)
            gate = g_ref[rows, h * RET_V:(h + 1) * RET_V].astype(F32)
            o_ref[rows, h * RET_V:(h + 1) * RET_V] = (on * (gate * jax.nn.sigmoid(gate))).astype(BF16)


def _retention(proj, dmat, qd, kd, cd, batch, seq):
    t = proj.shape[0]
    tb = TOK_TILE
    nj = seq // tb
    d_qk = RET_HEADS * RET_QK
    d_v = RET_HEADS * RET_V
    row = lambda b, j: b * nj + j
    return pl.pallas_call(
        _retention_kernel,
        out_shape=jax.ShapeDtypeStruct((t, d_v), BF16),
        grid=(batch, nj),
        in_specs=[
            pl.BlockSpec((tb, d_qk), lambda b, j: (row(b, j), 0)),
            pl.BlockSpec((tb, d_qk), lambda b, j: (row(b, j), 1)),
            pl.BlockSpec((tb, d_v), lambda b, j: (row(b, j), 1)),
            pl.BlockSpec((tb, d_v), lambda b, j: (row(b, j), 2)),
            _const_spec(dmat.shape),
            _const_spec(qd.shape),
            _const_spec(kd.shape),
            _const_spec(cd.shape),
        ],
        out_specs=pl.BlockSpec((tb, d_v), lambda b, j: (row(b, j), 0)),
        scratch_shapes=[pltpu.VMEM((RET_HEADS, RET_QK, RET_V), F32)],
        compiler_params=_cparams(("parallel", "arbitrary")),
        name="retention",
    )(proj, proj, proj, proj, dmat, qd, kd, cd)


def _ret_out_ffn_kernel(x_ref, o_ref, wo_ref, g_ref, wgu_ref, wd_ref, out_ref, act_ref, *, fc):
    x1 = x_ref[...] + _dot(o_ref[...], wo_ref[...])
    h = _rms(x1, g_ref[...]).astype(BF16)
    f = wd_ref.shape[0]
    for c in range(f // fc):
        gt = _dot(h, wgu_ref[:, c * fc:(c + 1) * fc])
        up = _dot(h, wgu_ref[:, f + c * fc:f + (c + 1) * fc])
        act_ref[:, c * fc:(c + 1) * fc] = (gt * jax.nn.sigmoid(gt) * up).astype(BF16)
    out_ref[...] = x1 + _dot(act_ref[...], wd_ref[...])


def _ret_out_ffn(x2d, o, w_o, gain, w_gu, w_down):
    t, d = x2d.shape
    tm = TOK_TILE
    f = w_down.shape[0]
    return pl.pallas_call(
        functools.partial(_ret_out_ffn_kernel, fc=256),
        out_shape=jax.ShapeDtypeStruct((t, d), F32),
        grid=(t // tm,),
        in_specs=[
            pl.BlockSpec((tm, d), lambda i: (i, 0)),
            pl.BlockSpec((tm, o.shape[1]), lambda i: (i, 0)),
            _const_spec(w_o.shape),
            _const_spec((1, d)),
            _const_spec(w_gu.shape),
            _const_spec(w_down.shape),
        ],
        out_specs=pl.BlockSpec((tm, d), lambda i: (i, 0)),
        scratch_shapes=[pltpu.VMEM((tm, f), BF16)],
        compiler_params=_cparams(("parallel",)),
        name="ret_out_ffn",
    )(x2d, o, w_o, gain, w_gu, w_down)


def _rope16(x, ctab, s1tab, s2tab):
    half = ROPE_DIM // 2
    return (x * ctab + pltpu.roll(x, LANES - half, 1) * s1tab + pltpu.roll(x, half, 1) * s2tab)


def _qkv_kernel(x_ref, gq_ref, gkv_ref, wq_ref, wk_ref, wvt_ref, c_ref, s1_ref, s2_ref, q_ref, k_ref, vt_ref):
    x = x_ref[...]
    xn = x * lax.rsqrt(jnp.mean(x * x, axis=-1, keepdims=True) + EPS)
    hq = (xn * gq_ref[...]).astype(BF16)
    hkv = (xn * gkv_ref[...]).astype(BF16)
    ctab, s1tab, s2tab = c_ref[...], s1_ref[...], s2_ref[...]
    d = q_ref.shape[1]
    q_scale = DIFF_HD ** -0.5 * math.log2(math.e)
    for c in range(d // MXU_COLS):
        c0 = c * MXU_COLS
        qa = _dot(hq, wq_ref[:, c0:c0 + MXU_COLS])
        ka = _dot(hkv, wk_ref[:, c0:c0 + MXU_COLS])
        for l0 in range(0, MXU_COLS, LANES):
            cols = slice(c0 + l0, c0 + l0 + LANES)
            q_ref[:, cols] = (_rope16(qa[:, l0:l0 + LANES], ctab, s1tab, s2tab) * q_scale).astype(BF16)
            k_ref[:, cols] = _rope16(ka[:, l0:l0 + LANES], ctab, s1tab, s2tab).astype(BF16)
    vt_ref[...] = lax.dot_general(wvt_ref[...], hkv, (((1,), (1,)), ((), ())),
                                  preferred_element_type=F32).astype(BF16)


def _qkv_proj(x2d, gq, gkv, w_q, w_k, w_vt, ctab, s1tab, s2tab, batch, seq):
    t, d = x2d.shape
    tm = TOK_TILE
    n_pos = seq // tm
    tok = pl.BlockSpec((tm, d), lambda i: (i, 0))
    tab = pl.BlockSpec((tm, LANES), lambda i: (i % n_pos, 0))
    return pl.pallas_call(
        _qkv_kernel,
        out_shape=[jax.ShapeDtypeStruct((t, d), BF16), jax.ShapeDtypeStruct((t, d), BF16),
                   jax.ShapeDtypeStruct((batch * d, seq), BF16)],
        grid=(t // tm,),
        in_specs=[tok, _const_spec((1, d)), _const_spec((1, d)), _const_spec(w_q.shape),
                  _const_spec(w_k.shape), _const_spec(w_vt.shape), tab, tab, tab],
        out_specs=[tok, tok, pl.BlockSpec((d, tm), lambda i: (i // n_pos, i % n_pos))],
        compiler_params=_cparams(("parallel",)),
        name="qkv_proj",
    )(x2d, gq, gkv, w_q, w_k, w_vt, ctab, s1tab, s2tab)


def _attn_kernel(lam_ref, bias_ref, q_ref, k_ref, vt_ref, sub_ref, o_ref, vta_ref, s_ref, acc_ref, *,
                 lambda_init):
    i = pl.program_id(2)
    tq = q_ref.shape[0]
    dv = 2 * DIFF_HD
    n_heads, n_kv, _, tk = vta_ref.shape
    chains = [(h, c) for h in range(n_heads) for c in range(2)]

    @pl.when(i == 0)
    def _():
        for h in range(n_heads):
            for j in range(n_kv):
                vta_ref[h, j, 0:dv, :] = vt_ref[h * dv:(h + 1) * dv, j * tk:(j + 1) * tk]
                vta_ref[h, j, dv:, :] = jnp.ones((ATT_ONES, tk), BF16)

    lane = lax.broadcasted_iota(jnp.int32, (tq, dv), 1)
    qc = []
    for h in range(n_heads):
        q = q_ref[:, h * dv:(h + 1) * dv]
        zero = jnp.zeros_like(q)
        qc.append((jnp.where(lane < DIFF_HD, q, zero), jnp.where(lane >= DIFF_HD, q, zero)))
    n_full = (i * tq) // tk
    tail_bias = 1 + (i * tq - n_full * tk) // tq

    def tile_max(s):
        m8 = jnp.max(s.reshape(tk // SUBLANES, SUBLANES, tq), axis=0)
        return jnp.max(m8, axis=0, keepdims=True)

    def score_chain(j, x):
        h, c = chains[x]
        bias = bias_ref[jnp.where(j == n_full, tail_bias, 0)]
        kt = k_ref[pl.ds(pl.multiple_of(j * tk, tk), tk), h * dv:(h + 1) * dv]
        s = lax.dot_general(kt, qc[h][c], (((1,), (1,)), ((), ())), preferred_element_type=F32) + bias
        s_ref[x] = s
        return tile_max(s)

    acc_ref[...] = jnp.zeros_like(acc_ref)
    m_init = tuple(jnp.full((1, tq), -jnp.inf, F32) for _ in chains)

    def step(j, carry, with_next):
        m_run, m_tile = carry
        m_out, m_next = [], []
        for x, (h, c) in enumerate(chains):
            m_new = jnp.maximum(m_run[x], m_tile[x])
            alpha = jnp.exp2(m_run[x] - m_new)
            p = jnp.exp2(s_ref[x] - m_new).astype(BF16)
            acc_ref[x] = alpha * acc_ref[x] + _dot(vta_ref[h, j], p)
            m_out.append(m_new)
            if with_next:
                m_next.append(score_chain(j + 1, x))
        return tuple(m_out), tuple(m_next)

    first = tuple(score_chain(0, x) for x in range(len(chains)))
    carry = lax.fori_loop(0, n_full, lambda j, carry: step(j, carry, True), (m_init, first))
    step(n_full, carry, False)

    lam_v = lam_ref[...]
    lam = (jnp.exp(jnp.sum(lam_v[0:1] * lam_v[1:2], axis=-1, keepdims=True))
           - jnp.exp(jnp.sum(lam_v[2:3] * lam_v[3:4], axis=-1, keepdims=True)) + lambda_init)
    for h in range(n_heads):
        a0 = acc_ref[2 * h]
        a1 = acc_ref[2 * h + 1]
        ot = a0[0:dv] / a0[dv:dv + 1] - lam * (a1[0:dv] / a1[dv:dv + 1])
        o = _rms(ot.T, sub_ref[...]) * (1.0 - lambda_init)
        o_ref[:, h * dv:(h + 1) * dv] = o.astype(BF16)


def _attn_bias(tq, tk):
    key = jnp.arange(tk)[:, None] // CHUNK
    tiles = [jnp.zeros((tk, tq), F32)]
    for r in range(tk // tq):
        qry = (r * tq + jnp.arange(tq))[None, :] // CHUNK
        tiles.append(jnp.where(key <= qry, 0.0, -1e30).astype(F32))
    return jnp.stack(tiles)


def _diff_attention(q, k, vt, lam_vecs, subln, batch, seq, lambda_init):
    t, d = q.shape
    tq = ATT_TQ
    nq = seq // tq
    dv = 2 * DIFF_HD
    hp = ATT_HEADS
    n_groups = DIFF_HEADS // hp
    bias = _attn_bias(tq, ATT_TK)
    return pl.pallas_call(
        functools.partial(_attn_kernel, lambda_init=lambda_init),
        out_shape=jax.ShapeDtypeStruct((t, d), BF16),
        grid=(batch, n_groups, nq),
        in_specs=[
            _const_spec(lam_vecs.shape),
            _const_spec(bias.shape),
            pl.BlockSpec((tq, hp * dv), lambda b, g, i: (b * nq + i, g)),
            pl.BlockSpec((seq, hp * dv), lambda b, g, i: (b, g)),
            pl.BlockSpec((hp * dv, seq), lambda b, g, i: (b * n_groups + g, 0)),
            _const_spec((1, dv)),
        ],
        out_specs=pl.BlockSpec((tq, hp * dv), lambda b, g, i: (b * nq + i, g)),
        scratch_shapes=[pltpu.VMEM((hp, seq // ATT_TK, dv + ATT_ONES, ATT_TK), BF16),
                        pltpu.VMEM((2 * hp, ATT_TK, tq), F32),
                        pltpu.VMEM((2 * hp, dv + ATT_ONES, tq), F32)],
        compiler_params=_cparams(("parallel", "parallel", "arbitrary")),
        name="diff_attn",
    )(lam_vecs, bias, q, k, vt, subln)


ROUTE_ROWS = 2 * SUBLANES


def _attn_out_kernel(x_ref, o_ref, wo_ref, g_ref, r_ref, tri_ref, x3_ref, h3_ref, route_ref, cnt_ref, carry_ref):
    @pl.when(pl.program_id(0) == 0)
    def _():
        carry_ref[...] = jnp.zeros_like(carry_ref)

    tm = x_ref.shape[0]
    x3 = x_ref[...] + _dot(o_ref[...], wo_ref[...])
    x3_ref[...] = x3
    h3 = _rms(x3, g_ref[...])
    for s in range(h3.shape[1] // LANES):
        h3_ref[pl.ds(s, tm, stride=SUBLANES), :] = h3[:, s * LANES:(s + 1) * LANES]

    hi = h3.astype(BF16)
    lo = (h3 - hi.astype(F32)).astype(BF16)
    both = _dot(hi, r_ref[...])
    logits = both[:, :LANES] + both[:, LANES:] + _dot(lo, r_ref[:, :LANES])
    lt = logits.T[0:ROUTE_ROWS]
    sub = lax.broadcasted_iota(jnp.int32, lt.shape, 0).astype(F32)
    lt = jnp.where(sub < N_EXPERTS, lt, -jnp.inf)
    v1 = jnp.max(lt, axis=0, keepdims=True)
    i1 = jnp.min(jnp.where(lt == v1, sub, float(ROUTE_ROWS)), axis=0, keepdims=True)
    lt2 = jnp.where(sub == i1, -jnp.inf, lt)
    v2 = jnp.max(lt2, axis=0, keepdims=True)
    i2 = jnp.min(jnp.where(lt2 == v2, sub, float(ROUTE_ROWS)), axis=0, keepdims=True)
    e = jnp.exp(v2 - v1)
    w1 = 1.0 / (1.0 + e)
    w2 = e / (1.0 + e)
    oh1 = sub == i1
    oh2 = sub == i2
    assign = jnp.where(oh1 | oh2, 1.0, 0.0)
    excl = _dot(assign.astype(BF16), tri_ref[...]) + carry_ref[:, 0:1]
    r1 = jnp.sum(jnp.where(oh1, excl, 0.0), axis=0, keepdims=True)
    r2 = jnp.sum(jnp.where(oh2, excl, 0.0), axis=0, keepdims=True)
    route = jnp.zeros_like(lt)
    for row, val in enumerate((i1, i2, r1, r2, w1, w2)):
        route = jnp.where(sub == float(row), val, route)
    route_ref[...] = route[0:SUBLANES]
    total = carry_ref[:, 0:1] + jnp.sum(assign, axis=1, keepdims=True)
    carry_ref[...] = jnp.broadcast_to(total, carry_ref.shape)
    cnt_ref[...] = jnp.broadcast_to(total, cnt_ref.shape)


def _attn_out(x2d, o, w_o, gain, router_split, tri):
    t, d = x2d.shape
    tm = TOK_TILE
    n_sub = d // LANES
    return pl.pallas_call(
        _attn_out_kernel,
        out_shape=[jax.ShapeDtypeStruct((t, d), F32),
                   jax.ShapeDtypeStruct((t * n_sub, LANES), F32),
                   jax.ShapeDtypeStruct((SUBLANES, t), F32),
                   jax.ShapeDtypeStruct((ROUTE_ROWS, LANES), F32)],
        grid=(t // tm,),
        in_specs=[pl.BlockSpec((tm, d), lambda i: (i, 0)),
                  pl.BlockSpec((tm, d), lambda i: (i, 0)),
                  _const_spec(w_o.shape), _const_spec((1, d)), _const_spec(router_split.shape),
                  _const_spec(tri.shape)],
        out_specs=[pl.BlockSpec((tm, d), lambda i: (i, 0)),
                   pl.BlockSpec((tm * n_sub, LANES), lambda i: (i, 0)),
                   pl.BlockSpec((SUBLANES, tm), lambda i: (0, i)),
                   _const_spec((ROUTE_ROWS, LANES))],
        scratch_shapes=[pltpu.VMEM((ROUTE_ROWS, LANES), F32)],
        compiler_params=_cparams(("arbitrary",)),
        name="attn_out_route",
    )(x2d, o, w_o, gain, router_split, tri)


def _dispatch_kernel(dest_ref, zpos_ref, zflag_ref, h_ref, dst_ref, zbuf_ref, zsem, sem):
    i = pl.program_id(0)
    tm = h_ref.shape[0]

    @pl.when(i == 0)
    def _():
        zbuf_ref[...] = jnp.zeros_like(zbuf_ref)

        def zcopy(e):
            return pltpu.make_async_copy(zbuf_ref, dst_ref.at[pl.ds(zpos_ref[e], MOE_TILE)], zsem.at[e])

        for e in range(2 * N_EXPERTS):
            @pl.when(zflag_ref[e] == 1)
            def _():
                zcopy(e).start()
        for e in range(2 * N_EXPERTS):
            @pl.when(zflag_ref[e] == 1)
            def _():
                zcopy(e).wait()

    def body(r, carry):
        base = 2 * (i * tm + r)
        for k in range(2):
            pltpu.make_async_copy(h_ref.at[r], dst_ref.at[dest_ref[base + k]], sem).start()
        return carry

    lax.fori_loop(0, tm, body, 0, unroll=8)
    for k in range(2):
        pltpu.make_async_copy(h_ref, dst_ref.at[pl.ds(0, tm)], sem).wait()


def _dispatch(h3_rows, dest, zpos, zflag, n_dst):
    t = h3_rows.shape[0]
    tm = TOK_TILE
    tail = h3_rows.shape[1:]
    return pl.pallas_call(
        _dispatch_kernel,
        out_shape=jax.ShapeDtypeStruct((n_dst,) + tail, F32),
        grid_spec=pltpu.PrefetchScalarGridSpec(
            num_scalar_prefetch=3, grid=(t // tm,),
            in_specs=[pl.BlockSpec((tm,) + tail, lambda i, *_: (i, 0, 0))],
            out_specs=pl.BlockSpec(memory_space=pl.ANY),
            scratch_shapes=[pltpu.VMEM((MOE_TILE,) + tail, F32),
                            pltpu.SemaphoreType.DMA((2 * N_EXPERTS,)), pltpu.SemaphoreType.DMA(())]),
        compiler_params=_cparams(("arbitrary",)),
        name="moe_dispatch",
    )(dest, zpos, zflag, h3_rows)


def _moe_kernel(te_ref, tv_ref, ts_ref, xs_ref, wg_ref, wu_ref, wd_ref, ys_ref, h_ref, acc_ref):
    i = pl.program_id(0)
    c = pl.program_id(1)
    last = pl.num_programs(1) - 1
    valid = tv_ref[i] == 1
    tile = h_ref.shape[0]
    n_sub = h_ref.shape[1] // LANES

    @pl.when(valid & (c == 0))
    def _():
        for s in range(n_sub):
            h_ref[:, s * LANES:(s + 1) * LANES] = xs_ref[pl.ds(s, tile, stride=SUBLANES), :].astype(BF16)
        acc_ref[...] = jnp.zeros_like(acc_ref)

    @pl.when(valid)
    def _():
        h = h_ref[...]
        gt = _dot(h, wg_ref[...])
        up = _dot(h, wu_ref[...])
        act = (gt * jax.nn.sigmoid(gt) * up).astype(BF16)
        acc_ref[...] += _dot(act, wd_ref[...])

    @pl.when(valid & (c == last))
    def _():
        for s in range(n_sub):
            ys_ref[pl.ds(s, tile, stride=SUBLANES), :] = acc_ref[:, s * LANES:(s + 1) * LANES]

    @pl.when(jnp.logical_not(valid) & (c == last))
    def _():
        ys_ref[...] = jnp.zeros_like(ys_ref)


def _moe_experts(xs2d, w_gu, w_down, tile_expert, tile_valid, tile_src, d):
    n_sub = d // LANES
    n_tiles = xs2d.shape[0] // (MOE_TILE * n_sub)
    f = w_down.shape[1]
    nfc = f // MOE_FC
    blk = MOE_TILE * n_sub

    def ceff(i, c, tv):
        return jnp.where(tv[i] == 1, c, nfc - 1)

    return pl.pallas_call(
        _moe_kernel,
        out_shape=jax.ShapeDtypeStruct(xs2d.shape, F32),
        grid_spec=pltpu.PrefetchScalarGridSpec(
            num_scalar_prefetch=3, grid=(n_tiles, nfc),
            in_specs=[
                pl.BlockSpec((blk, LANES), lambda i, c, te, tv, ts: (ts[i], 0)),
                pl.BlockSpec((None, d, MOE_FC), lambda i, c, te, tv, ts: (te[i], 0, ceff(i, c, tv))),
                pl.BlockSpec((None, d, MOE_FC), lambda i, c, te, tv, ts: (te[i], 0, nfc + ceff(i, c, tv))),
                pl.BlockSpec((None, MOE_FC, d), lambda i, c, te, tv, ts: (te[i], ceff(i, c, tv), 0)),
            ],
            out_specs=pl.BlockSpec((blk, LANES), lambda i, c, te, tv, ts: (i, 0)),
            scratch_shapes=[pltpu.VMEM((MOE_TILE, d), BF16), pltpu.VMEM((MOE_TILE, d), F32)]),
        compiler_params=_cparams(("arbitrary", "arbitrary")),
        name="moe_experts",
    )(tile_expert, tile_valid, tile_src, xs2d, w_gu, w_gu, w_down)


def _combine_kernel(dest_ref, x_ref, ys_ref, gate_ref, g_ref, o_ref, ybuf_ref, sem):
    i = pl.program_id(0)
    n = pl.num_programs(0)
    tm, d = x_ref.shape
    n_sub = d // LANES

    def gather(tile, slot):
        def body(r, carry):
            base = 2 * (tile * tm + r)
            for k in range(2):
                src = ys_ref.at[pl.ds(pl.multiple_of(dest_ref[base + k] * n_sub, n_sub), n_sub)]
                dst = ybuf_ref.at[slot, pl.ds(pl.multiple_of((2 * r + k) * n_sub, n_sub), n_sub)]
                pltpu.make_async_copy(src, dst, sem.at[slot]).start()
            return carry

        lax.fori_loop(0, tm, body, 0, unroll=8)

    @pl.when(i == 0)
    def _():
        gather(0, 0)

    @pl.when(i + 1 < n)
    def _():
        gather(i + 1, (i + 1) % 2)

    slot = i % 2
    pltpu.make_async_copy(ys_ref.at[pl.ds(0, 2 * tm * n_sub)], ybuf_ref.at[slot], sem.at[slot]).wait()

    y_ref = ybuf_ref.at[slot]
    gates = gate_ref[...]
    w1 = gates[:, 0:1]
    w2 = gates[:, 1:2]
    for s in range(n_sub):
        cols = slice(s * LANES, (s + 1) * LANES)
        y1 = y_ref[pl.ds(s, tm, stride=2 * n_sub), :]
        y2 = y_ref[pl.ds(n_sub + s, tm, stride=2 * n_sub), :]
        o_ref[:, cols] = x_ref[:, cols] + w1 * y1 + w2 * y2
    o_ref[...] = _rms(o_ref[...], g_ref[...])


def _combine(x2d, ys2d, dest, gates, gain):
    t, d = x2d.shape
    tm = TOK_TILE
    n_sub = d // LANES
    return pl.pallas_call(
        _combine_kernel,
        out_shape=jax.ShapeDtypeStruct((t, d), F32),
        grid_spec=pltpu.PrefetchScalarGridSpec(
            num_scalar_prefetch=1, grid=(t // tm,),
            in_specs=[pl.BlockSpec((tm, d), lambda i, *_: (i, 0)),
                      pl.BlockSpec(memory_space=pl.ANY),
                      pl.BlockSpec((tm, gates.shape[1]), lambda i, *_: (i, 0)),
                      pl.BlockSpec((1, d), lambda i, *_: (0, 0), pipeline_mode=pl.Buffered(1))],
            out_specs=pl.BlockSpec((tm, d), lambda i, *_: (i, 0)),
            scratch_shapes=[pltpu.VMEM((2, 2 * tm * n_sub, LANES), F32), pltpu.SemaphoreType.DMA((2,))]),
        compiler_params=_cparams(("arbitrary",)),
        name="moe_combine",
    )(dest, x2d, ys2d, gates, gain)


def _rope_tables(seq, rot_dim, theta):
    inv = 1.0 / (theta ** (jnp.arange(0, rot_dim, 2, dtype=F32) / rot_dim))
    ang = jnp.arange(seq, dtype=F32)[:, None] * inv[None, :]
    return jnp.cos(ang), jnp.sin(ang)


def _retention_tables():
    c = RET_CHUNK
    log_gamma = jnp.log(1.0 - 2.0 ** (-5.0 - jnp.arange(RET_HEADS, dtype=F32)))
    idx = jnp.arange(c, dtype=F32)
    rel = idx[:, None] - idx[None, :]
    dmat = jnp.where(rel[None] >= 0, jnp.exp(jnp.maximum(rel, 0.0)[None] * log_gamma[:, None, None]), 0.0)
    qd = jnp.exp((idx + 1.0)[None, :] * log_gamma[:, None])[:, :, None]
    kd = jnp.exp((c - 1.0 - idx)[None, :] * log_gamma[:, None])[:, :, None]
    cd = jnp.exp(c * log_gamma)[:, None, None]
    return dmat, qd, kd, cd


def _attn_rope_tables(seq):
    cos, sin = _rope_tables(seq, ROPE_DIM, ROPE_THETA)
    half = ROPE_DIM // 2
    pad = DIFF_HD - ROPE_DIM
    ones = jnp.ones((seq, pad), F32)
    zeros = jnp.zeros((seq, pad), F32)
    zh = jnp.zeros((seq, half), F32)
    ctab = jnp.concatenate([cos, cos, ones], axis=1)
    s1tab = jnp.concatenate([-sin, zh, zeros], axis=1)
    s2tab = jnp.concatenate([zh, sin, zeros], axis=1)
    rep = LANES // DIFF_HD
    return tuple(jnp.tile(tb, (1, rep)) for tb in (ctab, s1tab, s2tab))


def kernel(x, ln_mix, ln_ffn, ret_w_in, ret_w_o, kv_norm, w_kv, diff_w_q, lam_q1, lam_k1, lam_q2, lam_k2,
           diff_subln, diff_w_o, ffn_w_gu, ffn_w_down, moe_router, moe_w_gu, moe_w_down, final_norm):
    batch, seq, d = x.shape
    t = batch * seq
    assert ln_mix.shape[0] == 2 and ret_w_in.shape[0] == 1 and diff_w_q.shape[0] == 1
    assert seq % TOK_TILE == 0 and TOK_TILE % RET_CHUNK == 0 and ATT_TQ % CHUNK == 0
    assert seq % ATT_TK == 0 and ATT_TK % ATT_TQ == 0
    x2d = x.reshape(t, d)
    row = lambda g: g.reshape(1, -1)

    cos_r, sin_r = _rope_tables(seq, RET_QK, RET_THETA)
    proj = _ret_in_proj(x2d, row(ln_mix[0]), ret_w_in[0].astype(BF16), cos_r, sin_r, seq)
    ret_o = _retention(proj, *_retention_tables(), batch, seq)
    x2 = _ret_out_ffn(x2d, ret_o, ret_w_o[0].astype(BF16), row(ln_ffn[0]),
                      ffn_w_gu[0].astype(BF16), ffn_w_down[0].astype(BF16))

    lambda_init = 0.8 - 0.6 * math.exp(-0.3 * 1)
    q, k, vt = _qkv_proj(x2, row(ln_mix[1]), row(kv_norm), diff_w_q[0].astype(BF16),
                         w_kv[:, :d].astype(BF16), w_kv[:, d:].T.astype(BF16),
                         *_attn_rope_tables(seq), batch, seq)
    lam_vecs = jnp.stack([lam_q1[0], lam_k1[0], lam_q2[0], lam_k2[0]]).astype(F32)
    att = _diff_attention(q, k, vt, lam_vecs, row(diff_subln[0]), batch, seq, lambda_init)
    router_pad = jnp.pad(moe_router[0], ((0, 0), (0, LANES - N_EXPERTS)))
    router_hi = router_pad.astype(BF16)
    router_lo = (router_pad - router_hi.astype(F32)).astype(BF16)
    tri = (jnp.arange(TOK_TILE)[:, None] < jnp.arange(TOK_TILE)[None, :]).astype(BF16)
    x3, h3_rows, route, cnt = _attn_out(x2, att, diff_w_o[0].astype(BF16), row(ln_ffn[1]),
                                        jnp.concatenate([router_hi, router_lo], axis=1), tri)

    n_sub = d // LANES
    n_rows = 2 * t + N_EXPERTS * MOE_TILE
    n_tiles = n_rows // MOE_TILE
    expert = route[0:2].T.astype(jnp.int32)
    rank = route[2:4].T.astype(jnp.int32)
    gates = route[4:6].T
    counts = cnt[:N_EXPERTS, 0].astype(jnp.int32)
    padded = (counts + MOE_TILE - 1) // MOE_TILE * MOE_TILE
    seg_end = jnp.cumsum(padded)
    seg_start = seg_end - padded
    dest = (jnp.sum(jnp.where(expert[..., None] == jnp.arange(N_EXPERTS), seg_start, 0), axis=-1)
            + rank).reshape(-1)
    tile_row = jnp.arange(n_tiles, dtype=jnp.int32) * MOE_TILE
    tile_valid = (tile_row < seg_end[-1]).astype(jnp.int32)
    n_valid = seg_end[-1] // MOE_TILE
    tile_src = jnp.minimum(jnp.arange(n_tiles, dtype=jnp.int32), jnp.maximum(n_valid - 1, 0))
    tile_expert = jnp.minimum(jnp.sum(tile_src[:, None] * MOE_TILE >= seg_end[None, :], axis=1),
                              N_EXPERTS - 1).astype(jnp.int32)
    zflag = jnp.concatenate([(padded > 0).astype(jnp.int32), 1 - tile_valid[-N_EXPERTS:]])
    zpos = jnp.concatenate([jnp.maximum(seg_end - MOE_TILE, 0), tile_row[-N_EXPERTS:]]).astype(jnp.int32)

    xs = _dispatch(h3_rows.reshape(t, n_sub, LANES), dest, zpos, zflag, n_rows)
    ys = _moe_experts(xs.reshape(n_rows * n_sub, LANES), moe_w_gu[0].astype(BF16),
                      moe_w_down[0].astype(BF16), tile_expert, tile_valid, tile_src, d)
    out = _combine(x3, ys, dest, gates, row(final_norm))
    return out.reshape(batch, seq, d)
```

```python
import functools
import math

import jax
import jax.numpy as jnp
from jax import lax
from jax.experimental import pallas as pl
from jax.experimental.pallas import tpu as pltpu

F32 = jnp.float32
BF16 = jnp.bfloat16

EPS = 1e-6
CHUNK = 64
RET_QK = 256
RET_V = 512
RET_HEADS = 4
RET_THETA = 10000.0
DIFF_HD = 64
DIFF_HEADS = 8
ROPE_THETA = 500000.0
ROPE_DIM = 16
N_EXPERTS = 8

LANES = 128
SUBLANES = 8
MXU_COLS = 256
VMEM_LIMIT = 56 * 1024 * 1024

RET_CHUNK = 256
TOK_TILE = 512
ATT_TQ = 512
ATT_TK = 512
ATT_HEADS = 4
ATT_ONES = 16
MOE_TILE = 1024
MOE_SPLIT = 2
MOE_FC = 512


def _cparams(sem, vmem=VMEM_LIMIT):
    return pltpu.CompilerParams(dimension_semantics=sem, vmem_limit_bytes=vmem)


def _const_spec(shape):
    nd = len(shape)
    return pl.BlockSpec(shape, lambda *_: (0,) * nd, pipeline_mode=pl.Buffered(1))


def _rms(x, g):
    return x * lax.rsqrt(jnp.mean(x * x, axis=-1, keepdims=True) + EPS) * g


def _dot(a, b):
    return jnp.dot(a, b, preferred_element_type=F32)


def _ret_in_kernel(x_ref, g_ref, w_ref, cos_ref, sin_ref, o_ref):
    h = _rms(x_ref[...], g_ref[...]).astype(BF16)
    cos = cos_ref[...]
    sin = sin_ref[...]
    d_qk = RET_HEADS * RET_QK
    half = RET_QK // 2
    for c in range(2 * RET_HEADS):
        c0 = c * RET_QK
        acc = _dot(h, w_ref[:, c0:c0 + RET_QK])
        x1 = acc[:, :half]
        x2 = acc[:, half:]
        scale = 1.0 if c < RET_HEADS else RET_QK ** -0.5
        o_ref[:, c0:c0 + half] = ((x1 * cos - x2 * sin) * scale).astype(BF16)
        o_ref[:, c0 + half:c0 + RET_QK] = ((x2 * cos + x1 * sin) * scale).astype(BF16)
    n_rest = (w_ref.shape[1] - 2 * d_qk) // RET_V
    for c in range(n_rest):
        c0 = 2 * d_qk + c * RET_V
        o_ref[:, c0:c0 + RET_V] = _dot(h, w_ref[:, c0:c0 + RET_V]).astype(BF16)


def _ret_in_proj(x2d, gain, w, cos, sin, seq):
    t, d = x2d.shape
    n = w.shape[1]
    tm = TOK_TILE
    n_pos = seq // tm
    return pl.pallas_call(
        _ret_in_kernel,
        out_shape=jax.ShapeDtypeStruct((t, n), BF16),
        grid=(t // tm,),
        in_specs=[
            pl.BlockSpec((tm, d), lambda i: (i, 0)),
            _const_spec((1, d)),
            _const_spec((d, n)),
            pl.BlockSpec((tm, RET_QK // 2), lambda i: (i % n_pos, 0)),
            pl.BlockSpec((tm, RET_QK // 2), lambda i: (i % n_pos, 0)),
        ],
        out_specs=pl.BlockSpec((tm, n), lambda i: (i, 0)),
        compiler_params=_cparams(("parallel",)),
        name="ret_in_proj",
    )(x2d, gain, w, cos, sin)


def _retention_kernel(q_ref, k_ref, v_ref, g_ref, dmat_ref, qd_ref, kd_ref, cd_ref, o_ref, state_ref):
    @pl.when(pl.program_id(1) == 0)
    def _():
        state_ref[...] = jnp.zeros_like(state_ref)

    n_chunks = q_ref.shape[0] // RET_CHUNK
    for h in range(RET_HEADS):
        for c in range(n_chunks):
            rows = slice(c * RET_CHUNK, (c + 1) * RET_CHUNK)
            q = q_ref[rows, h * RET_QK:(h + 1) * RET_QK]
            k = k_ref[rows, h * RET_QK:(h + 1) * RET_QK]
            v = v_ref[rows, h * RET_V:(h + 1) * RET_V]
            state = state_ref[h]
            s = lax.dot_general(q, k, (((1,), (1,)), ((), ())), preferred_element_type=F32)
            s = s * dmat_ref[h]
            o = _dot(s.astype(BF16), v)
            qs = (q.astype(F32) * qd_ref[h]).astype(BF16)
            o = o + _dot(qs, state.astype(BF16))
            ks = (k.astype(F32) * kd_ref[h]).astype(BF16)
            state_ref[h] = state * cd_ref[h] + lax.dot_general(
                ks, v, (((0,), (0,)), ((), ())), preferred_element_type=F32)
            mu = jnp.mean(o, axis=-1, keepdims=True)
            oc = o - mu
            var = jnp.mean(oc * oc, axis=-1, keepdims=True)
            on = oc * lax.rsqrt(var + EPS)
            gate = g_ref[rows, h * RET_V:(h + 1) * RET_V].astype(F32)
            o_ref[rows, h * RET_V:(h + 1) * RET_V] = (on * (gate * jax.nn.sigmoid(gate))).astype(BF16)


def _retention(proj, dmat, qd, kd, cd, batch, seq):
    t = proj.shape[0]
    tb = TOK_TILE
    nj = seq // tb
    d_qk = RET_HEADS * RET_QK
    d_v = RET_HEADS * RET_V
    row = lambda b, j: b * nj + j
    return pl.pallas_call(
        _retention_kernel,
        out_shape=jax.ShapeDtypeStruct((t, d_v), BF16),
        grid=(batch, nj),
        in_specs=[
            pl.BlockSpec((tb, d_qk), lambda b, j: (row(b, j), 0)),
            pl.BlockSpec((tb, d_qk), lambda b, j: (row(b, j), 1)),
            pl.BlockSpec((tb, d_v), lambda b, j: (row(b, j), 1)),
            pl.BlockSpec((tb, d_v), lambda b, j: (row(b, j), 2)),
            _const_spec(dmat.shape),
            _const_spec(qd.shape),
            _const_spec(kd.shape),
            _const_spec(cd.shape),
        ],
        out_specs=pl.BlockSpec((tb, d_v), lambda b, j: (row(b, j), 0)),
        scratch_shapes=[pltpu.VMEM((RET_HEADS, RET_QK, RET_V), F32)],
        compiler_params=_cparams(("parallel", "arbitrary")),
        name="retention",
    )(proj, proj, proj, proj, dmat, qd, kd, cd)


def _ret_out_ffn_kernel(x_ref, o_ref, wo_ref, g_ref, wgu_ref, wd_ref, out_ref, act_ref, *, fc):
    x1 = x_ref[...] + _dot(o_ref[...], wo_ref[...])
    h = _rms(x1, g_ref[...]).astype(BF16)
    f = wd_ref.shape[0]
    for c in range(f // fc):
        gt = _dot(h, wgu_ref[:, c * fc:(c + 1) * fc])
        up = _dot(h, wgu_ref[:, f + c * fc:f + (c + 1) * fc])
        act_ref[:, c * fc:(c + 1) * fc] = (gt * jax.nn.sigmoid(gt) * up).astype(BF16)
    out_ref[...] = x1 + _dot(act_ref[...], wd_ref[...])


def _ret_out_ffn(x2d, o, w_o, gain, w_gu, w_down):
    t, d = x2d.shape
    tm = TOK_TILE
    f = w_down.shape[0]
    return pl.pallas_call(
        functools.partial(_ret_out_ffn_kernel, fc=256),
        out_shape=jax.ShapeDtypeStruct((t, d), F32),
        grid=(t // tm,),
        in_specs=[
            pl.BlockSpec((tm, d), lambda i: (i, 0)),
            pl.BlockSpec((tm, o.shape[1]), lambda i: (i, 0)),
            _const_spec(w_o.shape),
            _const_spec((1, d)),
            _const_spec(w_gu.shape),
            _const_spec(w_down.shape),
        ],
        out_specs=pl.BlockSpec((tm, d), lambda i: (i, 0)),
        scratch_shapes=[pltpu.VMEM((tm, f), BF16)],
        compiler_params=_cparams(("parallel",)),
        name="ret_out_ffn",
    )(x2d, o, w_o, gain, w_gu, w_down)


def _rope16(x, ctab, s1tab, s2tab):
    half = ROPE_DIM // 2
    return (x * ctab + pltpu.roll(x, LANES - half, 1) * s1tab + pltpu.roll(x, half, 1) * s2tab)


def _qkv_kernel(x_ref, gq_ref, gkv_ref, wq_ref, wk_ref, wvt_ref, c_ref, s1_ref, s2_ref, q_ref, k_ref, vt_ref):
    x = x_ref[...]
    xn = x * lax.rsqrt(jnp.mean(x * x, axis=-1, keepdims=True) + EPS)
    hq = (xn * gq_ref[...]).astype(BF16)
    hkv = (xn * gkv_ref[...]).astype(BF16)
    ctab, s1tab, s2tab = c_ref[...], s1_ref[...], s2_ref[...]
    d = q_ref.shape[1]
    q_scale = DIFF_HD ** -0.5 * math.log2(math.e)
    for c in range(d // MXU_COLS):
        c0 = c * MXU_COLS
        qa = _dot(hq, wq_ref[:, c0:c0 + MXU_COLS])
        ka = _dot(hkv, wk_ref[:, c0:c0 + MXU_COLS])
        for l0 in range(0, MXU_COLS, LANES):
            cols = slice(c0 + l0, c0 + l0 + LANES)
            q_ref[:, cols] = (_rope16(qa[:, l0:l0 + LANES], ctab, s1tab, s2tab) * q_scale).astype(BF16)
            k_ref[:, cols] = _rope16(ka[:, l0:l0 + LANES], ctab, s1tab, s2tab).astype(BF16)
    vt_ref[...] = lax.dot_general(wvt_ref[...], hkv, (((1,), (1,)), ((), ())),
                                  preferred_element_type=F32).astype(BF16)


def _qkv_proj(x2d, gq, gkv, w_q, w_k, w_vt, ctab, s1tab, s2tab, batch, seq):
    t, d = x2d.shape
    tm = TOK_TILE
    n_pos = seq // tm
    tok = pl.BlockSpec((tm, d), lambda i: (i, 0))
    tab = pl.BlockSpec((tm, LANES), lambda i: (i % n_pos, 0))
    return pl.pallas_call(
        _qkv_kernel,
        out_shape=[jax.ShapeDtypeStruct((t, d), BF16), jax.ShapeDtypeStruct((t, d), BF16),
                   jax.ShapeDtypeStruct((batch * d, seq), BF16)],
        grid=(t // tm,),
        in_specs=[tok, _const_spec((1, d)), _const_spec((1, d)), _const_spec(w_q.shape),
                  _const_spec(w_k.shape), _const_spec(w_vt.shape), tab, tab, tab],
        out_specs=[tok, tok, pl.BlockSpec((d, tm), lambda i: (i // n_pos, i % n_pos))],
        compiler_params=_cparams(("parallel",)),
        name="qkv_proj",
    )(x2d, gq, gkv, w_q, w_k, w_vt, ctab, s1tab, s2tab)


def _attn_kernel(lam_ref, bias_ref, q_ref, k_ref, vt_ref, sub_ref, o_ref, vta_ref, s_ref, acc_ref, *,
                 lambda_init):
    i = pl.program_id(2)
    tq = q_ref.shape[0]
    dv = 2 * DIFF_HD
    n_heads, n_kv, _, tk = vta_ref.shape
    chains = [(h, c) for h in range(n_heads) for c in range(2)]

    @pl.when(i == 0)
    def _():
        for h in range(n_heads):
            for j in range(n_kv):
                vta_ref[h, j, 0:dv, :] = vt_ref[h * dv:(h + 1) * dv, j * tk:(j + 1) * tk]
                vta_ref[h, j, dv:, :] = jnp.ones((ATT_ONES, tk), BF16)

    lane = lax.broadcasted_iota(jnp.int32, (tq, dv), 1)
    qc = []
    for h in range(n_heads):
        q = q_ref[:, h * dv:(h + 1) * dv]
        zero = jnp.zeros_like(q)
        qc.append((jnp.where(lane < DIFF_HD, q, zero), jnp.where(lane >= DIFF_HD, q, zero)))
    n_full = (i * tq) // tk
    tail_bias = 1 + (i * tq - n_full * tk) // tq

    def tile_max(s):
        m8 = jnp.max(s.reshape(tk // SUBLANES, SUBLANES, tq), axis=0)
        return jnp.max(m8, axis=0, keepdims=True)

    def score_chain(j, x):
        h, c = chains[x]
        bias = bias_ref[jnp.where(j == n_full, tail_bias, 0)]
        kt = k_ref[pl.ds(pl.multiple_of(j * tk, tk), tk), h * dv:(h + 1) * dv]
        s = lax.dot_general(kt, qc[h][c], (((1,), (1,)), ((), ())), preferred_element_type=F32) + bias
        s_ref[x] = s
        return tile_max(s)

    acc_ref[...] = jnp.zeros_like(acc_ref)
    m_init = tuple(jnp.full((1, tq), -jnp.inf, F32) for _ in chains)

    def step(j, carry, with_next):
        m_run, m_tile = carry
        m_out, m_next = [], []
        for x, (h, c) in enumerate(chains):
            m_new = jnp.maximum(m_run[x], m_tile[x])
            alpha = jnp.exp2(m_run[x] - m_new)
            p = jnp.exp2(s_ref[x] - m_new).astype(BF16)
            acc_ref[x] = alpha * acc_ref[x] + _dot(vta_ref[h, j], p)
            m_out.append(m_new)
            if with_next:
                m_next.append(score_chain(j + 1, x))
        return tuple(m_out), tuple(m_next)

    first = tuple(score_chain(0, x) for x in range(len(chains)))
    carry = lax.fori_loop(0, n_full, lambda j, carry: step(j, carry, True), (m_init, first))
    step(n_full, carry, False)

    lam_v = lam_ref[...]
    lam = (jnp.exp(jnp.sum(lam_v[0:1] * lam_v[1:2], axis=-1, keepdims=True))
           - jnp.exp(jnp.sum(lam_v[2:3] * lam_v[3:4], axis=-1, keepdims=True)) + lambda_init)
    for h in range(n_heads):
        a0 = acc_ref[2 * h]
        a1 = acc_ref[2 * h + 1]
        ot = a0[0:dv] / a0[dv:dv + 1] - lam * (a1[0:dv] / a1[dv:dv + 1])
        o = _rms(ot.T, sub_ref[...]) * (1.0 - lambda_init)
        o_ref[:, h * dv:(h + 1) * dv] = o.astype(BF16)


def _attn_bias(tq, tk):
    key = jnp.arange(tk)[:, None] // CHUNK
    tiles = [jnp.zeros((tk, tq), F32)]
    for r in range(tk // tq):
        qry = (r * tq + jnp.arange(tq))[None, :] // CHUNK
        tiles.append(jnp.where(key <= qry, 0.0, -1e30).astype(F32))
    return jnp.stack(tiles)


def _diff_attention(q, k, vt, lam_vecs, subln, batch, seq, lambda_init):
    t, d = q.shape
    tq = ATT_TQ
    nq = seq // tq
    dv = 2 * DIFF_HD
    hp = ATT_HEADS
    n_groups = DIFF_HEADS // hp
    bias = _attn_bias(tq, ATT_TK)
    return pl.pallas_call(
        functools.partial(_attn_kernel, lambda_init=lambda_init),
        out_shape=jax.ShapeDtypeStruct((t, d), BF16),
        grid=(batch, n_groups, nq),
        in_specs=[
            _const_spec(lam_vecs.shape),
            _const_spec(bias.shape),
            pl.BlockSpec((tq, hp * dv), lambda b, g, i: (b * nq + i, g)),
            pl.BlockSpec((seq, hp * dv), lambda b, g, i: (b, g)),
            pl.BlockSpec((hp * dv, seq), lambda b, g, i: (b * n_groups + g, 0)),
            _const_spec((1, dv)),
        ],
        out_specs=pl.BlockSpec((tq, hp * dv), lambda b, g, i: (b * nq + i, g)),
        scratch_shapes=[pltpu.VMEM((hp, seq // ATT_TK, dv + ATT_ONES, ATT_TK), BF16),
                        pltpu.VMEM((2 * hp, ATT_TK, tq), F32),
                        pltpu.VMEM((2 * hp, dv + ATT_ONES, tq), F32)],
        compiler_params=_cparams(("parallel", "parallel", "arbitrary")),
        name="diff_attn",
    )(lam_vecs, bias, q, k, vt, subln)


ROUTE_ROWS = 2 * SUBLANES


def _attn_out_kernel(x_ref, o_ref, wo_ref, g_ref, r_ref, tri_ref, x3_ref, h3_ref, route_ref, cnt_ref, carry_ref):
    @pl.when(pl.program_id(0) == 0)
    def _():
        carry_ref[...] = jnp.zeros_like(carry_ref)

    tm = x_ref.shape[0]
    x3 = x_ref[...] + _dot(o_ref[...], wo_ref[...])
    x3_ref[...] = x3
    h3 = _rms(x3, g_ref[...])
    for s in range(h3.shape[1] // LANES):
        h3_ref[pl.ds(s, tm, stride=SUBLANES), :] = h3[:, s * LANES:(s + 1) * LANES]

    hi = h3.astype(BF16)
    lo = (h3 - hi.astype(F32)).astype(BF16)
    both = _dot(hi, r_ref[...])
    logits = both[:, :LANES] + both[:, LANES:] + _dot(lo, r_ref[:, :LANES])
    lt = logits.T[0:ROUTE_ROWS]
    sub = lax.broadcasted_iota(jnp.int32, lt.shape, 0).astype(F32)
    lt = jnp.where(sub < N_EXPERTS, lt, -jnp.inf)
    v1 = jnp.max(lt, axis=0, keepdims=True)
    i1 = jnp.min(jnp.where(lt == v1, sub, float(ROUTE_ROWS)), axis=0, keepdims=True)
    lt2 = jnp.where(sub == i1, -jnp.inf, lt)
    v2 = jnp.max(lt2, axis=0, keepdims=True)
    i2 = jnp.min(jnp.where(lt2 == v2, sub, float(ROUTE_ROWS)), axis=0, keepdims=True)
    e = jnp.exp(v2 - v1)
    w1 = 1.0 / (1.0 + e)
    w2 = e / (1.0 + e)
    oh1 = sub == i1
    oh2 = sub == i2
    assign = jnp.where(oh1 | oh2, 1.0, 0.0)
    excl = _dot(assign.astype(BF16), tri_ref[...]) + carry_ref[:, 0:1]
    r1 = jnp.sum(jnp.where(oh1, excl, 0.0), axis=0, keepdims=True)
    r2 = jnp.sum(jnp.where(oh2, excl, 0.0), axis=0, keepdims=True)
    route = jnp.zeros_like(lt)
    for row, val in enumerate((i1, i2, r1, r2, w1, w2)):
        route = jnp.where(sub == float(row), val, route)
    route_ref[...] = route[0:SUBLANES]
    total = carry_ref[:, 0:1] + jnp.sum(assign, axis=1, keepdims=True)
    carry_ref[...] = jnp.broadcast_to(total, carry_ref.shape)
    cnt_ref[...] = jnp.broadcast_to(total, cnt_ref.shape)


def _attn_out(x2d, o, w_o, gain, router_split, tri):
    t, d = x2d.shape
    tm = TOK_TILE
    n_sub = d // LANES
    return pl.pallas_call(
        _attn_out_kernel,
        out_shape=[jax.ShapeDtypeStruct((t, d), F32),
                   jax.ShapeDtypeStruct((t * n_sub, LANES), F32),
                   jax.ShapeDtypeStruct((SUBLANES, t), F32),
                   jax.ShapeDtypeStruct((ROUTE_ROWS, LANES), F32)],
        grid=(t // tm,),
        in_specs=[pl.BlockSpec((tm, d), lambda i: (i, 0)),
                  pl.BlockSpec((tm, d), lambda i: (i, 0)),
                  _const_spec(w_o.shape), _const_spec((1, d)), _const_spec(router_split.shape),
                  _const_spec(tri.shape)],
        out_specs=[pl.BlockSpec((tm, d), lambda i: (i, 0)),
                   pl.BlockSpec((tm * n_sub, LANES), lambda i: (i, 0)),
                   pl.BlockSpec((SUBLANES, tm), lambda i: (0, i)),
                   _const_spec((ROUTE_ROWS, LANES))],
        scratch_shapes=[pltpu.VMEM((ROUTE_ROWS, LANES), F32)],
        compiler_params=_cparams(("arbitrary",)),
        name="attn_out_route",
    )(x2d, o, w_o, gain, router_split, tri)


def _dispatch_kernel(dest_ref, zpos_ref, zflag_ref, h_ref, dst_ref, zbuf_ref, zsem, sem):
    i = pl.program_id(0)
    tm = h_ref.shape[0]

    @pl.when(i == 0)
    def _():
        zbuf_ref[...] = jnp.zeros_like(zbuf_ref)

        def zcopy(e):
            return pltpu.make_async_copy(zbuf_ref, dst_ref.at[pl.ds(zpos_ref[e], MOE_TILE)], zsem.at[e])

        for e in range(2 * N_EXPERTS):
            @pl.when(zflag_ref[e] == 1)
            def _():
                zcopy(e).start()
        for e in range(2 * N_EXPERTS):
            @pl.when(zflag_ref[e] == 1)
            def _():
                zcopy(e).wait()

    def body(r, carry):
        base = 2 * (i * tm + r)
        for k in range(2):
            pltpu.make_async_copy(h_ref.at[r], dst_ref.at[dest_ref[base + k]], sem).start(priority=k)
        return carry

    lax.fori_loop(0, tm, body, 0, unroll=8)
    for k in range(2):
        pltpu.make_async_copy(h_ref, dst_ref.at[pl.ds(0, tm)], sem).wait()


def _dispatch(h3_rows, dest, zpos, zflag, n_dst):
    t = h3_rows.shape[0]
    tm = TOK_TILE
    tail = h3_rows.shape[1:]
    return pl.pallas_call(
        _dispatch_kernel,
        out_shape=jax.ShapeDtypeStruct((n_dst,) + tail, F32),
        grid_spec=pltpu.PrefetchScalarGridSpec(
            num_scalar_prefetch=3, grid=(t // tm,),
            in_specs=[pl.BlockSpec((tm,) + tail, lambda i, *_: (i, 0, 0))],
            out_specs=pl.BlockSpec(memory_space=pl.ANY),
            scratch_shapes=[pltpu.VMEM((MOE_TILE,) + tail, F32),
                            pltpu.SemaphoreType.DMA((2 * N_EXPERTS,)), pltpu.SemaphoreType.DMA(())]),
        compiler_params=_cparams(("arbitrary",)),
        name="moe_dispatch",
    )(dest, zpos, zflag, h3_rows)


def _moe_kernel(te_ref, tr_ref, ts_ref, xs_ref, wg_ref, wu_ref, wd_ref, ys_ref, h_ref, acc_ref):
    i = pl.program_id(0)
    c = pl.program_id(1)
    last = pl.num_programs(1) - 1
    n_sub = h_ref.shape[1] // LANES
    grp = h_ref.shape[0] // MOE_SPLIT

    for part in range(MOE_SPLIT):
        r0 = part * grp
        rows = slice(r0, r0 + grp)
        live = tr_ref[i] > r0

        @pl.when(live & (c == 0))
        def _():
            for s in range(n_sub):
                h_ref[rows, s * LANES:(s + 1) * LANES] = (
                    xs_ref[pl.ds(r0 * n_sub + s, grp, stride=n_sub), :].astype(BF16))
            acc_ref[rows, :] = jnp.zeros((grp, acc_ref.shape[1]), F32)

        @pl.when(live)
        def _():
            h = h_ref[rows, :]
            gt = _dot(h, wg_ref[...].astype(BF16))
            up = _dot(h, wu_ref[...].astype(BF16))
            act = (gt * jax.nn.sigmoid(gt) * up).astype(BF16)
            acc_ref[rows, :] += _dot(act, wd_ref[...].astype(BF16))

        @pl.when(live & (c == last))
        def _():
            for s in range(n_sub):
                ys_ref[pl.ds(r0 * n_sub + s, grp, stride=n_sub), :] = acc_ref[rows, s * LANES:(s + 1) * LANES]

        @pl.when(jnp.logical_not(live) & (c == last))
        def _():
            ys_ref[r0 * n_sub:(r0 + grp) * n_sub, :] = jnp.zeros((grp * n_sub, LANES), F32)


def _moe_experts(xs2d, w_gu, w_down, tile_expert, tile_rows, tile_src, d):
    n_sub = d // LANES
    n_tiles = xs2d.shape[0] // (MOE_TILE * n_sub)
    f = w_down.shape[1]
    nfc = f // MOE_FC
    blk = MOE_TILE * n_sub

    def ceff(i, c, tv):
        return jnp.where(tv[i] > 0, c, nfc - 1)

    return pl.pallas_call(
        _moe_kernel,
        out_shape=jax.ShapeDtypeStruct(xs2d.shape, F32),
        grid_spec=pltpu.PrefetchScalarGridSpec(
            num_scalar_prefetch=3, grid=(n_tiles, nfc),
            in_specs=[
                pl.BlockSpec((blk, LANES), lambda i, c, te, tv, ts: (ts[i], 0)),
                pl.BlockSpec((None, d, MOE_FC), lambda i, c, te, tv, ts: (te[i], 0, ceff(i, c, tv))),
                pl.BlockSpec((None, d, MOE_FC), lambda i, c, te, tv, ts: (te[i], 0, nfc + ceff(i, c, tv))),
                pl.BlockSpec((None, MOE_FC, d), lambda i, c, te, tv, ts: (te[i], ceff(i, c, tv), 0)),
            ],
            out_specs=pl.BlockSpec((blk, LANES), lambda i, c, te, tv, ts: (i, 0)),
            scratch_shapes=[pltpu.VMEM((MOE_TILE, d), BF16), pltpu.VMEM((MOE_TILE, d), F32)]),
        compiler_params=_cparams(("arbitrary", "arbitrary")),
        name="moe_experts",
    )(tile_expert, tile_rows, tile_src, xs2d, w_gu, w_gu, w_down)


def _combine_kernel(dest_ref, x_ref, ys_ref, gate_ref, g_ref, o_ref, ybuf_ref, sem):
    i = pl.program_id(0)
    n = pl.num_programs(0)
    tm, d = x_ref.shape
    n_sub = d // LANES

    def gather(tile, slot):
        def body(r, carry):
            base = 2 * (tile * tm + r)
            for k in range(2):
                src = ys_ref.at[pl.ds(pl.multiple_of(dest_ref[base + k] * n_sub, n_sub), n_sub)]
                dst = ybuf_ref.at[slot, pl.ds(pl.multiple_of((2 * r + k) * n_sub, n_sub), n_sub)]
                pltpu.make_async_copy(src, dst, sem.at[slot]).start(priority=k)
            return carry

        lax.fori_loop(0, tm, body, 0, unroll=8)

    @pl.when(i == 0)
    def _():
        gather(0, 0)

    @pl.when(i + 1 < n)
    def _():
        gather(i + 1, (i + 1) % 2)

    slot = i % 2
    pltpu.make_async_copy(ys_ref.at[pl.ds(0, 2 * tm * n_sub)], ybuf_ref.at[slot], sem.at[slot]).wait()

    y_ref = ybuf_ref.at[slot]
    gates = gate_ref[...]
    w1 = gates[:, 0:1]
    w2 = gates[:, 1:2]
    for s in range(n_sub):
        cols = slice(s * LANES, (s + 1) * LANES)
        y1 = y_ref[pl.ds(s, tm, stride=2 * n_sub), :]
        y2 = y_ref[pl.ds(n_sub + s, tm, stride=2 * n_sub), :]
        o_ref[:, cols] = x_ref[:, cols] + w1 * y1 + w2 * y2
    o_ref[...] = _rms(o_ref[...], g_ref[...])


def _combine(x2d, ys2d, dest, gates, gain):
    t, d = x2d.shape
    tm = TOK_TILE
    n_sub = d // LANES
    return pl.pallas_call(
        _combine_kernel,
        out_shape=jax.ShapeDtypeStruct((t, d), F32),
        grid_spec=pltpu.PrefetchScalarGridSpec(
            num_scalar_prefetch=1, grid=(t // tm,),
            in_specs=[pl.BlockSpec((tm, d), lambda i, *_: (i, 0)),
                      pl.BlockSpec(memory_space=pl.ANY),
                      pl.BlockSpec((tm, gates.shape[1]), lambda i, *_: (i, 0)),
                      pl.BlockSpec((1, d), lambda i, *_: (0, 0), pipeline_mode=pl.Buffered(1))],
            out_specs=pl.BlockSpec((tm, d), lambda i, *_: (i, 0)),
            scratch_shapes=[pltpu.VMEM((2, 2 * tm * n_sub, LANES), F32), pltpu.SemaphoreType.DMA((2,))]),
        compiler_params=_cparams(("arbitrary",)),
        name="moe_combine",
    )(dest, x2d, ys2d, gates, gain)


def _rope_tables(seq, rot_dim, theta):
    inv = 1.0 / (theta ** (jnp.arange(0, rot_dim, 2, dtype=F32) / rot_dim))
    ang = jnp.arange(seq, dtype=F32)[:, None] * inv[None, :]
    return jnp.cos(ang), jnp.sin(ang)


def _retention_tables():
    c = RET_CHUNK
    log_gamma = jnp.log(1.0 - 2.0 ** (-5.0 - jnp.arange(RET_HEADS, dtype=F32)))
    idx = jnp.arange(c, dtype=F32)
    rel = idx[:, None] - idx[None, :]
    dmat = jnp.where(rel[None] >= 0, jnp.exp(jnp.maximum(rel, 0.0)[None] * log_gamma[:, None, None]), 0.0)
    qd = jnp.exp((idx + 1.0)[None, :] * log_gamma[:, None])[:, :, None]
    kd = jnp.exp((c - 1.0 - idx)[None, :] * log_gamma[:, None])[:, :, None]
    cd = jnp.exp(c * log_gamma)[:, None, None]
    return dmat, qd, kd, cd


def _attn_rope_tables(seq):
    cos, sin = _rope_tables(seq, ROPE_DIM, ROPE_THETA)
    half = ROPE_DIM // 2
    pad = DIFF_HD - ROPE_DIM
    ones = jnp.ones((seq, pad), F32)
    zeros = jnp.zeros((seq, pad), F32)
    zh = jnp.zeros((seq, half), F32)
    ctab = jnp.concatenate([cos, cos, ones], axis=1)
    s1tab = jnp.concatenate([-sin, zh, zeros], axis=1)
    s2tab = jnp.concatenate([zh, sin, zeros], axis=1)
    rep = LANES // DIFF_HD
    return tuple(jnp.tile(tb, (1, rep)) for tb in (ctab, s1tab, s2tab))


def kernel(x, ln_mix, ln_ffn, ret_w_in, ret_w_o, kv_norm, w_kv, diff_w_q, lam_q1, lam_k1, lam_q2, lam_k2,
           diff_subln, diff_w_o, ffn_w_gu, ffn_w_down, moe_router, moe_w_gu, moe_w_down, final_norm):
    batch, seq, d = x.shape
    t = batch * seq
    assert ln_mix.shape[0] == 2 and ret_w_in.shape[0] == 1 and diff_w_q.shape[0] == 1
    assert seq % TOK_TILE == 0 and TOK_TILE % RET_CHUNK == 0 and ATT_TQ % CHUNK == 0
    assert seq % ATT_TK == 0 and ATT_TK % ATT_TQ == 0
    x2d = x.reshape(t, d)
    row = lambda g: g.reshape(1, -1)

    cos_r, sin_r = _rope_tables(seq, RET_QK, RET_THETA)
    proj = _ret_in_proj(x2d, row(ln_mix[0]), ret_w_in[0].astype(BF16), cos_r, sin_r, seq)
    ret_o = _retention(proj, *_retention_tables(), batch, seq)
    x2 = _ret_out_ffn(x2d, ret_o, ret_w_o[0].astype(BF16), row(ln_ffn[0]),
                      ffn_w_gu[0].astype(BF16), ffn_w_down[0].astype(BF16))

    lambda_init = 0.8 - 0.6 * math.exp(-0.3 * 1)
    q, k, vt = _qkv_proj(x2, row(ln_mix[1]), row(kv_norm), diff_w_q[0].astype(BF16),
                         w_kv[:, :d].astype(BF16), w_kv[:, d:].T.astype(BF16),
                         *_attn_rope_tables(seq), batch, seq)
    lam_vecs = jnp.stack([lam_q1[0], lam_k1[0], lam_q2[0], lam_k2[0]]).astype(F32)
    att = _diff_attention(q, k, vt, lam_vecs, row(diff_subln[0]), batch, seq, lambda_init)
    router_pad = jnp.pad(moe_router[0], ((0, 0), (0, LANES - N_EXPERTS)))
    router_hi = router_pad.astype(BF16)
    router_lo = (router_pad - router_hi.astype(F32)).astype(BF16)
    tri = (jnp.arange(TOK_TILE)[:, None] < jnp.arange(TOK_TILE)[None, :]).astype(BF16)
    x3, h3_rows, route, cnt = _attn_out(x2, att, diff_w_o[0].astype(BF16), row(ln_ffn[1]),
                                        jnp.concatenate([router_hi, router_lo], axis=1), tri)

    n_sub = d // LANES
    n_rows = 2 * t + N_EXPERTS * MOE_TILE
    n_tiles = n_rows // MOE_TILE
    expert = route[0:2].T.astype(jnp.int32)
    rank = route[2:4].T.astype(jnp.int32)
    gates = route[4:6].T
    counts = cnt[:N_EXPERTS, 0].astype(jnp.int32)
    padded = (counts + MOE_TILE - 1) // MOE_TILE * MOE_TILE
    seg_end = jnp.cumsum(padded)
    seg_start = seg_end - padded
    dest = (jnp.sum(jnp.where(expert[..., None] == jnp.arange(N_EXPERTS), seg_start, 0), axis=-1)
            + rank).reshape(-1)
    tile_row = jnp.arange(n_tiles, dtype=jnp.int32) * MOE_TILE
    tile_valid = (tile_row < seg_end[-1]).astype(jnp.int32)
    n_valid = seg_end[-1] // MOE_TILE
    tile_src = jnp.minimum(jnp.arange(n_tiles, dtype=jnp.int32), jnp.maximum(n_valid - 1, 0))
    tile_expert = jnp.minimum(jnp.sum(tile_src[:, None] * MOE_TILE >= seg_end[None, :], axis=1),
                              N_EXPERTS - 1).astype(jnp.int32)
    token_end = jnp.sum(jnp.where(tile_expert[:, None] == jnp.arange(N_EXPERTS), seg_start + counts, 0), axis=1)
    tile_rows = (tile_valid * jnp.clip(token_end - tile_row, 0, MOE_TILE)).astype(jnp.int32)
    zflag = jnp.concatenate([(padded > 0).astype(jnp.int32), 1 - tile_valid[-N_EXPERTS:]])
    zpos = jnp.concatenate([jnp.maximum(seg_end - MOE_TILE, 0), tile_row[-N_EXPERTS:]]).astype(jnp.int32)

    xs = _dispatch(h3_rows.reshape(t, n_sub, LANES), dest, zpos, zflag, n_rows)
    ys = _moe_experts(xs.reshape(n_rows * n_sub, LANES), moe_w_gu[0], moe_w_down[0],
                      tile_expert, tile_rows, tile_src, d)
    out = _combine(x3, ys, dest, gates, row(final_norm))
    return out.reshape(batch, seq, d)
```

```python
import functools
import math

import jax
import jax.numpy as jnp
from jax import lax
from jax.experimental import pallas as pl
from jax.experimental.pallas import tpu as pltpu

F32 = jnp.float32
BF16 = jnp.bfloat16

EPS = 1e-6
CHUNK = 64
RET_QK = 256
RET_V = 512
RET_HEADS = 4
RET_THETA = 10000.0
DIFF_HD = 64
DIFF_HEADS = 8
ROPE_THETA = 500000.0
ROPE_DIM = 16
N_EXPERTS = 8

LANES = 128
SUBLANES = 8
MXU_COLS = 256
VMEM_LIMIT = 56 * 1024 * 1024

RET_CHUNK = 256
TOK_TILE = 512
ATT_TQ = 512
ATT_TK = 512
ATT_HEADS = 4
ATT_ONES = 16
MOE_TILE = 1024
MOE_SPLIT = 2
MOE_FC = 1792


def _cparams(sem, vmem=VMEM_LIMIT):
    return pltpu.CompilerParams(dimension_semantics=sem, vmem_limit_bytes=vmem)


def _const_spec(shape):
    nd = len(shape)
    return pl.BlockSpec(shape, lambda *_: (0,) * nd, pipeline_mode=pl.Buffered(1))


def _rms(x, g):
    return x * lax.rsqrt(jnp.mean(x * x, axis=-1, keepdims=True) + EPS) * g


def _dot(a, b):
    return jnp.dot(a, b, preferred_element_type=F32)


def _ret_in_kernel(x_ref, g_ref, w_ref, cos_ref, sin_ref, o_ref):
    h = _rms(x_ref[...], g_ref[...]).astype(BF16)
    cos = cos_ref[...]
    sin = sin_ref[...]
    d_qk = RET_HEADS * RET_QK
    half = RET_QK // 2
    for c in range(2 * RET_HEADS):
        c0 = c * RET_QK
        acc = _dot(h, w_ref[:, c0:c0 + RET_QK])
        x1 = acc[:, :half]
        x2 = acc[:, half:]
        scale = 1.0 if c < RET_HEADS else RET_QK ** -0.5
        o_ref[:, c0:c0 + half] = ((x1 * cos - x2 * sin) * scale).astype(BF16)
        o_ref[:, c0 + half:c0 + RET_QK] = ((x2 * cos + x1 * sin) * scale).astype(BF16)
    n_rest = (w_ref.shape[1] - 2 * d_qk) // RET_V
    for c in range(n_rest):
        c0 = 2 * d_qk + c * RET_V
        o_ref[:, c0:c0 + RET_V] = _dot(h, w_ref[:, c0:c0 + RET_V]).astype(BF16)


def _ret_in_proj(x2d, gain, w, cos, sin, seq):
    t, d = x2d.shape
    n = w.shape[1]
    tm = TOK_TILE
    n_pos = seq // tm
    return pl.pallas_call(
        _ret_in_kernel,
        out_shape=jax.ShapeDtypeStruct((t, n), BF16),
        grid=(t // tm,),
        in_specs=[
            pl.BlockSpec((tm, d), lambda i: (i, 0)),
            _const_spec((1, d)),
            _const_spec((d, n)),
            pl.BlockSpec((tm, RET_QK // 2), lambda i: (i % n_pos, 0)),
            pl.BlockSpec((tm, RET_QK // 2), lambda i: (i % n_pos, 0)),
        ],
        out_specs=pl.BlockSpec((tm, n), lambda i: (i, 0)),
        compiler_params=_cparams(("parallel",)),
        name="ret_in_proj",
    )(x2d, gain, w, cos, sin)


def _retention_kernel(q_ref, k_ref, v_ref, g_ref, dmat_ref, qd_ref, kd_ref, cd_ref, o_ref, state_ref):
    @pl.when(pl.program_id(1) == 0)
    def _():
        state_ref[...] = jnp.zeros_like(state_ref)

    n_chunks = q_ref.shape[0] // RET_CHUNK
    for h in range(RET_HEADS):
        for c in range(n_chunks):
            rows = slice(c * RET_CHUNK, (c + 1) * RET_CHUNK)
            q = q_ref[rows, h * RET_QK:(h + 1) * RET_QK]
            k = k_ref[rows, h * RET_QK:(h + 1) * RET_QK]
            v = v_ref[rows, h * RET_V:(h + 1) * RET_V]
            state = state_ref[h]
            s = lax.dot_general(q, k, (((1,), (1,)), ((), ())), preferred_element_type=F32)
            s = s * dmat_ref[h]
            o = _dot(s.astype(BF16), v)
            qs = (q.astype(F32) * qd_ref[h]).astype(BF16)
            o = o + _dot(qs, state.astype(BF16))
            ks = (k.astype(F32) * kd_ref[h]).astype(BF16)
            state_ref[h] = state * cd_ref[h] + lax.dot_general(
                ks, v, (((0,), (0,)), ((), ())), preferred_element_type=F32)
            mu = jnp.mean(o, axis=-1, keepdims=True)
            oc = o - mu
            var = jnp.mean(oc * oc, axis=-1, keepdims=True)
            on = oc * lax.rsqrt(var + EPS)
            gate = g_ref[rows, h * RET_V:(h + 1) * RET_V].astype(F32)
            o_ref[rows, h * RET_V:(h + 1) * RET_V] = (on * (gate * jax.nn.sigmoid(gate))).astype(BF16)


def _retention(proj, dmat, qd, kd, cd, batch, seq):
    t = proj.shape[0]
    tb = TOK_TILE
    nj = seq // tb
    d_qk = RET_HEADS * RET_QK
    d_v = RET_HEADS * RET_V
    row = lambda b, j: b * nj + j
    return pl.pallas_call(
        _retention_kernel,
        out_shape=jax.ShapeDtypeStruct((t, d_v), BF16),
        grid=(batch, nj),
        in_specs=[
            pl.BlockSpec((tb, d_qk), lambda b, j: (row(b, j), 0)),
            pl.BlockSpec((tb, d_qk), lambda b, j: (row(b, j), 1)),
            pl.BlockSpec((tb, d_v), lambda b, j: (row(b, j), 1)),
            pl.BlockSpec((tb, d_v), lambda b, j: (row(b, j), 2)),
            _const_spec(dmat.shape),
            _const_spec(qd.shape),
            _const_spec(kd.shape),
            _const_spec(cd.shape),
        ],
        out_specs=pl.BlockSpec((tb, d_v), lambda b, j: (row(b, j), 0)),
        scratch_shapes=[pltpu.VMEM((RET_HEADS, RET_QK, RET_V), F32)],
        compiler_params=_cparams(("parallel", "arbitrary")),
        name="retention",
    )(proj, proj, proj, proj, dmat, qd, kd, cd)


def _ret_out_ffn_kernel(x_ref, o_ref, wo_ref, g_ref, wgu_ref, wd_ref, out_ref, act_ref, *, fc):
    x1 = x_ref[...] + _dot(o_ref[...], wo_ref[...])
    h = _rms(x1, g_ref[...]).astype(BF16)
    f = wd_ref.shape[0]
    for c in range(f // fc):
        gt = _dot(h, wgu_ref[:, c * fc:(c + 1) * fc])
        up = _dot(h, wgu_ref[:, f + c * fc:f + (c + 1) * fc])
        act_ref[:, c * fc:(c + 1) * fc] = (gt * jax.nn.sigmoid(gt) * up).astype(BF16)
    out_ref[...] = x1 + _dot(act_ref[...], wd_ref[...])


def _ret_out_ffn(x2d, o, w_o, gain, w_gu, w_down):
    t, d = x2d.shape
    tm = TOK_TILE
    f = w_down.shape[0]
    return pl.pallas_call(
        functools.partial(_ret_out_ffn_kernel, fc=256),
        out_shape=jax.ShapeDtypeStruct((t, d), F32),
        grid=(t // tm,),
        in_specs=[
            pl.BlockSpec((tm, d), lambda i: (i, 0)),
            pl.BlockSpec((tm, o.shape[1]), lambda i: (i, 0)),
            _const_spec(w_o.shape),
            _const_spec((1, d)),
            _const_spec(w_gu.shape),
            _const_spec(w_down.shape),
        ],
        out_specs=pl.BlockSpec((tm, d), lambda i: (i, 0)),
        scratch_shapes=[pltpu.VMEM((tm, f), BF16)],
        compiler_params=_cparams(("parallel",)),
        name="ret_out_ffn",
    )(x2d, o, w_o, gain, w_gu, w_down)


def _rope16(x, ctab, s1tab, s2tab):
    half = ROPE_DIM // 2
    return (x * ctab + pltpu.roll(x, LANES - half, 1) * s1tab + pltpu.roll(x, half, 1) * s2tab)


def _qkv_kernel(x_ref, gq_ref, gkv_ref, wq_ref, wk_ref, wvt_ref, c_ref, s1_ref, s2_ref, q_ref, k_ref, vt_ref):
    x = x_ref[...]
    xn = x * lax.rsqrt(jnp.mean(x * x, axis=-1, keepdims=True) + EPS)
    hq = (xn * gq_ref[...]).astype(BF16)
    hkv = (xn * gkv_ref[...]).astype(BF16)
    ctab, s1tab, s2tab = c_ref[...], s1_ref[...], s2_ref[...]
    d = q_ref.shape[1]
    q_scale = DIFF_HD ** -0.5 * math.log2(math.e)
    for c in range(d // MXU_COLS):
        c0 = c * MXU_COLS
        qa = _dot(hq, wq_ref[:, c0:c0 + MXU_COLS])
        ka = _dot(hkv, wk_ref[:, c0:c0 + MXU_COLS])
        for l0 in range(0, MXU_COLS, LANES):
            cols = slice(c0 + l0, c0 + l0 + LANES)
            q_ref[:, cols] = (_rope16(qa[:, l0:l0 + LANES], ctab, s1tab, s2tab) * q_scale).astype(BF16)
            k_ref[:, cols] = _rope16(ka[:, l0:l0 + LANES], ctab, s1tab, s2tab).astype(BF16)
    vt_ref[...] = lax.dot_general(wvt_ref[...], hkv, (((1,), (1,)), ((), ())),
                                  preferred_element_type=F32).astype(BF16)


def _qkv_proj(x2d, gq, gkv, w_q, w_k, w_vt, ctab, s1tab, s2tab, batch, seq):
    t, d = x2d.shape
    tm = TOK_TILE
    n_pos = seq // tm
    tok = pl.BlockSpec((tm, d), lambda i: (i, 0))
    tab = pl.BlockSpec((tm, LANES), lambda i: (i % n_pos, 0))
    return pl.pallas_call(
        _qkv_kernel,
        out_shape=[jax.ShapeDtypeStruct((t, d), BF16), jax.ShapeDtypeStruct((t, d), BF16),
                   jax.ShapeDtypeStruct((batch * d, seq), BF16)],
        grid=(t // tm,),
        in_specs=[tok, _const_spec((1, d)), _const_spec((1, d)), _const_spec(w_q.shape),
                  _const_spec(w_k.shape), _const_spec(w_vt.shape), tab, tab, tab],
        out_specs=[tok, tok, pl.BlockSpec((d, tm), lambda i: (i // n_pos, i % n_pos))],
        compiler_params=_cparams(("parallel",)),
        name="qkv_proj",
    )(x2d, gq, gkv, w_q, w_k, w_vt, ctab, s1tab, s2tab)


def _attn_kernel(lam_ref, bias_ref, q_ref, k_ref, vt_ref, sub_ref, wa_ref, wb_ref,
                 o_ref, wa_out_ref, wb_out_ref, vta_ref, s_ref, acc_ref, *, lambda_init):
    i = pl.program_id(2)
    tq = q_ref.shape[0]
    dv = 2 * DIFF_HD
    n_heads, n_kv, _, tk = vta_ref.shape
    chains = [(h, c) for h in range(n_heads) for c in range(2)]

    wa_out_ref[...] = wa_ref[...].astype(BF16)
    wb_out_ref[...] = wb_ref[...].astype(BF16)

    @pl.when(i == 0)
    def _():
        for h in range(n_heads):
            for j in range(n_kv):
                vta_ref[h, j, 0:dv, :] = vt_ref[h * dv:(h + 1) * dv, j * tk:(j + 1) * tk]
                vta_ref[h, j, dv:, :] = jnp.ones((ATT_ONES, tk), BF16)

    lane = lax.broadcasted_iota(jnp.int32, (tq, dv), 1)
    qc = []
    for h in range(n_heads):
        q = q_ref[:, h * dv:(h + 1) * dv]
        zero = jnp.zeros_like(q)
        qc.append((jnp.where(lane < DIFF_HD, q, zero), jnp.where(lane >= DIFF_HD, q, zero)))
    n_full = (i * tq) // tk
    tail_bias = 1 + (i * tq - n_full * tk) // tq

    def tile_max(s):
        m8 = jnp.max(s.reshape(tk // SUBLANES, SUBLANES, tq), axis=0)
        return jnp.max(m8, axis=0, keepdims=True)

    def score_chain(j, x):
        h, c = chains[x]
        bias = bias_ref[jnp.where(j == n_full, tail_bias, 0)]
        kt = k_ref[pl.ds(pl.multiple_of(j * tk, tk), tk), h * dv:(h + 1) * dv]
        s = lax.dot_general(kt, qc[h][c], (((1,), (1,)), ((), ())), preferred_element_type=F32) + bias
        s_ref[x] = s
        return tile_max(s)

    acc_ref[...] = jnp.zeros_like(acc_ref)
    m_init = tuple(jnp.full((1, tq), -jnp.inf, F32) for _ in chains)

    def step(j, carry, with_next):
        m_run, m_tile = carry
        m_out, m_next = [], []
        for x, (h, c) in enumerate(chains):
            m_new = jnp.maximum(m_run[x], m_tile[x])
            alpha = jnp.exp2(m_run[x] - m_new)
            p = jnp.exp2(s_ref[x] - m_new).astype(BF16)
            acc_ref[x] = alpha * acc_ref[x] + _dot(vta_ref[h, j], p)
            m_out.append(m_new)
            if with_next:
                m_next.append(score_chain(j + 1, x))
        return tuple(m_out), tuple(m_next)

    first = tuple(score_chain(0, x) for x in range(len(chains)))
    carry = lax.fori_loop(0, n_full, lambda j, carry: step(j, carry, True), (m_init, first))
    step(n_full, carry, False)

    lam_v = lam_ref[...]
    lam = (jnp.exp(jnp.sum(lam_v[0:1] * lam_v[1:2], axis=-1, keepdims=True))
           - jnp.exp(jnp.sum(lam_v[2:3] * lam_v[3:4], axis=-1, keepdims=True)) + lambda_init)
    for h in range(n_heads):
        a0 = acc_ref[2 * h]
        a1 = acc_ref[2 * h + 1]
        ot = a0[0:dv] / a0[dv:dv + 1] - lam * (a1[0:dv] / a1[dv:dv + 1])
        o = _rms(ot.T, sub_ref[...]) * (1.0 - lambda_init)
        o_ref[:, h * dv:(h + 1) * dv] = o.astype(BF16)


def _attn_bias(tq, tk):
    key = jnp.arange(tk)[:, None] // CHUNK
    tiles = [jnp.zeros((tk, tq), F32)]
    for r in range(tk // tq):
        qry = (r * tq + jnp.arange(tq))[None, :] // CHUNK
        tiles.append(jnp.where(key <= qry, 0.0, -1e30).astype(F32))
    return jnp.stack(tiles)


def _diff_attention(q, k, vt, lam_vecs, subln, w_a, w_b, batch, seq, lambda_init):
    t, d = q.shape
    tq = ATT_TQ
    nq = seq // tq
    dv = 2 * DIFF_HD
    hp = ATT_HEADS
    n_groups = DIFF_HEADS // hp
    n_steps = batch * n_groups * nq
    bias = _attn_bias(tq, ATT_TK)
    step = lambda b, g, i: ((b * n_groups + g) * nq + i, 0)
    slabs = [pl.BlockSpec((w.shape[0] // n_steps, w.shape[1]), step) for w in (w_a, w_b)]
    assert all(w.shape[0] % (n_steps * 2 * SUBLANES) == 0 for w in (w_a, w_b))
    return pl.pallas_call(
        functools.partial(_attn_kernel, lambda_init=lambda_init),
        out_shape=[jax.ShapeDtypeStruct((t, d), BF16),
                   jax.ShapeDtypeStruct(w_a.shape, BF16), jax.ShapeDtypeStruct(w_b.shape, BF16)],
        grid=(batch, n_groups, nq),
        in_specs=[
            _const_spec(lam_vecs.shape),
            _const_spec(bias.shape),
            pl.BlockSpec((tq, hp * dv), lambda b, g, i: (b * nq + i, g)),
            pl.BlockSpec((seq, hp * dv), lambda b, g, i: (b, g)),
            pl.BlockSpec((hp * dv, seq), lambda b, g, i: (b * n_groups + g, 0)),
            _const_spec((1, dv)),
        ] + slabs,
        out_specs=[pl.BlockSpec((tq, hp * dv), lambda b, g, i: (b * nq + i, g))] + slabs,
        scratch_shapes=[pltpu.VMEM((hp, seq // ATT_TK, dv + ATT_ONES, ATT_TK), BF16),
                        pltpu.VMEM((2 * hp, ATT_TK, tq), F32),
                        pltpu.VMEM((2 * hp, dv + ATT_ONES, tq), F32)],
        compiler_params=_cparams(("parallel", "parallel", "arbitrary")),
        name="diff_attn",
    )(lam_vecs, bias, q, k, vt, subln, w_a, w_b)


ROUTE_ROWS = 2 * SUBLANES


def _attn_out_kernel(x_ref, o_ref, wo_ref, g_ref, r_ref, tri_ref, x3_ref, h3_ref, route_ref, cnt_ref, carry_ref):
    @pl.when(pl.program_id(0) == 0)
    def _():
        carry_ref[...] = jnp.zeros_like(carry_ref)

    tm = x_ref.shape[0]
    x3 = x_ref[...] + _dot(o_ref[...], wo_ref[...])
    x3_ref[...] = x3
    h3 = _rms(x3, g_ref[...])
    for s in range(h3.shape[1] // LANES):
        h3_ref[pl.ds(s, tm, stride=SUBLANES), :] = h3[:, s * LANES:(s + 1) * LANES]

    hi = h3.astype(BF16)
    lo = (h3 - hi.astype(F32)).astype(BF16)
    both = _dot(hi, r_ref[...])
    logits = both[:, :LANES] + both[:, LANES:] + _dot(lo, r_ref[:, :LANES])
    lt = logits.T[0:ROUTE_ROWS]
    sub = lax.broadcasted_iota(jnp.int32, lt.shape, 0).astype(F32)
    lt = jnp.where(sub < N_EXPERTS, lt, -jnp.inf)
    v1 = jnp.max(lt, axis=0, keepdims=True)
    i1 = jnp.min(jnp.where(lt == v1, sub, float(ROUTE_ROWS)), axis=0, keepdims=True)
    lt2 = jnp.where(sub == i1, -jnp.inf, lt)
    v2 = jnp.max(lt2, axis=0, keepdims=True)
    i2 = jnp.min(jnp.where(lt2 == v2, sub, float(ROUTE_ROWS)), axis=0, keepdims=True)
    e = jnp.exp(v2 - v1)
    w1 = 1.0 / (1.0 + e)
    w2 = e / (1.0 + e)
    oh1 = sub == i1
    oh2 = sub == i2
    assign = jnp.where(oh1 | oh2, 1.0, 0.0)
    excl = _dot(assign.astype(BF16), tri_ref[...]) + carry_ref[:, 0:1]
    r1 = jnp.sum(jnp.where(oh1, excl, 0.0), axis=0, keepdims=True)
    r2 = jnp.sum(jnp.where(oh2, excl, 0.0), axis=0, keepdims=True)
    route = jnp.zeros_like(lt)
    for row, val in enumerate((i1, i2, r1, r2, w1, w2)):
        route = jnp.where(sub == float(row), val, route)
    route_ref[...] = route[0:SUBLANES]
    total = carry_ref[:, 0:1] + jnp.sum(assign, axis=1, keepdims=True)
    carry_ref[...] = jnp.broadcast_to(total, carry_ref.shape)
    cnt_ref[...] = jnp.broadcast_to(total, cnt_ref.shape)


def _attn_out(x2d, o, w_o, gain, router_split, tri):
    t, d = x2d.shape
    tm = TOK_TILE
    n_sub = d // LANES
    return pl.pallas_call(
        _attn_out_kernel,
        out_shape=[jax.ShapeDtypeStruct((t, d), F32),
                   jax.ShapeDtypeStruct((t * n_sub, LANES), F32),
                   jax.ShapeDtypeStruct((SUBLANES, t), F32),
                   jax.ShapeDtypeStruct((ROUTE_ROWS, LANES), F32)],
        grid=(t // tm,),
        in_specs=[pl.BlockSpec((tm, d), lambda i: (i, 0)),
                  pl.BlockSpec((tm, d), lambda i: (i, 0)),
                  _const_spec(w_o.shape), _const_spec((1, d)), _const_spec(router_split.shape),
                  _const_spec(tri.shape)],
        out_specs=[pl.BlockSpec((tm, d), lambda i: (i, 0)),
                   pl.BlockSpec((tm * n_sub, LANES), lambda i: (i, 0)),
                   pl.BlockSpec((SUBLANES, tm), lambda i: (0, i)),
                   _const_spec((ROUTE_ROWS, LANES))],
        scratch_shapes=[pltpu.VMEM((ROUTE_ROWS, LANES), F32)],
        compiler_params=_cparams(("arbitrary",)),
        name="attn_out_route",
    )(x2d, o, w_o, gain, router_split, tri)


def _dispatch_kernel(dest_ref, zpos_ref, zflag_ref, h_ref, dst_ref, zbuf_ref, zsem, sem):
    i = pl.program_id(0)
    tm = h_ref.shape[0]

    @pl.when(i == 0)
    def _():
        zbuf_ref[...] = jnp.zeros_like(zbuf_ref)

        def zcopy(e):
            return pltpu.make_async_copy(zbuf_ref, dst_ref.at[pl.ds(zpos_ref[e], MOE_TILE)], zsem.at[e])

        for e in range(2 * N_EXPERTS):
            @pl.when(zflag_ref[e] == 1)
            def _():
                zcopy(e).start()
        for e in range(2 * N_EXPERTS):
            @pl.when(zflag_ref[e] == 1)
            def _():
                zcopy(e).wait()

    def body(r, carry):
        base = 2 * (i * tm + r)
        for k in range(2):
            pltpu.make_async_copy(h_ref.at[r], dst_ref.at[dest_ref[base + k]], sem).start(priority=k)
        return carry

    lax.fori_loop(0, tm, body, 0, unroll=8)
    for k in range(2):
        pltpu.make_async_copy(h_ref, dst_ref.at[pl.ds(0, tm)], sem).wait()


def _dispatch(h3_rows, dest, zpos, zflag, n_dst):
    t = h3_rows.shape[0]
    tm = TOK_TILE
    tail = h3_rows.shape[1:]
    return pl.pallas_call(
        _dispatch_kernel,
        out_shape=jax.ShapeDtypeStruct((n_dst,) + tail, F32),
        grid_spec=pltpu.PrefetchScalarGridSpec(
            num_scalar_prefetch=3, grid=(t // tm,),
            in_specs=[pl.BlockSpec((tm,) + tail, lambda i, *_: (i, 0, 0))],
            out_specs=pl.BlockSpec(memory_space=pl.ANY),
            scratch_shapes=[pltpu.VMEM((MOE_TILE,) + tail, F32),
                            pltpu.SemaphoreType.DMA((2 * N_EXPERTS,)), pltpu.SemaphoreType.DMA(())]),
        compiler_params=_cparams(("arbitrary",)),
        name="moe_dispatch",
    )(dest, zpos, zflag, h3_rows)


def _moe_kernel(te_ref, tr_ref, ts_ref, xs_ref, wg_ref, wu_ref, wd_ref, ys_ref, h_ref, acc_ref):
    i = pl.program_id(0)
    c = pl.program_id(1)
    last = pl.num_programs(1) - 1
    n_sub = h_ref.shape[1] // LANES
    grp = h_ref.shape[0] // MOE_SPLIT

    for part in range(MOE_SPLIT):
        r0 = part * grp
        rows = slice(r0, r0 + grp)
        live = tr_ref[i] > r0

        @pl.when(live & (c == 0))
        def _():
            for s in range(n_sub):
                h_ref[rows, s * LANES:(s + 1) * LANES] = (
                    xs_ref[pl.ds(r0 * n_sub + s, grp, stride=n_sub), :].astype(BF16))
            acc_ref[rows, :] = jnp.zeros((grp, acc_ref.shape[1]), F32)

        @pl.when(live)
        def _():
            h = h_ref[rows, :]
            gt = _dot(h, wg_ref[...])
            up = _dot(h, wu_ref[...])
            act = (gt * jax.nn.sigmoid(gt) * up).astype(BF16)
            acc_ref[rows, :] += _dot(act, wd_ref[...])

        @pl.when(live & (c == last))
        def _():
            for s in range(n_sub):
                ys_ref[pl.ds(r0 * n_sub + s, grp, stride=n_sub), :] = acc_ref[rows, s * LANES:(s + 1) * LANES]

        @pl.when(jnp.logical_not(live) & (c == last))
        def _():
            ys_ref[r0 * n_sub:(r0 + grp) * n_sub, :] = jnp.zeros((grp * n_sub, LANES), F32)


def _moe_experts(xs2d, w_gu, w_down, tile_expert, tile_rows, tile_src, d):
    n_sub = d // LANES
    n_tiles = xs2d.shape[0] // (MOE_TILE * n_sub)
    f = w_down.shape[1]
    nfc = f // MOE_FC
    blk = MOE_TILE * n_sub

    def ceff(i, c, tv):
        return jnp.where(tv[i] > 0, c, nfc - 1)

    return pl.pallas_call(
        _moe_kernel,
        out_shape=jax.ShapeDtypeStruct(xs2d.shape, F32),
        grid_spec=pltpu.PrefetchScalarGridSpec(
            num_scalar_prefetch=3, grid=(n_tiles, nfc),
            in_specs=[
                pl.BlockSpec((blk, LANES), lambda i, c, te, tv, ts: (ts[i], 0)),
                pl.BlockSpec((None, d, MOE_FC), lambda i, c, te, tv, ts: (te[i], 0, ceff(i, c, tv))),
                pl.BlockSpec((None, d, MOE_FC), lambda i, c, te, tv, ts: (te[i], 0, nfc + ceff(i, c, tv))),
                pl.BlockSpec((None, MOE_FC, d), lambda i, c, te, tv, ts: (te[i], ceff(i, c, tv), 0)),
            ],
            out_specs=pl.BlockSpec((blk, LANES), lambda i, c, te, tv, ts: (i, 0)),
            scratch_shapes=[pltpu.VMEM((MOE_TILE, d), BF16), pltpu.VMEM((MOE_TILE, d), F32)]),
        compiler_params=_cparams(("arbitrary", "arbitrary")),
        name="moe_experts",
    )(tile_expert, tile_rows, tile_src, xs2d, w_gu, w_gu, w_down)


def _combine_kernel(dest_ref, x_ref, ys_ref, gate_ref, g_ref, o_ref, ybuf_ref, sem):
    i = pl.program_id(0)
    n = pl.num_programs(0)
    tm, d = x_ref.shape
    n_sub = d // LANES

    def gather(tile, slot):
        def body(r, carry):
            base = 2 * (tile * tm + r)
            for k in range(2):
                src = ys_ref.at[pl.ds(pl.multiple_of(dest_ref[base + k] * n_sub, n_sub), n_sub)]
                dst = ybuf_ref.at[slot, pl.ds(pl.multiple_of((2 * r + k) * n_sub, n_sub), n_sub)]
                pltpu.make_async_copy(src, dst, sem.at[slot]).start(priority=k)
            return carry

        lax.fori_loop(0, tm, body, 0, unroll=8)

    @pl.when(i == 0)
    def _():
        gather(0, 0)

    @pl.when(i + 1 < n)
    def _():
        gather(i + 1, (i + 1) % 2)

    slot = i % 2
    pltpu.make_async_copy(ys_ref.at[pl.ds(0, 2 * tm * n_sub)], ybuf_ref.at[slot], sem.at[slot]).wait()

    y_ref = ybuf_ref.at[slot]
    gates = gate_ref[...]
    w1 = gates[:, 0:1]
    w2 = gates[:, 1:2]
    for s in range(n_sub):
        cols = slice(s * LANES, (s + 1) * LANES)
        y1 = y_ref[pl.ds(s, tm, stride=2 * n_sub), :]
        y2 = y_ref[pl.ds(n_sub + s, tm, stride=2 * n_sub), :]
        o_ref[:, cols] = x_ref[:, cols] + w1 * y1 + w2 * y2
    o_ref[...] = _rms(o_ref[...], g_ref[...])


def _combine(x2d, ys2d, dest, gates, gain):
    t, d = x2d.shape
    tm = TOK_TILE
    n_sub = d // LANES
    return pl.pallas_call(
        _combine_kernel,
        out_shape=jax.ShapeDtypeStruct((t, d), F32),
        grid_spec=pltpu.PrefetchScalarGridSpec(
            num_scalar_prefetch=1, grid=(t // tm,),
            in_specs=[pl.BlockSpec((tm, d), lambda i, *_: (i, 0)),
                      pl.BlockSpec(memory_space=pl.ANY),
                      pl.BlockSpec((tm, gates.shape[1]), lambda i, *_: (i, 0)),
                      pl.BlockSpec((1, d), lambda i, *_: (0, 0), pipeline_mode=pl.Buffered(1))],
            out_specs=pl.BlockSpec((tm, d), lambda i, *_: (i, 0)),
            scratch_shapes=[pltpu.VMEM((2, 2 * tm * n_sub, LANES), F32), pltpu.SemaphoreType.DMA((2,))]),
        compiler_params=_cparams(("arbitrary",)),
        name="moe_combine",
    )(dest, x2d, ys2d, gates, gain)


def _rope_tables(seq, rot_dim, theta):
    inv = 1.0 / (theta ** (jnp.arange(0, rot_dim, 2, dtype=F32) / rot_dim))
    ang = jnp.arange(seq, dtype=F32)[:, None] * inv[None, :]
    return jnp.cos(ang), jnp.sin(ang)


def _retention_tables():
    c = RET_CHUNK
    log_gamma = jnp.log(1.0 - 2.0 ** (-5.0 - jnp.arange(RET_HEADS, dtype=F32)))
    idx = jnp.arange(c, dtype=F32)
    rel = idx[:, None] - idx[None, :]
    dmat = jnp.where(rel[None] >= 0, jnp.exp(jnp.maximum(rel, 0.0)[None] * log_gamma[:, None, None]), 0.0)
    qd = jnp.exp((idx + 1.0)[None, :] * log_gamma[:, None])[:, :, None]
    kd = jnp.exp((c - 1.0 - idx)[None, :] * log_gamma[:, None])[:, :, None]
    cd = jnp.exp(c * log_gamma)[:, None, None]
    return dmat, qd, kd, cd


def _attn_rope_tables(seq):
    cos, sin = _rope_tables(seq, ROPE_DIM, ROPE_THETA)
    half = ROPE_DIM // 2
    pad = DIFF_HD - ROPE_DIM
    ones = jnp.ones((seq, pad), F32)
    zeros = jnp.zeros((seq, pad), F32)
    zh = jnp.zeros((seq, half), F32)
    ctab = jnp.concatenate([cos, cos, ones], axis=1)
    s1tab = jnp.concatenate([-sin, zh, zeros], axis=1)
    s2tab = jnp.concatenate([zh, sin, zeros], axis=1)
    rep = LANES // DIFF_HD
    return tuple(jnp.tile(tb, (1, rep)) for tb in (ctab, s1tab, s2tab))


def kernel(x, ln_mix, ln_ffn, ret_w_in, ret_w_o, kv_norm, w_kv, diff_w_q, lam_q1, lam_k1, lam_q2, lam_k2,
           diff_subln, diff_w_o, ffn_w_gu, ffn_w_down, moe_router, moe_w_gu, moe_w_down, final_norm):
    batch, seq, d = x.shape
    t = batch * seq
    assert ln_mix.shape[0] == 2 and ret_w_in.shape[0] == 1 and diff_w_q.shape[0] == 1
    assert seq % TOK_TILE == 0 and TOK_TILE % RET_CHUNK == 0 and ATT_TQ % CHUNK == 0
    assert seq % ATT_TK == 0 and ATT_TK % ATT_TQ == 0
    x2d = x.reshape(t, d)
    row = lambda g: g.reshape(1, -1)

    cos_r, sin_r = _rope_tables(seq, RET_QK, RET_THETA)
    proj = _ret_in_proj(x2d, row(ln_mix[0]), ret_w_in[0].astype(BF16), cos_r, sin_r, seq)
    ret_o = _retention(proj, *_retention_tables(), batch, seq)
    x2 = _ret_out_ffn(x2d, ret_o, ret_w_o[0].astype(BF16), row(ln_ffn[0]),
                      ffn_w_gu[0].astype(BF16), ffn_w_down[0].astype(BF16))

    lambda_init = 0.8 - 0.6 * math.exp(-0.3 * 1)
    q, k, vt = _qkv_proj(x2, row(ln_mix[1]), row(kv_norm), diff_w_q[0].astype(BF16),
                         w_kv[:, :d].astype(BF16), w_kv[:, d:].T.astype(BF16),
                         *_attn_rope_tables(seq), batch, seq)
    lam_vecs = jnp.stack([lam_q1[0], lam_k1[0], lam_q2[0], lam_k2[0]]).astype(F32)
    n_exp, _, two_f = moe_w_gu[0].shape
    att, w_gu_bf, w_down_bf = _diff_attention(
        q, k, vt, lam_vecs, row(diff_subln[0]), moe_w_gu[0].reshape(n_exp * d, two_f),
        moe_w_down[0].reshape(n_exp * (two_f // 2), d), batch, seq, lambda_init)
    router_pad = jnp.pad(moe_router[0], ((0, 0), (0, LANES - N_EXPERTS)))
    router_hi = router_pad.astype(BF16)
    router_lo = (router_pad - router_hi.astype(F32)).astype(BF16)
    tri = (jnp.arange(TOK_TILE)[:, None] < jnp.arange(TOK_TILE)[None, :]).astype(BF16)
    x3, h3_rows, route, cnt = _attn_out(x2, att, diff_w_o[0].astype(BF16), row(ln_ffn[1]),
                                        jnp.concatenate([router_hi, router_lo], axis=1), tri)

    n_sub = d // LANES
    n_rows = 2 * t + N_EXPERTS * MOE_TILE
    n_tiles = n_rows // MOE_TILE
    expert = route[0:2].T.astype(jnp.int32)
    rank = route[2:4].T.astype(jnp.int32)
    gates = route[4:6].T
    counts = cnt[:N_EXPERTS, 0].astype(jnp.int32)
    padded = (counts + MOE_TILE - 1) // MOE_TILE * MOE_TILE
    seg_end = jnp.cumsum(padded)
    seg_start = seg_end - padded
    dest = (jnp.sum(jnp.where(expert[..., None] == jnp.arange(N_EXPERTS), seg_start, 0), axis=-1)
            + rank).reshape(-1)
    tile_row = jnp.arange(n_tiles, dtype=jnp.int32) * MOE_TILE
    tile_valid = (tile_row < seg_end[-1]).astype(jnp.int32)
    n_valid = seg_end[-1] // MOE_TILE
    tile_src = jnp.minimum(jnp.arange(n_tiles, dtype=jnp.int32), jnp.maximum(n_valid - 1, 0))
    tile_expert = jnp.minimum(jnp.sum(tile_src[:, None] * MOE_TILE >= seg_end[None, :], axis=1),
                              N_EXPERTS - 1).astype(jnp.int32)
    token_end = jnp.sum(jnp.where(tile_expert[:, None] == jnp.arange(N_EXPERTS), seg_start + counts, 0), axis=1)
    tile_rows = (tile_valid * jnp.clip(token_end - tile_row, 0, MOE_TILE)).astype(jnp.int32)
    zflag = jnp.concatenate([(padded > 0).astype(jnp.int32), 1 - tile_valid[-N_EXPERTS:]])
    zpos = jnp.concatenate([jnp.maximum(seg_end - MOE_TILE, 0), tile_row[-N_EXPERTS:]]).astype(jnp.int32)

    xs = _dispatch(h3_rows.reshape(t, n_sub, LANES), dest, zpos, zflag, n_rows)
    ys = _moe_experts(xs.reshape(n_rows * n_sub, LANES), w_gu_bf.reshape(n_exp, d, two_f),
                      w_down_bf.reshape(n_exp, two_f // 2, d), tile_expert, tile_rows, tile_src, d)
    out = _combine(x3, ys, dest, gates, row(final_norm))
    return out.reshape(batch, seq, d)
```

```python
import functools
import math

import jax
import jax.numpy as jnp
import numpy as np
from jax import lax
from jax.experimental import pallas as pl
from jax.experimental.pallas import tpu as pltpu

F32 = jnp.float32
BF16 = jnp.bfloat16

EPS = 1e-6
CHUNK = 64
RET_QK = 256
RET_V = 512
RET_HEADS = 4
RET_THETA = 10000.0
DIFF_HD = 64
DIFF_HEADS = 8
ROPE_THETA = 500000.0
ROPE_DIM = 16
N_EXPERTS = 8

LANES = 128
SUBLANES = 8
MXU_COLS = 256
VMEM_LIMIT = 56 * 1024 * 1024

RET_CHUNK = 256
TOK_TILE = 512
DISPATCH_TILE = 1024
ATT_TQ = 512
ATT_TK = 512
ATT_HEADS = 4
ATT_ONES = 16
MOE_TILE = 1024
MOE_SPLIT = 2
MOE_FC = 1792


def _cparams(sem, vmem=VMEM_LIMIT):
    return pltpu.CompilerParams(dimension_semantics=sem, vmem_limit_bytes=vmem)


def _const_spec(shape):
    nd = len(shape)
    return pl.BlockSpec(shape, lambda *_: (0,) * nd, pipeline_mode=pl.Buffered(1))


def _rms(x, g):
    return x * lax.rsqrt(jnp.mean(x * x, axis=-1, keepdims=True) + EPS) * g


def _dot(a, b):
    return jnp.dot(a, b, preferred_element_type=F32)


def _ret_in_kernel(x_ref, g_ref, w_ref, cos_ref, sin_ref, o_ref):
    h = _rms(x_ref[...], g_ref[...]).astype(BF16)
    cos = cos_ref[...]
    sin = sin_ref[...]
    d_qk = RET_HEADS * RET_QK
    half = RET_QK // 2
    for c in range(2 * RET_HEADS):
        c0 = c * RET_QK
        acc = _dot(h, w_ref[:, c0:c0 + RET_QK])
        x1 = acc[:, :half]
        x2 = acc[:, half:]
        scale = 1.0 if c < RET_HEADS else RET_QK ** -0.5
        o_ref[:, c0:c0 + half] = ((x1 * cos - x2 * sin) * scale).astype(BF16)
        o_ref[:, c0 + half:c0 + RET_QK] = ((x2 * cos + x1 * sin) * scale).astype(BF16)
    n_rest = (w_ref.shape[1] - 2 * d_qk) // RET_V
    for c in range(n_rest):
        c0 = 2 * d_qk + c * RET_V
        o_ref[:, c0:c0 + RET_V] = _dot(h, w_ref[:, c0:c0 + RET_V]).astype(BF16)


def _ret_in_proj(x2d, gain, w, cos, sin, seq):
    t, d = x2d.shape
    n = w.shape[1]
    tm = TOK_TILE
    n_pos = seq // tm
    return pl.pallas_call(
        _ret_in_kernel,
        out_shape=jax.ShapeDtypeStruct((t, n), BF16),
        grid=(t // tm,),
        in_specs=[
            pl.BlockSpec((tm, d), lambda i: (i, 0)),
            _const_spec((1, d)),
            _const_spec((d, n)),
            pl.BlockSpec((tm, RET_QK // 2), lambda i: (i % n_pos, 0)),
            pl.BlockSpec((tm, RET_QK // 2), lambda i: (i % n_pos, 0)),
        ],
        out_specs=pl.BlockSpec((tm, n), lambda i: (i, 0)),
        compiler_params=_cparams(("parallel",)),
        name="ret_in_proj",
    )(x2d, gain, w, cos, sin)


def _retention_kernel(q_ref, k_ref, v_ref, g_ref, dmat_ref, qd_ref, kd_ref, cd_ref, o_ref, state_ref):
    @pl.when(pl.program_id(1) == 0)
    def _():
        state_ref[...] = jnp.zeros_like(state_ref)

    n_chunks = q_ref.shape[0] // RET_CHUNK
    for h in range(RET_HEADS):
        for c in range(n_chunks):
            rows = slice(c * RET_CHUNK, (c + 1) * RET_CHUNK)
            q = q_ref[rows, h * RET_QK:(h + 1) * RET_QK]
            k = k_ref[rows, h * RET_QK:(h + 1) * RET_QK]
            v = v_ref[rows, h * RET_V:(h + 1) * RET_V]
            state = state_ref[h]
            s = lax.dot_general(q, k, (((1,), (1,)), ((), ())), preferred_element_type=F32)
            s = s * dmat_ref[h]
            o = _dot(s.astype(BF16), v)
            qs = (q.astype(F32) * qd_ref[h]).astype(BF16)
            o = o + _dot(qs, state.astype(BF16))
            ks = (k.astype(F32) * kd_ref[h]).astype(BF16)
            state_ref[h] = state * cd_ref[h] + lax.dot_general(
                ks, v, (((0,), (0,)), ((), ())), preferred_element_type=F32)
            mu = jnp.mean(o, axis=-1, keepdims=True)
            oc = o - mu
            var = jnp.mean(oc * oc, axis=-1, keepdims=True)
            on = oc * lax.rsqrt(var + EPS)
            gate = g_ref[rows, h * RET_V:(h + 1) * RET_V].astype(F32)
            o_ref[rows, h * RET_V:(h + 1) * RET_V] = (on * (gate * jax.nn.sigmoid(gate))).astype(BF16)


def _retention(proj, dmat, qd, kd, cd, batch, seq):
    t = proj.shape[0]
    tb = TOK_TILE
    nj = seq // tb
    d_qk = RET_HEADS * RET_QK
    d_v = RET_HEADS * RET_V
    row = lambda b, j: b * nj + j
    return pl.pallas_call(
        _retention_kernel,
        out_shape=jax.ShapeDtypeStruct((t, d_v), BF16),
        grid=(batch, nj),
        in_specs=[
            pl.BlockSpec((tb, d_qk), lambda b, j: (row(b, j), 0)),
            pl.BlockSpec((tb, d_qk), lambda b, j: (row(b, j), 1)),
            pl.BlockSpec((tb, d_v), lambda b, j: (row(b, j), 1)),
            pl.BlockSpec((tb, d_v), lambda b, j: (row(b, j), 2)),
            _const_spec(dmat.shape),
            _const_spec(qd.shape),
            _const_spec(kd.shape),
            _const_spec(cd.shape),
        ],
        out_specs=pl.BlockSpec((tb, d_v), lambda b, j: (row(b, j), 0)),
        scratch_shapes=[pltpu.VMEM((RET_HEADS, RET_QK, RET_V), F32)],
        compiler_params=_cparams(("parallel", "arbitrary")),
        name="retention",
    )(proj, proj, proj, proj, dmat, qd, kd, cd)


def _ret_out_ffn_kernel(x_ref, o_ref, wo_ref, g_ref, wgu_ref, wd_ref, out_ref, act_ref, *, fc):
    x1 = x_ref[...] + _dot(o_ref[...], wo_ref[...])
    h = _rms(x1, g_ref[...]).astype(BF16)
    f = wd_ref.shape[0]
    for c in range(f // fc):
        gt = _dot(h, wgu_ref[:, c * fc:(c + 1) * fc])
        up = _dot(h, wgu_ref[:, f + c * fc:f + (c + 1) * fc])
        act_ref[:, c * fc:(c + 1) * fc] = (gt * jax.nn.sigmoid(gt) * up).astype(BF16)
    out_ref[...] = x1 + _dot(act_ref[...], wd_ref[...])


def _ret_out_ffn(x2d, o, w_o, gain, w_gu, w_down):
    t, d = x2d.shape
    tm = TOK_TILE
    f = w_down.shape[0]
    return pl.pallas_call(
        functools.partial(_ret_out_ffn_kernel, fc=256),
        out_shape=jax.ShapeDtypeStruct((t, d), F32),
        grid=(t // tm,),
        in_specs=[
            pl.BlockSpec((tm, d), lambda i: (i, 0)),
            pl.BlockSpec((tm, o.shape[1]), lambda i: (i, 0)),
            _const_spec(w_o.shape),
            _const_spec((1, d)),
            _const_spec(w_gu.shape),
            _const_spec(w_down.shape),
        ],
        out_specs=pl.BlockSpec((tm, d), lambda i: (i, 0)),
        scratch_shapes=[pltpu.VMEM((tm, f), BF16)],
        compiler_params=_cparams(("parallel",)),
        name="ret_out_ffn",
    )(x2d, o, w_o, gain, w_gu, w_down)


def _rope16(x, ctab, s1tab, s2tab):
    half = ROPE_DIM // 2
    return (x * ctab + pltpu.roll(x, LANES - half, 1) * s1tab + pltpu.roll(x, half, 1) * s2tab)


def _qkv_kernel(x_ref, gq_ref, gkv_ref, wq_ref, wk_ref, wvt_ref, c_ref, s1_ref, s2_ref, q_ref, k_ref, vt_ref):
    x = x_ref[...]
    xn = x * lax.rsqrt(jnp.mean(x * x, axis=-1, keepdims=True) + EPS)
    hq = (xn * gq_ref[...]).astype(BF16)
    hkv = (xn * gkv_ref[...]).astype(BF16)
    ctab, s1tab, s2tab = c_ref[...], s1_ref[...], s2_ref[...]
    d = q_ref.shape[1]
    q_scale = DIFF_HD ** -0.5 * math.log2(math.e)
    for c in range(d // MXU_COLS):
        c0 = c * MXU_COLS
        qa = _dot(hq, wq_ref[:, c0:c0 + MXU_COLS])
        ka = _dot(hkv, wk_ref[:, c0:c0 + MXU_COLS])
        for l0 in range(0, MXU_COLS, LANES):
            cols = slice(c0 + l0, c0 + l0 + LANES)
            q_ref[:, cols] = (_rope16(qa[:, l0:l0 + LANES], ctab, s1tab, s2tab) * q_scale).astype(BF16)
            k_ref[:, cols] = _rope16(ka[:, l0:l0 + LANES], ctab, s1tab, s2tab).astype(BF16)
    vt_ref[...] = lax.dot_general(wvt_ref[...], hkv, (((1,), (1,)), ((), ())),
                                  preferred_element_type=F32).astype(BF16)


def _qkv_proj(x2d, gq, gkv, w_q, w_k, w_vt, ctab, s1tab, s2tab, batch, seq):
    t, d = x2d.shape
    tm = TOK_TILE
    n_pos = seq // tm
    tok = pl.BlockSpec((tm, d), lambda i: (i, 0))
    tab = pl.BlockSpec((tm, LANES), lambda i: (i % n_pos, 0))
    return pl.pallas_call(
        _qkv_kernel,
        out_shape=[jax.ShapeDtypeStruct((t, d), BF16), jax.ShapeDtypeStruct((t, d), BF16),
                   jax.ShapeDtypeStruct((batch * d, seq), BF16)],
        grid=(t // tm,),
        in_specs=[tok, _const_spec((1, d)), _const_spec((1, d)), _const_spec(w_q.shape),
                  _const_spec(w_k.shape), _const_spec(w_vt.shape), tab, tab, tab],
        out_specs=[tok, tok, pl.BlockSpec((d, tm), lambda i: (i // n_pos, i % n_pos))],
        compiler_params=_cparams(("parallel",)),
        name="qkv_proj",
    )(x2d, gq, gkv, w_q, w_k, w_vt, ctab, s1tab, s2tab)


def _attn_kernel(lam_ref, bias_ref, q_ref, qn_ref, k_ref, vt_ref, sub_ref, wa_ref, wb_ref,
                 o_ref, wa_out_ref, wb_out_ref, vta_ref, s_ref, mt_ref, acc_ref, *, lambda_init):
    i = pl.program_id(2)
    tq = q_ref.shape[0]
    dv = 2 * DIFF_HD
    n_heads, n_kv, _, tk = vta_ref.shape
    chains = [(h, c) for h in range(n_heads) for c in range(2)]

    wa_out_ref[...] = wa_ref[...].astype(BF16)
    wb_out_ref[...] = wb_ref[...].astype(BF16)

    @pl.when(i == 0)
    def _():
        for h in range(n_heads):
            for j in range(n_kv):
                vta_ref[h, j, 0:dv, :] = vt_ref[h * dv:(h + 1) * dv, j * tk:(j + 1) * tk]
                vta_ref[h, j, dv:, :] = jnp.ones((ATT_ONES, tk), BF16)

    lane = lax.broadcasted_iota(jnp.int32, (tq, dv), 1)

    def split_heads(ref):
        out = []
        for h in range(n_heads):
            q = ref[:, h * dv:(h + 1) * dv]
            zero = jnp.zeros_like(q)
            out.append((jnp.where(lane < DIFF_HD, q, zero), jnp.where(lane >= DIFF_HD, q, zero)))
        return out

    def first_tiles(qi):
        n = (qi * tq) // tk
        return n, 1 + (qi * tq - n * tk) // tq

    n_full, tail_bias = first_tiles(i)

    def tile_max(s):
        m8 = jnp.max(s.reshape(tk // SUBLANES, SUBLANES, tq), axis=0)
        return jnp.max(m8, axis=0, keepdims=True)

    def score_chain(j, x, qsplit, n_vis, tail):
        h, c = chains[x]
        bias = bias_ref[jnp.where(j == n_vis, tail, 0)]
        kt = k_ref[pl.ds(pl.multiple_of(j * tk, tk), tk), h * dv:(h + 1) * dv]
        s = lax.dot_general(kt, qsplit[h][c], (((1,), (1,)), ((), ())), preferred_element_type=F32) + bias
        s_ref[x] = s
        return tile_max(s)

    @pl.when(i == 0)
    def _():
        qc0 = split_heads(q_ref)
        for x in range(len(chains)):
            mt_ref[x] = score_chain(0, x, qc0, n_full, tail_bias)

    qc = split_heads(q_ref)
    acc_ref[...] = jnp.zeros_like(acc_ref)
    m_init = tuple(jnp.full((1, tq), -jnp.inf, F32) for _ in chains)

    def step(j, carry, next_scores):
        m_run, m_tile = carry
        m_out, m_next = [], []
        for x, (h, c) in enumerate(chains):
            m_new = jnp.maximum(m_run[x], m_tile[x])
            alpha = jnp.exp2(m_run[x] - m_new)
            p = jnp.exp2(s_ref[x] - m_new).astype(BF16)
            acc_ref[x] = alpha * acc_ref[x] + _dot(vta_ref[h, j], p)
            m_out.append(m_new)
            if next_scores is not None:
                m_next.append(next_scores(x))
        return tuple(m_out), tuple(m_next)

    first = tuple(mt_ref[x] for x in range(len(chains)))
    carry = lax.fori_loop(
        0, n_full,
        lambda j, carry: step(j, carry, lambda x: score_chain(j + 1, x, qc, n_full, tail_bias)),
        (m_init, first))

    @pl.when(i + 1 < pl.num_programs(2))
    def _():
        qn = split_heads(qn_ref)
        n_vis, tail = first_tiles(i + 1)
        _, m_first = step(n_full, carry, lambda x: score_chain(0, x, qn, n_vis, tail))
        for x in range(len(chains)):
            mt_ref[x] = m_first[x]

    @pl.when(i + 1 == pl.num_programs(2))
    def _():
        step(n_full, carry, None)

    lam_v = lam_ref[...]
    lam = (jnp.exp(jnp.sum(lam_v[0:1] * lam_v[1:2], axis=-1, keepdims=True))
           - jnp.exp(jnp.sum(lam_v[2:3] * lam_v[3:4], axis=-1, keepdims=True)) + lambda_init)
    for h in range(n_heads):
        a0 = acc_ref[2 * h]
        a1 = acc_ref[2 * h + 1]
        ot = a0[0:dv] / a0[dv:dv + 1] - lam * (a1[0:dv] / a1[dv:dv + 1])
        o = _rms(ot.T, sub_ref[...]) * (1.0 - lambda_init)
        o_ref[:, h * dv:(h + 1) * dv] = o.astype(BF16)


def _attn_bias(tq, tk):
    key = np.arange(tk)[:, None] // CHUNK
    tiles = [np.zeros((tk, tq))]
    for r in range(tk // tq):
        qry = (r * tq + np.arange(tq))[None, :] // CHUNK
        tiles.append(np.where(key <= qry, 0.0, -1e30))
    return jnp.asarray(np.stack(tiles), F32)


def _diff_attention(q, k, vt, lam_vecs, subln, w_a, w_b, batch, seq, lambda_init):
    t, d = q.shape
    tq = ATT_TQ
    nq = seq // tq
    dv = 2 * DIFF_HD
    hp = ATT_HEADS
    n_groups = DIFF_HEADS // hp
    n_steps = batch * n_groups * nq
    bias = _attn_bias(tq, ATT_TK)
    step = lambda b, g, i: ((b * n_groups + g) * nq + i, 0)
    slabs = [pl.BlockSpec((w.shape[0] // n_steps, w.shape[1]), step) for w in (w_a, w_b)]
    assert all(w.shape[0] % (n_steps * 2 * SUBLANES) == 0 for w in (w_a, w_b))
    return pl.pallas_call(
        functools.partial(_attn_kernel, lambda_init=lambda_init),
        out_shape=[jax.ShapeDtypeStruct((t, d), BF16),
                   jax.ShapeDtypeStruct(w_a.shape, BF16), jax.ShapeDtypeStruct(w_b.shape, BF16)],
        grid=(batch, n_groups, nq),
        in_specs=[
            _const_spec(lam_vecs.shape),
            _const_spec(bias.shape),
            pl.BlockSpec((tq, hp * dv), lambda b, g, i: (b * nq + i, g)),
            pl.BlockSpec((tq, hp * dv), lambda b, g, i: (b * nq + jnp.minimum(i + 1, nq - 1), g)),
            pl.BlockSpec((seq, hp * dv), lambda b, g, i: (b, g)),
            pl.BlockSpec((hp * dv, seq), lambda b, g, i: (b * n_groups + g, 0)),
            _const_spec((1, dv)),
        ] + slabs,
        out_specs=[pl.BlockSpec((tq, hp * dv), lambda b, g, i: (b * nq + i, g))] + slabs,
        scratch_shapes=[pltpu.VMEM((hp, seq // ATT_TK, dv + ATT_ONES, ATT_TK), BF16),
                        pltpu.VMEM((2 * hp, ATT_TK, tq), F32),
                        pltpu.VMEM((2 * hp, 1, tq), F32),
                        pltpu.VMEM((2 * hp, dv + ATT_ONES, tq), F32)],
        compiler_params=_cparams(("parallel", "parallel", "arbitrary")),
        name="diff_attn",
    )(lam_vecs, bias, q, q, k, vt, subln, w_a, w_b)


ROUTE_ROWS = 2 * SUBLANES


def _attn_out_kernel(x_ref, o_ref, wo_ref, g_ref, r_ref, tri_ref, x3_ref, h3_ref, route_ref, cnt_ref, carry_ref):
    @pl.when(pl.program_id(0) == 0)
    def _():
        carry_ref[...] = jnp.zeros_like(carry_ref)

    tm = x_ref.shape[0]
    x3 = x_ref[...] + _dot(o_ref[...], wo_ref[...])
    x3_ref[...] = x3
    h3 = _rms(x3, g_ref[...])
    for s in range(h3.shape[1] // LANES):
        h3_ref[pl.ds(s, tm, stride=SUBLANES), :] = h3[:, s * LANES:(s + 1) * LANES]

    hi = h3.astype(BF16)
    lo = (h3 - hi.astype(F32)).astype(BF16)
    both = _dot(hi, r_ref[...])
    logits = both[:, :LANES] + both[:, LANES:] + _dot(lo, r_ref[:, :LANES])
    lt = logits.T[0:ROUTE_ROWS]
    sub = lax.broadcasted_iota(jnp.int32, lt.shape, 0).astype(F32)
    lt = jnp.where(sub < N_EXPERTS, lt, -jnp.inf)
    v1 = jnp.max(lt, axis=0, keepdims=True)
    i1 = jnp.min(jnp.where(lt == v1, sub, float(ROUTE_ROWS)), axis=0, keepdims=True)
    lt2 = jnp.where(sub == i1, -jnp.inf, lt)
    v2 = jnp.max(lt2, axis=0, keepdims=True)
    i2 = jnp.min(jnp.where(lt2 == v2, sub, float(ROUTE_ROWS)), axis=0, keepdims=True)
    e = jnp.exp(v2 - v1)
    w1 = 1.0 / (1.0 + e)
    w2 = e / (1.0 + e)
    oh1 = sub == i1
    oh2 = sub == i2
    assign = jnp.where(oh1 | oh2, 1.0, 0.0)
    excl = _dot(assign.astype(BF16), tri_ref[...]) + carry_ref[:, 0:1]
    r1 = jnp.sum(jnp.where(oh1, excl, 0.0), axis=0, keepdims=True)
    r2 = jnp.sum(jnp.where(oh2, excl, 0.0), axis=0, keepdims=True)
    route = jnp.zeros_like(lt)
    for row, val in enumerate((i1, i2, r1, r2, w1, w2)):
        route = jnp.where(sub == float(row), val, route)
    route_ref[...] = route[0:SUBLANES]
    total = carry_ref[:, 0:1] + jnp.sum(assign, axis=1, keepdims=True)
    carry_ref[...] = jnp.broadcast_to(total, carry_ref.shape)
    cnt_ref[...] = jnp.broadcast_to(total, cnt_ref.shape)


def _attn_out(x2d, o, w_o, gain, router_split, tri):
    t, d = x2d.shape
    tm = TOK_TILE
    n_sub = d // LANES
    return pl.pallas_call(
        _attn_out_kernel,
        out_shape=[jax.ShapeDtypeStruct((t, d), F32),
                   jax.ShapeDtypeStruct((t * n_sub, LANES), F32),
                   jax.ShapeDtypeStruct((SUBLANES, t), F32),
                   jax.ShapeDtypeStruct((ROUTE_ROWS, LANES), F32)],
        grid=(t // tm,),
        in_specs=[pl.BlockSpec((tm, d), lambda i: (i, 0)),
                  pl.BlockSpec((tm, d), lambda i: (i, 0)),
                  _const_spec(w_o.shape), _const_spec((1, d)), _const_spec(router_split.shape),
                  _const_spec(tri.shape)],
        out_specs=[pl.BlockSpec((tm, d), lambda i: (i, 0)),
                   pl.BlockSpec((tm * n_sub, LANES), lambda i: (i, 0)),
                   pl.BlockSpec((SUBLANES, tm), lambda i: (0, i)),
                   _const_spec((ROUTE_ROWS, LANES))],
        scratch_shapes=[pltpu.VMEM((ROUTE_ROWS, LANES), F32)],
        compiler_params=_cparams(("arbitrary",)),
        name="attn_out_route",
    )(x2d, o, w_o, gain, router_split, tri)


def _dispatch_kernel(dest_ref, zpos_ref, zflag_ref, h_ref, dst_ref, zbuf_ref, zsem, sem):
    i = pl.program_id(0)
    tm = h_ref.shape[0]

    @pl.when(i == 0)
    def _():
        zbuf_ref[...] = jnp.zeros_like(zbuf_ref)

        def zcopy(e):
            return pltpu.make_async_copy(zbuf_ref, dst_ref.at[pl.ds(zpos_ref[e], MOE_TILE)], zsem.at[e])

        for e in range(2 * N_EXPERTS):
            @pl.when(zflag_ref[e] == 1)
            def _():
                zcopy(e).start()
        for e in range(2 * N_EXPERTS):
            @pl.when(zflag_ref[e] == 1)
            def _():
                zcopy(e).wait()

    def body(r, carry):
        base = 2 * (i * tm + r)
        for k in range(2):
            pltpu.make_async_copy(h_ref.at[r], dst_ref.at[dest_ref[base + k]], sem).start(priority=k)
        return carry

    lax.fori_loop(0, tm, body, 0, unroll=8)
    for k in range(2):
        pltpu.make_async_copy(h_ref, dst_ref.at[pl.ds(0, tm)], sem).wait()


def _dispatch(h3_rows, dest, zpos, zflag, n_dst):
    t = h3_rows.shape[0]
    tm = DISPATCH_TILE
    tail = h3_rows.shape[1:]
    return pl.pallas_call(
        _dispatch_kernel,
        out_shape=jax.ShapeDtypeStruct((n_dst,) + tail, F32),
        grid_spec=pltpu.PrefetchScalarGridSpec(
            num_scalar_prefetch=3, grid=(t // tm,),
            in_specs=[pl.BlockSpec((tm,) + tail, lambda i, *_: (i, 0, 0))],
            out_specs=pl.BlockSpec(memory_space=pl.ANY),
            scratch_shapes=[pltpu.VMEM((MOE_TILE,) + tail, F32),
                            pltpu.SemaphoreType.DMA((2 * N_EXPERTS,)), pltpu.SemaphoreType.DMA(())]),
        compiler_params=_cparams(("arbitrary",)),
        name="moe_dispatch",
    )(dest, zpos, zflag, h3_rows)


def _moe_kernel(te_ref, tr_ref, ts_ref, xs_ref, wg_ref, wu_ref, wd_ref, ys_ref, h_ref, acc_ref):
    i = pl.program_id(0)
    c = pl.program_id(1)
    last = pl.num_programs(1) - 1
    n_sub = h_ref.shape[1] // LANES
    grp = h_ref.shape[0] // MOE_SPLIT

    for part in range(MOE_SPLIT):
        r0 = part * grp
        rows = slice(r0, r0 + grp)
        live = tr_ref[i] > r0

        @pl.when(live & (c == 0))
        def _():
            for s in range(n_sub):
                h_ref[rows, s * LANES:(s + 1) * LANES] = (
                    xs_ref[pl.ds(r0 * n_sub + s, grp, stride=n_sub), :].astype(BF16))
            acc_ref[rows, :] = jnp.zeros((grp, acc_ref.shape[1]), F32)

        @pl.when(live)
        def _():
            h = h_ref[rows, :]
            gt = _dot(h, wg_ref[...])
            up = _dot(h, wu_ref[...])
            act = (gt * jax.nn.sigmoid(gt) * up).astype(BF16)
            acc_ref[rows, :] += _dot(act, wd_ref[...])

        @pl.when(live & (c == last))
        def _():
            for s in range(n_sub):
                ys_ref[pl.ds(r0 * n_sub + s, grp, stride=n_sub), :] = acc_ref[rows, s * LANES:(s + 1) * LANES]

        @pl.when(jnp.logical_not(live) & (c == last))
        def _():
            ys_ref[r0 * n_sub:(r0 + grp) * n_sub, :] = jnp.zeros((grp * n_sub, LANES), F32)


def _moe_experts(xs2d, w_gu, w_down, tile_expert, tile_rows, tile_src, d):
    n_sub = d // LANES
    n_tiles = xs2d.shape[0] // (MOE_TILE * n_sub)
    f = w_down.shape[1]
    nfc = f // MOE_FC
    blk = MOE_TILE * n_sub

    def ceff(i, c, tv):
        return jnp.where(tv[i] > 0, c, nfc - 1)

    return pl.pallas_call(
        _moe_kernel,
        out_shape=jax.ShapeDtypeStruct(xs2d.shape, F32),
        grid_spec=pltpu.PrefetchScalarGridSpec(
            num_scalar_prefetch=3, grid=(n_tiles, nfc),
            in_specs=[
                pl.BlockSpec((blk, LANES), lambda i, c, te, tv, ts: (ts[i], 0)),
                pl.BlockSpec((None, d, MOE_FC), lambda i, c, te, tv, ts: (te[i], 0, ceff(i, c, tv))),
                pl.BlockSpec((None, d, MOE_FC), lambda i, c, te, tv, ts: (te[i], 0, nfc + ceff(i, c, tv))),
                pl.BlockSpec((None, MOE_FC, d), lambda i, c, te, tv, ts: (te[i], ceff(i, c, tv), 0)),
            ],
            out_specs=pl.BlockSpec((blk, LANES), lambda i, c, te, tv, ts: (i, 0)),
            scratch_shapes=[pltpu.VMEM((MOE_TILE, d), BF16), pltpu.VMEM((MOE_TILE, d), F32)]),
        compiler_params=_cparams(("arbitrary", "arbitrary")),
        name="moe_experts",
    )(tile_expert, tile_rows, tile_src, xs2d, w_gu, w_gu, w_down)


def _combine_kernel(dest_ref, x_ref, ys_ref, gate_ref, g_ref, o_ref, ybuf_ref, sem):
    i = pl.program_id(0)
    n = pl.num_programs(0)
    tm, d = x_ref.shape
    n_sub = d // LANES

    def gather(tile, slot):
        def body(r, carry):
            base = 2 * (tile * tm + r)
            for k in range(2):
                src = ys_ref.at[pl.ds(pl.multiple_of(dest_ref[base + k] * n_sub, n_sub), n_sub)]
                dst = ybuf_ref.at[slot, pl.ds(pl.multiple_of((2 * r + k) * n_sub, n_sub), n_sub)]
                pltpu.make_async_copy(src, dst, sem.at[slot]).start(priority=k)
            return carry

        lax.fori_loop(0, tm, body, 0, unroll=8)

    @pl.when(i == 0)
    def _():
        gather(0, 0)

    @pl.when(i + 1 < n)
    def _():
        gather(i + 1, (i + 1) % 2)

    slot = i % 2
    pltpu.make_async_copy(ys_ref.at[pl.ds(0, 2 * tm * n_sub)], ybuf_ref.at[slot], sem.at[slot]).wait()

    y_ref = ybuf_ref.at[slot]
    gates = gate_ref[...]
    w1 = gates[:, 0:1]
    w2 = gates[:, 1:2]
    for s in range(n_sub):
        cols = slice(s * LANES, (s + 1) * LANES)
        y1 = y_ref[pl.ds(s, tm, stride=2 * n_sub), :]
        y2 = y_ref[pl.ds(n_sub + s, tm, stride=2 * n_sub), :]
        o_ref[:, cols] = x_ref[:, cols] + w1 * y1 + w2 * y2
    o_ref[...] = _rms(o_ref[...], g_ref[...])


def _combine(x2d, ys2d, dest, gates, gain):
    t, d = x2d.shape
    tm = TOK_TILE
    n_sub = d // LANES
    return pl.pallas_call(
        _combine_kernel,
        out_shape=jax.ShapeDtypeStruct((t, d), F32),
        grid_spec=pltpu.PrefetchScalarGridSpec(
            num_scalar_prefetch=1, grid=(t // tm,),
            in_specs=[pl.BlockSpec((tm, d), lambda i, *_: (i, 0)),
                      pl.BlockSpec(memory_space=pl.ANY),
                      pl.BlockSpec((tm, gates.shape[1]), lambda i, *_: (i, 0)),
                      pl.BlockSpec((1, d), lambda i, *_: (0, 0), pipeline_mode=pl.Buffered(1))],
            out_specs=pl.BlockSpec((tm, d), lambda i, *_: (i, 0)),
            scratch_shapes=[pltpu.VMEM((2, 2 * tm * n_sub, LANES), F32), pltpu.SemaphoreType.DMA((2,))]),
        compiler_params=_cparams(("arbitrary",)),
        name="moe_combine",
    )(dest, x2d, ys2d, gates, gain)


def _rope_tables(seq, rot_dim, theta):
    inv = 1.0 / (theta ** (np.arange(0, rot_dim, 2, dtype=np.float64) / rot_dim))
    ang = np.arange(seq, dtype=np.float64)[:, None] * inv[None, :]
    return np.cos(ang), np.sin(ang)


def _retention_tables():
    c = RET_CHUNK
    log_gamma = np.log(1.0 - 2.0 ** (-5.0 - np.arange(RET_HEADS, dtype=np.float64)))
    idx = np.arange(c, dtype=np.float64)
    rel = idx[:, None] - idx[None, :]
    dmat = np.where(rel[None] >= 0, np.exp(np.maximum(rel, 0.0)[None] * log_gamma[:, None, None]), 0.0)
    qd = np.exp((idx + 1.0)[None, :] * log_gamma[:, None])[:, :, None]
    kd = np.exp((c - 1.0 - idx)[None, :] * log_gamma[:, None])[:, :, None]
    cd = np.exp(c * log_gamma)[:, None, None]
    return tuple(jnp.asarray(tb, F32) for tb in (dmat, qd, kd, cd))


def _attn_rope_tables(seq):
    cos, sin = _rope_tables(seq, ROPE_DIM, ROPE_THETA)
    half = ROPE_DIM // 2
    pad = DIFF_HD - ROPE_DIM
    ones = np.ones((seq, pad))
    zeros = np.zeros((seq, pad))
    zh = np.zeros((seq, half))
    ctab = np.concatenate([cos, cos, ones], axis=1)
    s1tab = np.concatenate([-sin, zh, zeros], axis=1)
    s2tab = np.concatenate([zh, sin, zeros], axis=1)
    rep = LANES // DIFF_HD
    return tuple(jnp.asarray(np.tile(tb, (1, rep)), F32) for tb in (ctab, s1tab, s2tab))


def kernel(x, ln_mix, ln_ffn, ret_w_in, ret_w_o, kv_norm, w_kv, diff_w_q, lam_q1, lam_k1, lam_q2, lam_k2,
           diff_subln, diff_w_o, ffn_w_gu, ffn_w_down, moe_router, moe_w_gu, moe_w_down, final_norm):
    batch, seq, d = x.shape
    t = batch * seq
    assert ln_mix.shape[0] == 2 and ret_w_in.shape[0] == 1 and diff_w_q.shape[0] == 1
    assert seq % TOK_TILE == 0 and TOK_TILE % RET_CHUNK == 0 and ATT_TQ % CHUNK == 0
    assert seq % ATT_TK == 0 and ATT_TK % ATT_TQ == 0
    x2d = x.reshape(t, d)
    row = lambda g: g.reshape(1, -1)

    cos_r, sin_r = (jnp.asarray(tb, F32) for tb in _rope_tables(seq, RET_QK, RET_THETA))
    proj = _ret_in_proj(x2d, row(ln_mix[0]), ret_w_in[0].astype(BF16), cos_r, sin_r, seq)
    ret_o = _retention(proj, *_retention_tables(), batch, seq)
    x2 = _ret_out_ffn(x2d, ret_o, ret_w_o[0].astype(BF16), row(ln_ffn[0]),
                      ffn_w_gu[0].astype(BF16), ffn_w_down[0].astype(BF16))

    lambda_init = 0.8 - 0.6 * math.exp(-0.3 * 1)
    q, k, vt = _qkv_proj(x2, row(ln_mix[1]), row(kv_norm), diff_w_q[0].astype(BF16),
                         w_kv[:, :d].astype(BF16), w_kv[:, d:].T.astype(BF16),
                         *_attn_rope_tables(seq), batch, seq)
    lam_vecs = jnp.stack([lam_q1[0], lam_k1[0], lam_q2[0], lam_k2[0]]).astype(F32)
    n_exp, _, two_f = moe_w_gu[0].shape
    att, w_gu_bf, w_down_bf = _diff_attention(
        q, k, vt, lam_vecs, row(diff_subln[0]), moe_w_gu[0].reshape(n_exp * d, two_f),
        moe_w_down[0].reshape(n_exp * (two_f // 2), d), batch, seq, lambda_init)
    router_pad = jnp.pad(moe_router[0], ((0, 0), (0, LANES - N_EXPERTS)))
    router_hi = router_pad.astype(BF16)
    router_lo = (router_pad - router_hi.astype(F32)).astype(BF16)
    tri = jnp.asarray(np.arange(TOK_TILE)[:, None] < np.arange(TOK_TILE)[None, :], BF16)
    x3, h3_rows, route, cnt = _attn_out(x2, att, diff_w_o[0].astype(BF16), row(ln_ffn[1]),
                                        jnp.concatenate([router_hi, router_lo], axis=1), tri)

    n_sub = d // LANES
    n_rows = 2 * t + N_EXPERTS * MOE_TILE
    n_tiles = n_rows // MOE_TILE
    expert = route[0:2].T.astype(jnp.int32)
    rank = route[2:4].T.astype(jnp.int32)
    gates = route[4:6].T
    counts = cnt[:N_EXPERTS, 0].astype(jnp.int32)
    padded = (counts + MOE_TILE - 1) // MOE_TILE * MOE_TILE
    seg_end = jnp.cumsum(padded)
    seg_start = seg_end - padded
    dest = (jnp.sum(jnp.where(expert[..., None] == jnp.arange(N_EXPERTS), seg_start, 0), axis=-1)
            + rank).reshape(-1)
    tile_row = jnp.arange(n_tiles, dtype=jnp.int32) * MOE_TILE
    tile_valid = (tile_row < seg_end[-1]).astype(jnp.int32)
    n_valid = seg_end[-1] // MOE_TILE
    tile_src = jnp.minimum(jnp.arange(n_tiles, dtype=jnp.int32), jnp.maximum(n_valid - 1, 0))
    tile_expert = jnp.minimum(jnp.sum(tile_src[:, None] * MOE_TILE >= seg_end[None, :], axis=1),
                              N_EXPERTS - 1).astype(jnp.int32)
    token_end = jnp.sum(jnp.where(tile_expert[:, None] == jnp.arange(N_EXPERTS), seg_start + counts, 0), axis=1)
    tile_rows = (tile_valid * jnp.clip(token_end - tile_row, 0, MOE_TILE)).astype(jnp.int32)
    zflag = jnp.concatenate([(padded > 0).astype(jnp.int32), 1 - tile_valid[-N_EXPERTS:]])
    zpos = jnp.concatenate([jnp.maximum(seg_end - MOE_TILE, 0), tile_row[-N_EXPERTS:]]).astype(jnp.int32)

    xs = _dispatch(h3_rows.reshape(t, n_sub, LANES), dest, zpos, zflag, n_rows)
    ys = _moe_experts(xs.reshape(n_rows * n_sub, LANES), w_gu_bf.reshape(n_exp, d, two_f),
                      w_down_bf.reshape(n_exp, two_f // 2, d), tile_expert, tile_rows, tile_src, d)
    out = _combine(x3, ys, dest, gates, row(final_norm))
    return out.reshape(batch, seq, d)
```

```python
import functools
import math

import jax
import jax.numpy as jnp
import numpy as np
from jax import lax
from jax.experimental import pallas as pl
from jax.experimental.pallas import tpu as pltpu

F32 = jnp.float32
BF16 = jnp.bfloat16

EPS = 1e-6
CHUNK = 64
RET_QK = 256
RET_V = 512
RET_HEADS = 4
RET_THETA = 10000.0
DIFF_HD = 64
DIFF_HEADS = 8
ROPE_THETA = 500000.0
ROPE_DIM = 16
N_EXPERTS = 8

LANES = 128
SUBLANES = 8
MXU_COLS = 256
VMEM_LIMIT = 56 * 1024 * 1024

RET_CHUNK = 256
TOK_TILE = 512
DISPATCH_TILE = 1024
ATT_TQ = 512
ATT_TK = 512
ATT_HEADS = 4
ATT_ONES = 16
MOE_TILE = 1024
MOE_SPLIT = 2
MOE_FC = 1792


def _cparams(sem, vmem=VMEM_LIMIT):
    return pltpu.CompilerParams(dimension_semantics=sem, vmem_limit_bytes=vmem)


def _const_spec(shape):
    nd = len(shape)
    return pl.BlockSpec(shape, lambda *_: (0,) * nd, pipeline_mode=pl.Buffered(1))


def _rms(x, g):
    return x * lax.rsqrt(jnp.mean(x * x, axis=-1, keepdims=True) + EPS) * g


def _dot(a, b):
    return jnp.dot(a, b, preferred_element_type=F32)


def _ret_in_kernel(x_ref, g_ref, w_ref, cos_ref, sin_ref, o_ref):
    h = _rms(x_ref[...], g_ref[...]).astype(BF16)
    cos = cos_ref[...]
    sin = sin_ref[...]
    d_qk = RET_HEADS * RET_QK
    half = RET_QK // 2
    for c in range(2 * RET_HEADS):
        c0 = c * RET_QK
        acc = _dot(h, w_ref[:, c0:c0 + RET_QK])
        x1 = acc[:, :half]
        x2 = acc[:, half:]
        scale = 1.0 if c < RET_HEADS else RET_QK ** -0.5
        o_ref[:, c0:c0 + half] = ((x1 * cos - x2 * sin) * scale).astype(BF16)
        o_ref[:, c0 + half:c0 + RET_QK] = ((x2 * cos + x1 * sin) * scale).astype(BF16)
    n_rest = (w_ref.shape[1] - 2 * d_qk) // RET_V
    for c in range(n_rest):
        c0 = 2 * d_qk + c * RET_V
        o_ref[:, c0:c0 + RET_V] = _dot(h, w_ref[:, c0:c0 + RET_V]).astype(BF16)


def _ret_in_proj(x2d, gain, w, cos, sin, seq):
    t, d = x2d.shape
    n = w.shape[1]
    tm = TOK_TILE
    n_pos = seq // tm
    return pl.pallas_call(
        _ret_in_kernel,
        out_shape=jax.ShapeDtypeStruct((t, n), BF16),
        grid=(t // tm,),
        in_specs=[
            pl.BlockSpec((tm, d), lambda i: (i, 0)),
            _const_spec((1, d)),
            _const_spec((d, n)),
            pl.BlockSpec((tm, RET_QK // 2), lambda i: (i % n_pos, 0)),
            pl.BlockSpec((tm, RET_QK // 2), lambda i: (i % n_pos, 0)),
        ],
        out_specs=pl.BlockSpec((tm, n), lambda i: (i, 0)),
        compiler_params=_cparams(("parallel",)),
        name="ret_in_proj",
    )(x2d, gain, w, cos, sin)


def _retention_kernel(q_ref, k_ref, v_ref, g_ref, dmat_ref, qd_ref, kd_ref, cd_ref, o_ref, state_ref):
    @pl.when(pl.program_id(1) == 0)
    def _():
        state_ref[...] = jnp.zeros_like(state_ref)

    n_chunks = q_ref.shape[0] // RET_CHUNK
    for h in range(RET_HEADS):
        for c in range(n_chunks):
            rows = slice(c * RET_CHUNK, (c + 1) * RET_CHUNK)
            q = q_ref[rows, h * RET_QK:(h + 1) * RET_QK]
            k = k_ref[rows, h * RET_QK:(h + 1) * RET_QK]
            v = v_ref[rows, h * RET_V:(h + 1) * RET_V]
            state = state_ref[h]
            s = lax.dot_general(q, k, (((1,), (1,)), ((), ())), preferred_element_type=F32)
            s = s * dmat_ref[h]
            lhs = jnp.concatenate([s.astype(BF16), q * qd_ref[h]], axis=1)
            rhs = jnp.concatenate([v, state.astype(BF16)], axis=0)
            o = _dot(lhs, rhs)
            state_ref[h] = state * cd_ref[h] + lax.dot_general(
                k * kd_ref[h], v, (((0,), (0,)), ((), ())), preferred_element_type=F32)
            mu = jnp.mean(o, axis=-1, keepdims=True)
            oc = o - mu
            var = jnp.mean(oc * oc, axis=-1, keepdims=True)
            on = oc * lax.rsqrt(var + EPS)
            gate = g_ref[rows, h * RET_V:(h + 1) * RET_V]
            o_ref[rows, h * RET_V:(h + 1) * RET_V] = on.astype(BF16) * (gate * jax.nn.sigmoid(gate))


def _retention(proj, dmat, qd, kd, cd, batch, seq):
    t = proj.shape[0]
    tb = TOK_TILE
    nj = seq // tb
    d_qk = RET_HEADS * RET_QK
    d_v = RET_HEADS * RET_V
    row = lambda b, j: b * nj + j
    return pl.pallas_call(
        _retention_kernel,
        out_shape=jax.ShapeDtypeStruct((t, d_v), BF16),
        grid=(batch, nj),
        in_specs=[
            pl.BlockSpec((tb, d_qk), lambda b, j: (row(b, j), 0)),
            pl.BlockSpec((tb, d_qk), lambda b, j: (row(b, j), 1)),
            pl.BlockSpec((tb, d_v), lambda b, j: (row(b, j), 1)),
            pl.BlockSpec((tb, d_v), lambda b, j: (row(b, j), 2)),
            _const_spec(dmat.shape),
            _const_spec(qd.shape),
            _const_spec(kd.shape),
            _const_spec(cd.shape),
        ],
        out_specs=pl.BlockSpec((tb, d_v), lambda b, j: (row(b, j), 0)),
        scratch_shapes=[pltpu.VMEM((RET_HEADS, RET_QK, RET_V), F32)],
        compiler_params=_cparams(("parallel", "arbitrary")),
        name="retention",
    )(proj, proj, proj, proj, dmat, qd, kd, cd)


def _ret_out_ffn_kernel(x_ref, o_ref, wo_ref, g_ref, wgu_ref, wd_ref, out_ref, act_ref, *, fc):
    x1 = x_ref[...] + _dot(o_ref[...], wo_ref[...])
    h = _rms(x1, g_ref[...]).astype(BF16)
    f = wd_ref.shape[0]
    for c in range(f // fc):
        gt = _dot(h, wgu_ref[:, c * fc:(c + 1) * fc])
        up = _dot(h, wgu_ref[:, f + c * fc:f + (c + 1) * fc])
        act_ref[:, c * fc:(c + 1) * fc] = (gt * jax.nn.sigmoid(gt) * up).astype(BF16)
    out_ref[...] = x1 + _dot(act_ref[...], wd_ref[...])


def _ret_out_ffn(x2d, o, w_o, gain, w_gu, w_down):
    t, d = x2d.shape
    tm = TOK_TILE
    f = w_down.shape[0]
    return pl.pallas_call(
        functools.partial(_ret_out_ffn_kernel, fc=256),
        out_shape=jax.ShapeDtypeStruct((t, d), F32),
        grid=(t // tm,),
        in_specs=[
            pl.BlockSpec((tm, d), lambda i: (i, 0)),
            pl.BlockSpec((tm, o.shape[1]), lambda i: (i, 0)),
            _const_spec(w_o.shape),
            _const_spec((1, d)),
            _const_spec(w_gu.shape),
            _const_spec(w_down.shape),
        ],
        out_specs=pl.BlockSpec((tm, d), lambda i: (i, 0)),
        scratch_shapes=[pltpu.VMEM((tm, f), BF16)],
        compiler_params=_cparams(("parallel",)),
        name="ret_out_ffn",
    )(x2d, o, w_o, gain, w_gu, w_down)


def _rope16(x, ctab, s1tab, s2tab):
    half = ROPE_DIM // 2
    return (x * ctab + pltpu.roll(x, LANES - half, 1) * s1tab + pltpu.roll(x, half, 1) * s2tab)


def _qkv_kernel(x_ref, gq_ref, gkv_ref, wq_ref, wk_ref, wvt_ref, c_ref, s1_ref, s2_ref, q_ref, k_ref, vt_ref):
    x = x_ref[...]
    xn = x * lax.rsqrt(jnp.mean(x * x, axis=-1, keepdims=True) + EPS)
    hq = (xn * gq_ref[...]).astype(BF16)
    hkv = (xn * gkv_ref[...]).astype(BF16)
    ctab, s1tab, s2tab = c_ref[...], s1_ref[...], s2_ref[...]
    d = q_ref.shape[1]
    q_scale = DIFF_HD ** -0.5 * math.log2(math.e)
    for c in range(d // MXU_COLS):
        c0 = c * MXU_COLS
        qa = _dot(hq, wq_ref[:, c0:c0 + MXU_COLS])
        ka = _dot(hkv, wk_ref[:, c0:c0 + MXU_COLS])
        for l0 in range(0, MXU_COLS, LANES):
            cols = slice(c0 + l0, c0 + l0 + LANES)
            q_ref[:, cols] = (_rope16(qa[:, l0:l0 + LANES], ctab, s1tab, s2tab) * q_scale).astype(BF16)
            k_ref[:, cols] = _rope16(ka[:, l0:l0 + LANES], ctab, s1tab, s2tab).astype(BF16)
    vt_ref[...] = lax.dot_general(wvt_ref[...], hkv, (((1,), (1,)), ((), ())),
                                  preferred_element_type=F32).astype(BF16)


def _qkv_proj(x2d, gq, gkv, w_q, w_k, w_vt, ctab, s1tab, s2tab, batch, seq):
    t, d = x2d.shape
    tm = TOK_TILE
    n_pos = seq // tm
    tok = pl.BlockSpec((tm, d), lambda i: (i, 0))
    tab = pl.BlockSpec((tm, LANES), lambda i: (i % n_pos, 0))
    return pl.pallas_call(
        _qkv_kernel,
        out_shape=[jax.ShapeDtypeStruct((t, d), BF16), jax.ShapeDtypeStruct((t, d), BF16),
                   jax.ShapeDtypeStruct((batch * d, seq), BF16)],
        grid=(t // tm,),
        in_specs=[tok, _const_spec((1, d)), _const_spec((1, d)), _const_spec(w_q.shape),
                  _const_spec(w_k.shape), _const_spec(w_vt.shape), tab, tab, tab],
        out_specs=[tok, tok, pl.BlockSpec((d, tm), lambda i: (i // n_pos, i % n_pos))],
        compiler_params=_cparams(("parallel",)),
        name="qkv_proj",
    )(x2d, gq, gkv, w_q, w_k, w_vt, ctab, s1tab, s2tab)


def _attn_kernel(lam_ref, bias_ref, q_ref, qn_ref, k_ref, vt_ref, sub_ref, wa_ref, wb_ref,
                 o_ref, wa_out_ref, wb_out_ref, vta_ref, s_ref, mt_ref, acc_ref, *, lambda_init):
    i = pl.program_id(2)
    tq = q_ref.shape[0]
    dv = 2 * DIFF_HD
    n_heads, n_kv, _, tk = vta_ref.shape
    chains = [(h, c) for h in range(n_heads) for c in range(2)]

    wa_out_ref[...] = wa_ref[...].astype(BF16)
    wb_out_ref[...] = wb_ref[...].astype(BF16)

    @pl.when(i == 0)
    def _():
        for h in range(n_heads):
            for j in range(n_kv):
                vta_ref[h, j, 0:dv, :] = vt_ref[h * dv:(h + 1) * dv, j * tk:(j + 1) * tk]
                vta_ref[h, j, dv:, :] = jnp.ones((ATT_ONES, tk), BF16)

    lane = lax.broadcasted_iota(jnp.int32, (tq, dv), 1)

    def split_heads(ref):
        out = []
        for h in range(n_heads):
            q = ref[:, h * dv:(h + 1) * dv]
            zero = jnp.zeros_like(q)
            out.append((jnp.where(lane < DIFF_HD, q, zero), jnp.where(lane >= DIFF_HD, q, zero)))
        return out

    def first_tiles(qi):
        n = (qi * tq) // tk
        return n, 1 + (qi * tq - n * tk) // tq

    n_full, tail_bias = first_tiles(i)

    def tile_max(s):
        m8 = jnp.max(s.reshape(tk // SUBLANES, SUBLANES, tq), axis=0)
        return jnp.max(m8, axis=0, keepdims=True)

    def score_chain(j, x, qsplit, n_vis, tail):
        h, c = chains[x]
        bias = bias_ref[jnp.where(j == n_vis, tail, 0)]
        kt = k_ref[pl.ds(pl.multiple_of(j * tk, tk), tk), h * dv:(h + 1) * dv]
        s = lax.dot_general(kt, qsplit[h][c], (((1,), (1,)), ((), ())), preferred_element_type=F32) + bias
        s_ref[x] = s
        return tile_max(s)

    @pl.when(i == 0)
    def _():
        qc0 = split_heads(q_ref)
        for x in range(len(chains)):
            mt_ref[x] = score_chain(0, x, qc0, n_full, tail_bias)

    qc = split_heads(q_ref)
    acc_ref[...] = jnp.zeros_like(acc_ref)
    m_init = tuple(jnp.full((1, tq), -jnp.inf, F32) for _ in chains)

    def step(j, carry, next_scores):
        m_run, m_tile = carry
        m_out, m_next = [], []
        for x, (h, c) in enumerate(chains):
            m_new = jnp.maximum(m_run[x], m_tile[x])
            alpha = jnp.exp2(m_run[x] - m_new)
            p = jnp.exp2(s_ref[x] - m_new).astype(BF16)
            acc_ref[x] = alpha * acc_ref[x] + _dot(vta_ref[h, j], p)
            m_out.append(m_new)
            if next_scores is not None:
                m_next.append(next_scores(x))
        return tuple(m_out), tuple(m_next)

    first = tuple(mt_ref[x] for x in range(len(chains)))
    carry = lax.fori_loop(
        0, n_full,
        lambda j, carry: step(j, carry, lambda x: score_chain(j + 1, x, qc, n_full, tail_bias)),
        (m_init, first))

    @pl.when(i + 1 < pl.num_programs(2))
    def _():
        qn = split_heads(qn_ref)
        n_vis, tail = first_tiles(i + 1)
        _, m_first = step(n_full, carry, lambda x: score_chain(0, x, qn, n_vis, tail))
        for x in range(len(chains)):
            mt_ref[x] = m_first[x]

    @pl.when(i + 1 == pl.num_programs(2))
    def _():
        step(n_full, carry, None)

    lam_v = lam_ref[...]
    lam = (jnp.exp(jnp.sum(lam_v[0:1] * lam_v[1:2], axis=-1, keepdims=True))
           - jnp.exp(jnp.sum(lam_v[2:3] * lam_v[3:4], axis=-1, keepdims=True)) + lambda_init)
    for h in range(n_heads):
        a0 = acc_ref[2 * h]
        a1 = acc_ref[2 * h + 1]
        ot = a0[0:dv] / a0[dv:dv + 1] - lam * (a1[0:dv] / a1[dv:dv + 1])
        o = _rms(ot.T, sub_ref[...]) * (1.0 - lambda_init)
        o_ref[:, h * dv:(h + 1) * dv] = o.astype(BF16)


def _attn_bias(tq, tk):
    key = np.arange(tk)[:, None] // CHUNK
    tiles = [np.zeros((tk, tq))]
    for r in range(tk // tq):
        qry = (r * tq + np.arange(tq))[None, :] // CHUNK
        tiles.append(np.where(key <= qry, 0.0, -1e30))
    return jnp.asarray(np.stack(tiles), F32)


def _diff_attention(q, k, vt, lam_vecs, subln, w_a, w_b, batch, seq, lambda_init):
    t, d = q.shape
    tq = ATT_TQ
    nq = seq // tq
    dv = 2 * DIFF_HD
    hp = ATT_HEADS
    n_groups = DIFF_HEADS // hp
    n_steps = batch * n_groups * nq
    bias = _attn_bias(tq, ATT_TK)
    step = lambda b, g, i: ((b * n_groups + g) * nq + i, 0)
    slabs = [pl.BlockSpec((w.shape[0] // n_steps, w.shape[1]), step) for w in (w_a, w_b)]
    assert all(w.shape[0] % (n_steps * 2 * SUBLANES) == 0 for w in (w_a, w_b))
    return pl.pallas_call(
        functools.partial(_attn_kernel, lambda_init=lambda_init),
        out_shape=[jax.ShapeDtypeStruct((t, d), BF16),
                   jax.ShapeDtypeStruct(w_a.shape, BF16), jax.ShapeDtypeStruct(w_b.shape, BF16)],
        grid=(batch, n_groups, nq),
        in_specs=[
            _const_spec(lam_vecs.shape),
            _const_spec(bias.shape),
            pl.BlockSpec((tq, hp * dv), lambda b, g, i: (b * nq + i, g)),
            pl.BlockSpec((tq, hp * dv), lambda b, g, i: (b * nq + jnp.minimum(i + 1, nq - 1), g)),
            pl.BlockSpec((seq, hp * dv), lambda b, g, i: (b, g)),
            pl.BlockSpec((hp * dv, seq), lambda b, g, i: (b * n_groups + g, 0)),
            _const_spec((1, dv)),
        ] + slabs,
        out_specs=[pl.BlockSpec((tq, hp * dv), lambda b, g, i: (b * nq + i, g))] + slabs,
        scratch_shapes=[pltpu.VMEM((hp, seq // ATT_TK, dv + ATT_ONES, ATT_TK), BF16),
                        pltpu.VMEM((2 * hp, ATT_TK, tq), F32),
                        pltpu.VMEM((2 * hp, 1, tq), F32),
                        pltpu.VMEM((2 * hp, dv + ATT_ONES, tq), F32)],
        compiler_params=_cparams(("parallel", "parallel", "arbitrary")),
        name="diff_attn",
    )(lam_vecs, bias, q, q, k, vt, subln, w_a, w_b)


ROUTE_ROWS = 2 * SUBLANES


def _attn_out_kernel(x_ref, o_ref, wo_ref, g_ref, r_ref, tri_ref, x3_ref, h3_ref, route_ref, cnt_ref, carry_ref):
    @pl.when(pl.program_id(0) == 0)
    def _():
        carry_ref[...] = jnp.zeros_like(carry_ref)

    tm = x_ref.shape[0]
    x3 = x_ref[...] + _dot(o_ref[...], wo_ref[...])
    x3_ref[...] = x3
    h3 = _rms(x3, g_ref[...])
    for s in range(h3.shape[1] // LANES):
        h3_ref[pl.ds(s, tm, stride=SUBLANES), :] = h3[:, s * LANES:(s + 1) * LANES]

    hi = h3.astype(BF16)
    lo = (h3 - hi.astype(F32)).astype(BF16)
    both = _dot(hi, r_ref[...])
    logits = both[:, :LANES] + both[:, LANES:] + _dot(lo, r_ref[:, :LANES])
    lt = logits.T[0:ROUTE_ROWS]
    sub = lax.broadcasted_iota(jnp.int32, lt.shape, 0).astype(F32)
    lt = jnp.where(sub < N_EXPERTS, lt, -jnp.inf)
    v1 = jnp.max(lt, axis=0, keepdims=True)
    i1 = jnp.min(jnp.where(lt == v1, sub, float(ROUTE_ROWS)), axis=0, keepdims=True)
    lt2 = jnp.where(sub == i1, -jnp.inf, lt)
    v2 = jnp.max(lt2, axis=0, keepdims=True)
    i2 = jnp.min(jnp.where(lt2 == v2, sub, float(ROUTE_ROWS)), axis=0, keepdims=True)
    e = jnp.exp(v2 - v1)
    w1 = 1.0 / (1.0 + e)
    w2 = e / (1.0 + e)
    oh1 = sub == i1
    oh2 = sub == i2
    assign = jnp.where(oh1 | oh2, 1.0, 0.0)
    excl = _dot(assign.astype(BF16), tri_ref[...]) + carry_ref[:, 0:1]
    r1 = jnp.sum(jnp.where(oh1, excl, 0.0), axis=0, keepdims=True)
    r2 = jnp.sum(jnp.where(oh2, excl, 0.0), axis=0, keepdims=True)
    route = jnp.zeros_like(lt)
    for row, val in enumerate((i1, i2, r1, r2, w1, w2)):
        route = jnp.where(sub == float(row), val, route)
    route_ref[...] = route[0:SUBLANES]
    total = carry_ref[:, 0:1] + jnp.sum(assign, axis=1, keepdims=True)
    carry_ref[...] = jnp.broadcast_to(total, carry_ref.shape)
    cnt_ref[...] = jnp.broadcast_to(total, cnt_ref.shape)


def _attn_out(x2d, o, w_o, gain, router_split, tri):
    t, d = x2d.shape
    tm = TOK_TILE
    n_sub = d // LANES
    return pl.pallas_call(
        _attn_out_kernel,
        out_shape=[jax.ShapeDtypeStruct((t, d), F32),
                   jax.ShapeDtypeStruct((t * n_sub, LANES), F32),
                   jax.ShapeDtypeStruct((SUBLANES, t), F32),
                   jax.ShapeDtypeStruct((ROUTE_ROWS, LANES), F32)],
        grid=(t // tm,),
        in_specs=[pl.BlockSpec((tm, d), lambda i: (i, 0)),
                  pl.BlockSpec((tm, d), lambda i: (i, 0)),
                  _const_spec(w_o.shape), _const_spec((1, d)), _const_spec(router_split.shape),
                  _const_spec(tri.shape)],
        out_specs=[pl.BlockSpec((tm, d), lambda i: (i, 0)),
                   pl.BlockSpec((tm * n_sub, LANES), lambda i: (i, 0)),
                   pl.BlockSpec((SUBLANES, tm), lambda i: (0, i)),
                   _const_spec((ROUTE_ROWS, LANES))],
        scratch_shapes=[pltpu.VMEM((ROUTE_ROWS, LANES), F32)],
        compiler_params=_cparams(("arbitrary",)),
        name="attn_out_route",
    )(x2d, o, w_o, gain, router_split, tri)


def _dispatch_kernel(dest_ref, zpos_ref, zflag_ref, h_ref, dst_ref, zbuf_ref, zsem, sem):
    i = pl.program_id(0)
    tm = h_ref.shape[0]

    @pl.when(i == 0)
    def _():
        zbuf_ref[...] = jnp.zeros_like(zbuf_ref)

        def zcopy(e):
            return pltpu.make_async_copy(zbuf_ref, dst_ref.at[pl.ds(zpos_ref[e], MOE_TILE)], zsem.at[e])

        for e in range(2 * N_EXPERTS):
            @pl.when(zflag_ref[e] == 1)
            def _():
                zcopy(e).start()
        for e in range(2 * N_EXPERTS):
            @pl.when(zflag_ref[e] == 1)
            def _():
                zcopy(e).wait()

    n_tok = dest_ref.shape[0] // 2

    def body(r, carry):
        for k in range(2):
            pltpu.make_async_copy(h_ref.at[r], dst_ref.at[dest_ref[k * n_tok + i * tm + r]],
                                  sem).start(priority=k)
        return carry

    lax.fori_loop(0, tm, body, 0, unroll=8)
    for k in range(2):
        pltpu.make_async_copy(h_ref, dst_ref.at[pl.ds(0, tm)], sem).wait()


def _dispatch(h3_rows, dest, zpos, zflag, n_dst):
    t = h3_rows.shape[0]
    tm = DISPATCH_TILE
    tail = h3_rows.shape[1:]
    return pl.pallas_call(
        _dispatch_kernel,
        out_shape=jax.ShapeDtypeStruct((n_dst,) + tail, F32),
        grid_spec=pltpu.PrefetchScalarGridSpec(
            num_scalar_prefetch=3, grid=(t // tm,),
            in_specs=[pl.BlockSpec((tm,) + tail, lambda i, *_: (i, 0, 0))],
            out_specs=pl.BlockSpec(memory_space=pl.ANY),
            scratch_shapes=[pltpu.VMEM((MOE_TILE,) + tail, F32),
                            pltpu.SemaphoreType.DMA((2 * N_EXPERTS,)), pltpu.SemaphoreType.DMA(())]),
        compiler_params=_cparams(("arbitrary",)),
        name="moe_dispatch",
    )(dest, zpos, zflag, h3_rows)


def _moe_kernel(te_ref, tr_ref, ts_ref, xs_ref, wg_ref, wu_ref, wd_ref, ys_ref, h_ref, acc_ref):
    i = pl.program_id(0)
    c = pl.program_id(1)
    last = pl.num_programs(1) - 1
    n_sub = h_ref.shape[1] // LANES
    grp = h_ref.shape[0] // MOE_SPLIT

    for part in range(MOE_SPLIT):
        r0 = part * grp
        rows = slice(r0, r0 + grp)
        live = tr_ref[i] > r0

        @pl.when(live & (c == 0))
        def _():
            for s in range(n_sub):
                h_ref[rows, s * LANES:(s + 1) * LANES] = (
                    xs_ref[pl.ds(r0 * n_sub + s, grp, stride=n_sub), :].astype(BF16))
            acc_ref[rows, :] = jnp.zeros((grp, acc_ref.shape[1]), F32)

        @pl.when(live)
        def _():
            h = h_ref[rows, :]
            gt = _dot(h, wg_ref[...])
            up = _dot(h, wu_ref[...])
            act = (gt * jax.nn.sigmoid(gt) * up).astype(BF16)
            acc_ref[rows, :] += _dot(act, wd_ref[...])

        @pl.when(live & (c == last))
        def _():
            for s in range(n_sub):
                ys_ref[pl.ds(r0 * n_sub + s, grp, stride=n_sub), :] = acc_ref[rows, s * LANES:(s + 1) * LANES]

        @pl.when(jnp.logical_not(live) & (c == last))
        def _():
            ys_ref[r0 * n_sub:(r0 + grp) * n_sub, :] = jnp.zeros((grp * n_sub, LANES), F32)


def _moe_experts(xs2d, w_gu, w_down, tile_expert, tile_rows, tile_src, d):
    n_sub = d // LANES
    n_tiles = xs2d.shape[0] // (MOE_TILE * n_sub)
    f = w_down.shape[1]
    nfc = f // MOE_FC
    blk = MOE_TILE * n_sub

    def ceff(i, c, tv):
        return jnp.where(tv[i] > 0, c, nfc - 1)

    return pl.pallas_call(
        _moe_kernel,
        out_shape=jax.ShapeDtypeStruct(xs2d.shape, F32),
        grid_spec=pltpu.PrefetchScalarGridSpec(
            num_scalar_prefetch=3, grid=(n_tiles, nfc),
            in_specs=[
                pl.BlockSpec((blk, LANES), lambda i, c, te, tv, ts: (ts[i], 0)),
                pl.BlockSpec((None, d, MOE_FC), lambda i, c, te, tv, ts: (te[i], 0, ceff(i, c, tv))),
                pl.BlockSpec((None, d, MOE_FC), lambda i, c, te, tv, ts: (te[i], 0, nfc + ceff(i, c, tv))),
                pl.BlockSpec((None, MOE_FC, d), lambda i, c, te, tv, ts: (te[i], ceff(i, c, tv), 0)),
            ],
            out_specs=pl.BlockSpec((blk, LANES), lambda i, c, te, tv, ts: (i, 0)),
            scratch_shapes=[pltpu.VMEM((MOE_TILE, d), BF16), pltpu.VMEM((MOE_TILE, d), F32)]),
        compiler_params=_cparams(("arbitrary", "arbitrary")),
        name="moe_experts",
    )(tile_expert, tile_rows, tile_src, xs2d, w_gu, w_gu, w_down)


def _combine_kernel(dest_ref, x_ref, ys_ref, gate_ref, g_ref, o_ref, ybuf_ref, sem):
    i = pl.program_id(0)
    n = pl.num_programs(0)
    tm, d = x_ref.shape
    n_sub = d // LANES

    def gather(tile, slot):
        def body(r, carry):
            for k in range(2):
                row = dest_ref[k * (dest_ref.shape[0] // 2) + tile * tm + r]
                src = ys_ref.at[pl.ds(pl.multiple_of(row * n_sub, n_sub), n_sub)]
                dst = ybuf_ref.at[slot, pl.ds(pl.multiple_of((2 * r + k) * n_sub, n_sub), n_sub)]
                pltpu.make_async_copy(src, dst, sem.at[slot]).start(priority=k)
            return carry

        lax.fori_loop(0, tm, body, 0, unroll=8)

    @pl.when(i == 0)
    def _():
        gather(0, 0)

    @pl.when(i + 1 < n)
    def _():
        gather(i + 1, (i + 1) % 2)

    slot = i % 2
    pltpu.make_async_copy(ys_ref.at[pl.ds(0, 2 * tm * n_sub)], ybuf_ref.at[slot], sem.at[slot]).wait()

    y_ref = ybuf_ref.at[slot]
    gates = gate_ref[...]
    w1 = gates[:, 0:1]
    w2 = gates[:, 1:2]
    for s in range(n_sub):
        cols = slice(s * LANES, (s + 1) * LANES)
        y1 = y_ref[pl.ds(s, tm, stride=2 * n_sub), :]
        y2 = y_ref[pl.ds(n_sub + s, tm, stride=2 * n_sub), :]
        o_ref[:, cols] = x_ref[:, cols] + w1 * y1 + w2 * y2
    o_ref[...] = _rms(o_ref[...], g_ref[...])


def _combine(x2d, ys2d, dest, gates, gain):
    t, d = x2d.shape
    tm = TOK_TILE
    n_sub = d // LANES
    return pl.pallas_call(
        _combine_kernel,
        out_shape=jax.ShapeDtypeStruct((t, d), F32),
        grid_spec=pltpu.PrefetchScalarGridSpec(
            num_scalar_prefetch=1, grid=(t // tm,),
            in_specs=[pl.BlockSpec((tm, d), lambda i, *_: (i, 0)),
                      pl.BlockSpec(memory_space=pl.ANY),
                      pl.BlockSpec((tm, gates.shape[1]), lambda i, *_: (i, 0)),
                      pl.BlockSpec((1, d), lambda i, *_: (0, 0), pipeline_mode=pl.Buffered(1))],
            out_specs=pl.BlockSpec((tm, d), lambda i, *_: (i, 0)),
            scratch_shapes=[pltpu.VMEM((2, 2 * tm * n_sub, LANES), F32), pltpu.SemaphoreType.DMA((2,))]),
        compiler_params=_cparams(("arbitrary",)),
        name="moe_combine",
    )(dest, x2d, ys2d, gates, gain)


def _rope_tables(seq, rot_dim, theta):
    inv = 1.0 / (theta ** (np.arange(0, rot_dim, 2, dtype=np.float64) / rot_dim))
    ang = np.arange(seq, dtype=np.float64)[:, None] * inv[None, :]
    return np.cos(ang), np.sin(ang)


def _retention_tables():
    c = RET_CHUNK
    log_gamma = np.log(1.0 - 2.0 ** (-5.0 - np.arange(RET_HEADS, dtype=np.float64)))
    idx = np.arange(c, dtype=np.float64)
    rel = idx[:, None] - idx[None, :]
    dmat = np.where(rel[None] >= 0, np.exp(np.maximum(rel, 0.0)[None] * log_gamma[:, None, None]), 0.0)
    qd = np.exp((idx + 1.0)[None, :] * log_gamma[:, None])[:, :, None]
    kd = np.exp((c - 1.0 - idx)[None, :] * log_gamma[:, None])[:, :, None]
    cd = np.exp(c * log_gamma)[:, None, None]
    qd, kd = (jnp.asarray(np.broadcast_to(tb, (RET_HEADS, c, RET_QK)), BF16) for tb in (qd, kd))
    return jnp.asarray(dmat, F32), qd, kd, jnp.asarray(cd, F32)


def _attn_rope_tables(seq):
    cos, sin = _rope_tables(seq, ROPE_DIM, ROPE_THETA)
    half = ROPE_DIM // 2
    pad = DIFF_HD - ROPE_DIM
    ones = np.ones((seq, pad))
    zeros = np.zeros((seq, pad))
    zh = np.zeros((seq, half))
    ctab = np.concatenate([cos, cos, ones], axis=1)
    s1tab = np.concatenate([-sin, zh, zeros], axis=1)
    s2tab = np.concatenate([zh, sin, zeros], axis=1)
    rep = LANES // DIFF_HD
    return tuple(jnp.asarray(np.tile(tb, (1, rep)), F32) for tb in (ctab, s1tab, s2tab))


def kernel(x, ln_mix, ln_ffn, ret_w_in, ret_w_o, kv_norm, w_kv, diff_w_q, lam_q1, lam_k1, lam_q2, lam_k2,
           diff_subln, diff_w_o, ffn_w_gu, ffn_w_down, moe_router, moe_w_gu, moe_w_down, final_norm):
    batch, seq, d = x.shape
    t = batch * seq
    assert ln_mix.shape[0] == 2 and ret_w_in.shape[0] == 1 and diff_w_q.shape[0] == 1
    assert seq % TOK_TILE == 0 and TOK_TILE % RET_CHUNK == 0 and ATT_TQ % CHUNK == 0
    assert seq % ATT_TK == 0 and ATT_TK % ATT_TQ == 0
    x2d = x.reshape(t, d)
    row = lambda g: g.reshape(1, -1)

    cos_r, sin_r = (jnp.asarray(tb, F32) for tb in _rope_tables(seq, RET_QK, RET_THETA))
    proj = _ret_in_proj(x2d, row(ln_mix[0]), ret_w_in[0].astype(BF16), cos_r, sin_r, seq)
    ret_o = _retention(proj, *_retention_tables(), batch, seq)
    x2 = _ret_out_ffn(x2d, ret_o, ret_w_o[0].astype(BF16), row(ln_ffn[0]),
                      ffn_w_gu[0].astype(BF16), ffn_w_down[0].astype(BF16))

    lambda_init = 0.8 - 0.6 * math.exp(-0.3 * 1)
    q, k, vt = _qkv_proj(x2, row(ln_mix[1]), row(kv_norm), diff_w_q[0].astype(BF16),
                         w_kv[:, :d].astype(BF16), w_kv[:, d:].T.astype(BF16),
                         *_attn_rope_tables(seq), batch, seq)
    lam_vecs = jnp.stack([lam_q1[0], lam_k1[0], lam_q2[0], lam_k2[0]]).astype(F32)
    n_exp, _, two_f = moe_w_gu[0].shape
    att, w_gu_bf, w_down_bf = _diff_attention(
        q, k, vt, lam_vecs, row(diff_subln[0]), moe_w_gu[0].reshape(n_exp * d, two_f),
        moe_w_down[0].reshape(n_exp * (two_f // 2), d), batch, seq, lambda_init)
    router_pad = jnp.pad(moe_router[0], ((0, 0), (0, LANES - N_EXPERTS)))
    router_hi = router_pad.astype(BF16)
    router_lo = (router_pad - router_hi.astype(F32)).astype(BF16)
    tri = jnp.asarray(np.arange(TOK_TILE)[:, None] < np.arange(TOK_TILE)[None, :], BF16)
    x3, h3_rows, route, cnt = _attn_out(x2, att, diff_w_o[0].astype(BF16), row(ln_ffn[1]),
                                        jnp.concatenate([router_hi, router_lo], axis=1), tri)

    n_sub = d // LANES
    n_rows = 2 * t + N_EXPERTS * MOE_TILE
    n_tiles = n_rows // MOE_TILE
    expert = route[0:2].astype(jnp.int32)
    rank = route[2:4].astype(jnp.int32)
    gates = route[4:6].T
    counts = cnt[:N_EXPERTS, 0].astype(jnp.int32)
    padded = (counts + MOE_TILE - 1) // MOE_TILE * MOE_TILE
    seg_end = jnp.cumsum(padded)
    seg_start = seg_end - padded
    dest = (jnp.sum(jnp.where(expert[..., None] == jnp.arange(N_EXPERTS), seg_start, 0), axis=-1)
            + rank).reshape(-1)
    tile_row = jnp.arange(n_tiles, dtype=jnp.int32) * MOE_TILE
    tile_valid = (tile_row < seg_end[-1]).astype(jnp.int32)
    n_valid = seg_end[-1] // MOE_TILE
    tile_src = jnp.minimum(jnp.arange(n_tiles, dtype=jnp.int32), jnp.maximum(n_valid - 1, 0))
    tile_expert = jnp.minimum(jnp.sum(tile_src[:, None] * MOE_TILE >= seg_end[None, :], axis=1),
                              N_EXPERTS - 1).astype(jnp.int32)
    token_end = jnp.sum(jnp.where(tile_expert[:, None] == jnp.arange(N_EXPERTS), seg_start + counts, 0), axis=1)
    tile_rows = (tile_valid * jnp.clip(token_end - tile_row, 0, MOE_TILE)).astype(jnp.int32)
    zflag = jnp.concatenate([(padded > 0).astype(jnp.int32), 1 - tile_valid[-N_EXPERTS:]])
    zpos = jnp.concatenate([jnp.maximum(seg_end - MOE_TILE, 0), tile_row[-N_EXPERTS:]]).astype(jnp.int32)

    xs = _dispatch(h3_rows.reshape(t, n_sub, LANES), dest, zpos, zflag, n_rows)
    ys = _moe_experts(xs.reshape(n_rows * n_sub, LANES), w_gu_bf.reshape(n_exp, d, two_f),
                      w_down_bf.reshape(n_exp, two_f // 2, d), tile_expert, tile_rows, tile_src, d)
    out = _combine(x3, ys, dest, gates, row(final_norm))
    return out.reshape(batch, seq, d)
```

```python
import functools
import math

import jax
import jax.numpy as jnp
import numpy as np
from jax import lax
from jax.experimental import pallas as pl
from jax.experimental.pallas import tpu as pltpu

F32 = jnp.float32
BF16 = jnp.bfloat16

EPS = 1e-6
CHUNK = 64
RET_QK = 256
RET_V = 512
RET_HEADS = 4
RET_THETA = 10000.0
DIFF_HD = 64
DIFF_HEADS = 8
ROPE_THETA = 500000.0
ROPE_DIM = 16
N_EXPERTS = 8

LANES = 128
SUBLANES = 8
MXU_COLS = 256
VMEM_LIMIT = 56 * 1024 * 1024

RET_CHUNK = 256
TOK_TILE = 512
DISPATCH_TILE = 1024
ATT_TQ = 512
ATT_TK = 512
ATT_HEADS = 4
ATT_ONES = 16
MOE_TILE = 1024
MOE_SPLIT = 2
MOE_FC = 1792


def _cparams(sem, vmem=VMEM_LIMIT):
    return pltpu.CompilerParams(dimension_semantics=sem, vmem_limit_bytes=vmem)


def _const_spec(shape):
    nd = len(shape)
    return pl.BlockSpec(shape, lambda *_: (0,) * nd, pipeline_mode=pl.Buffered(1))


def _slab_specs(weights, n_steps, step_of):
    pack = 2 * SUBLANES
    specs = []
    for w in weights:
        rows = w.shape[0]
        per = -(-rows // n_steps)
        br = next(b for b in range(-(-per // pack) * pack, rows + 1, pack) if rows % b == 0)
        last = rows // br - 1
        specs.append(pl.BlockSpec((br, w.shape[1]),
                                  lambda *g, last=last: (jnp.minimum(step_of(*g), last), 0)))
    return specs


def _round_slabs(in_refs, out_refs):
    for src, dst in zip(in_refs, out_refs):
        dst[...] = src[...].astype(BF16)


def _rms(x, g):
    return x * lax.rsqrt(jnp.mean(x * x, axis=-1, keepdims=True) + EPS) * g


def _dot(a, b):
    return jnp.dot(a, b, preferred_element_type=F32)


def _ret_in_kernel(x_ref, g_ref, w_ref, cos_ref, sin_ref, *refs):
    n_ride = (len(refs) - 1) // 2
    o_ref = refs[n_ride]
    _round_slabs(refs[:n_ride], refs[n_ride + 1:])
    h = _rms(x_ref[...], g_ref[...]).astype(BF16)
    cos = cos_ref[...]
    sin = sin_ref[...]
    d_qk = RET_HEADS * RET_QK
    half = RET_QK // 2
    for c in range(2 * RET_HEADS):
        c0 = c * RET_QK
        acc = _dot(h, w_ref[:, c0:c0 + RET_QK])
        x1 = acc[:, :half]
        x2 = acc[:, half:]
        scale = 1.0 if c < RET_HEADS else RET_QK ** -0.5
        o_ref[:, c0:c0 + half] = ((x1 * cos - x2 * sin) * scale).astype(BF16)
        o_ref[:, c0 + half:c0 + RET_QK] = ((x2 * cos + x1 * sin) * scale).astype(BF16)
    n_rest = (w_ref.shape[1] - 2 * d_qk) // RET_V
    for c in range(n_rest):
        c0 = 2 * d_qk + c * RET_V
        o_ref[:, c0:c0 + RET_V] = _dot(h, w_ref[:, c0:c0 + RET_V]).astype(BF16)


def _ret_in_proj(x2d, gain, w, cos, sin, seq, ride):
    t, d = x2d.shape
    n = w.shape[1]
    tm = TOK_TILE
    n_pos = seq // tm
    slabs = _slab_specs(ride, t // tm, lambda i: i)
    return pl.pallas_call(
        _ret_in_kernel,
        out_shape=[jax.ShapeDtypeStruct((t, n), BF16)] + [jax.ShapeDtypeStruct(r.shape, BF16) for r in ride],
        grid=(t // tm,),
        in_specs=[
            pl.BlockSpec((tm, d), lambda i: (i, 0)),
            _const_spec((1, d)),
            _const_spec((d, n)),
            pl.BlockSpec((tm, RET_QK // 2), lambda i: (i % n_pos, 0)),
            pl.BlockSpec((tm, RET_QK // 2), lambda i: (i % n_pos, 0)),
        ] + slabs,
        out_specs=[pl.BlockSpec((tm, n), lambda i: (i, 0))] + slabs,
        compiler_params=_cparams(("arbitrary",)),
        name="ret_in_proj",
    )(x2d, gain, w, cos, sin, *ride)


def _retention_kernel(q_ref, k_ref, v_ref, g_ref, dmat_ref, qd_ref, kd_ref, cd_ref, o_ref, state_ref):
    @pl.when(pl.program_id(1) == 0)
    def _():
        state_ref[...] = jnp.zeros_like(state_ref)

    n_chunks = q_ref.shape[0] // RET_CHUNK
    for h in range(RET_HEADS):
        for c in range(n_chunks):
            rows = slice(c * RET_CHUNK, (c + 1) * RET_CHUNK)
            q = q_ref[rows, h * RET_QK:(h + 1) * RET_QK]
            k = k_ref[rows, h * RET_QK:(h + 1) * RET_QK]
            v = v_ref[rows, h * RET_V:(h + 1) * RET_V]
            state = state_ref[h]
            s = lax.dot_general(q, k, (((1,), (1,)), ((), ())), preferred_element_type=F32)
            s = s * dmat_ref[h]
            o = _dot(s.astype(BF16), v)
            qs = (q.astype(F32) * qd_ref[h]).astype(BF16)
            o = o + _dot(qs, state.astype(BF16))
            ks = (k.astype(F32) * kd_ref[h]).astype(BF16)
            state_ref[h] = state * cd_ref[h] + lax.dot_general(
                ks, v, (((0,), (0,)), ((), ())), preferred_element_type=F32)
            mu = jnp.mean(o, axis=-1, keepdims=True)
            oc = o - mu
            var = jnp.mean(oc * oc, axis=-1, keepdims=True)
            on = oc * lax.rsqrt(var + EPS)
            gate = g_ref[rows, h * RET_V:(h + 1) * RET_V].astype(F32)
            o_ref[rows, h * RET_V:(h + 1) * RET_V] = (on * (gate * jax.nn.sigmoid(gate))).astype(BF16)


def _retention(proj, dmat, qd, kd, cd, batch, seq):
    t = proj.shape[0]
    tb = TOK_TILE
    nj = seq // tb
    d_qk = RET_HEADS * RET_QK
    d_v = RET_HEADS * RET_V
    row = lambda b, j: b * nj + j
    return pl.pallas_call(
        _retention_kernel,
        out_shape=jax.ShapeDtypeStruct((t, d_v), BF16),
        grid=(batch, nj),
        in_specs=[
            pl.BlockSpec((tb, d_qk), lambda b, j: (row(b, j), 0)),
            pl.BlockSpec((tb, d_qk), lambda b, j: (row(b, j), 1)),
            pl.BlockSpec((tb, d_v), lambda b, j: (row(b, j), 1)),
            pl.BlockSpec((tb, d_v), lambda b, j: (row(b, j), 2)),
            _const_spec(dmat.shape),
            _const_spec(qd.shape),
            _const_spec(kd.shape),
            _const_spec(cd.shape),
        ],
        out_specs=pl.BlockSpec((tb, d_v), lambda b, j: (row(b, j), 0)),
        scratch_shapes=[pltpu.VMEM((RET_HEADS, RET_QK, RET_V), F32)],
        compiler_params=_cparams(("parallel", "arbitrary")),
        name="retention",
    )(proj, proj, proj, proj, dmat, qd, kd, cd)


def _ret_out_ffn_kernel(x_ref, o_ref, wo_ref, g_ref, wgu_ref, wd_ref, out_ref, act_ref, *, fc):
    x1 = x_ref[...] + _dot(o_ref[...], wo_ref[...])
    h = _rms(x1, g_ref[...]).astype(BF16)
    f = wd_ref.shape[0]
    for c in range(f // fc):
        gt = _dot(h, wgu_ref[:, c * fc:(c + 1) * fc])
        up = _dot(h, wgu_ref[:, f + c * fc:f + (c + 1) * fc])
        act_ref[:, c * fc:(c + 1) * fc] = (gt * jax.nn.sigmoid(gt) * up).astype(BF16)
    out_ref[...] = x1 + _dot(act_ref[...], wd_ref[...])


def _ret_out_ffn(x2d, o, w_o, gain, w_gu, w_down):
    t, d = x2d.shape
    tm = TOK_TILE
    f = w_down.shape[0]
    return pl.pallas_call(
        functools.partial(_ret_out_ffn_kernel, fc=256),
        out_shape=jax.ShapeDtypeStruct((t, d), F32),
        grid=(t // tm,),
        in_specs=[
            pl.BlockSpec((tm, d), lambda i: (i, 0)),
            pl.BlockSpec((tm, o.shape[1]), lambda i: (i, 0)),
            _const_spec(w_o.shape),
            _const_spec((1, d)),
            _const_spec(w_gu.shape),
            _const_spec(w_down.shape),
        ],
        out_specs=pl.BlockSpec((tm, d), lambda i: (i, 0)),
        scratch_shapes=[pltpu.VMEM((tm, f), BF16)],
        compiler_params=_cparams(("parallel",)),
        name="ret_out_ffn",
    )(x2d, o, w_o, gain, w_gu, w_down)


def _rope16(x, ctab, s1tab, s2tab):
    half = ROPE_DIM // 2
    return (x * ctab + pltpu.roll(x, LANES - half, 1) * s1tab + pltpu.roll(x, half, 1) * s2tab)


def _qkv_kernel(x_ref, gq_ref, gkv_ref, wq_ref, wk_ref, wvt_ref, c_ref, s1_ref, s2_ref, q_ref, k_ref, vt_ref):
    x = x_ref[...]
    xn = x * lax.rsqrt(jnp.mean(x * x, axis=-1, keepdims=True) + EPS)
    hq = (xn * gq_ref[...]).astype(BF16)
    hkv = (xn * gkv_ref[...]).astype(BF16)
    ctab, s1tab, s2tab = c_ref[...], s1_ref[...], s2_ref[...]
    d = q_ref.shape[1]
    q_scale = DIFF_HD ** -0.5 * math.log2(math.e)
    for c in range(d // MXU_COLS):
        c0 = c * MXU_COLS
        qa = _dot(hq, wq_ref[:, c0:c0 + MXU_COLS])
        ka = _dot(hkv, wk_ref[:, c0:c0 + MXU_COLS])
        for l0 in range(0, MXU_COLS, LANES):
            cols = slice(c0 + l0, c0 + l0 + LANES)
            q_ref[:, cols] = (_rope16(qa[:, l0:l0 + LANES], ctab, s1tab, s2tab) * q_scale).astype(BF16)
            k_ref[:, cols] = _rope16(ka[:, l0:l0 + LANES], ctab, s1tab, s2tab).astype(BF16)
    vt_ref[...] = lax.dot_general(wvt_ref[...], hkv, (((1,), (1,)), ((), ())),
                                  preferred_element_type=F32).astype(BF16)


def _qkv_proj(x2d, gq, gkv, w_q, w_k, w_vt, ctab, s1tab, s2tab, batch, seq):
    t, d = x2d.shape
    tm = TOK_TILE
    n_pos = seq // tm
    tok = pl.BlockSpec((tm, d), lambda i: (i, 0))
    tab = pl.BlockSpec((tm, LANES), lambda i: (i % n_pos, 0))
    return pl.pallas_call(
        _qkv_kernel,
        out_shape=[jax.ShapeDtypeStruct((t, d), BF16), jax.ShapeDtypeStruct((t, d), BF16),
                   jax.ShapeDtypeStruct((batch * d, seq), BF16)],
        grid=(t // tm,),
        in_specs=[tok, _const_spec((1, d)), _const_spec((1, d)), _const_spec(w_q.shape),
                  _const_spec(w_k.shape), _const_spec(w_vt.shape), tab, tab, tab],
        out_specs=[tok, tok, pl.BlockSpec((d, tm), lambda i: (i // n_pos, i % n_pos))],
        compiler_params=_cparams(("parallel",)),
        name="qkv_proj",
    )(x2d, gq, gkv, w_q, w_k, w_vt, ctab, s1tab, s2tab)


def _attn_kernel(lam_ref, bias_ref, q_ref, qn_ref, k_ref, vt_ref, sub_ref, wa_ref, wb_ref,
                 o_ref, wa_out_ref, wb_out_ref, vta_ref, s_ref, mt_ref, acc_ref, *, lambda_init):
    i = pl.program_id(2)
    tq = q_ref.shape[0]
    dv = 2 * DIFF_HD
    n_heads, n_kv, _, tk = vta_ref.shape
    chains = [(h, c) for h in range(n_heads) for c in range(2)]

    _round_slabs((wa_ref, wb_ref), (wa_out_ref, wb_out_ref))

    @pl.when(i == 0)
    def _():
        for h in range(n_heads):
            for j in range(n_kv):
                vta_ref[h, j, 0:dv, :] = vt_ref[h * dv:(h + 1) * dv, j * tk:(j + 1) * tk]
                vta_ref[h, j, dv:, :] = jnp.ones((ATT_ONES, tk), BF16)

    lane = lax.broadcasted_iota(jnp.int32, (tq, dv), 1)

    def split_heads(ref):
        out = []
        for h in range(n_heads):
            q = ref[:, h * dv:(h + 1) * dv]
            zero = jnp.zeros_like(q)
            out.append((jnp.where(lane < DIFF_HD, q, zero), jnp.where(lane >= DIFF_HD, q, zero)))
        return out

    def first_tiles(qi):
        n = (qi * tq) // tk
        return n, 1 + (qi * tq - n * tk) // tq

    n_full, tail_bias = first_tiles(i)

    def tile_max(s):
        m8 = jnp.max(s.reshape(tk // SUBLANES, SUBLANES, tq), axis=0)
        return jnp.max(m8, axis=0, keepdims=True)

    def score_chain(j, x, qsplit, n_vis, tail):
        h, c = chains[x]
        bias = bias_ref[jnp.where(j == n_vis, tail, 0)]
        kt = k_ref[pl.ds(pl.multiple_of(j * tk, tk), tk), h * dv:(h + 1) * dv]
        s = lax.dot_general(kt, qsplit[h][c], (((1,), (1,)), ((), ())), preferred_element_type=F32) + bias
        s_ref[x] = s
        return tile_max(s)

    @pl.when(i == 0)
    def _():
        qc0 = split_heads(q_ref)
        for x in range(len(chains)):
            mt_ref[x] = score_chain(0, x, qc0, n_full, tail_bias)

    qc = split_heads(q_ref)
    acc_ref[...] = jnp.zeros_like(acc_ref)
    m_init = tuple(jnp.full((1, tq), -jnp.inf, F32) for _ in chains)

    def step(j, carry, next_scores):
        m_run, m_tile = carry
        m_out, m_next = [], []
        for x, (h, c) in enumerate(chains):
            m_new = jnp.maximum(m_run[x], m_tile[x])
            alpha = jnp.exp2(m_run[x] - m_new)
            p = jnp.exp2(s_ref[x] - m_new).astype(BF16)
            acc_ref[x] = alpha * acc_ref[x] + _dot(vta_ref[h, j], p)
            m_out.append(m_new)
            if next_scores is not None:
                m_next.append(next_scores(x))
        return tuple(m_out), tuple(m_next)

    first = tuple(mt_ref[x] for x in range(len(chains)))
    carry = lax.fori_loop(
        0, n_full,
        lambda j, carry: step(j, carry, lambda x: score_chain(j + 1, x, qc, n_full, tail_bias)),
        (m_init, first))

    @pl.when(i + 1 < pl.num_programs(2))
    def _():
        qn = split_heads(qn_ref)
        n_vis, tail = first_tiles(i + 1)
        _, m_first = step(n_full, carry, lambda x: score_chain(0, x, qn, n_vis, tail))
        for x in range(len(chains)):
            mt_ref[x] = m_first[x]

    @pl.when(i + 1 == pl.num_programs(2))
    def _():
        step(n_full, carry, None)

    lam_v = lam_ref[...]
    lam = (jnp.exp(jnp.sum(lam_v[0:1] * lam_v[1:2], axis=-1, keepdims=True))
           - jnp.exp(jnp.sum(lam_v[2:3] * lam_v[3:4], axis=-1, keepdims=True)) + lambda_init)
    for h in range(n_heads):
        a0 = acc_ref[2 * h]
        a1 = acc_ref[2 * h + 1]
        ot = a0[0:dv] / a0[dv:dv + 1] - lam * (a1[0:dv] / a1[dv:dv + 1])
        o = _rms(ot.T, sub_ref[...]) * (1.0 - lambda_init)
        o_ref[:, h * dv:(h + 1) * dv] = o.astype(BF16)


def _attn_bias(tq, tk):
    key = np.arange(tk)[:, None] // CHUNK
    tiles = [np.zeros((tk, tq))]
    for r in range(tk // tq):
        qry = (r * tq + np.arange(tq))[None, :] // CHUNK
        tiles.append(np.where(key <= qry, 0.0, -1e30))
    return jnp.asarray(np.stack(tiles), F32)


def _diff_attention(q, k, vt, lam_vecs, subln, w_a, w_b, batch, seq, lambda_init):
    t, d = q.shape
    tq = ATT_TQ
    nq = seq // tq
    dv = 2 * DIFF_HD
    hp = ATT_HEADS
    n_groups = DIFF_HEADS // hp
    n_steps = batch * n_groups * nq
    bias = _attn_bias(tq, ATT_TK)
    slabs = _slab_specs((w_a, w_b), n_steps, lambda b, g, i: (b * n_groups + g) * nq + i)
    return pl.pallas_call(
        functools.partial(_attn_kernel, lambda_init=lambda_init),
        out_shape=[jax.ShapeDtypeStruct((t, d), BF16),
                   jax.ShapeDtypeStruct(w_a.shape, BF16), jax.ShapeDtypeStruct(w_b.shape, BF16)],
        grid=(batch, n_groups, nq),
        in_specs=[
            _const_spec(lam_vecs.shape),
            _const_spec(bias.shape),
            pl.BlockSpec((tq, hp * dv), lambda b, g, i: (b * nq + i, g)),
            pl.BlockSpec((tq, hp * dv), lambda b, g, i: (b * nq + jnp.minimum(i + 1, nq - 1), g)),
            pl.BlockSpec((seq, hp * dv), lambda b, g, i: (b, g)),
            pl.BlockSpec((hp * dv, seq), lambda b, g, i: (b * n_groups + g, 0)),
            _const_spec((1, dv)),
        ] + slabs,
        out_specs=[pl.BlockSpec((tq, hp * dv), lambda b, g, i: (b * nq + i, g))] + slabs,
        scratch_shapes=[pltpu.VMEM((hp, seq // ATT_TK, dv + ATT_ONES, ATT_TK), BF16),
                        pltpu.VMEM((2 * hp, ATT_TK, tq), F32),
                        pltpu.VMEM((2 * hp, 1, tq), F32),
                        pltpu.VMEM((2 * hp, dv + ATT_ONES, tq), F32)],
        compiler_params=_cparams(("parallel", "parallel", "arbitrary")),
        name="diff_attn",
    )(lam_vecs, bias, q, q, k, vt, subln, w_a, w_b)


ROUTE_ROWS = 2 * SUBLANES


def _attn_out_kernel(x_ref, o_ref, wo_ref, g_ref, r_ref, tri_ref, x3_ref, h3_ref, route_ref, cnt_ref, carry_ref):
    @pl.when(pl.program_id(0) == 0)
    def _():
        carry_ref[...] = jnp.zeros_like(carry_ref)

    tm = x_ref.shape[0]
    x3 = x_ref[...] + _dot(o_ref[...], wo_ref[...])
    x3_ref[...] = x3
    h3 = _rms(x3, g_ref[...])
    for s in range(h3.shape[1] // LANES):
        h3_ref[pl.ds(s, tm, stride=SUBLANES), :] = h3[:, s * LANES:(s + 1) * LANES]

    hi = h3.astype(BF16)
    lo = (h3 - hi.astype(F32)).astype(BF16)
    both = _dot(hi, r_ref[...])
    logits = both[:, :LANES] + both[:, LANES:] + _dot(lo, r_ref[:, :LANES])
    lt = logits.T[0:ROUTE_ROWS]
    sub = lax.broadcasted_iota(jnp.int32, lt.shape, 0).astype(F32)
    lt = jnp.where(sub < N_EXPERTS, lt, -jnp.inf)
    v1 = jnp.max(lt, axis=0, keepdims=True)
    i1 = jnp.min(jnp.where(lt == v1, sub, float(ROUTE_ROWS)), axis=0, keepdims=True)
    lt2 = jnp.where(sub == i1, -jnp.inf, lt)
    v2 = jnp.max(lt2, axis=0, keepdims=True)
    i2 = jnp.min(jnp.where(lt2 == v2, sub, float(ROUTE_ROWS)), axis=0, keepdims=True)
    e = jnp.exp(v2 - v1)
    w1 = 1.0 / (1.0 + e)
    w2 = e / (1.0 + e)
    oh1 = sub == i1
    oh2 = sub == i2
    assign = jnp.where(oh1 | oh2, 1.0, 0.0)
    excl = _dot(assign.astype(BF16), tri_ref[...]) + carry_ref[:, 0:1]
    r1 = jnp.sum(jnp.where(oh1, excl, 0.0), axis=0, keepdims=True)
    r2 = jnp.sum(jnp.where(oh2, excl, 0.0), axis=0, keepdims=True)
    route = jnp.zeros_like(lt)
    for row, val in enumerate((i1, i2, r1, r2, w1, w2)):
        route = jnp.where(sub == float(row), val, route)
    route_ref[...] = route[0:SUBLANES]
    total = carry_ref[:, 0:1] + jnp.sum(assign, axis=1, keepdims=True)
    carry_ref[...] = jnp.broadcast_to(total, carry_ref.shape)
    cnt_ref[...] = jnp.broadcast_to(total, cnt_ref.shape)


def _attn_out(x2d, o, w_o, gain, router_split, tri):
    t, d = x2d.shape
    tm = TOK_TILE
    n_sub = d // LANES
    return pl.pallas_call(
        _attn_out_kernel,
        out_shape=[jax.ShapeDtypeStruct((t, d), F32),
                   jax.ShapeDtypeStruct((t * n_sub, LANES), F32),
                   jax.ShapeDtypeStruct((SUBLANES, t), F32),
                   jax.ShapeDtypeStruct((ROUTE_ROWS, LANES), F32)],
        grid=(t // tm,),
        in_specs=[pl.BlockSpec((tm, d), lambda i: (i, 0)),
                  pl.BlockSpec((tm, d), lambda i: (i, 0)),
                  _const_spec(w_o.shape), _const_spec((1, d)), _const_spec(router_split.shape),
                  _const_spec(tri.shape)],
        out_specs=[pl.BlockSpec((tm, d), lambda i: (i, 0)),
                   pl.BlockSpec((tm * n_sub, LANES), lambda i: (i, 0)),
                   pl.BlockSpec((SUBLANES, tm), lambda i: (0, i)),
                   _const_spec((ROUTE_ROWS, LANES))],
        scratch_shapes=[pltpu.VMEM((ROUTE_ROWS, LANES), F32)],
        compiler_params=_cparams(("arbitrary",)),
        name="attn_out_route",
    )(x2d, o, w_o, gain, router_split, tri)


def _dispatch_kernel(dest_ref, zpos_ref, zflag_ref, h_ref, dst_ref, zbuf_ref, zsem, sem):
    i = pl.program_id(0)
    tm = h_ref.shape[0]

    @pl.when(i == 0)
    def _():
        zbuf_ref[...] = jnp.zeros_like(zbuf_ref)

        def zcopy(e):
            return pltpu.make_async_copy(zbuf_ref, dst_ref.at[pl.ds(zpos_ref[e], MOE_TILE)], zsem.at[e])

        for e in range(2 * N_EXPERTS):
            @pl.when(zflag_ref[e] == 1)
            def _():
                zcopy(e).start()
        for e in range(2 * N_EXPERTS):
            @pl.when(zflag_ref[e] == 1)
            def _():
                zcopy(e).wait()

    n_tok = dest_ref.shape[0] // 2

    def body(r, carry):
        for k in range(2):
            pltpu.make_async_copy(h_ref.at[r], dst_ref.at[dest_ref[k * n_tok + i * tm + r]],
                                  sem).start(priority=k)
        return carry

    lax.fori_loop(0, tm, body, 0, unroll=8)
    for k in range(2):
        pltpu.make_async_copy(h_ref, dst_ref.at[pl.ds(0, tm)], sem).wait()


def _dispatch(h3_rows, dest, zpos, zflag, n_dst):
    t = h3_rows.shape[0]
    tm = DISPATCH_TILE
    tail = h3_rows.shape[1:]
    return pl.pallas_call(
        _dispatch_kernel,
        out_shape=jax.ShapeDtypeStruct((n_dst,) + tail, F32),
        grid_spec=pltpu.PrefetchScalarGridSpec(
            num_scalar_prefetch=3, grid=(t // tm,),
            in_specs=[pl.BlockSpec((tm,) + tail, lambda i, *_: (i, 0, 0))],
            out_specs=pl.BlockSpec(memory_space=pl.ANY),
            scratch_shapes=[pltpu.VMEM((MOE_TILE,) + tail, F32),
                            pltpu.SemaphoreType.DMA((2 * N_EXPERTS,)), pltpu.SemaphoreType.DMA(())]),
        compiler_params=_cparams(("arbitrary",)),
        name="moe_dispatch",
    )(dest, zpos, zflag, h3_rows)


def _moe_kernel(te_ref, tr_ref, ts_ref, xs_ref, wg_ref, wu_ref, wd_ref, ys_ref, h_ref, acc_ref):
    i = pl.program_id(0)
    c = pl.program_id(1)
    last = pl.num_programs(1) - 1
    n_sub = h_ref.shape[1] // LANES
    grp = h_ref.shape[0] // MOE_SPLIT

    for part in range(MOE_SPLIT):
        r0 = part * grp
        rows = slice(r0, r0 + grp)
        live = tr_ref[i] > r0

        @pl.when(live & (c == 0))
        def _():
            for s in range(n_sub):
                h_ref[rows, s * LANES:(s + 1) * LANES] = (
                    xs_ref[pl.ds(r0 * n_sub + s, grp, stride=n_sub), :].astype(BF16))
            acc_ref[rows, :] = jnp.zeros((grp, acc_ref.shape[1]), F32)

        @pl.when(live)
        def _():
            h = h_ref[rows, :]
            gt = _dot(h, wg_ref[...])
            up = _dot(h, wu_ref[...])
            act = (gt * jax.nn.sigmoid(gt) * up).astype(BF16)
            acc_ref[rows, :] += _dot(act, wd_ref[...])

        @pl.when(live & (c == last))
        def _():
            for s in range(n_sub):
                ys_ref[pl.ds(r0 * n_sub + s, grp, stride=n_sub), :] = acc_ref[rows, s * LANES:(s + 1) * LANES]

        @pl.when(jnp.logical_not(live) & (c == last))
        def _():
            ys_ref[r0 * n_sub:(r0 + grp) * n_sub, :] = jnp.zeros((grp * n_sub, LANES), F32)


def _moe_experts(xs2d, w_gu, w_down, tile_expert, tile_rows, tile_src, d):
    n_sub = d // LANES
    n_tiles = xs2d.shape[0] // (MOE_TILE * n_sub)
    f = w_down.shape[1]
    nfc = f // MOE_FC
    blk = MOE_TILE * n_sub

    def ceff(i, c, tv):
        return jnp.where(tv[i] > 0, c, nfc - 1)

    return pl.pallas_call(
        _moe_kernel,
        out_shape=jax.ShapeDtypeStruct(xs2d.shape, F32),
        grid_spec=pltpu.PrefetchScalarGridSpec(
            num_scalar_prefetch=3, grid=(n_tiles, nfc),
            in_specs=[
                pl.BlockSpec((blk, LANES), lambda i, c, te, tv, ts: (ts[i], 0)),
                pl.BlockSpec((None, d, MOE_FC), lambda i, c, te, tv, ts: (te[i], 0, ceff(i, c, tv))),
                pl.BlockSpec((None, d, MOE_FC), lambda i, c, te, tv, ts: (te[i], 0, nfc + ceff(i, c, tv))),
                pl.BlockSpec((None, MOE_FC, d), lambda i, c, te, tv, ts: (te[i], ceff(i, c, tv), 0)),
            ],
            out_specs=pl.BlockSpec((blk, LANES), lambda i, c, te, tv, ts: (i, 0)),
            scratch_shapes=[pltpu.VMEM((MOE_TILE, d), BF16), pltpu.VMEM((MOE_TILE, d), F32)]),
        compiler_params=_cparams(("arbitrary", "arbitrary")),
        name="moe_experts",
    )(tile_expert, tile_rows, tile_src, xs2d, w_gu, w_gu, w_down)


def _combine_kernel(dest_ref, x_ref, ys_ref, gate_ref, g_ref, o_ref, ybuf_ref, sem):
    i = pl.program_id(0)
    n = pl.num_programs(0)
    tm, d = x_ref.shape
    n_sub = d // LANES

    def gather(tile, slot):
        def body(r, carry):
            for k in range(2):
                row = dest_ref[k * (dest_ref.shape[0] // 2) + tile * tm + r]
                src = ys_ref.at[pl.ds(pl.multiple_of(row * n_sub, n_sub), n_sub)]
                dst = ybuf_ref.at[slot, pl.ds(pl.multiple_of((2 * r + k) * n_sub, n_sub), n_sub)]
                pltpu.make_async_copy(src, dst, sem.at[slot]).start(priority=k)
            return carry

        lax.fori_loop(0, tm, body, 0, unroll=8)

    @pl.when(i == 0)
    def _():
        gather(0, 0)

    @pl.when(i + 1 < n)
    def _():
        gather(i + 1, (i + 1) % 2)

    slot = i % 2
    pltpu.make_async_copy(ys_ref.at[pl.ds(0, 2 * tm * n_sub)], ybuf_ref.at[slot], sem.at[slot]).wait()

    y_ref = ybuf_ref.at[slot]
    gates = gate_ref[...]
    w1 = gates[:, 0:1]
    w2 = gates[:, 1:2]
    for s in range(n_sub):
        cols = slice(s * LANES, (s + 1) * LANES)
        y1 = y_ref[pl.ds(s, tm, stride=2 * n_sub), :]
        y2 = y_ref[pl.ds(n_sub + s, tm, stride=2 * n_sub), :]
        o_ref[:, cols] = x_ref[:, cols] + w1 * y1 + w2 * y2
    o_ref[...] = _rms(o_ref[...], g_ref[...])


def _combine(x2d, ys2d, dest, gates, gain):
    t, d = x2d.shape
    tm = TOK_TILE
    n_sub = d // LANES
    return pl.pallas_call(
        _combine_kernel,
        out_shape=jax.ShapeDtypeStruct((t, d), F32),
        grid_spec=pltpu.PrefetchScalarGridSpec(
            num_scalar_prefetch=1, grid=(t // tm,),
            in_specs=[pl.BlockSpec((tm, d), lambda i, *_: (i, 0)),
                      pl.BlockSpec(memory_space=pl.ANY),
                      pl.BlockSpec((tm, gates.shape[1]), lambda i, *_: (i, 0)),
                      pl.BlockSpec((1, d), lambda i, *_: (0, 0), pipeline_mode=pl.Buffered(1))],
            out_specs=pl.BlockSpec((tm, d), lambda i, *_: (i, 0)),
            scratch_shapes=[pltpu.VMEM((2, 2 * tm * n_sub, LANES), F32), pltpu.SemaphoreType.DMA((2,))]),
        compiler_params=_cparams(("arbitrary",)),
        name="moe_combine",
    )(dest, x2d, ys2d, gates, gain)


def _rope_tables(seq, rot_dim, theta):
    inv = 1.0 / (theta ** (np.arange(0, rot_dim, 2, dtype=np.float64) / rot_dim))
    ang = np.arange(seq, dtype=np.float64)[:, None] * inv[None, :]
    return np.cos(ang), np.sin(ang)


def _retention_tables():
    c = RET_CHUNK
    log_gamma = np.log(1.0 - 2.0 ** (-5.0 - np.arange(RET_HEADS, dtype=np.float64)))
    idx = np.arange(c, dtype=np.float64)
    rel = idx[:, None] - idx[None, :]
    dmat = np.where(rel[None] >= 0, np.exp(np.maximum(rel, 0.0)[None] * log_gamma[:, None, None]), 0.0)
    qd = np.exp((idx + 1.0)[None, :] * log_gamma[:, None])[:, :, None]
    kd = np.exp((c - 1.0 - idx)[None, :] * log_gamma[:, None])[:, :, None]
    cd = np.exp(c * log_gamma)[:, None, None]
    return tuple(jnp.asarray(tb, F32) for tb in (dmat, qd, kd, cd))


def _attn_rope_tables(seq):
    cos, sin = _rope_tables(seq, ROPE_DIM, ROPE_THETA)
    half = ROPE_DIM // 2
    pad = DIFF_HD - ROPE_DIM
    ones = np.ones((seq, pad))
    zeros = np.zeros((seq, pad))
    zh = np.zeros((seq, half))
    ctab = np.concatenate([cos, cos, ones], axis=1)
    s1tab = np.concatenate([-sin, zh, zeros], axis=1)
    s2tab = np.concatenate([zh, sin, zeros], axis=1)
    rep = LANES // DIFF_HD
    return tuple(jnp.asarray(np.tile(tb, (1, rep)), F32) for tb in (ctab, s1tab, s2tab))


def kernel(x, ln_mix, ln_ffn, ret_w_in, ret_w_o, kv_norm, w_kv, diff_w_q, lam_q1, lam_k1, lam_q2, lam_k2,
           diff_subln, diff_w_o, ffn_w_gu, ffn_w_down, moe_router, moe_w_gu, moe_w_down, final_norm):
    batch, seq, d = x.shape
    t = batch * seq
    assert ln_mix.shape[0] == 2 and ret_w_in.shape[0] == 1 and diff_w_q.shape[0] == 1
    assert seq % TOK_TILE == 0 and TOK_TILE % RET_CHUNK == 0 and ATT_TQ % CHUNK == 0
    assert seq % ATT_TK == 0 and ATT_TK % ATT_TQ == 0
    x2d = x.reshape(t, d)
    row = lambda g: g.reshape(1, -1)

    cos_r, sin_r = (jnp.asarray(tb, F32) for tb in _rope_tables(seq, RET_QK, RET_THETA))
    proj, w_o_bf, w_gu_d_bf, w_down_d_bf, w_q_bf, w_kv_bf, w_ao_bf = _ret_in_proj(
        x2d, row(ln_mix[0]), ret_w_in[0].astype(BF16), cos_r, sin_r, seq,
        (ret_w_o[0], ffn_w_gu[0], ffn_w_down[0], diff_w_q[0], w_kv, diff_w_o[0]))
    ret_o = _retention(proj, *_retention_tables(), batch, seq)
    x2 = _ret_out_ffn(x2d, ret_o, w_o_bf, row(ln_ffn[0]), w_gu_d_bf, w_down_d_bf)

    lambda_init = 0.8 - 0.6 * math.exp(-0.3 * 1)
    q, k, vt = _qkv_proj(x2, row(ln_mix[1]), row(kv_norm), w_q_bf, w_kv_bf[:, :d], w_kv_bf[:, d:].T,
                         *_attn_rope_tables(seq), batch, seq)
    lam_vecs = jnp.stack([lam_q1[0], lam_k1[0], lam_q2[0], lam_k2[0]]).astype(F32)
    n_exp, _, two_f = moe_w_gu[0].shape
    att, w_gu_bf, w_down_bf = _diff_attention(
        q, k, vt, lam_vecs, row(diff_subln[0]), moe_w_gu[0].reshape(n_exp * d, two_f),
        moe_w_down[0].reshape(n_exp * (two_f // 2), d), batch, seq, lambda_init)
    router_pad = jnp.pad(moe_router[0], ((0, 0), (0, LANES - N_EXPERTS)))
    router_hi = router_pad.astype(BF16)
    router_lo = (router_pad - router_hi.astype(F32)).astype(BF16)
    tri = jnp.asarray(np.arange(TOK_TILE)[:, None] < np.arange(TOK_TILE)[None, :], BF16)
    x3, h3_rows, route, cnt = _attn_out(x2, att, w_ao_bf, row(ln_ffn[1]),
                                        jnp.concatenate([router_hi, router_lo], axis=1), tri)

    n_sub = d // LANES
    n_rows = 2 * t + N_EXPERTS * MOE_TILE
    n_tiles = n_rows // MOE_TILE
    expert = route[0:2].astype(jnp.int32)
    rank = route[2:4].astype(jnp.int32)
    gates = route[4:6].T
    counts = cnt[:N_EXPERTS, 0].astype(jnp.int32)
    padded = (counts + MOE_TILE - 1) // MOE_TILE * MOE_TILE
    seg_end = jnp.cumsum(padded)
    seg_start = seg_end - padded
    dest = (jnp.sum(jnp.where(expert[..., None] == jnp.arange(N_EXPERTS), seg_start, 0), axis=-1)
            + rank).reshape(-1)
    tile_row = jnp.arange(n_tiles, dtype=jnp.int32) * MOE_TILE
    tile_valid = (tile_row < seg_end[-1]).astype(jnp.int32)
    n_valid = seg_end[-1] // MOE_TILE
    tile_src = jnp.minimum(jnp.arange(n_tiles, dtype=jnp.int32), jnp.maximum(n_valid - 1, 0))
    tile_expert = jnp.minimum(jnp.sum(tile_src[:, None] * MOE_TILE >= seg_end[None, :], axis=1),
                              N_EXPERTS - 1).astype(jnp.int32)
    token_end = jnp.sum(jnp.where(tile_expert[:, None] == jnp.arange(N_EXPERTS), seg_start + counts, 0), axis=1)
    tile_rows = (tile_valid * jnp.clip(token_end - tile_row, 0, MOE_TILE)).astype(jnp.int32)
    zflag = jnp.concatenate([(padded > 0).astype(jnp.int32), 1 - tile_valid[-N_EXPERTS:]])
    zpos = jnp.concatenate([jnp.maximum(seg_end - MOE_TILE, 0), tile_row[-N_EXPERTS:]]).astype(jnp.int32)

    xs = _dispatch(h3_rows.reshape(t, n_sub, LANES), dest, zpos, zflag, n_rows)
    ys = _moe_experts(xs.reshape(n_rows * n_sub, LANES), w_gu_bf.reshape(n_exp, d, two_f),
                      w_down_bf.reshape(n_exp, two_f // 2, d), tile_expert, tile_rows, tile_src, d)
    out = _combine(x3, ys, dest, gates, row(final_norm))
    return out.reshape(batch, seq, d)
```

```python
import functools
import math

import jax
import jax.numpy as jnp
import numpy as np
from jax import lax
from jax.experimental import pallas as pl
from jax.experimental.pallas import tpu as pltpu

F32 = jnp.float32
BF16 = jnp.bfloat16

EPS = 1e-6
CHUNK = 64
RET_QK = 256
RET_V = 512
RET_HEADS = 4
RET_THETA = 10000.0
DIFF_HD = 64
DIFF_HEADS = 8
ROPE_THETA = 500000.0
ROPE_DIM = 16
N_EXPERTS = 8

LANES = 128
SUBLANES = 8
MXU_COLS = 256
VMEM_LIMIT = 56 * 1024 * 1024

RET_CHUNK = 256
TOK_TILE = 512
LIGHT_TILE = 1024
DISPATCH_TILE = 1024
ATT_TQ = 512
ATT_TK = 512
ATT_HEADS = 4
ATT_ONES = 16
MOE_TILE = 1024
MOE_SPLIT = 2
MOE_FC = 1792


def _cparams(sem, vmem=VMEM_LIMIT):
    return pltpu.CompilerParams(dimension_semantics=sem, vmem_limit_bytes=vmem)


def _const_spec(shape):
    nd = len(shape)
    return pl.BlockSpec(shape, lambda *_: (0,) * nd, pipeline_mode=pl.Buffered(1))


def _slab_specs(weights, n_steps, step_of):
    pack = 2 * SUBLANES
    specs = []
    for w in weights:
        rows = w.shape[0]
        per = -(-rows // n_steps)
        br = next(b for b in range(-(-per // pack) * pack, rows + 1, pack) if rows % b == 0)
        last = rows // br - 1
        specs.append(pl.BlockSpec((br, w.shape[1]),
                                  lambda *g, last=last: (jnp.minimum(step_of(*g), last), 0)))
    return specs


def _round_slabs(in_refs, out_refs):
    for src, dst in zip(in_refs, out_refs):
        dst[...] = src[...].astype(BF16)


def _rms(x, g):
    return x * lax.rsqrt(jnp.mean(x * x, axis=-1, keepdims=True) + EPS) * g


def _dot(a, b):
    return jnp.dot(a, b, preferred_element_type=F32)


def _ret_in_kernel(x_ref, g_ref, w_ref, cos_ref, sin_ref, *refs):
    n_ride = (len(refs) - 1) // 2
    o_ref = refs[n_ride]
    _round_slabs(refs[:n_ride], refs[n_ride + 1:])
    h = _rms(x_ref[...], g_ref[...]).astype(BF16)
    cos = cos_ref[...]
    sin = sin_ref[...]
    d_qk = RET_HEADS * RET_QK
    half = RET_QK // 2
    for c in range(2 * RET_HEADS):
        c0 = c * RET_QK
        acc = _dot(h, w_ref[:, c0:c0 + RET_QK])
        x1 = acc[:, :half]
        x2 = acc[:, half:]
        scale = 1.0 if c < RET_HEADS else RET_QK ** -0.5
        o_ref[:, c0:c0 + half] = ((x1 * cos - x2 * sin) * scale).astype(BF16)
        o_ref[:, c0 + half:c0 + RET_QK] = ((x2 * cos + x1 * sin) * scale).astype(BF16)
    n_rest = (w_ref.shape[1] - 2 * d_qk) // RET_V
    for c in range(n_rest):
        c0 = 2 * d_qk + c * RET_V
        o_ref[:, c0:c0 + RET_V] = _dot(h, w_ref[:, c0:c0 + RET_V]).astype(BF16)


def _ret_in_proj(x2d, gain, w, cos, sin, seq, ride):
    t, d = x2d.shape
    n = w.shape[1]
    tm = TOK_TILE
    n_pos = seq // tm
    slabs = _slab_specs(ride, t // tm, lambda i: i)
    return pl.pallas_call(
        _ret_in_kernel,
        out_shape=[jax.ShapeDtypeStruct((t, n), BF16)] + [jax.ShapeDtypeStruct(r.shape, BF16) for r in ride],
        grid=(t // tm,),
        in_specs=[
            pl.BlockSpec((tm, d), lambda i: (i, 0)),
            _const_spec((1, d)),
            _const_spec((d, n)),
            pl.BlockSpec((tm, RET_QK // 2), lambda i: (i % n_pos, 0)),
            pl.BlockSpec((tm, RET_QK // 2), lambda i: (i % n_pos, 0)),
        ] + slabs,
        out_specs=[pl.BlockSpec((tm, n), lambda i: (i, 0))] + slabs,
        compiler_params=_cparams(("arbitrary",)),
        name="ret_in_proj",
    )(x2d, gain, w, cos, sin, *ride)


def _retention_kernel(q_ref, k_ref, v_ref, g_ref, dmat_ref, qd_ref, kd_ref, cd_ref, o_ref, state_ref):
    @pl.when(pl.program_id(1) == 0)
    def _():
        state_ref[...] = jnp.zeros_like(state_ref)

    n_chunks = q_ref.shape[0] // RET_CHUNK
    for h in range(RET_HEADS):
        for c in range(n_chunks):
            rows = slice(c * RET_CHUNK, (c + 1) * RET_CHUNK)
            q = q_ref[rows, h * RET_QK:(h + 1) * RET_QK]
            k = k_ref[rows, h * RET_QK:(h + 1) * RET_QK]
            v = v_ref[rows, h * RET_V:(h + 1) * RET_V]
            state = state_ref[h]
            s = lax.dot_general(q, k, (((1,), (1,)), ((), ())), preferred_element_type=F32)
            s = s * dmat_ref[h]
            o = _dot(s.astype(BF16), v)
            qs = (q.astype(F32) * qd_ref[h]).astype(BF16)
            o = o + _dot(qs, state.astype(BF16))
            ks = (k.astype(F32) * kd_ref[h]).astype(BF16)
            state_ref[h] = state * cd_ref[h] + lax.dot_general(
                ks, v, (((0,), (0,)), ((), ())), preferred_element_type=F32)
            mu = jnp.mean(o, axis=-1, keepdims=True)
            oc = o - mu
            var = jnp.mean(oc * oc, axis=-1, keepdims=True)
            on = oc * lax.rsqrt(var + EPS)
            gate = g_ref[rows, h * RET_V:(h + 1) * RET_V].astype(F32)
            o_ref[rows, h * RET_V:(h + 1) * RET_V] = (on * (gate * jax.nn.sigmoid(gate))).astype(BF16)


def _retention(proj, dmat, qd, kd, cd, batch, seq):
    t = proj.shape[0]
    tb = TOK_TILE
    nj = seq // tb
    d_qk = RET_HEADS * RET_QK
    d_v = RET_HEADS * RET_V
    row = lambda b, j: b * nj + j
    return pl.pallas_call(
        _retention_kernel,
        out_shape=jax.ShapeDtypeStruct((t, d_v), BF16),
        grid=(batch, nj),
        in_specs=[
            pl.BlockSpec((tb, d_qk), lambda b, j: (row(b, j), 0)),
            pl.BlockSpec((tb, d_qk), lambda b, j: (row(b, j), 1)),
            pl.BlockSpec((tb, d_v), lambda b, j: (row(b, j), 1)),
            pl.BlockSpec((tb, d_v), lambda b, j: (row(b, j), 2)),
            _const_spec(dmat.shape),
            _const_spec(qd.shape),
            _const_spec(kd.shape),
            _const_spec(cd.shape),
        ],
        out_specs=pl.BlockSpec((tb, d_v), lambda b, j: (row(b, j), 0)),
        scratch_shapes=[pltpu.VMEM((RET_HEADS, RET_QK, RET_V), F32)],
        compiler_params=_cparams(("parallel", "arbitrary")),
        name="retention",
    )(proj, proj, proj, proj, dmat, qd, kd, cd)


def _ret_out_ffn_kernel(x_ref, o_ref, wo_ref, g_ref, wgu_ref, wd_ref, out_ref, act_ref, *, fc):
    x1 = x_ref[...] + _dot(o_ref[...], wo_ref[...])
    h = _rms(x1, g_ref[...]).astype(BF16)
    f = wd_ref.shape[0]
    for c in range(f // fc):
        gt = _dot(h, wgu_ref[:, c * fc:(c + 1) * fc])
        up = _dot(h, wgu_ref[:, f + c * fc:f + (c + 1) * fc])
        act_ref[:, c * fc:(c + 1) * fc] = (gt * jax.nn.sigmoid(gt) * up).astype(BF16)
    out_ref[...] = x1 + _dot(act_ref[...], wd_ref[...])


def _ret_out_ffn(x2d, o, w_o, gain, w_gu, w_down):
    t, d = x2d.shape
    tm = TOK_TILE
    f = w_down.shape[0]
    return pl.pallas_call(
        functools.partial(_ret_out_ffn_kernel, fc=256),
        out_shape=jax.ShapeDtypeStruct((t, d), F32),
        grid=(t // tm,),
        in_specs=[
            pl.BlockSpec((tm, d), lambda i: (i, 0)),
            pl.BlockSpec((tm, o.shape[1]), lambda i: (i, 0)),
            _const_spec(w_o.shape),
            _const_spec((1, d)),
            _const_spec(w_gu.shape),
            _const_spec(w_down.shape),
        ],
        out_specs=pl.BlockSpec((tm, d), lambda i: (i, 0)),
        scratch_shapes=[pltpu.VMEM((tm, f), BF16)],
        compiler_params=_cparams(("parallel",)),
        name="ret_out_ffn",
    )(x2d, o, w_o, gain, w_gu, w_down)


def _rope16(x, ctab, s1tab, s2tab):
    half = ROPE_DIM // 2
    return (x * ctab + pltpu.roll(x, LANES - half, 1) * s1tab + pltpu.roll(x, half, 1) * s2tab)


def _qkv_kernel(x_ref, gq_ref, gkv_ref, wq_ref, wk_ref, wvt_ref, c_ref, s1_ref, s2_ref, q_ref, k_ref, vt_ref):
    x = x_ref[...]
    xn = x * lax.rsqrt(jnp.mean(x * x, axis=-1, keepdims=True) + EPS)
    hq = (xn * gq_ref[...]).astype(BF16)
    hkv = (xn * gkv_ref[...]).astype(BF16)
    ctab, s1tab, s2tab = c_ref[...], s1_ref[...], s2_ref[...]
    d = q_ref.shape[1]
    q_scale = DIFF_HD ** -0.5 * math.log2(math.e)
    for c in range(d // MXU_COLS):
        c0 = c * MXU_COLS
        qa = _dot(hq, wq_ref[:, c0:c0 + MXU_COLS])
        ka = _dot(hkv, wk_ref[:, c0:c0 + MXU_COLS])
        for l0 in range(0, MXU_COLS, LANES):
            cols = slice(c0 + l0, c0 + l0 + LANES)
            q_ref[:, cols] = (_rope16(qa[:, l0:l0 + LANES], ctab, s1tab, s2tab) * q_scale).astype(BF16)
            k_ref[:, cols] = _rope16(ka[:, l0:l0 + LANES], ctab, s1tab, s2tab).astype(BF16)
    vt_ref[...] = lax.dot_general(wvt_ref[...], hkv, (((1,), (1,)), ((), ())),
                                  preferred_element_type=F32).astype(BF16)


def _qkv_proj(x2d, gq, gkv, w_q, w_k, w_vt, ctab, s1tab, s2tab, batch, seq):
    t, d = x2d.shape
    tm = LIGHT_TILE
    n_pos = seq // tm
    tok = pl.BlockSpec((tm, d), lambda i: (i, 0))
    tab = pl.BlockSpec((tm, LANES), lambda i: (i % n_pos, 0))
    return pl.pallas_call(
        _qkv_kernel,
        out_shape=[jax.ShapeDtypeStruct((t, d), BF16), jax.ShapeDtypeStruct((t, d), BF16),
                   jax.ShapeDtypeStruct((batch * d, seq), BF16)],
        grid=(t // tm,),
        in_specs=[tok, _const_spec((1, d)), _const_spec((1, d)), _const_spec(w_q.shape),
                  _const_spec(w_k.shape), _const_spec(w_vt.shape), tab, tab, tab],
        out_specs=[tok, tok, pl.BlockSpec((d, tm), lambda i: (i // n_pos, i % n_pos))],
        compiler_params=_cparams(("parallel",)),
        name="qkv_proj",
    )(x2d, gq, gkv, w_q, w_k, w_vt, ctab, s1tab, s2tab)


def _attn_kernel(lam_ref, bias_ref, q_ref, qn_ref, k_ref, vt_ref, sub_ref, wa_ref, wb_ref,
                 o_ref, wa_out_ref, wb_out_ref, vta_ref, s_ref, mt_ref, acc_ref, *, lambda_init):
    i = pl.program_id(2)
    tq = q_ref.shape[0]
    dv = 2 * DIFF_HD
    n_heads, n_kv, _, tk = vta_ref.shape
    chains = [(h, c) for h in range(n_heads) for c in range(2)]

    _round_slabs((wa_ref, wb_ref), (wa_out_ref, wb_out_ref))

    @pl.when(i == 0)
    def _():
        for h in range(n_heads):
            for j in range(n_kv):
                vta_ref[h, j, 0:dv, :] = vt_ref[h * dv:(h + 1) * dv, j * tk:(j + 1) * tk]
                vta_ref[h, j, dv:, :] = jnp.ones((ATT_ONES, tk), BF16)

    lane = lax.broadcasted_iota(jnp.int32, (tq, dv), 1)

    def split_heads(ref):
        out = []
        for h in range(n_heads):
            q = ref[:, h * dv:(h + 1) * dv]
            zero = jnp.zeros_like(q)
            out.append((jnp.where(lane < DIFF_HD, q, zero), jnp.where(lane >= DIFF_HD, q, zero)))
        return out

    def first_tiles(qi):
        n = (qi * tq) // tk
        return n, 1 + (qi * tq - n * tk) // tq

    n_full, tail_bias = first_tiles(i)

    def tile_max(s):
        m8 = jnp.max(s.reshape(tk // SUBLANES, SUBLANES, tq), axis=0)
        return jnp.max(m8, axis=0, keepdims=True)

    def score_chain(j, x, qsplit, n_vis, tail):
        h, c = chains[x]
        bias = bias_ref[jnp.where(j == n_vis, tail, 0)]
        kt = k_ref[pl.ds(pl.multiple_of(j * tk, tk), tk), h * dv:(h + 1) * dv]
        s = lax.dot_general(kt, qsplit[h][c], (((1,), (1,)), ((), ())), preferred_element_type=F32) + bias
        s_ref[x] = s
        return tile_max(s)

    @pl.when(i == 0)
    def _():
        qc0 = split_heads(q_ref)
        for x in range(len(chains)):
            mt_ref[x] = score_chain(0, x, qc0, n_full, tail_bias)

    qc = split_heads(q_ref)
    acc_ref[...] = jnp.zeros_like(acc_ref)
    m_init = tuple(jnp.full((1, tq), -jnp.inf, F32) for _ in chains)

    def step(j, carry, next_scores):
        m_run, m_tile = carry
        m_out, m_next = [], []
        for x, (h, c) in enumerate(chains):
            m_new = jnp.maximum(m_run[x], m_tile[x])
            alpha = jnp.exp2(m_run[x] - m_new)
            p = jnp.exp2(s_ref[x] - m_new).astype(BF16)
            acc_ref[x] = alpha * acc_ref[x] + _dot(vta_ref[h, j], p)
            m_out.append(m_new)
            if next_scores is not None:
                m_next.append(next_scores(x))
        return tuple(m_out), tuple(m_next)

    first = tuple(mt_ref[x] for x in range(len(chains)))
    carry = lax.fori_loop(
        0, n_full,
        lambda j, carry: step(j, carry, lambda x: score_chain(j + 1, x, qc, n_full, tail_bias)),
        (m_init, first))

    @pl.when(i + 1 < pl.num_programs(2))
    def _():
        qn = split_heads(qn_ref)
        n_vis, tail = first_tiles(i + 1)
        _, m_first = step(n_full, carry, lambda x: score_chain(0, x, qn, n_vis, tail))
        for x in range(len(chains)):
            mt_ref[x] = m_first[x]

    @pl.when(i + 1 == pl.num_programs(2))
    def _():
        step(n_full, carry, None)

    lam_v = lam_ref[...]
    lam = (jnp.exp(jnp.sum(lam_v[0:1] * lam_v[1:2], axis=-1, keepdims=True))
           - jnp.exp(jnp.sum(lam_v[2:3] * lam_v[3:4], axis=-1, keepdims=True)) + lambda_init)
    for h in range(n_heads):
        a0 = acc_ref[2 * h]
        a1 = acc_ref[2 * h + 1]
        ot = a0[0:dv] / a0[dv:dv + 1] - lam * (a1[0:dv] / a1[dv:dv + 1])
        o = _rms(ot.T, sub_ref[...]) * (1.0 - lambda_init)
        o_ref[:, h * dv:(h + 1) * dv] = o.astype(BF16)


def _attn_bias(tq, tk):
    key = np.arange(tk)[:, None] // CHUNK
    tiles = [np.zeros((tk, tq))]
    for r in range(tk // tq):
        qry = (r * tq + np.arange(tq))[None, :] // CHUNK
        tiles.append(np.where(key <= qry, 0.0, -1e30))
    return jnp.asarray(np.stack(tiles), F32)


def _diff_attention(q, k, vt, lam_vecs, subln, w_a, w_b, batch, seq, lambda_init):
    t, d = q.shape
    tq = ATT_TQ
    nq = seq // tq
    dv = 2 * DIFF_HD
    hp = ATT_HEADS
    n_groups = DIFF_HEADS // hp
    n_steps = batch * n_groups * nq
    bias = _attn_bias(tq, ATT_TK)
    slabs = _slab_specs((w_a, w_b), n_steps, lambda b, g, i: (b * n_groups + g) * nq + i)
    return pl.pallas_call(
        functools.partial(_attn_kernel, lambda_init=lambda_init),
        out_shape=[jax.ShapeDtypeStruct((t, d), BF16),
                   jax.ShapeDtypeStruct(w_a.shape, BF16), jax.ShapeDtypeStruct(w_b.shape, BF16)],
        grid=(batch, n_groups, nq),
        in_specs=[
            _const_spec(lam_vecs.shape),
            _const_spec(bias.shape),
            pl.BlockSpec((tq, hp * dv), lambda b, g, i: (b * nq + i, g)),
            pl.BlockSpec((tq, hp * dv), lambda b, g, i: (b * nq + jnp.minimum(i + 1, nq - 1), g)),
            pl.BlockSpec((seq, hp * dv), lambda b, g, i: (b, g)),
            pl.BlockSpec((hp * dv, seq), lambda b, g, i: (b * n_groups + g, 0)),
            _const_spec((1, dv)),
        ] + slabs,
        out_specs=[pl.BlockSpec((tq, hp * dv), lambda b, g, i: (b * nq + i, g))] + slabs,
        scratch_shapes=[pltpu.VMEM((hp, seq // ATT_TK, dv + ATT_ONES, ATT_TK), BF16),
                        pltpu.VMEM((2 * hp, ATT_TK, tq), F32),
                        pltpu.VMEM((2 * hp, 1, tq), F32),
                        pltpu.VMEM((2 * hp, dv + ATT_ONES, tq), F32)],
        compiler_params=_cparams(("parallel", "parallel", "arbitrary")),
        name="diff_attn",
    )(lam_vecs, bias, q, q, k, vt, subln, w_a, w_b)


ROUTE_ROWS = 2 * SUBLANES


def _attn_out_kernel(x_ref, o_ref, wo_ref, g_ref, r_ref, tri_ref, x3_ref, h3_ref, route_ref, cnt_ref, carry_ref):
    @pl.when(pl.program_id(0) == 0)
    def _():
        carry_ref[...] = jnp.zeros_like(carry_ref)

    tm = x_ref.shape[0]
    x3 = x_ref[...] + _dot(o_ref[...], wo_ref[...])
    x3_ref[...] = x3
    h3 = _rms(x3, g_ref[...])
    for s in range(h3.shape[1] // LANES):
        h3_ref[pl.ds(s, tm, stride=SUBLANES), :] = h3[:, s * LANES:(s + 1) * LANES]

    hi = h3.astype(BF16)
    lo = (h3 - hi.astype(F32)).astype(BF16)
    both = _dot(hi, r_ref[...])
    logits = both[:, :LANES] + both[:, LANES:] + _dot(lo, r_ref[:, :LANES])
    lt = logits.T[0:ROUTE_ROWS]
    sub = lax.broadcasted_iota(jnp.int32, lt.shape, 0).astype(F32)
    lt = jnp.where(sub < N_EXPERTS, lt, -jnp.inf)
    v1 = jnp.max(lt, axis=0, keepdims=True)
    i1 = jnp.min(jnp.where(lt == v1, sub, float(ROUTE_ROWS)), axis=0, keepdims=True)
    lt2 = jnp.where(sub == i1, -jnp.inf, lt)
    v2 = jnp.max(lt2, axis=0, keepdims=True)
    i2 = jnp.min(jnp.where(lt2 == v2, sub, float(ROUTE_ROWS)), axis=0, keepdims=True)
    e = jnp.exp(v2 - v1)
    w1 = 1.0 / (1.0 + e)
    w2 = e / (1.0 + e)
    oh1 = sub == i1
    oh2 = sub == i2
    assign = jnp.where(oh1 | oh2, 1.0, 0.0)
    excl = _dot(assign.astype(BF16), tri_ref[...]) + carry_ref[:, 0:1]
    r1 = jnp.sum(jnp.where(oh1, excl, 0.0), axis=0, keepdims=True)
    r2 = jnp.sum(jnp.where(oh2, excl, 0.0), axis=0, keepdims=True)
    route = jnp.zeros_like(lt)
    for row, val in enumerate((i1, i2, r1, r2, w1, w2)):
        route = jnp.where(sub == float(row), val, route)
    route_ref[...] = route[0:SUBLANES]
    total = carry_ref[:, 0:1] + jnp.sum(assign, axis=1, keepdims=True)
    carry_ref[...] = jnp.broadcast_to(total, carry_ref.shape)
    cnt_ref[...] = jnp.broadcast_to(total, cnt_ref.shape)


def _attn_out(x2d, o, w_o, gain, router_split, tri):
    t, d = x2d.shape
    tm = LIGHT_TILE
    n_sub = d // LANES
    return pl.pallas_call(
        _attn_out_kernel,
        out_shape=[jax.ShapeDtypeStruct((t, d), F32),
                   jax.ShapeDtypeStruct((t * n_sub, LANES), F32),
                   jax.ShapeDtypeStruct((SUBLANES, t), F32),
                   jax.ShapeDtypeStruct((ROUTE_ROWS, LANES), F32)],
        grid=(t // tm,),
        in_specs=[pl.BlockSpec((tm, d), lambda i: (i, 0)),
                  pl.BlockSpec((tm, d), lambda i: (i, 0)),
                  _const_spec(w_o.shape), _const_spec((1, d)), _const_spec(router_split.shape),
                  _const_spec(tri.shape)],
        out_specs=[pl.BlockSpec((tm, d), lambda i: (i, 0)),
                   pl.BlockSpec((tm * n_sub, LANES), lambda i: (i, 0)),
                   pl.BlockSpec((SUBLANES, tm), lambda i: (0, i)),
                   _const_spec((ROUTE_ROWS, LANES))],
        scratch_shapes=[pltpu.VMEM((ROUTE_ROWS, LANES), F32)],
        compiler_params=_cparams(("arbitrary",)),
        name="attn_out_route",
    )(x2d, o, w_o, gain, router_split, tri)


def _dispatch_kernel(dest_ref, zpos_ref, zflag_ref, h_ref, dst_ref, zbuf_ref, zsem, sem):
    i = pl.program_id(0)
    tm = h_ref.shape[0]

    @pl.when(i == 0)
    def _():
        zbuf_ref[...] = jnp.zeros_like(zbuf_ref)

        def zcopy(e):
            return pltpu.make_async_copy(zbuf_ref, dst_ref.at[pl.ds(zpos_ref[e], MOE_TILE)], zsem.at[e])

        for e in range(2 * N_EXPERTS):
            @pl.when(zflag_ref[e] == 1)
            def _():
                zcopy(e).start()
        for e in range(2 * N_EXPERTS):
            @pl.when(zflag_ref[e] == 1)
            def _():
                zcopy(e).wait()

    n_tok = dest_ref.shape[0] // 2

    def body(r, carry):
        for k in range(2):
            pltpu.make_async_copy(h_ref.at[r], dst_ref.at[dest_ref[k * n_tok + i * tm + r]],
                                  sem).start(priority=k)
        return carry

    lax.fori_loop(0, tm, body, 0, unroll=8)
    for k in range(2):
        pltpu.make_async_copy(h_ref, dst_ref.at[pl.ds(0, tm)], sem).wait()


def _dispatch(h3_rows, dest, zpos, zflag, n_dst):
    t = h3_rows.shape[0]
    tm = DISPATCH_TILE
    tail = h3_rows.shape[1:]
    return pl.pallas_call(
        _dispatch_kernel,
        out_shape=jax.ShapeDtypeStruct((n_dst,) + tail, F32),
        grid_spec=pltpu.PrefetchScalarGridSpec(
            num_scalar_prefetch=3, grid=(t // tm,),
            in_specs=[pl.BlockSpec((tm,) + tail, lambda i, *_: (i, 0, 0))],
            out_specs=pl.BlockSpec(memory_space=pl.ANY),
            scratch_shapes=[pltpu.VMEM((MOE_TILE,) + tail, F32),
                            pltpu.SemaphoreType.DMA((2 * N_EXPERTS,)), pltpu.SemaphoreType.DMA(())]),
        compiler_params=_cparams(("arbitrary",)),
        name="moe_dispatch",
    )(dest, zpos, zflag, h3_rows)


def _moe_kernel(te_ref, tr_ref, ts_ref, xs_ref, wg_ref, wu_ref, wd_ref, ys_ref, h_ref, acc_ref):
    i = pl.program_id(0)
    c = pl.program_id(1)
    last = pl.num_programs(1) - 1
    n_sub = h_ref.shape[1] // LANES
    grp = h_ref.shape[0] // MOE_SPLIT

    for part in range(MOE_SPLIT):
        r0 = part * grp
        rows = slice(r0, r0 + grp)
        live = tr_ref[i] > r0

        @pl.when(live & (c == 0))
        def _():
            for s in range(n_sub):
                h_ref[rows, s * LANES:(s + 1) * LANES] = (
                    xs_ref[pl.ds(r0 * n_sub + s, grp, stride=n_sub), :].astype(BF16))
            acc_ref[rows, :] = jnp.zeros((grp, acc_ref.shape[1]), F32)

        @pl.when(live)
        def _():
            h = h_ref[rows, :]
            gt = _dot(h, wg_ref[...])
            up = _dot(h, wu_ref[...])
            act = (gt * jax.nn.sigmoid(gt) * up).astype(BF16)
            acc_ref[rows, :] += _dot(act, wd_ref[...])

        @pl.when(live & (c == last))
        def _():
            for s in range(n_sub):
                ys_ref[pl.ds(r0 * n_sub + s, grp, stride=n_sub), :] = acc_ref[rows, s * LANES:(s + 1) * LANES]

        @pl.when(jnp.logical_not(live) & (c == last))
        def _():
            ys_ref[r0 * n_sub:(r0 + grp) * n_sub, :] = jnp.zeros((grp * n_sub, LANES), F32)


def _moe_experts(xs2d, w_gu, w_down, tile_expert, tile_rows, tile_src, d):
    n_sub = d // LANES
    n_tiles = xs2d.shape[0] // (MOE_TILE * n_sub)
    f = w_down.shape[1]
    nfc = f // MOE_FC
    blk = MOE_TILE * n_sub

    def ceff(i, c, tv):
        return jnp.where(tv[i] > 0, c, nfc - 1)

    return pl.pallas_call(
        _moe_kernel,
        out_shape=jax.ShapeDtypeStruct(xs2d.shape, F32),
        grid_spec=pltpu.PrefetchScalarGridSpec(
            num_scalar_prefetch=3, grid=(n_tiles, nfc),
            in_specs=[
                pl.BlockSpec((blk, LANES), lambda i, c, te, tv, ts: (ts[i], 0)),
                pl.BlockSpec((None, d, MOE_FC), lambda i, c, te, tv, ts: (te[i], 0, ceff(i, c, tv))),
                pl.BlockSpec((None, d, MOE_FC), lambda i, c, te, tv, ts: (te[i], 0, nfc + ceff(i, c, tv))),
                pl.BlockSpec((None, MOE_FC, d), lambda i, c, te, tv, ts: (te[i], ceff(i, c, tv), 0)),
            ],
            out_specs=pl.BlockSpec((blk, LANES), lambda i, c, te, tv, ts: (i, 0)),
            scratch_shapes=[pltpu.VMEM((MOE_TILE, d), BF16), pltpu.VMEM((MOE_TILE, d), F32)]),
        compiler_params=_cparams(("arbitrary", "arbitrary")),
        name="moe_experts",
    )(tile_expert, tile_rows, tile_src, xs2d, w_gu, w_gu, w_down)


def _combine_kernel(dest_ref, x_ref, ys_ref, gate_ref, g_ref, o_ref, ybuf_ref, sem):
    i = pl.program_id(0)
    n = pl.num_programs(0)
    tm, d = x_ref.shape
    n_sub = d // LANES

    def gather(tile, slot):
        def body(r, carry):
            for k in range(2):
                row = dest_ref[k * (dest_ref.shape[0] // 2) + tile * tm + r]
                src = ys_ref.at[pl.ds(pl.multiple_of(row * n_sub, n_sub), n_sub)]
                dst = ybuf_ref.at[slot, pl.ds(pl.multiple_of((2 * r + k) * n_sub, n_sub), n_sub)]
                pltpu.make_async_copy(src, dst, sem.at[slot]).start(priority=k)
            return carry

        lax.fori_loop(0, tm, body, 0, unroll=8)

    @pl.when(i == 0)
    def _():
        gather(0, 0)

    @pl.when(i + 1 < n)
    def _():
        gather(i + 1, (i + 1) % 2)

    slot = i % 2
    pltpu.make_async_copy(ys_ref.at[pl.ds(0, 2 * tm * n_sub)], ybuf_ref.at[slot], sem.at[slot]).wait()

    y_ref = ybuf_ref.at[slot]
    gates = gate_ref[...]
    w1 = gates[:, 0:1]
    w2 = gates[:, 1:2]
    for s in range(n_sub):
        cols = slice(s * LANES, (s + 1) * LANES)
        y1 = y_ref[pl.ds(s, tm, stride=2 * n_sub), :]
        y2 = y_ref[pl.ds(n_sub + s, tm, stride=2 * n_sub), :]
        o_ref[:, cols] = x_ref[:, cols] + w1 * y1 + w2 * y2
    o_ref[...] = _rms(o_ref[...], g_ref[...])


def _combine(x2d, ys2d, dest, gates, gain):
    t, d = x2d.shape
    tm = LIGHT_TILE
    n_sub = d // LANES
    return pl.pallas_call(
        _combine_kernel,
        out_shape=jax.ShapeDtypeStruct((t, d), F32),
        grid_spec=pltpu.PrefetchScalarGridSpec(
            num_scalar_prefetch=1, grid=(t // tm,),
            in_specs=[pl.BlockSpec((tm, d), lambda i, *_: (i, 0)),
                      pl.BlockSpec(memory_space=pl.ANY),
                      pl.BlockSpec((tm, gates.shape[1]), lambda i, *_: (i, 0)),
                      pl.BlockSpec((1, d), lambda i, *_: (0, 0), pipeline_mode=pl.Buffered(1))],
            out_specs=pl.BlockSpec((tm, d), lambda i, *_: (i, 0)),
            scratch_shapes=[pltpu.VMEM((2, 2 * tm * n_sub, LANES), F32), pltpu.SemaphoreType.DMA((2,))]),
        compiler_params=_cparams(("arbitrary",)),
        name="moe_combine",
    )(dest, x2d, ys2d, gates, gain)


def _rope_tables(seq, rot_dim, theta):
    inv = 1.0 / (theta ** (np.arange(0, rot_dim, 2, dtype=np.float64) / rot_dim))
    ang = np.arange(seq, dtype=np.float64)[:, None] * inv[None, :]
    return np.cos(ang), np.sin(ang)


def _retention_tables():
    c = RET_CHUNK
    log_gamma = np.log(1.0 - 2.0 ** (-5.0 - np.arange(RET_HEADS, dtype=np.float64)))
    idx = np.arange(c, dtype=np.float64)
    rel = idx[:, None] - idx[None, :]
    dmat = np.where(rel[None] >= 0, np.exp(np.maximum(rel, 0.0)[None] * log_gamma[:, None, None]), 0.0)
    qd = np.exp((idx + 1.0)[None, :] * log_gamma[:, None])[:, :, None]
    kd = np.exp((c - 1.0 - idx)[None, :] * log_gamma[:, None])[:, :, None]
    cd = np.exp(c * log_gamma)[:, None, None]
    return tuple(jnp.asarray(tb, F32) for tb in (dmat, qd, kd, cd))


def _attn_rope_tables(seq):
    cos, sin = _rope_tables(seq, ROPE_DIM, ROPE_THETA)
    half = ROPE_DIM // 2
    pad = DIFF_HD - ROPE_DIM
    ones = np.ones((seq, pad))
    zeros = np.zeros((seq, pad))
    zh = np.zeros((seq, half))
    ctab = np.concatenate([cos, cos, ones], axis=1)
    s1tab = np.concatenate([-sin, zh, zeros], axis=1)
    s2tab = np.concatenate([zh, sin, zeros], axis=1)
    rep = LANES // DIFF_HD
    return tuple(jnp.asarray(np.tile(tb, (1, rep)), F32) for tb in (ctab, s1tab, s2tab))


def kernel(x, ln_mix, ln_ffn, ret_w_in, ret_w_o, kv_norm, w_kv, diff_w_q, lam_q1, lam_k1, lam_q2, lam_k2,
           diff_subln, diff_w_o, ffn_w_gu, ffn_w_down, moe_router, moe_w_gu, moe_w_down, final_norm):
    batch, seq, d = x.shape
    t = batch * seq
    assert ln_mix.shape[0] == 2 and ret_w_in.shape[0] == 1 and diff_w_q.shape[0] == 1
    assert seq % TOK_TILE == 0 and TOK_TILE % RET_CHUNK == 0 and ATT_TQ % CHUNK == 0
    assert seq % LIGHT_TILE == 0 and seq % DISPATCH_TILE == 0
    assert seq % ATT_TK == 0 and ATT_TK % ATT_TQ == 0
    x2d = x.reshape(t, d)
    row = lambda g: g.reshape(1, -1)

    cos_r, sin_r = (jnp.asarray(tb, F32) for tb in _rope_tables(seq, RET_QK, RET_THETA))
    proj, w_o_bf, w_gu_d_bf, w_down_d_bf, w_q_bf, w_kv_bf, w_ao_bf = _ret_in_proj(
        x2d, row(ln_mix[0]), ret_w_in[0].astype(BF16), cos_r, sin_r, seq,
        (ret_w_o[0], ffn_w_gu[0], ffn_w_down[0], diff_w_q[0], w_kv, diff_w_o[0]))
    ret_o = _retention(proj, *_retention_tables(), batch, seq)
    x2 = _ret_out_ffn(x2d, ret_o, w_o_bf, row(ln_ffn[0]), w_gu_d_bf, w_down_d_bf)

    lambda_init = 0.8 - 0.6 * math.exp(-0.3 * 1)
    q, k, vt = _qkv_proj(x2, row(ln_mix[1]), row(kv_norm), w_q_bf, w_kv_bf[:, :d], w_kv_bf[:, d:].T,
                         *_attn_rope_tables(seq), batch, seq)
    lam_vecs = jnp.stack([lam_q1[0], lam_k1[0], lam_q2[0], lam_k2[0]]).astype(F32)
    n_exp, _, two_f = moe_w_gu[0].shape
    att, w_gu_bf, w_down_bf = _diff_attention(
        q, k, vt, lam_vecs, row(diff_subln[0]), moe_w_gu[0].reshape(n_exp * d, two_f),
        moe_w_down[0].reshape(n_exp * (two_f // 2), d), batch, seq, lambda_init)
    router_pad = jnp.pad(moe_router[0], ((0, 0), (0, LANES - N_EXPERTS)))
    router_hi = router_pad.astype(BF16)
    router_lo = (router_pad - router_hi.astype(F32)).astype(BF16)
    tri = jnp.asarray(np.arange(LIGHT_TILE)[:, None] < np.arange(LIGHT_TILE)[None, :], BF16)
    x3, h3_rows, route, cnt = _attn_out(x2, att, w_ao_bf, row(ln_ffn[1]),
                                        jnp.concatenate([router_hi, router_lo], axis=1), tri)

    n_sub = d // LANES
    n_rows = 2 * t + N_EXPERTS * MOE_TILE
    n_tiles = n_rows // MOE_TILE
    expert = route[0:2].astype(jnp.int32)
    rank = route[2:4].astype(jnp.int32)
    gates = route[4:6].T
    counts = cnt[:N_EXPERTS, 0].astype(jnp.int32)
    padded = (counts + MOE_TILE - 1) // MOE_TILE * MOE_TILE
    seg_end = jnp.cumsum(padded)
    seg_start = seg_end - padded
    dest = (jnp.sum(jnp.where(expert[..., None] == jnp.arange(N_EXPERTS), seg_start, 0), axis=-1)
            + rank).reshape(-1)
    tile_row = jnp.arange(n_tiles, dtype=jnp.int32) * MOE_TILE
    tile_valid = (tile_row < seg_end[-1]).astype(jnp.int32)
    n_valid = seg_end[-1] // MOE_TILE
    tile_src = jnp.minimum(jnp.arange(n_tiles, dtype=jnp.int32), jnp.maximum(n_valid - 1, 0))
    tile_expert = jnp.minimum(jnp.sum(tile_src[:, None] * MOE_TILE >= seg_end[None, :], axis=1),
                              N_EXPERTS - 1).astype(jnp.int32)
    token_end = jnp.sum(jnp.where(tile_expert[:, None] == jnp.arange(N_EXPERTS), seg_start + counts, 0), axis=1)
    tile_rows = (tile_valid * jnp.clip(token_end - tile_row, 0, MOE_TILE)).astype(jnp.int32)
    zflag = jnp.concatenate([(padded > 0).astype(jnp.int32), 1 - tile_valid[-N_EXPERTS:]])
    zpos = jnp.concatenate([jnp.maximum(seg_end - MOE_TILE, 0), tile_row[-N_EXPERTS:]]).astype(jnp.int32)

    xs = _dispatch(h3_rows.reshape(t, n_sub, LANES), dest, zpos, zflag, n_rows)
    ys = _moe_experts(xs.reshape(n_rows * n_sub, LANES), w_gu_bf.reshape(n_exp, d, two_f),
                      w_down_bf.reshape(n_exp, two_f // 2, d), tile_expert, tile_rows, tile_src, d)
    out = _combine(x3, ys, dest, gates, row(final_norm))
    return out.reshape(batch, seq, d)
```

```python
import functools
import math

import jax
import jax.numpy as jnp
import numpy as np
from jax import lax
from jax.experimental import pallas as pl
from jax.experimental.pallas import tpu as pltpu

F32 = jnp.float32
BF16 = jnp.bfloat16

EPS = 1e-6
CHUNK = 64
RET_QK = 256
RET_V = 512
RET_HEADS = 4
RET_THETA = 10000.0
DIFF_HD = 64
DIFF_HEADS = 8
ROPE_THETA = 500000.0
ROPE_DIM = 16
N_EXPERTS = 8

LANES = 128
SUBLANES = 8
MXU_COLS = 256
VMEM_LIMIT = 56 * 1024 * 1024

RET_CHUNK = 256
TOK_TILE = 512
LIGHT_TILE = 1024
DISPATCH_TILE = 1024
ATT_TQ = 512
ATT_TK = 512
ATT_HEADS = 4
ATT_ONES = 16
MOE_TILE = 1024
MOE_SPLIT = 2
MOE_FC = 1792


def _cparams(sem, vmem=VMEM_LIMIT):
    return pltpu.CompilerParams(dimension_semantics=sem, vmem_limit_bytes=vmem)


def _const_spec(shape):
    nd = len(shape)
    return pl.BlockSpec(shape, lambda *_: (0,) * nd, pipeline_mode=pl.Buffered(1))


def _slab_specs(weights, n_steps, step_of):
    pack = 2 * SUBLANES
    specs = []
    for w in weights:
        rows = w.shape[0]
        per = -(-rows // n_steps)
        br = next(b for b in range(-(-per // pack) * pack, rows + 1, pack) if rows % b == 0)
        last = rows // br - 1
        specs.append(pl.BlockSpec((br, w.shape[1]),
                                  lambda *g, last=last: (jnp.minimum(step_of(*g), last), 0)))
    return specs


def _round_slabs(in_refs, out_refs):
    for src, dst in zip(in_refs, out_refs):
        dst[...] = src[...].astype(BF16)


def _rms(x, g):
    return x * lax.rsqrt(jnp.mean(x * x, axis=-1, keepdims=True) + EPS) * g


def _dot(a, b):
    return jnp.dot(a, b, preferred_element_type=F32)


def _ret_in_kernel(x_ref, g_ref, w_ref, cos_ref, sin_ref, *refs):
    n_ride = (len(refs) - 1) // 2
    o_ref = refs[n_ride]
    _round_slabs(refs[:n_ride], refs[n_ride + 1:])
    h = _rms(x_ref[...], g_ref[...]).astype(BF16)
    cos = cos_ref[...]
    sin = sin_ref[...]
    d_qk = RET_HEADS * RET_QK
    half = RET_QK // 2
    for c in range(2 * RET_HEADS):
        c0 = c * RET_QK
        acc = _dot(h, w_ref[:, c0:c0 + RET_QK])
        x1 = acc[:, :half]
        x2 = acc[:, half:]
        scale = 1.0 if c < RET_HEADS else RET_QK ** -0.5
        o_ref[:, c0:c0 + half] = ((x1 * cos - x2 * sin) * scale).astype(BF16)
        o_ref[:, c0 + half:c0 + RET_QK] = ((x2 * cos + x1 * sin) * scale).astype(BF16)
    n_rest = (w_ref.shape[1] - 2 * d_qk) // RET_V
    for c in range(n_rest):
        c0 = 2 * d_qk + c * RET_V
        o_ref[:, c0:c0 + RET_V] = _dot(h, w_ref[:, c0:c0 + RET_V]).astype(BF16)


def _ret_in_proj(x2d, gain, w, cos, sin, seq, ride):
    t, d = x2d.shape
    n = w.shape[1]
    tm = TOK_TILE
    n_pos = seq // tm
    slabs = _slab_specs(ride, t // tm, lambda i: i)
    return pl.pallas_call(
        _ret_in_kernel,
        out_shape=[jax.ShapeDtypeStruct((t, n), BF16)] + [jax.ShapeDtypeStruct(r.shape, BF16) for r in ride],
        grid=(t // tm,),
        in_specs=[
            pl.BlockSpec((tm, d), lambda i: (i, 0)),
            _const_spec((1, d)),
            _const_spec((d, n)),
            pl.BlockSpec((tm, RET_QK // 2), lambda i: (i % n_pos, 0)),
            pl.BlockSpec((tm, RET_QK // 2), lambda i: (i % n_pos, 0)),
        ] + slabs,
        out_specs=[pl.BlockSpec((tm, n), lambda i: (i, 0))] + slabs,
        compiler_params=_cparams(("arbitrary",)),
        name="ret_in_proj",
    )(x2d, gain, w, cos, sin, *ride)


def _retention_kernel(q_ref, k_ref, v_ref, g_ref, dmat_ref, qd_ref, kd_ref, cd_ref, o_ref, state_ref):
    @pl.when(pl.program_id(1) == 0)
    def _():
        state_ref[...] = jnp.zeros_like(state_ref)

    n_chunks = q_ref.shape[0] // RET_CHUNK
    for h in range(RET_HEADS):
        for c in range(n_chunks):
            rows = slice(c * RET_CHUNK, (c + 1) * RET_CHUNK)
            q = q_ref[rows, h * RET_QK:(h + 1) * RET_QK]
            k = k_ref[rows, h * RET_QK:(h + 1) * RET_QK]
            v = v_ref[rows, h * RET_V:(h + 1) * RET_V]
            state = state_ref[h]
            s = lax.dot_general(q, k, (((1,), (1,)), ((), ())), preferred_element_type=F32)
            s = s * dmat_ref[h]
            o = _dot(s.astype(BF16), v)
            qs = (q.astype(F32) * qd_ref[h]).astype(BF16)
            o = o + _dot(qs, state.astype(BF16))
            ks = (k.astype(F32) * kd_ref[h]).astype(BF16)
            state_ref[h] = state * cd_ref[h] + lax.dot_general(
                ks, v, (((0,), (0,)), ((), ())), preferred_element_type=F32)
            mu = jnp.mean(o, axis=-1, keepdims=True)
            oc = o - mu
            var = jnp.mean(oc * oc, axis=-1, keepdims=True)
            on = oc * lax.rsqrt(var + EPS)
            gate = g_ref[rows, h * RET_V:(h + 1) * RET_V].astype(F32)
            o_ref[rows, h * RET_V:(h + 1) * RET_V] = (on * (gate * jax.nn.sigmoid(gate))).astype(BF16)


def _retention(proj, dmat, qd, kd, cd, batch, seq):
    t = proj.shape[0]
    tb = TOK_TILE
    nj = seq // tb
    d_qk = RET_HEADS * RET_QK
    d_v = RET_HEADS * RET_V
    row = lambda b, j: b * nj + j
    return pl.pallas_call(
        _retention_kernel,
        out_shape=jax.ShapeDtypeStruct((t, d_v), BF16),
        grid=(batch, nj),
        in_specs=[
            pl.BlockSpec((tb, d_qk), lambda b, j: (row(b, j), 0)),
            pl.BlockSpec((tb, d_qk), lambda b, j: (row(b, j), 1)),
            pl.BlockSpec((tb, d_v), lambda b, j: (row(b, j), 1)),
            pl.BlockSpec((tb, d_v), lambda b, j: (row(b, j), 2)),
            _const_spec(dmat.shape),
            _const_spec(qd.shape),
            _const_spec(kd.shape),
            _const_spec(cd.shape),
        ],
        out_specs=pl.BlockSpec((tb, d_v), lambda b, j: (row(b, j), 0)),
        scratch_shapes=[pltpu.VMEM((RET_HEADS, RET_QK, RET_V), F32)],
        compiler_params=_cparams(("parallel", "arbitrary")),
        name="retention",
    )(proj, proj, proj, proj, dmat, qd, kd, cd)


def _ret_out_ffn_kernel(x_ref, o_ref, wo_ref, g_ref, wgu_ref, wd_ref, out_ref, act_ref, *, fc):
    x1 = x_ref[...] + _dot(o_ref[...], wo_ref[...])
    h = _rms(x1, g_ref[...]).astype(BF16)
    f = wd_ref.shape[0]
    for c in range(f // fc):
        gt = _dot(h, wgu_ref[:, c * fc:(c + 1) * fc])
        up = _dot(h, wgu_ref[:, f + c * fc:f + (c + 1) * fc])
        act_ref[:, c * fc:(c + 1) * fc] = (gt * jax.nn.sigmoid(gt) * up).astype(BF16)
    out_ref[...] = x1 + _dot(act_ref[...], wd_ref[...])


def _ret_out_ffn(x2d, o, w_o, gain, w_gu, w_down):
    t, d = x2d.shape
    tm = TOK_TILE
    f = w_down.shape[0]
    return pl.pallas_call(
        functools.partial(_ret_out_ffn_kernel, fc=256),
        out_shape=jax.ShapeDtypeStruct((t, d), F32),
        grid=(t // tm,),
        in_specs=[
            pl.BlockSpec((tm, d), lambda i: (i, 0)),
            pl.BlockSpec((tm, o.shape[1]), lambda i: (i, 0)),
            _const_spec(w_o.shape),
            _const_spec((1, d)),
            _const_spec(w_gu.shape),
            _const_spec(w_down.shape),
        ],
        out_specs=pl.BlockSpec((tm, d), lambda i: (i, 0)),
        scratch_shapes=[pltpu.VMEM((tm, f), BF16)],
        compiler_params=_cparams(("parallel",)),
        name="ret_out_ffn",
    )(x2d, o, w_o, gain, w_gu, w_down)


def _rope16(x, ctab, s1tab, s2tab):
    half = ROPE_DIM // 2
    return (x * ctab + pltpu.roll(x, LANES - half, 1) * s1tab + pltpu.roll(x, half, 1) * s2tab)


def _rope16_t(x, ctab, s1tab, s2tab):
    half = ROPE_DIM // 2
    return (x * ctab + pltpu.roll(x, x.shape[0] - half, 0) * s1tab + pltpu.roll(x, half, 0) * s2tab)


def _qkv_kernel(x_ref, gq_ref, gkv_ref, wqt_ref, wk_ref, wvt_ref, c_ref, s1_ref, s2_ref,
                ct_ref, s1t_ref, s2t_ref, qt_ref, k_ref, vt_ref, qacc_ref):
    x = x_ref[...]
    xn = x * lax.rsqrt(jnp.mean(x * x, axis=-1, keepdims=True) + EPS)
    hq = (xn * gq_ref[...]).astype(BF16)
    hkv = (xn * gkv_ref[...]).astype(BF16)
    ctab, s1tab, s2tab = c_ref[...], s1_ref[...], s2_ref[...]
    d = k_ref.shape[1]
    dv = 2 * DIFF_HD
    for c in range(d // MXU_COLS):
        c0 = c * MXU_COLS
        ka = _dot(hkv, wk_ref[:, c0:c0 + MXU_COLS])
        for l0 in range(0, MXU_COLS, LANES):
            cols = slice(c0 + l0, c0 + l0 + LANES)
            k_ref[:, cols] = _rope16(ka[:, l0:l0 + LANES], ctab, s1tab, s2tab).astype(BF16)
    nt = (((1,), (1,)), ((), ()))
    vt_ref[...] = lax.dot_general(wvt_ref[...], hkv, nt, preferred_element_type=F32).astype(BF16)
    qacc_ref[...] = lax.dot_general(wqt_ref[...], hq, nt, preferred_element_type=F32)
    q_scale = DIFF_HD ** -0.5 * math.log2(math.e)
    ctab_t, s1tab_t, s2tab_t = ct_ref[...], s1t_ref[...], s2t_ref[...]
    for h in range(d // dv):
        rows = slice(h * dv, (h + 1) * dv)
        qt_ref[rows, :] = (_rope16_t(qacc_ref[rows, :], ctab_t, s1tab_t, s2tab_t) * q_scale).astype(BF16)


def _qkv_proj(x2d, gq, gkv, w_qt, w_k, w_vt, tabs, tabs_t, batch, seq):
    t, d = x2d.shape
    tm = LIGHT_TILE
    n_pos = seq // tm
    tok = pl.BlockSpec((tm, d), lambda i: (i, 0))
    tab = pl.BlockSpec((tm, LANES), lambda i: (i % n_pos, 0))
    tab_t = pl.BlockSpec((LANES, tm), lambda i: (0, i % n_pos))
    chan = pl.BlockSpec((d, tm), lambda i: (i // n_pos, i % n_pos))
    return pl.pallas_call(
        _qkv_kernel,
        out_shape=[jax.ShapeDtypeStruct((batch * d, seq), BF16), jax.ShapeDtypeStruct((t, d), BF16),
                   jax.ShapeDtypeStruct((batch * d, seq), BF16)],
        grid=(t // tm,),
        in_specs=[tok, _const_spec((1, d)), _const_spec((1, d)), _const_spec(w_qt.shape),
                  _const_spec(w_k.shape), _const_spec(w_vt.shape), tab, tab, tab, tab_t, tab_t, tab_t],
        out_specs=[chan, tok, chan],
        scratch_shapes=[pltpu.VMEM((d, tm), F32)],
        compiler_params=_cparams(("parallel",)),
        name="qkv_proj",
    )(x2d, gq, gkv, w_qt, w_k, w_vt, *tabs, *tabs_t)


def _attn_kernel(lam_ref, bias_ref, q_ref, qn_ref, k_ref, vt_ref, sub_ref, wa_ref, wb_ref,
                 o_ref, wa_out_ref, wb_out_ref, vta_ref, s_ref, mt_ref, acc_ref, *, lambda_init):
    i = pl.program_id(2)
    tq = q_ref.shape[1]
    dv = 2 * DIFF_HD
    n_heads, n_kv, _, tk = vta_ref.shape
    chains = [(h, c) for h in range(n_heads) for c in range(2)]

    _round_slabs((wa_ref, wb_ref), (wa_out_ref, wb_out_ref))

    @pl.when(i == 0)
    def _():
        for h in range(n_heads):
            for j in range(n_kv):
                vta_ref[h, j, 0:dv, :] = vt_ref[h * dv:(h + 1) * dv, j * tk:(j + 1) * tk]
                vta_ref[h, j, dv:, :] = jnp.ones((ATT_ONES, tk), BF16)

    chan = lax.broadcasted_iota(jnp.int32, (dv, tq), 0)

    def split_heads(ref):
        out = []
        for h in range(n_heads):
            q = ref[h * dv:(h + 1) * dv, :]
            zero = jnp.zeros_like(q)
            out.append((jnp.where(chan < DIFF_HD, q, zero), jnp.where(chan >= DIFF_HD, q, zero)))
        return out

    def first_tiles(qi):
        n = (qi * tq) // tk
        return n, 1 + (qi * tq - n * tk) // tq

    n_full, tail_bias = first_tiles(i)

    def tile_max(s):
        m8 = jnp.max(s.reshape(tk // SUBLANES, SUBLANES, tq), axis=0)
        return jnp.max(m8, axis=0, keepdims=True)

    def score_chain(j, x, qsplit, n_vis, tail):
        h, c = chains[x]
        bias = bias_ref[jnp.where(j == n_vis, tail, 0)]
        kt = k_ref[pl.ds(pl.multiple_of(j * tk, tk), tk), h * dv:(h + 1) * dv]
        s = _dot(kt, qsplit[h][c]) + bias
        s_ref[x] = s
        return tile_max(s)

    @pl.when(i == 0)
    def _():
        qc0 = split_heads(q_ref)
        for x in range(len(chains)):
            mt_ref[x] = score_chain(0, x, qc0, n_full, tail_bias)

    qc = split_heads(q_ref)
    acc_ref[...] = jnp.zeros_like(acc_ref)
    m_init = tuple(jnp.full((1, tq), -jnp.inf, F32) for _ in chains)

    def step(j, carry, next_scores):
        m_run, m_tile = carry
        m_out, m_next = [], []
        for x, (h, c) in enumerate(chains):
            m_new = jnp.maximum(m_run[x], m_tile[x])
            alpha = jnp.exp2(m_run[x] - m_new)
            p = jnp.exp2(s_ref[x] - m_new).astype(BF16)
            acc_ref[x] = alpha * acc_ref[x] + _dot(vta_ref[h, j], p)
            m_out.append(m_new)
            if next_scores is not None:
                m_next.append(next_scores(x))
        return tuple(m_out), tuple(m_next)

    first = tuple(mt_ref[x] for x in range(len(chains)))
    carry = lax.fori_loop(
        0, n_full,
        lambda j, carry: step(j, carry, lambda x: score_chain(j + 1, x, qc, n_full, tail_bias)),
        (m_init, first))

    @pl.when(i + 1 < pl.num_programs(2))
    def _():
        qn = split_heads(qn_ref)
        n_vis, tail = first_tiles(i + 1)
        _, m_first = step(n_full, carry, lambda x: score_chain(0, x, qn, n_vis, tail))
        for x in range(len(chains)):
            mt_ref[x] = m_first[x]

    @pl.when(i + 1 == pl.num_programs(2))
    def _():
        step(n_full, carry, None)

    lam_v = lam_ref[...]
    lam = (jnp.exp(jnp.sum(lam_v[0:1] * lam_v[1:2], axis=-1, keepdims=True))
           - jnp.exp(jnp.sum(lam_v[2:3] * lam_v[3:4], axis=-1, keepdims=True)) + lambda_init)
    for h in range(n_heads):
        a0 = acc_ref[2 * h]
        a1 = acc_ref[2 * h + 1]
        ot = a0[0:dv] / a0[dv:dv + 1] - lam * (a1[0:dv] / a1[dv:dv + 1])
        o = _rms(ot.T, sub_ref[...]) * (1.0 - lambda_init)
        o_ref[:, h * dv:(h + 1) * dv] = o.astype(BF16)


def _attn_bias(tq, tk):
    key = np.arange(tk)[:, None] // CHUNK
    tiles = [np.zeros((tk, tq))]
    for r in range(tk // tq):
        qry = (r * tq + np.arange(tq))[None, :] // CHUNK
        tiles.append(np.where(key <= qry, 0.0, -1e30))
    return jnp.asarray(np.stack(tiles), F32)


def _diff_attention(qt, k, vt, lam_vecs, subln, w_a, w_b, batch, seq, lambda_init):
    t, d = k.shape
    tq = ATT_TQ
    nq = seq // tq
    dv = 2 * DIFF_HD
    hp = ATT_HEADS
    n_groups = DIFF_HEADS // hp
    n_steps = batch * n_groups * nq
    bias = _attn_bias(tq, ATT_TK)
    slabs = _slab_specs((w_a, w_b), n_steps, lambda b, g, i: (b * n_groups + g) * nq + i)
    return pl.pallas_call(
        functools.partial(_attn_kernel, lambda_init=lambda_init),
        out_shape=[jax.ShapeDtypeStruct((t, d), BF16),
                   jax.ShapeDtypeStruct(w_a.shape, BF16), jax.ShapeDtypeStruct(w_b.shape, BF16)],
        grid=(batch, n_groups, nq),
        in_specs=[
            _const_spec(lam_vecs.shape),
            _const_spec(bias.shape),
            pl.BlockSpec((hp * dv, tq), lambda b, g, i: (b * n_groups + g, i)),
            pl.BlockSpec((hp * dv, tq), lambda b, g, i: (b * n_groups + g, jnp.minimum(i + 1, nq - 1))),
            pl.BlockSpec((seq, hp * dv), lambda b, g, i: (b, g)),
            pl.BlockSpec((hp * dv, seq), lambda b, g, i: (b * n_groups + g, 0)),
            _const_spec((1, dv)),
        ] + slabs,
        out_specs=[pl.BlockSpec((tq, hp * dv), lambda b, g, i: (b * nq + i, g))] + slabs,
        scratch_shapes=[pltpu.VMEM((hp, seq // ATT_TK, dv + ATT_ONES, ATT_TK), BF16),
                        pltpu.VMEM((2 * hp, ATT_TK, tq), F32),
                        pltpu.VMEM((2 * hp, 1, tq), F32),
                        pltpu.VMEM((2 * hp, dv + ATT_ONES, tq), F32)],
        compiler_params=_cparams(("parallel", "parallel", "arbitrary")),
        name="diff_attn",
    )(lam_vecs, bias, qt, qt, k, vt, subln, w_a, w_b)


ROUTE_ROWS = 2 * SUBLANES


def _attn_out_kernel(x_ref, o_ref, wo_ref, g_ref, r_ref, tri_ref, x3_ref, h3_ref, route_ref, cnt_ref, carry_ref):
    @pl.when(pl.program_id(0) == 0)
    def _():
        carry_ref[...] = jnp.zeros_like(carry_ref)

    tm = x_ref.shape[0]
    x3 = x_ref[...] + _dot(o_ref[...], wo_ref[...])
    x3_ref[...] = x3
    h3 = _rms(x3, g_ref[...])
    for s in range(h3.shape[1] // LANES):
        h3_ref[pl.ds(s, tm, stride=SUBLANES), :] = h3[:, s * LANES:(s + 1) * LANES]

    hi = h3.astype(BF16)
    lo = (h3 - hi.astype(F32)).astype(BF16)
    both = _dot(hi, r_ref[...])
    logits = both[:, :LANES] + both[:, LANES:] + _dot(lo, r_ref[:, :LANES])
    lt = logits.T[0:ROUTE_ROWS]
    sub = lax.broadcasted_iota(jnp.int32, lt.shape, 0).astype(F32)
    lt = jnp.where(sub < N_EXPERTS, lt, -jnp.inf)
    v1 = jnp.max(lt, axis=0, keepdims=True)
    i1 = jnp.min(jnp.where(lt == v1, sub, float(ROUTE_ROWS)), axis=0, keepdims=True)
    lt2 = jnp.where(sub == i1, -jnp.inf, lt)
    v2 = jnp.max(lt2, axis=0, keepdims=True)
    i2 = jnp.min(jnp.where(lt2 == v2, sub, float(ROUTE_ROWS)), axis=0, keepdims=True)
    e = jnp.exp(v2 - v1)
    w1 = 1.0 / (1.0 + e)
    w2 = e / (1.0 + e)
    oh1 = sub == i1
    oh2 = sub == i2
    assign = jnp.where(oh1 | oh2, 1.0, 0.0)
    excl = _dot(assign.astype(BF16), tri_ref[...]) + carry_ref[:, 0:1]
    r1 = jnp.sum(jnp.where(oh1, excl, 0.0), axis=0, keepdims=True)
    r2 = jnp.sum(jnp.where(oh2, excl, 0.0), axis=0, keepdims=True)
    route = jnp.zeros_like(lt)
    for row, val in enumerate((i1, i2, r1, r2, w1, w2)):
        route = jnp.where(sub == float(row), val, route)
    route_ref[...] = route[0:SUBLANES]
    total = carry_ref[:, 0:1] + jnp.sum(assign, axis=1, keepdims=True)
    carry_ref[...] = jnp.broadcast_to(total, carry_ref.shape)
    cnt_ref[...] = jnp.broadcast_to(total, cnt_ref.shape)


def _attn_out(x2d, o, w_o, gain, router_split, tri):
    t, d = x2d.shape
    tm = LIGHT_TILE
    n_sub = d // LANES
    return pl.pallas_call(
        _attn_out_kernel,
        out_shape=[jax.ShapeDtypeStruct((t, d), F32),
                   jax.ShapeDtypeStruct((t * n_sub, LANES), F32),
                   jax.ShapeDtypeStruct((SUBLANES, t), F32),
                   jax.ShapeDtypeStruct((ROUTE_ROWS, LANES), F32)],
        grid=(t // tm,),
        in_specs=[pl.BlockSpec((tm, d), lambda i: (i, 0)),
                  pl.BlockSpec((tm, d), lambda i: (i, 0)),
                  _const_spec(w_o.shape), _const_spec((1, d)), _const_spec(router_split.shape),
                  _const_spec(tri.shape)],
        out_specs=[pl.BlockSpec((tm, d), lambda i: (i, 0)),
                   pl.BlockSpec((tm * n_sub, LANES), lambda i: (i, 0)),
                   pl.BlockSpec((SUBLANES, tm), lambda i: (0, i)),
                   _const_spec((ROUTE_ROWS, LANES))],
        scratch_shapes=[pltpu.VMEM((ROUTE_ROWS, LANES), F32)],
        compiler_params=_cparams(("arbitrary",)),
        name="attn_out_route",
    )(x2d, o, w_o, gain, router_split, tri)


def _dispatch_kernel(dest_ref, zpos_ref, zflag_ref, h_ref, dst_ref, zbuf_ref, zsem, sem):
    i = pl.program_id(0)
    tm = h_ref.shape[0]

    @pl.when(i == 0)
    def _():
        zbuf_ref[...] = jnp.zeros_like(zbuf_ref)

        def zcopy(e):
            return pltpu.make_async_copy(zbuf_ref, dst_ref.at[pl.ds(zpos_ref[e], MOE_TILE)], zsem.at[e])

        for e in range(2 * N_EXPERTS):
            @pl.when(zflag_ref[e] == 1)
            def _():
                zcopy(e).start()
        for e in range(2 * N_EXPERTS):
            @pl.when(zflag_ref[e] == 1)
            def _():
                zcopy(e).wait()

    n_tok = dest_ref.shape[0] // 2

    def body(r, carry):
        for k in range(2):
            pltpu.make_async_copy(h_ref.at[r], dst_ref.at[dest_ref[k * n_tok + i * tm + r]],
                                  sem).start(priority=k)
        return carry

    lax.fori_loop(0, tm, body, 0, unroll=8)
    for k in range(2):
        pltpu.make_async_copy(h_ref, dst_ref.at[pl.ds(0, tm)], sem).wait()


def _dispatch(h3_rows, dest, zpos, zflag, n_dst):
    t = h3_rows.shape[0]
    tm = DISPATCH_TILE
    tail = h3_rows.shape[1:]
    return pl.pallas_call(
        _dispatch_kernel,
        out_shape=jax.ShapeDtypeStruct((n_dst,) + tail, F32),
        grid_spec=pltpu.PrefetchScalarGridSpec(
            num_scalar_prefetch=3, grid=(t // tm,),
            in_specs=[pl.BlockSpec((tm,) + tail, lambda i, *_: (i, 0, 0))],
            out_specs=pl.BlockSpec(memory_space=pl.ANY),
            scratch_shapes=[pltpu.VMEM((MOE_TILE,) + tail, F32),
                            pltpu.SemaphoreType.DMA((2 * N_EXPERTS,)), pltpu.SemaphoreType.DMA(())]),
        compiler_params=_cparams(("arbitrary",)),
        name="moe_dispatch",
    )(dest, zpos, zflag, h3_rows)


def _moe_kernel(te_ref, tr_ref, ts_ref, xs_ref, wg_ref, wu_ref, wd_ref, ys_ref, h_ref, acc_ref):
    i = pl.program_id(0)
    c = pl.program_id(1)
    last = pl.num_programs(1) - 1
    n_sub = h_ref.shape[1] // LANES
    grp = h_ref.shape[0] // MOE_SPLIT

    for part in range(MOE_SPLIT):
        r0 = part * grp
        rows = slice(r0, r0 + grp)
        live = tr_ref[i] > r0

        @pl.when(live & (c == 0))
        def _():
            for s in range(n_sub):
                h_ref[rows, s * LANES:(s + 1) * LANES] = (
                    xs_ref[pl.ds(r0 * n_sub + s, grp, stride=n_sub), :].astype(BF16))
            acc_ref[rows, :] = jnp.zeros((grp, acc_ref.shape[1]), F32)

        @pl.when(live)
        def _():
            h = h_ref[rows, :]
            gt = _dot(h, wg_ref[...])
            up = _dot(h, wu_ref[...])
            act = (gt * jax.nn.sigmoid(gt) * up).astype(BF16)
            acc_ref[rows, :] += _dot(act, wd_ref[...])

        @pl.when(live & (c == last))
        def _():
            for s in range(n_sub):
                ys_ref[pl.ds(r0 * n_sub + s, grp, stride=n_sub), :] = acc_ref[rows, s * LANES:(s + 1) * LANES]

        @pl.when(jnp.logical_not(live) & (c == last))
        def _():
            ys_ref[r0 * n_sub:(r0 + grp) * n_sub, :] = jnp.zeros((grp * n_sub, LANES), F32)


def _moe_experts(xs2d, w_gu, w_down, tile_expert, tile_rows, tile_src, d):
    n_sub = d // LANES
    n_tiles = xs2d.shape[0] // (MOE_TILE * n_sub)
    f = w_down.shape[1]
    nfc = f // MOE_FC
    blk = MOE_TILE * n_sub

    def ceff(i, c, tv):
        return jnp.where(tv[i] > 0, c, nfc - 1)

    return pl.pallas_call(
        _moe_kernel,
        out_shape=jax.ShapeDtypeStruct(xs2d.shape, F32),
        grid_spec=pltpu.PrefetchScalarGridSpec(
            num_scalar_prefetch=3, grid=(n_tiles, nfc),
            in_specs=[
                pl.BlockSpec((blk, LANES), lambda i, c, te, tv, ts: (ts[i], 0)),
                pl.BlockSpec((None, d, MOE_FC), lambda i, c, te, tv, ts: (te[i], 0, ceff(i, c, tv))),
                pl.BlockSpec((None, d, MOE_FC), lambda i, c, te, tv, ts: (te[i], 0, nfc + ceff(i, c, tv))),
                pl.BlockSpec((None, MOE_FC, d), lambda i, c, te, tv, ts: (te[i], ceff(i, c, tv), 0)),
            ],
            out_specs=pl.BlockSpec((blk, LANES), lambda i, c, te, tv, ts: (i, 0)),
            scratch_shapes=[pltpu.VMEM((MOE_TILE, d), BF16), pltpu.VMEM((MOE_TILE, d), F32)]),
        compiler_params=_cparams(("arbitrary", "arbitrary")),
        name="moe_experts",
    )(tile_expert, tile_rows, tile_src, xs2d, w_gu, w_gu, w_down)


def _combine_kernel(dest_ref, x_ref, ys_ref, gate_ref, g_ref, o_ref, ybuf_ref, sem):
    i = pl.program_id(0)
    n = pl.num_programs(0)
    tm, d = x_ref.shape
    n_sub = d // LANES

    def gather(tile, slot):
        def body(r, carry):
            for k in range(2):
                row = dest_ref[k * (dest_ref.shape[0] // 2) + tile * tm + r]
                src = ys_ref.at[pl.ds(pl.multiple_of(row * n_sub, n_sub), n_sub)]
                dst = ybuf_ref.at[slot, pl.ds(pl.multiple_of((2 * r + k) * n_sub, n_sub), n_sub)]
                pltpu.make_async_copy(src, dst, sem.at[slot]).start(priority=k)
            return carry

        lax.fori_loop(0, tm, body, 0, unroll=8)

    @pl.when(i == 0)
    def _():
        gather(0, 0)

    @pl.when(i + 1 < n)
    def _():
        gather(i + 1, (i + 1) % 2)

    slot = i % 2
    pltpu.make_async_copy(ys_ref.at[pl.ds(0, 2 * tm * n_sub)], ybuf_ref.at[slot], sem.at[slot]).wait()

    y_ref = ybuf_ref.at[slot]
    gates = gate_ref[...]
    w1 = gates[:, 0:1]
    w2 = gates[:, 1:2]
    for s in range(n_sub):
        cols = slice(s * LANES, (s + 1) * LANES)
        y1 = y_ref[pl.ds(s, tm, stride=2 * n_sub), :]
        y2 = y_ref[pl.ds(n_sub + s, tm, stride=2 * n_sub), :]
        o_ref[:, cols] = x_ref[:, cols] + w1 * y1 + w2 * y2
    o_ref[...] = _rms(o_ref[...], g_ref[...])


def _combine(x2d, ys2d, dest, gates, gain):
    t, d = x2d.shape
    tm = TOK_TILE
    n_sub = d // LANES
    return pl.pallas_call(
        _combine_kernel,
        out_shape=jax.ShapeDtypeStruct((t, d), F32),
        grid_spec=pltpu.PrefetchScalarGridSpec(
            num_scalar_prefetch=1, grid=(t // tm,),
            in_specs=[pl.BlockSpec((tm, d), lambda i, *_: (i, 0)),
                      pl.BlockSpec(memory_space=pl.ANY),
                      pl.BlockSpec((tm, gates.shape[1]), lambda i, *_: (i, 0)),
                      pl.BlockSpec((1, d), lambda i, *_: (0, 0), pipeline_mode=pl.Buffered(1))],
            out_specs=pl.BlockSpec((tm, d), lambda i, *_: (i, 0)),
            scratch_shapes=[pltpu.VMEM((2, 2 * tm * n_sub, LANES), F32), pltpu.SemaphoreType.DMA((2,))]),
        compiler_params=_cparams(("arbitrary",)),
        name="moe_combine",
    )(dest, x2d, ys2d, gates, gain)


def _rope_tables(seq, rot_dim, theta):
    inv = 1.0 / (theta ** (np.arange(0, rot_dim, 2, dtype=np.float64) / rot_dim))
    ang = np.arange(seq, dtype=np.float64)[:, None] * inv[None, :]
    return np.cos(ang), np.sin(ang)


def _retention_tables():
    c = RET_CHUNK
    log_gamma = np.log(1.0 - 2.0 ** (-5.0 - np.arange(RET_HEADS, dtype=np.float64)))
    idx = np.arange(c, dtype=np.float64)
    rel = idx[:, None] - idx[None, :]
    dmat = np.where(rel[None] >= 0, np.exp(np.maximum(rel, 0.0)[None] * log_gamma[:, None, None]), 0.0)
    qd = np.exp((idx + 1.0)[None, :] * log_gamma[:, None])[:, :, None]
    kd = np.exp((c - 1.0 - idx)[None, :] * log_gamma[:, None])[:, :, None]
    cd = np.exp(c * log_gamma)[:, None, None]
    return tuple(jnp.asarray(tb, F32) for tb in (dmat, qd, kd, cd))


def _attn_rope_tables(seq):
    cos, sin = _rope_tables(seq, ROPE_DIM, ROPE_THETA)
    half = ROPE_DIM // 2
    pad = DIFF_HD - ROPE_DIM
    ones = np.ones((seq, pad))
    zeros = np.zeros((seq, pad))
    zh = np.zeros((seq, half))
    ctab = np.concatenate([cos, cos, ones], axis=1)
    s1tab = np.concatenate([-sin, zh, zeros], axis=1)
    s2tab = np.concatenate([zh, sin, zeros], axis=1)
    rep = LANES // DIFF_HD
    tabs = [np.tile(tb, (1, rep)) for tb in (ctab, s1tab, s2tab)]
    return (tuple(jnp.asarray(tb, F32) for tb in tabs),
            tuple(jnp.asarray(tb.T, F32) for tb in tabs))


def kernel(x, ln_mix, ln_ffn, ret_w_in, ret_w_o, kv_norm, w_kv, diff_w_q, lam_q1, lam_k1, lam_q2, lam_k2,
           diff_subln, diff_w_o, ffn_w_gu, ffn_w_down, moe_router, moe_w_gu, moe_w_down, final_norm):
    batch, seq, d = x.shape
    t = batch * seq
    assert ln_mix.shape[0] == 2 and ret_w_in.shape[0] == 1 and diff_w_q.shape[0] == 1
    assert seq % TOK_TILE == 0 and TOK_TILE % RET_CHUNK == 0 and ATT_TQ % CHUNK == 0
    assert seq % LIGHT_TILE == 0 and seq % DISPATCH_TILE == 0
    assert seq % ATT_TK == 0 and ATT_TK % ATT_TQ == 0
    x2d = x.reshape(t, d)
    row = lambda g: g.reshape(1, -1)

    cos_r, sin_r = (jnp.asarray(tb, F32) for tb in _rope_tables(seq, RET_QK, RET_THETA))
    proj, w_o_bf, w_gu_d_bf, w_down_d_bf, w_q_bf, w_kv_bf, w_ao_bf = _ret_in_proj(
        x2d, row(ln_mix[0]), ret_w_in[0].astype(BF16), cos_r, sin_r, seq,
        (ret_w_o[0], ffn_w_gu[0], ffn_w_down[0], diff_w_q[0], w_kv, diff_w_o[0]))
    ret_o = _retention(proj, *_retention_tables(), batch, seq)
    x2 = _ret_out_ffn(x2d, ret_o, w_o_bf, row(ln_ffn[0]), w_gu_d_bf, w_down_d_bf)

    lambda_init = 0.8 - 0.6 * math.exp(-0.3 * 1)
    qt, k, vt = _qkv_proj(x2, row(ln_mix[1]), row(kv_norm), w_q_bf.T, w_kv_bf[:, :d], w_kv_bf[:, d:].T,
                          *_attn_rope_tables(seq), batch, seq)
    lam_vecs = jnp.stack([lam_q1[0], lam_k1[0], lam_q2[0], lam_k2[0]]).astype(F32)
    n_exp, _, two_f = moe_w_gu[0].shape
    att, w_gu_bf, w_down_bf = _diff_attention(
        qt, k, vt, lam_vecs, row(diff_subln[0]), moe_w_gu[0].reshape(n_exp * d, two_f),
        moe_w_down[0].reshape(n_exp * (two_f // 2), d), batch, seq, lambda_init)
    router_pad = jnp.pad(moe_router[0], ((0, 0), (0, LANES - N_EXPERTS)))
    router_hi = router_pad.astype(BF16)
    router_lo = (router_pad - router_hi.astype(F32)).astype(BF16)
    tri = jnp.asarray(np.arange(LIGHT_TILE)[:, None] < np.arange(LIGHT_TILE)[None, :], BF16)
    x3, h3_rows, route, cnt = _attn_out(x2, att, w_ao_bf, row(ln_ffn[1]),
                                        jnp.concatenate([router_hi, router_lo], axis=1), tri)

    n_sub = d // LANES
    n_rows = 2 * t + N_EXPERTS * MOE_TILE
    n_tiles = n_rows // MOE_TILE
    expert = route[0:2].astype(jnp.int32)
    rank = route[2:4].astype(jnp.int32)
    gates = route[4:6].T
    counts = cnt[:N_EXPERTS, 0].astype(jnp.int32)
    padded = (counts + MOE_TILE - 1) // MOE_TILE * MOE_TILE
    seg_end = jnp.cumsum(padded)
    seg_start = seg_end - padded
    dest = (jnp.sum(jnp.where(expert[..., None] == jnp.arange(N_EXPERTS), seg_start, 0), axis=-1)
            + rank).reshape(-1)
    tile_row = jnp.arange(n_tiles, dtype=jnp.int32) * MOE_TILE
    tile_valid = (tile_row < seg_end[-1]).astype(jnp.int32)
    n_valid = seg_end[-1] // MOE_TILE
    tile_src = jnp.minimum(jnp.arange(n_tiles, dtype=jnp.int32), jnp.maximum(n_valid - 1, 0))
    tile_expert = jnp.minimum(jnp.sum(tile_src[:, None] * MOE_TILE >= seg_end[None, :], axis=1),
                              N_EXPERTS - 1).astype(jnp.int32)
    token_end = jnp.sum(jnp.where(tile_expert[:, None] == jnp.arange(N_EXPERTS), seg_start + counts, 0), axis=1)
    tile_rows = (tile_valid * jnp.clip(token_end - tile_row, 0, MOE_TILE)).astype(jnp.int32)
    zflag = jnp.concatenate([(padded > 0).astype(jnp.int32), 1 - tile_valid[-N_EXPERTS:]])
    zpos = jnp.concatenate([jnp.maximum(seg_end - MOE_TILE, 0), tile_row[-N_EXPERTS:]]).astype(jnp.int32)

    xs = _dispatch(h3_rows.reshape(t, n_sub, LANES), dest, zpos, zflag, n_rows)
    ys = _moe_experts(xs.reshape(n_rows * n_sub, LANES), w_gu_bf.reshape(n_exp, d, two_f),
                      w_down_bf.reshape(n_exp, two_f // 2, d), tile_expert, tile_rows, tile_src, d)
    out = _combine(x3, ys, dest, gates, row(final_norm))
    return out.reshape(batch, seq, d)
```

```python
import functools
import math

import jax
import jax.numpy as jnp
import numpy as np
from jax import lax
from jax.experimental import pallas as pl
from jax.experimental.pallas import tpu as pltpu

F32 = jnp.float32
BF16 = jnp.bfloat16

EPS = 1e-6
CHUNK = 64
RET_QK = 256
RET_V = 512
RET_HEADS = 4
RET_THETA = 10000.0
DIFF_HD = 64
DIFF_HEADS = 8
ROPE_THETA = 500000.0
ROPE_DIM = 16
N_EXPERTS = 8

LANES = 128
SUBLANES = 8
MXU_COLS = 256
VMEM_LIMIT = 56 * 1024 * 1024

RET_CHUNK = 256
TOK_TILE = 512
LIGHT_TILE = 1024
DISPATCH_TILE = 1024
ATT_TQ = 512
ATT_TK = 512
ATT_HEADS = 4
ATT_ONES = 16
MOE_TILE = 1024
MOE_SPLIT = 2
MOE_FC = 1792


def _cparams(sem, vmem=VMEM_LIMIT):
    return pltpu.CompilerParams(dimension_semantics=sem, vmem_limit_bytes=vmem)


def _const_spec(shape):
    nd = len(shape)
    return pl.BlockSpec(shape, lambda *_: (0,) * nd, pipeline_mode=pl.Buffered(1))


def _slab_specs(weights, n_steps, step_of):
    pack = 2 * SUBLANES
    specs = []
    for w in weights:
        rows = w.shape[0]
        per = -(-rows // n_steps)
        br = next(b for b in range(-(-per // pack) * pack, rows + 1, pack) if rows % b == 0)
        last = rows // br - 1
        specs.append(pl.BlockSpec((br, w.shape[1]),
                                  lambda *g, last=last: (jnp.minimum(step_of(*g), last), 0)))
    return specs


def _round_slabs(in_refs, out_refs):
    for src, dst in zip(in_refs, out_refs):
        dst[...] = src[...].astype(BF16)


def _rms(x, g):
    return x * lax.rsqrt(jnp.mean(x * x, axis=-1, keepdims=True) + EPS) * g


def _dot(a, b):
    return jnp.dot(a, b, preferred_element_type=F32)


def _ret_in_kernel(x_ref, g_ref, w_ref, cos_ref, sin_ref, *refs):
    n_ride = (len(refs) - 1) // 2
    o_ref = refs[n_ride]
    _round_slabs(refs[:n_ride], refs[n_ride + 1:])
    h = _rms(x_ref[...], g_ref[...]).astype(BF16)
    cos = cos_ref[...]
    sin = sin_ref[...]
    d_qk = RET_HEADS * RET_QK
    half = RET_QK // 2
    for c in range(2 * RET_HEADS):
        c0 = c * RET_QK
        acc = _dot(h, w_ref[:, c0:c0 + RET_QK])
        x1 = acc[:, :half]
        x2 = acc[:, half:]
        scale = 1.0 if c < RET_HEADS else RET_QK ** -0.5
        o_ref[:, c0:c0 + half] = ((x1 * cos - x2 * sin) * scale).astype(BF16)
        o_ref[:, c0 + half:c0 + RET_QK] = ((x2 * cos + x1 * sin) * scale).astype(BF16)
    n_rest = (w_ref.shape[1] - 2 * d_qk) // RET_V
    for c in range(n_rest):
        c0 = 2 * d_qk + c * RET_V
        o_ref[:, c0:c0 + RET_V] = _dot(h, w_ref[:, c0:c0 + RET_V]).astype(BF16)


def _ret_in_proj(x2d, gain, w, cos, sin, seq, ride):
    t, d = x2d.shape
    n = w.shape[1]
    tm = TOK_TILE
    n_pos = seq // tm
    slabs = _slab_specs(ride, t // tm, lambda i: i)
    return pl.pallas_call(
        _ret_in_kernel,
        out_shape=[jax.ShapeDtypeStruct((t, n), BF16)] + [jax.ShapeDtypeStruct(r.shape, BF16) for r in ride],
        grid=(t // tm,),
        in_specs=[
            pl.BlockSpec((tm, d), lambda i: (i, 0)),
            _const_spec((1, d)),
            _const_spec((d, n)),
            pl.BlockSpec((tm, RET_QK // 2), lambda i: (i % n_pos, 0)),
            pl.BlockSpec((tm, RET_QK // 2), lambda i: (i % n_pos, 0)),
        ] + slabs,
        out_specs=[pl.BlockSpec((tm, n), lambda i: (i, 0))] + slabs,
        compiler_params=_cparams(("arbitrary",)),
        name="ret_in_proj",
    )(x2d, gain, w, cos, sin, *ride)


def _retention_kernel(q_ref, k_ref, v_ref, g_ref, dmat_ref, qd_ref, kd_ref, cd_ref, o_ref, state_ref):
    @pl.when(pl.program_id(1) == 0)
    def _():
        state_ref[...] = jnp.zeros_like(state_ref)

    n_chunks = q_ref.shape[0] // RET_CHUNK
    for h in range(RET_HEADS):
        for c in range(n_chunks):
            rows = slice(c * RET_CHUNK, (c + 1) * RET_CHUNK)
            q = q_ref[rows, h * RET_QK:(h + 1) * RET_QK]
            k = k_ref[rows, h * RET_QK:(h + 1) * RET_QK]
            v = v_ref[rows, h * RET_V:(h + 1) * RET_V]
            state = state_ref[h]
            s = lax.dot_general(q, k, (((1,), (1,)), ((), ())), preferred_element_type=F32)
            s = s * dmat_ref[h]
            o = _dot(s.astype(BF16), v)
            qs = (q.astype(F32) * qd_ref[h]).astype(BF16)
            o = o + _dot(qs, state.astype(BF16))
            ks = (k.astype(F32) * kd_ref[h]).astype(BF16)
            state_ref[h] = state * cd_ref[h] + lax.dot_general(
                ks, v, (((0,), (0,)), ((), ())), preferred_element_type=F32)
            mu = jnp.mean(o, axis=-1, keepdims=True)
            oc = o - mu
            var = jnp.mean(oc * oc, axis=-1, keepdims=True)
            on = oc * lax.rsqrt(var + EPS)
            gate = g_ref[rows, h * RET_V:(h + 1) * RET_V].astype(F32)
            o_ref[rows, h * RET_V:(h + 1) * RET_V] = (on * (gate * jax.nn.sigmoid(gate))).astype(BF16)


def _retention(proj, dmat, qd, kd, cd, batch, seq):
    t = proj.shape[0]
    tb = TOK_TILE
    nj = seq // tb
    d_qk = RET_HEADS * RET_QK
    d_v = RET_HEADS * RET_V
    row = lambda b, j: b * nj + j
    return pl.pallas_call(
        _retention_kernel,
        out_shape=jax.ShapeDtypeStruct((t, d_v), BF16),
        grid=(batch, nj),
        in_specs=[
            pl.BlockSpec((tb, d_qk), lambda b, j: (row(b, j), 0)),
            pl.BlockSpec((tb, d_qk), lambda b, j: (row(b, j), 1)),
            pl.BlockSpec((tb, d_v), lambda b, j: (row(b, j), 1)),
            pl.BlockSpec((tb, d_v), lambda b, j: (row(b, j), 2)),
            _const_spec(dmat.shape),
            _const_spec(qd.shape),
            _const_spec(kd.shape),
            _const_spec(cd.shape),
        ],
        out_specs=pl.BlockSpec((tb, d_v), lambda b, j: (row(b, j), 0)),
        scratch_shapes=[pltpu.VMEM((RET_HEADS, RET_QK, RET_V), F32)],
        compiler_params=_cparams(("parallel", "arbitrary")),
        name="retention",
    )(proj, proj, proj, proj, dmat, qd, kd, cd)


def _ret_out_ffn_kernel(x_ref, o_ref, wo_ref, g_ref, wgu_ref, wd_ref, out_ref, act_ref, *, fc):
    x1 = x_ref[...] + _dot(o_ref[...], wo_ref[...])
    h = _rms(x1, g_ref[...]).astype(BF16)
    f = wd_ref.shape[0]
    for c in range(f // fc):
        gt = _dot(h, wgu_ref[:, c * fc:(c + 1) * fc])
        up = _dot(h, wgu_ref[:, f + c * fc:f + (c + 1) * fc])
        act_ref[:, c * fc:(c + 1) * fc] = (gt * jax.nn.sigmoid(gt) * up).astype(BF16)
    out_ref[...] = x1 + _dot(act_ref[...], wd_ref[...])


def _ret_out_ffn(x2d, o, w_o, gain, w_gu, w_down):
    t, d = x2d.shape
    tm = TOK_TILE
    f = w_down.shape[0]
    return pl.pallas_call(
        functools.partial(_ret_out_ffn_kernel, fc=256),
        out_shape=jax.ShapeDtypeStruct((t, d), F32),
        grid=(t // tm,),
        in_specs=[
            pl.BlockSpec((tm, d), lambda i: (i, 0)),
            pl.BlockSpec((tm, o.shape[1]), lambda i: (i, 0)),
            _const_spec(w_o.shape),
            _const_spec((1, d)),
            _const_spec(w_gu.shape),
            _const_spec(w_down.shape),
        ],
        out_specs=pl.BlockSpec((tm, d), lambda i: (i, 0)),
        scratch_shapes=[pltpu.VMEM((tm, f), BF16)],
        compiler_params=_cparams(("parallel",)),
        name="ret_out_ffn",
    )(x2d, o, w_o, gain, w_gu, w_down)


def _rope16(x, ctab, s1tab, s2tab):
    half = ROPE_DIM // 2
    return (x * ctab + pltpu.roll(x, LANES - half, 1) * s1tab + pltpu.roll(x, half, 1) * s2tab)


def _rope16_t(x, ctab, s1tab, s2tab):
    half = ROPE_DIM // 2
    return (x * ctab + pltpu.roll(x, x.shape[0] - half, 0) * s1tab + pltpu.roll(x, half, 0) * s2tab)


def _qkv_kernel(x_ref, gq_ref, gkv_ref, wqt_ref, wk_ref, wvt_ref, c_ref, s1_ref, s2_ref,
                ct_ref, s1t_ref, s2t_ref, qt_ref, k_ref, vt_ref, qacc_ref):
    x = x_ref[...]
    xn = x * lax.rsqrt(jnp.mean(x * x, axis=-1, keepdims=True) + EPS)
    hq = (xn * gq_ref[...]).astype(BF16)
    hkv = (xn * gkv_ref[...]).astype(BF16)
    ctab, s1tab, s2tab = c_ref[...], s1_ref[...], s2_ref[...]
    d = k_ref.shape[1]
    dv = 2 * DIFF_HD
    for c in range(d // MXU_COLS):
        c0 = c * MXU_COLS
        ka = _dot(hkv, wk_ref[:, c0:c0 + MXU_COLS])
        for l0 in range(0, MXU_COLS, LANES):
            cols = slice(c0 + l0, c0 + l0 + LANES)
            k_ref[:, cols] = _rope16(ka[:, l0:l0 + LANES], ctab, s1tab, s2tab).astype(BF16)
    nt = (((1,), (1,)), ((), ()))
    vt_ref[...] = lax.dot_general(wvt_ref[...], hkv, nt, preferred_element_type=F32).astype(BF16)
    qacc_ref[...] = lax.dot_general(wqt_ref[...], hq, nt, preferred_element_type=F32)
    q_scale = DIFF_HD ** -0.5 * math.log2(math.e)
    ctab_t, s1tab_t, s2tab_t = ct_ref[...], s1t_ref[...], s2t_ref[...]
    for h in range(d // dv):
        rows = slice(h * dv, (h + 1) * dv)
        qt_ref[rows, :] = (_rope16_t(qacc_ref[rows, :], ctab_t, s1tab_t, s2tab_t) * q_scale).astype(BF16)


def _qkv_proj(x2d, gq, gkv, w_qt, w_k, w_vt, tabs, tabs_t, batch, seq):
    t, d = x2d.shape
    tm = LIGHT_TILE
    n_pos = seq // tm
    tok = pl.BlockSpec((tm, d), lambda i: (i, 0))
    tab = pl.BlockSpec((tm, LANES), lambda i: (i % n_pos, 0))
    tab_t = pl.BlockSpec((LANES, tm), lambda i: (0, i % n_pos))
    chan = pl.BlockSpec((d, tm), lambda i: (i // n_pos, i % n_pos))
    return pl.pallas_call(
        _qkv_kernel,
        out_shape=[jax.ShapeDtypeStruct((batch * d, seq), BF16), jax.ShapeDtypeStruct((t, d), BF16),
                   jax.ShapeDtypeStruct((batch * d, seq), BF16)],
        grid=(t // tm,),
        in_specs=[tok, _const_spec((1, d)), _const_spec((1, d)), _const_spec(w_qt.shape),
                  _const_spec(w_k.shape), _const_spec(w_vt.shape), tab, tab, tab, tab_t, tab_t, tab_t],
        out_specs=[chan, tok, chan],
        scratch_shapes=[pltpu.VMEM((d, tm), F32)],
        compiler_params=_cparams(("parallel",)),
        name="qkv_proj",
    )(x2d, gq, gkv, w_qt, w_k, w_vt, *tabs, *tabs_t)


def _attn_kernel(lam_ref, bias_ref, q_ref, qn_ref, k_ref, vt_ref, sub_ref, wa_ref, wb_ref,
                 o_ref, wa_out_ref, wb_out_ref, vta_ref, s_ref, mt_ref, acc_ref, *, lambda_init):
    i = pl.program_id(2)
    tq = q_ref.shape[1]
    dv = 2 * DIFF_HD
    n_heads, n_kv, _, tk = vta_ref.shape
    chains = [(h, c) for h in range(n_heads) for c in range(2)]

    _round_slabs((wa_ref, wb_ref), (wa_out_ref, wb_out_ref))

    @pl.when(i == 0)
    def _():
        for h in range(n_heads):
            for j in range(n_kv):
                vta_ref[h, j, 0:dv, :] = vt_ref[h * dv:(h + 1) * dv, j * tk:(j + 1) * tk]
                vta_ref[h, j, dv:, :] = jnp.ones((ATT_ONES, tk), BF16)

    chan = lax.broadcasted_iota(jnp.int32, (dv, tq), 0)

    def split_heads(ref):
        out = []
        for h in range(n_heads):
            q = ref[h * dv:(h + 1) * dv, :]
            zero = jnp.zeros_like(q)
            out.append((jnp.where(chan < DIFF_HD, q, zero), jnp.where(chan >= DIFF_HD, q, zero)))
        return out

    def first_tiles(qi):
        n = (qi * tq) // tk
        return n, 1 + (qi * tq - n * tk) // tq

    n_full, tail_bias = first_tiles(i)

    def tile_max(s):
        m8 = jnp.max(s.reshape(tk // SUBLANES, SUBLANES, tq), axis=0)
        return jnp.max(m8, axis=0, keepdims=True)

    def score_chain(j, x, qsplit, n_vis, tail):
        h, c = chains[x]
        bias = bias_ref[jnp.where(j == n_vis, tail, 0)]
        kt = k_ref[pl.ds(pl.multiple_of(j * tk, tk), tk), h * dv:(h + 1) * dv]
        s = _dot(kt, qsplit[h][c]) + bias
        s_ref[x] = s
        return tile_max(s)

    @pl.when(i == 0)
    def _():
        qc0 = split_heads(q_ref)
        for x in range(len(chains)):
            mt_ref[x] = score_chain(0, x, qc0, n_full, tail_bias)

    qc = split_heads(q_ref)
    acc_ref[...] = jnp.zeros_like(acc_ref)
    m_init = tuple(jnp.full((1, tq), -jnp.inf, F32) for _ in chains)

    def step(j, carry, next_scores):
        m_run, m_tile = carry
        m_out, m_next = [], []
        for x, (h, c) in enumerate(chains):
            m_new = jnp.maximum(m_run[x], m_tile[x])
            alpha = jnp.exp2(m_run[x] - m_new)
            p = jnp.exp2(s_ref[x] - m_new).astype(BF16)
            acc_ref[x] = alpha * acc_ref[x] + _dot(vta_ref[h, j], p)
            m_out.append(m_new)
            if next_scores is not None:
                m_next.append(next_scores(x))
        return tuple(m_out), tuple(m_next)

    first = tuple(mt_ref[x] for x in range(len(chains)))
    carry = lax.fori_loop(
        0, n_full,
        lambda j, carry: step(j, carry, lambda x: score_chain(j + 1, x, qc, n_full, tail_bias)),
        (m_init, first))

    @pl.when(i + 1 < pl.num_programs(2))
    def _():
        qn = split_heads(qn_ref)
        n_vis, tail = first_tiles(i + 1)
        _, m_first = step(n_full, carry, lambda x: score_chain(0, x, qn, n_vis, tail))
        for x in range(len(chains)):
            mt_ref[x] = m_first[x]

    @pl.when(i + 1 == pl.num_programs(2))
    def _():
        step(n_full, carry, None)

    lam_v = lam_ref[...]
    lam = (jnp.exp(jnp.sum(lam_v[0:1] * lam_v[1:2], axis=-1, keepdims=True))
           - jnp.exp(jnp.sum(lam_v[2:3] * lam_v[3:4], axis=-1, keepdims=True)) + lambda_init)
    for h in range(n_heads):
        a0 = acc_ref[2 * h]
        a1 = acc_ref[2 * h + 1]
        ot = a0[0:dv] / a0[dv:dv + 1] - lam * (a1[0:dv] / a1[dv:dv + 1])
        o = _rms(ot.T, sub_ref[...]) * (1.0 - lambda_init)
        o_ref[:, h * dv:(h + 1) * dv] = o.astype(BF16)


def _attn_bias(tq, tk):
    key = np.arange(tk)[:, None] // CHUNK
    tiles = [np.zeros((tk, tq))]
    for r in range(tk // tq):
        qry = (r * tq + np.arange(tq))[None, :] // CHUNK
        tiles.append(np.where(key <= qry, 0.0, -1e30))
    return jnp.asarray(np.stack(tiles), F32)


def _diff_attention(qt, k, vt, lam_vecs, subln, w_a, w_b, batch, seq, lambda_init):
    t, d = k.shape
    tq = ATT_TQ
    nq = seq // tq
    dv = 2 * DIFF_HD
    hp = ATT_HEADS
    n_groups = DIFF_HEADS // hp
    n_steps = batch * n_groups * nq
    bias = _attn_bias(tq, ATT_TK)
    slabs = _slab_specs((w_a, w_b), n_steps, lambda b, g, i: (b * n_groups + g) * nq + i)
    return pl.pallas_call(
        functools.partial(_attn_kernel, lambda_init=lambda_init),
        out_shape=[jax.ShapeDtypeStruct((t, d), BF16),
                   jax.ShapeDtypeStruct(w_a.shape, BF16), jax.ShapeDtypeStruct(w_b.shape, BF16)],
        grid=(batch, n_groups, nq),
        in_specs=[
            _const_spec(lam_vecs.shape),
            _const_spec(bias.shape),
            pl.BlockSpec((hp * dv, tq), lambda b, g, i: (b * n_groups + g, i)),
            pl.BlockSpec((hp * dv, tq), lambda b, g, i: (b * n_groups + g, jnp.minimum(i + 1, nq - 1))),
            pl.BlockSpec((seq, hp * dv), lambda b, g, i: (b, g)),
            pl.BlockSpec((hp * dv, seq), lambda b, g, i: (b * n_groups + g, 0)),
            _const_spec((1, dv)),
        ] + slabs,
        out_specs=[pl.BlockSpec((tq, hp * dv), lambda b, g, i: (b * nq + i, g))] + slabs,
        scratch_shapes=[pltpu.VMEM((hp, seq // ATT_TK, dv + ATT_ONES, ATT_TK), BF16),
                        pltpu.VMEM((2 * hp, ATT_TK, tq), F32),
                        pltpu.VMEM((2 * hp, 1, tq), F32),
                        pltpu.VMEM((2 * hp, dv + ATT_ONES, tq), F32)],
        compiler_params=_cparams(("parallel", "parallel", "arbitrary")),
        name="diff_attn",
    )(lam_vecs, bias, qt, qt, k, vt, subln, w_a, w_b)


ROUTE_ROWS = 2 * SUBLANES


def _attn_out_kernel(x_ref, o_ref, wo_ref, g_ref, r_ref, tri_ref, x3_ref, h3_ref, route_ref, cnt_ref, carry_ref):
    @pl.when(pl.program_id(0) == 0)
    def _():
        carry_ref[...] = jnp.zeros_like(carry_ref)

    tm = x_ref.shape[0]
    x3 = x_ref[...] + _dot(o_ref[...], wo_ref[...])
    x3_ref[...] = x3
    h3 = _rms(x3, g_ref[...])
    for s in range(h3.shape[1] // LANES):
        h3_ref[pl.ds(s, tm, stride=SUBLANES), :] = h3[:, s * LANES:(s + 1) * LANES]

    hi = h3.astype(BF16)
    lo = (h3 - hi.astype(F32)).astype(BF16)
    both = _dot(hi, r_ref[...])
    logits = both[:, :LANES] + both[:, LANES:] + _dot(lo, r_ref[:, :LANES])
    lt = logits.T[0:ROUTE_ROWS]
    sub = lax.broadcasted_iota(jnp.int32, lt.shape, 0).astype(F32)
    lt = jnp.where(sub < N_EXPERTS, lt, -jnp.inf)
    v1 = jnp.max(lt, axis=0, keepdims=True)
    i1 = jnp.min(jnp.where(lt == v1, sub, float(ROUTE_ROWS)), axis=0, keepdims=True)
    lt2 = jnp.where(sub == i1, -jnp.inf, lt)
    v2 = jnp.max(lt2, axis=0, keepdims=True)
    i2 = jnp.min(jnp.where(lt2 == v2, sub, float(ROUTE_ROWS)), axis=0, keepdims=True)
    e = jnp.exp(v2 - v1)
    w1 = 1.0 / (1.0 + e)
    w2 = e / (1.0 + e)
    oh1 = sub == i1
    oh2 = sub == i2
    assign = jnp.where(oh1 | oh2, 1.0, 0.0)
    excl = _dot(assign.astype(BF16), tri_ref[...]) + carry_ref[:, 0:1]
    r1 = jnp.sum(jnp.where(oh1, excl, 0.0), axis=0, keepdims=True)
    r2 = jnp.sum(jnp.where(oh2, excl, 0.0), axis=0, keepdims=True)
    route = jnp.zeros_like(lt)
    for row, val in enumerate((i1, i2, r1, r2, w1, w2)):
        route = jnp.where(sub == float(row), val, route)
    route_ref[...] = route[0:SUBLANES]
    total = carry_ref[:, 0:1] + jnp.sum(assign, axis=1, keepdims=True)
    carry_ref[...] = jnp.broadcast_to(total, carry_ref.shape)
    cnt_ref[...] = jnp.broadcast_to(total, cnt_ref.shape)


def _attn_out(x2d, o, w_o, gain, router_split, tri):
    t, d = x2d.shape
    tm = LIGHT_TILE
    n_sub = d // LANES
    return pl.pallas_call(
        _attn_out_kernel,
        out_shape=[jax.ShapeDtypeStruct((t, d), F32),
                   jax.ShapeDtypeStruct((t * n_sub, LANES), F32),
                   jax.ShapeDtypeStruct((SUBLANES, t), F32),
                   jax.ShapeDtypeStruct((ROUTE_ROWS, LANES), F32)],
        grid=(t // tm,),
        in_specs=[pl.BlockSpec((tm, d), lambda i: (i, 0)),
                  pl.BlockSpec((tm, d), lambda i: (i, 0)),
                  _const_spec(w_o.shape), _const_spec((1, d)), _const_spec(router_split.shape),
                  _const_spec(tri.shape)],
        out_specs=[pl.BlockSpec((tm, d), lambda i: (i, 0)),
                   pl.BlockSpec((tm * n_sub, LANES), lambda i: (i, 0)),
                   pl.BlockSpec((SUBLANES, tm), lambda i: (0, i)),
                   _const_spec((ROUTE_ROWS, LANES))],
        scratch_shapes=[pltpu.VMEM((ROUTE_ROWS, LANES), F32)],
        compiler_params=_cparams(("arbitrary",)),
        name="attn_out_route",
    )(x2d, o, w_o, gain, router_split, tri)


def _dispatch_kernel(dest_ref, zpos_ref, zflag_ref, h_ref, dst_ref, zbuf_ref, zsem, sem):
    i = pl.program_id(0)
    tm = h_ref.shape[0]

    @pl.when(i == 0)
    def _():
        zbuf_ref[...] = jnp.zeros_like(zbuf_ref)

        def zcopy(e):
            return pltpu.make_async_copy(zbuf_ref, dst_ref.at[pl.ds(zpos_ref[e], MOE_TILE)], zsem.at[e])

        for e in range(2 * N_EXPERTS):
            @pl.when(zflag_ref[e] == 1)
            def _():
                zcopy(e).start()
        for e in range(2 * N_EXPERTS):
            @pl.when(zflag_ref[e] == 1)
            def _():
                zcopy(e).wait()

    n_tok = dest_ref.shape[0] // 2

    def body(r, carry):
        for k in range(2):
            pltpu.make_async_copy(h_ref.at[r], dst_ref.at[dest_ref[k * n_tok + i * tm + r]],
                                  sem).start(priority=k)
        return carry

    lax.fori_loop(0, tm, body, 0, unroll=8)
    for k in range(2):
        pltpu.make_async_copy(h_ref, dst_ref.at[pl.ds(0, tm)], sem).wait()


def _dispatch(h3_rows, dest, zpos, zflag, n_dst):
    t = h3_rows.shape[0]
    tm = DISPATCH_TILE
    tail = h3_rows.shape[1:]
    return pl.pallas_call(
        _dispatch_kernel,
        out_shape=jax.ShapeDtypeStruct((n_dst,) + tail, F32),
        grid_spec=pltpu.PrefetchScalarGridSpec(
            num_scalar_prefetch=3, grid=(t // tm,),
            in_specs=[pl.BlockSpec((tm,) + tail, lambda i, *_: (i, 0, 0))],
            out_specs=pl.BlockSpec(memory_space=pl.ANY),
            scratch_shapes=[pltpu.VMEM((MOE_TILE,) + tail, F32),
                            pltpu.SemaphoreType.DMA((2 * N_EXPERTS,)), pltpu.SemaphoreType.DMA(())]),
        compiler_params=_cparams(("arbitrary",)),
        name="moe_dispatch",
    )(dest, zpos, zflag, h3_rows)


def _moe_kernel(te_ref, tr_ref, ts_ref, xs_ref, wg_ref, wu_ref, wd_ref, ys_ref, h_ref, acc_ref, *, nfc):
    i = pl.program_id(0)
    c = pl.program_id(1)
    last = nfc - 1
    n_sub = h_ref.shape[1] // LANES
    grp = h_ref.shape[0] // MOE_SPLIT

    def expert_mlp(h):
        gt = _dot(h, wg_ref[...])
        up = _dot(h, wu_ref[...])
        act = (gt * jax.nn.sigmoid(gt) * up).astype(BF16)
        return _dot(act, wd_ref[...])

    for part in range(MOE_SPLIT):
        r0 = part * grp
        rows = slice(r0, r0 + grp)
        live = tr_ref[i] > r0

        def load_rows():
            h = jnp.concatenate([xs_ref[pl.ds(r0 * n_sub + s, grp, stride=n_sub), :].astype(BF16)
                                 for s in range(n_sub)], axis=1)
            h_ref[rows, :] = h
            return h

        def store_rows(y):
            for s in range(n_sub):
                ys_ref[pl.ds(r0 * n_sub + s, grp, stride=n_sub), :] = y[:, s * LANES:(s + 1) * LANES]

        if nfc == 1:
            @pl.when(live)
            def _():
                store_rows(expert_mlp(load_rows()))
        else:
            @pl.when(live & (c == 0))
            def _():
                acc_ref[rows, :] = expert_mlp(load_rows())

            if nfc > 2:
                @pl.when(live & (c > 0) & (c < last))
                def _():
                    acc_ref[rows, :] += expert_mlp(h_ref[rows, :])

            @pl.when(live & (c == last))
            def _():
                store_rows(acc_ref[rows, :] + expert_mlp(h_ref[rows, :]))

        @pl.when(jnp.logical_not(live) & (c == last))
        def _():
            ys_ref[r0 * n_sub:(r0 + grp) * n_sub, :] = jnp.zeros((grp * n_sub, LANES), F32)


def _moe_experts(xs2d, w_gu, w_down, tile_expert, tile_rows, tile_src, d):
    n_sub = d // LANES
    n_tiles = xs2d.shape[0] // (MOE_TILE * n_sub)
    f = w_down.shape[1]
    nfc = f // MOE_FC
    blk = MOE_TILE * n_sub

    def ceff(i, c, tv):
        return jnp.where(tv[i] > 0, c, nfc - 1)

    return pl.pallas_call(
        functools.partial(_moe_kernel, nfc=nfc),
        out_shape=jax.ShapeDtypeStruct(xs2d.shape, F32),
        grid_spec=pltpu.PrefetchScalarGridSpec(
            num_scalar_prefetch=3, grid=(n_tiles, nfc),
            in_specs=[
                pl.BlockSpec((blk, LANES), lambda i, c, te, tv, ts: (ts[i], 0)),
                pl.BlockSpec((None, d, MOE_FC), lambda i, c, te, tv, ts: (te[i], 0, ceff(i, c, tv))),
                pl.BlockSpec((None, d, MOE_FC), lambda i, c, te, tv, ts: (te[i], 0, nfc + ceff(i, c, tv))),
                pl.BlockSpec((None, MOE_FC, d), lambda i, c, te, tv, ts: (te[i], ceff(i, c, tv), 0)),
            ],
            out_specs=pl.BlockSpec((blk, LANES), lambda i, c, te, tv, ts: (i, 0)),
            scratch_shapes=[pltpu.VMEM((MOE_TILE, d), BF16), pltpu.VMEM((MOE_TILE, d), F32)]),
        compiler_params=_cparams(("arbitrary", "arbitrary")),
        name="moe_experts",
    )(tile_expert, tile_rows, tile_src, xs2d, w_gu, w_gu, w_down)


def _combine_kernel(dest_ref, x_ref, ys_ref, gate_ref, g_ref, o_ref, ybuf_ref, sem):
    i = pl.program_id(0)
    n = pl.num_programs(0)
    tm, d = x_ref.shape
    n_sub = d // LANES

    def gather(tile, slot):
        def body(r, carry):
            for k in range(2):
                row = dest_ref[k * (dest_ref.shape[0] // 2) + tile * tm + r]
                src = ys_ref.at[pl.ds(pl.multiple_of(row * n_sub, n_sub), n_sub)]
                dst = ybuf_ref.at[slot, pl.ds(pl.multiple_of((2 * r + k) * n_sub, n_sub), n_sub)]
                pltpu.make_async_copy(src, dst, sem.at[slot]).start(priority=k)
            return carry

        lax.fori_loop(0, tm, body, 0, unroll=8)

    @pl.when(i == 0)
    def _():
        gather(0, 0)

    @pl.when(i + 1 < n)
    def _():
        gather(i + 1, (i + 1) % 2)

    slot = i % 2
    pltpu.make_async_copy(ys_ref.at[pl.ds(0, 2 * tm * n_sub)], ybuf_ref.at[slot], sem.at[slot]).wait()

    y_ref = ybuf_ref.at[slot]
    gates = gate_ref[...]
    w1 = gates[:, 0:1]
    w2 = gates[:, 1:2]
    for s in range(n_sub):
        cols = slice(s * LANES, (s + 1) * LANES)
        y1 = y_ref[pl.ds(s, tm, stride=2 * n_sub), :]
        y2 = y_ref[pl.ds(n_sub + s, tm, stride=2 * n_sub), :]
        o_ref[:, cols] = x_ref[:, cols] + w1 * y1 + w2 * y2
    o_ref[...] = _rms(o_ref[...], g_ref[...])


def _combine(x2d, ys2d, dest, gates, gain):
    t, d = x2d.shape
    tm = TOK_TILE
    n_sub = d // LANES
    return pl.pallas_call(
        _combine_kernel,
        out_shape=jax.ShapeDtypeStruct((t, d), F32),
        grid_spec=pltpu.PrefetchScalarGridSpec(
            num_scalar_prefetch=1, grid=(t // tm,),
            in_specs=[pl.BlockSpec((tm, d), lambda i, *_: (i, 0)),
                      pl.BlockSpec(memory_space=pl.ANY),
                      pl.BlockSpec((tm, gates.shape[1]), lambda i, *_: (i, 0)),
                      pl.BlockSpec((1, d), lambda i, *_: (0, 0), pipeline_mode=pl.Buffered(1))],
            out_specs=pl.BlockSpec((tm, d), lambda i, *_: (i, 0)),
            scratch_shapes=[pltpu.VMEM((2, 2 * tm * n_sub, LANES), F32), pltpu.SemaphoreType.DMA((2,))]),
        compiler_params=_cparams(("arbitrary",)),
        name="moe_combine",
    )(dest, x2d, ys2d, gates, gain)


def _rope_tables(seq, rot_dim, theta):
    inv = 1.0 / (theta ** (np.arange(0, rot_dim, 2, dtype=np.float64) / rot_dim))
    ang = np.arange(seq, dtype=np.float64)[:, None] * inv[None, :]
    return np.cos(ang), np.sin(ang)


def _retention_tables():
    c = RET_CHUNK
    log_gamma = np.log(1.0 - 2.0 ** (-5.0 - np.arange(RET_HEADS, dtype=np.float64)))
    idx = np.arange(c, dtype=np.float64)
    rel = idx[:, None] - idx[None, :]
    dmat = np.where(rel[None] >= 0, np.exp(np.maximum(rel, 0.0)[None] * log_gamma[:, None, None]), 0.0)
    qd = np.exp((idx + 1.0)[None, :] * log_gamma[:, None])[:, :, None]
    kd = np.exp((c - 1.0 - idx)[None, :] * log_gamma[:, None])[:, :, None]
    cd = np.exp(c * log_gamma)[:, None, None]
    return tuple(jnp.asarray(tb, F32) for tb in (dmat, qd, kd, cd))


def _attn_rope_tables(seq):
    cos, sin = _rope_tables(seq, ROPE_DIM, ROPE_THETA)
    half = ROPE_DIM // 2
    pad = DIFF_HD - ROPE_DIM
    ones = np.ones((seq, pad))
    zeros = np.zeros((seq, pad))
    zh = np.zeros((seq, half))
    ctab = np.concatenate([cos, cos, ones], axis=1)
    s1tab = np.concatenate([-sin, zh, zeros], axis=1)
    s2tab = np.concatenate([zh, sin, zeros], axis=1)
    rep = LANES // DIFF_HD
    tabs = [np.tile(tb, (1, rep)) for tb in (ctab, s1tab, s2tab)]
    return (tuple(jnp.asarray(tb, F32) for tb in tabs),
            tuple(jnp.asarray(tb.T, F32) for tb in tabs))


def kernel(x, ln_mix, ln_ffn, ret_w_in, ret_w_o, kv_norm, w_kv, diff_w_q, lam_q1, lam_k1, lam_q2, lam_k2,
           diff_subln, diff_w_o, ffn_w_gu, ffn_w_down, moe_router, moe_w_gu, moe_w_down, final_norm):
    batch, seq, d = x.shape
    t = batch * seq
    assert ln_mix.shape[0] == 2 and ret_w_in.shape[0] == 1 and diff_w_q.shape[0] == 1
    assert seq % TOK_TILE == 0 and TOK_TILE % RET_CHUNK == 0 and ATT_TQ % CHUNK == 0
    assert seq % LIGHT_TILE == 0 and seq % DISPATCH_TILE == 0
    assert seq % ATT_TK == 0 and ATT_TK % ATT_TQ == 0
    x2d = x.reshape(t, d)
    row = lambda g: g.reshape(1, -1)

    cos_r, sin_r = (jnp.asarray(tb, F32) for tb in _rope_tables(seq, RET_QK, RET_THETA))
    proj, w_o_bf, w_gu_d_bf, w_down_d_bf, w_q_bf, w_kv_bf, w_ao_bf = _ret_in_proj(
        x2d, row(ln_mix[0]), ret_w_in[0].astype(BF16), cos_r, sin_r, seq,
        (ret_w_o[0], ffn_w_gu[0], ffn_w_down[0], diff_w_q[0], w_kv, diff_w_o[0]))
    ret_o = _retention(proj, *_retention_tables(), batch, seq)
    x2 = _ret_out_ffn(x2d, ret_o, w_o_bf, row(ln_ffn[0]), w_gu_d_bf, w_down_d_bf)

    lambda_init = 0.8 - 0.6 * math.exp(-0.3 * 1)
    qt, k, vt = _qkv_proj(x2, row(ln_mix[1]), row(kv_norm), w_q_bf.T, w_kv_bf[:, :d], w_kv_bf[:, d:].T,
                          *_attn_rope_tables(seq), batch, seq)
    lam_vecs = jnp.stack([lam_q1[0], lam_k1[0], lam_q2[0], lam_k2[0]]).astype(F32)
    n_exp, _, two_f = moe_w_gu[0].shape
    att, w_gu_bf, w_down_bf = _diff_attention(
        qt, k, vt, lam_vecs, row(diff_subln[0]), moe_w_gu[0].reshape(n_exp * d, two_f),
        moe_w_down[0].reshape(n_exp * (two_f // 2), d), batch, seq, lambda_init)
    router_pad = jnp.pad(moe_router[0], ((0, 0), (0, LANES - N_EXPERTS)))
    router_hi = router_pad.astype(BF16)
    router_lo = (router_pad - router_hi.astype(F32)).astype(BF16)
    tri = jnp.asarray(np.arange(LIGHT_TILE)[:, None] < np.arange(LIGHT_TILE)[None, :], BF16)
    x3, h3_rows, route, cnt = _attn_out(x2, att, w_ao_bf, row(ln_ffn[1]),
                                        jnp.concatenate([router_hi, router_lo], axis=1), tri)

    n_sub = d // LANES
    n_rows = 2 * t + N_EXPERTS * MOE_TILE
    n_tiles = n_rows // MOE_TILE
    expert = route[0:2].astype(jnp.int32)
    rank = route[2:4].astype(jnp.int32)
    gates = route[4:6].T
    counts = cnt[:N_EXPERTS, 0].astype(jnp.int32)
    padded = (counts + MOE_TILE - 1) // MOE_TILE * MOE_TILE
    seg_end = jnp.cumsum(padded)
    seg_start = seg_end - padded
    dest = (jnp.sum(jnp.where(expert[..., None] == jnp.arange(N_EXPERTS), seg_start, 0), axis=-1)
            + rank).reshape(-1)
    tile_row = jnp.arange(n_tiles, dtype=jnp.int32) * MOE_TILE
    tile_valid = (tile_row < seg_end[-1]).astype(jnp.int32)
    n_valid = seg_end[-1] // MOE_TILE
    tile_src = jnp.minimum(jnp.arange(n_tiles, dtype=jnp.int32), jnp.maximum(n_valid - 1, 0))
    tile_expert = jnp.minimum(jnp.sum(tile_src[:, None] * MOE_TILE >= seg_end[None, :], axis=1),
                              N_EXPERTS - 1).astype(jnp.int32)
    token_end = jnp.sum(jnp.where(tile_expert[:, None] == jnp.arange(N_EXPERTS), seg_start + counts, 0), axis=1)
    tile_rows = (tile_valid * jnp.clip(token_end - tile_row, 0, MOE_TILE)).astype(jnp.int32)
    zflag = jnp.concatenate([(padded > 0).astype(jnp.int32), 1 - tile_valid[-N_EXPERTS:]])
    zpos = jnp.concatenate([jnp.maximum(seg_end - MOE_TILE, 0), tile_row[-N_EXPERTS:]]).astype(jnp.int32)

    xs = _dispatch(h3_rows.reshape(t, n_sub, LANES), dest, zpos, zflag, n_rows)
    ys = _moe_experts(xs.reshape(n_rows * n_sub, LANES), w_gu_bf.reshape(n_exp, d, two_f),
                      w_down_bf.reshape(n_exp, two_f // 2, d), tile_expert, tile_rows, tile_src, d)
    out = _combine(x3, ys, dest, gates, row(final_norm))
    return out.reshape(batch, seq, d)
```

```python
import functools
import math

import jax
import jax.numpy as jnp
import numpy as np
from jax import lax
from jax.experimental import pallas as pl
from jax.experimental.pallas import tpu as pltpu

F32 = jnp.float32
BF16 = jnp.bfloat16

EPS = 1e-6
CHUNK = 64
RET_QK = 256
RET_V = 512
RET_HEADS = 4
RET_THETA = 10000.0
DIFF_HD = 64
DIFF_HEADS = 8
ROPE_THETA = 500000.0
ROPE_DIM = 16
N_EXPERTS = 8

LANES = 128
SUBLANES = 8
MXU_COLS = 256
VMEM_LIMIT = 56 * 1024 * 1024

RET_CHUNK = 256
TOK_TILE = 512
LIGHT_TILE = 1024
DISPATCH_TILE = 1024
ATT_TQ = 512
ATT_TK = 512
ATT_HEADS = 4
ATT_ONES = 16
MOE_TILE = 1024
MOE_SPLIT = 2
MOE_FC = 1792


def _cparams(sem, vmem=VMEM_LIMIT):
    return pltpu.CompilerParams(dimension_semantics=sem, vmem_limit_bytes=vmem)


def _const_spec(shape):
    nd = len(shape)
    return pl.BlockSpec(shape, lambda *_: (0,) * nd, pipeline_mode=pl.Buffered(1))


def _slab_specs(weights, n_steps, step_of):
    pack = 2 * SUBLANES
    specs = []
    for w in weights:
        rows = w.shape[0]
        per = -(-rows // n_steps)
        br = next(b for b in range(-(-per // pack) * pack, rows + 1, pack) if rows % b == 0)
        last = rows // br - 1
        specs.append(pl.BlockSpec((br, w.shape[1]),
                                  lambda *g, last=last: (jnp.minimum(step_of(*g), last), 0)))
    return specs


def _round_slabs(in_refs, out_refs):
    for src, dst in zip(in_refs, out_refs):
        dst[...] = src[...].astype(BF16)


def _rms(x, g):
    return x * lax.rsqrt(jnp.mean(x * x, axis=-1, keepdims=True) + EPS) * g


def _dot(a, b):
    return jnp.dot(a, b, preferred_element_type=F32)


def _ret_in_kernel(x_ref, g_ref, w_ref, cos_ref, sin_ref, *refs):
    n_ride = (len(refs) - 1) // 2
    o_ref = refs[n_ride]
    _round_slabs(refs[:n_ride], refs[n_ride + 1:])
    h = _rms(x_ref[...], g_ref[...]).astype(BF16)
    cos = cos_ref[...]
    sin = sin_ref[...]
    d_qk = RET_HEADS * RET_QK
    half = RET_QK // 2
    for c in range(2 * RET_HEADS):
        c0 = c * RET_QK
        acc = _dot(h, w_ref[:, c0:c0 + RET_QK])
        x1 = acc[:, :half]
        x2 = acc[:, half:]
        scale = 1.0 if c < RET_HEADS else RET_QK ** -0.5
        o_ref[:, c0:c0 + half] = ((x1 * cos - x2 * sin) * scale).astype(BF16)
        o_ref[:, c0 + half:c0 + RET_QK] = ((x2 * cos + x1 * sin) * scale).astype(BF16)
    n_rest = (w_ref.shape[1] - 2 * d_qk) // RET_V
    for c in range(n_rest):
        c0 = 2 * d_qk + c * RET_V
        o_ref[:, c0:c0 + RET_V] = _dot(h, w_ref[:, c0:c0 + RET_V]).astype(BF16)


def _ret_in_proj(x2d, gain, w, cos, sin, seq, ride):
    t, d = x2d.shape
    n = w.shape[1]
    tm = TOK_TILE
    n_pos = seq // tm
    slabs = _slab_specs(ride, t // tm, lambda i: i)
    return pl.pallas_call(
        _ret_in_kernel,
        out_shape=[jax.ShapeDtypeStruct((t, n), BF16)] + [jax.ShapeDtypeStruct(r.shape, BF16) for r in ride],
        grid=(t // tm,),
        in_specs=[
            pl.BlockSpec((tm, d), lambda i: (i, 0)),
            _const_spec((1, d)),
            _const_spec((d, n)),
            pl.BlockSpec((tm, RET_QK // 2), lambda i: (i % n_pos, 0)),
            pl.BlockSpec((tm, RET_QK // 2), lambda i: (i % n_pos, 0)),
        ] + slabs,
        out_specs=[pl.BlockSpec((tm, n), lambda i: (i, 0))] + slabs,
        compiler_params=_cparams(("arbitrary",)),
        name="ret_in_proj",
    )(x2d, gain, w, cos, sin, *ride)


def _retention_kernel(q_ref, k_ref, v_ref, g_ref, dmat_ref, qd_ref, kd_ref, cd_ref, o_ref, state_ref):
    @pl.when(pl.program_id(1) == 0)
    def _():
        state_ref[...] = jnp.zeros_like(state_ref)

    n_chunks = q_ref.shape[0] // RET_CHUNK
    for h in range(RET_HEADS):
        for c in range(n_chunks):
            rows = slice(c * RET_CHUNK, (c + 1) * RET_CHUNK)
            q = q_ref[rows, h * RET_QK:(h + 1) * RET_QK]
            k = k_ref[rows, h * RET_QK:(h + 1) * RET_QK]
            v = v_ref[rows, h * RET_V:(h + 1) * RET_V]
            state = state_ref[h]
            s = lax.dot_general(q, k, (((1,), (1,)), ((), ())), preferred_element_type=F32)
            s = s * dmat_ref[h]
            o = _dot(s.astype(BF16), v)
            qs = (q.astype(F32) * qd_ref[h]).astype(BF16)
            o = o + _dot(qs, state.astype(BF16))
            ks = (k.astype(F32) * kd_ref[h]).astype(BF16)
            state_ref[h] = state * cd_ref[h] + lax.dot_general(
                ks, v, (((0,), (0,)), ((), ())), preferred_element_type=F32)
            mu = jnp.mean(o, axis=-1, keepdims=True)
            oc = o - mu
            var = jnp.mean(oc * oc, axis=-1, keepdims=True)
            on = oc * lax.rsqrt(var + EPS)
            gate = g_ref[rows, h * RET_V:(h + 1) * RET_V].astype(F32)
            o_ref[rows, h * RET_V:(h + 1) * RET_V] = (on * (gate * jax.nn.sigmoid(gate))).astype(BF16)


def _retention(proj, dmat, qd, kd, cd, batch, seq):
    t = proj.shape[0]
    tb = TOK_TILE
    nj = seq // tb
    d_qk = RET_HEADS * RET_QK
    d_v = RET_HEADS * RET_V
    row = lambda b, j: b * nj + j
    return pl.pallas_call(
        _retention_kernel,
        out_shape=jax.ShapeDtypeStruct((t, d_v), BF16),
        grid=(batch, nj),
        in_specs=[
            pl.BlockSpec((tb, d_qk), lambda b, j: (row(b, j), 0)),
            pl.BlockSpec((tb, d_qk), lambda b, j: (row(b, j), 1)),
            pl.BlockSpec((tb, d_v), lambda b, j: (row(b, j), 1)),
            pl.BlockSpec((tb, d_v), lambda b, j: (row(b, j), 2)),
            _const_spec(dmat.shape),
            _const_spec(qd.shape),
            _const_spec(kd.shape),
            _const_spec(cd.shape),
        ],
        out_specs=pl.BlockSpec((tb, d_v), lambda b, j: (row(b, j), 0)),
        scratch_shapes=[pltpu.VMEM((RET_HEADS, RET_QK, RET_V), F32)],
        compiler_params=_cparams(("parallel", "arbitrary")),
        name="retention",
    )(proj, proj, proj, proj, dmat, qd, kd, cd)


def _ret_out_ffn_kernel(x_ref, o_ref, wo_ref, g_ref, wgu_ref, wd_ref, out_ref, act_ref, *, fc):
    x1 = x_ref[...] + _dot(o_ref[...], wo_ref[...])
    h = _rms(x1, g_ref[...]).astype(BF16)
    f = wd_ref.shape[0]
    for c in range(f // fc):
        gt = _dot(h, wgu_ref[:, c * fc:(c + 1) * fc])
        up = _dot(h, wgu_ref[:, f + c * fc:f + (c + 1) * fc])
        act_ref[:, c * fc:(c + 1) * fc] = (gt * jax.nn.sigmoid(gt) * up).astype(BF16)
    out_ref[...] = x1 + _dot(act_ref[...], wd_ref[...])


def _ret_out_ffn(x2d, o, w_o, gain, w_gu, w_down):
    t, d = x2d.shape
    tm = TOK_TILE
    f = w_down.shape[0]
    return pl.pallas_call(
        functools.partial(_ret_out_ffn_kernel, fc=256),
        out_shape=jax.ShapeDtypeStruct((t, d), F32),
        grid=(t // tm,),
        in_specs=[
            pl.BlockSpec((tm, d), lambda i: (i, 0)),
            pl.BlockSpec((tm, o.shape[1]), lambda i: (i, 0)),
            _const_spec(w_o.shape),
            _const_spec((1, d)),
            _const_spec(w_gu.shape),
            _const_spec(w_down.shape),
        ],
        out_specs=pl.BlockSpec((tm, d), lambda i: (i, 0)),
        scratch_shapes=[pltpu.VMEM((tm, f), BF16)],
        compiler_params=_cparams(("parallel",)),
        name="ret_out_ffn",
    )(x2d, o, w_o, gain, w_gu, w_down)


def _rope16(x, ctab, s1tab, s2tab):
    half = ROPE_DIM // 2
    return (x * ctab + pltpu.roll(x, LANES - half, 1) * s1tab + pltpu.roll(x, half, 1) * s2tab)


def _rope16_t(x, ctab, s1tab, s2tab):
    half = ROPE_DIM // 2
    return (x * ctab + pltpu.roll(x, x.shape[0] - half, 0) * s1tab + pltpu.roll(x, half, 0) * s2tab)


def _qkv_kernel(x_ref, gq_ref, gkv_ref, wqt_ref, wk_ref, wvt_ref, c_ref, s1_ref, s2_ref,
                ct_ref, s1t_ref, s2t_ref, qt_ref, k_ref, vt_ref, qacc_ref):
    x = x_ref[...]
    xn = x * lax.rsqrt(jnp.mean(x * x, axis=-1, keepdims=True) + EPS)
    hq = (xn * gq_ref[...]).astype(BF16)
    hkv = (xn * gkv_ref[...]).astype(BF16)
    ctab, s1tab, s2tab = c_ref[...], s1_ref[...], s2_ref[...]
    d = k_ref.shape[1]
    dv = 2 * DIFF_HD
    for c in range(d // MXU_COLS):
        c0 = c * MXU_COLS
        ka = _dot(hkv, wk_ref[:, c0:c0 + MXU_COLS])
        for l0 in range(0, MXU_COLS, LANES):
            cols = slice(c0 + l0, c0 + l0 + LANES)
            k_ref[:, cols] = _rope16(ka[:, l0:l0 + LANES], ctab, s1tab, s2tab).astype(BF16)
    nt = (((1,), (1,)), ((), ()))
    vt_ref[...] = lax.dot_general(wvt_ref[...], hkv, nt, preferred_element_type=F32).astype(BF16)
    qacc_ref[...] = lax.dot_general(wqt_ref[...], hq, nt, preferred_element_type=F32)
    q_scale = DIFF_HD ** -0.5 * math.log2(math.e)
    ctab_t, s1tab_t, s2tab_t = ct_ref[...], s1t_ref[...], s2t_ref[...]
    for h in range(d // dv):
        rows = slice(h * dv, (h + 1) * dv)
        qt_ref[rows, :] = (_rope16_t(qacc_ref[rows, :], ctab_t, s1tab_t, s2tab_t) * q_scale).astype(BF16)


def _qkv_proj(x2d, gq, gkv, w_qt, w_k, w_vt, tabs, tabs_t, batch, seq):
    t, d = x2d.shape
    tm = LIGHT_TILE
    n_pos = seq // tm
    tok = pl.BlockSpec((tm, d), lambda i: (i, 0))
    tab = pl.BlockSpec((tm, LANES), lambda i: (i % n_pos, 0))
    tab_t = pl.BlockSpec((LANES, tm), lambda i: (0, i % n_pos))
    chan = pl.BlockSpec((d, tm), lambda i: (i // n_pos, i % n_pos))
    return pl.pallas_call(
        _qkv_kernel,
        out_shape=[jax.ShapeDtypeStruct((batch * d, seq), BF16), jax.ShapeDtypeStruct((t, d), BF16),
                   jax.ShapeDtypeStruct((batch * d, seq), BF16)],
        grid=(t // tm,),
        in_specs=[tok, _const_spec((1, d)), _const_spec((1, d)), _const_spec(w_qt.shape),
                  _const_spec(w_k.shape), _const_spec(w_vt.shape), tab, tab, tab, tab_t, tab_t, tab_t],
        out_specs=[chan, tok, chan],
        scratch_shapes=[pltpu.VMEM((d, tm), F32)],
        compiler_params=_cparams(("parallel",)),
        name="qkv_proj",
    )(x2d, gq, gkv, w_qt, w_k, w_vt, *tabs, *tabs_t)


def _attn_kernel(lam_ref, bias_ref, q_ref, qn_ref, k_ref, vt_ref, sub_ref, wa_ref, wb_ref,
                 o_ref, wa_out_ref, wb_out_ref, vta_ref, s_ref, mt_ref, acc_ref, *, lambda_init):
    i = pl.program_id(2)
    tq = q_ref.shape[1]
    dv = 2 * DIFF_HD
    n_heads, n_kv, _, tk = vta_ref.shape
    chains = [(h, c) for h in range(n_heads) for c in range(2)]

    _round_slabs((wa_ref, wb_ref), (wa_out_ref, wb_out_ref))

    @pl.when(i == 0)
    def _():
        for h in range(n_heads):
            for j in range(n_kv):
                vta_ref[h, j, 0:dv, :] = vt_ref[h * dv:(h + 1) * dv, j * tk:(j + 1) * tk]
                vta_ref[h, j, dv:, :] = jnp.ones((ATT_ONES, tk), BF16)

    chan = lax.broadcasted_iota(jnp.int32, (dv, tq), 0)

    def split_heads(ref):
        out = []
        for h in range(n_heads):
            q = ref[h * dv:(h + 1) * dv, :]
            zero = jnp.zeros_like(q)
            out.append((jnp.where(chan < DIFF_HD, q, zero), jnp.where(chan >= DIFF_HD, q, zero)))
        return out

    def first_tiles(qi):
        n = (qi * tq) // tk
        return n, 1 + (qi * tq - n * tk) // tq

    n_full, tail_bias = first_tiles(i)

    def tile_max(s):
        m8 = jnp.max(s.reshape(tk // SUBLANES, SUBLANES, tq), axis=0)
        return jnp.max(m8, axis=0, keepdims=True)

    def score_chain(j, x, qsplit, n_vis, tail):
        h, c = chains[x]
        bias = bias_ref[jnp.where(j == n_vis, tail, 0)]
        kt = k_ref[pl.ds(pl.multiple_of(j * tk, tk), tk), h * dv:(h + 1) * dv]
        s = _dot(kt, qsplit[h][c]) + bias
        s_ref[x] = s
        return tile_max(s)

    @pl.when(i == 0)
    def _():
        qc0 = split_heads(q_ref)
        for x in range(len(chains)):
            mt_ref[x] = score_chain(0, x, qc0, n_full, tail_bias)

    qc = split_heads(q_ref)
    acc_ref[...] = jnp.zeros_like(acc_ref)
    m_init = tuple(jnp.full((1, tq), -jnp.inf, F32) for _ in chains)

    def step(j, carry, next_scores, finish_head=None):
        m_run, m_tile = carry
        m_out, m_next, acc = [], [], []
        for x, (h, c) in enumerate(chains):
            m_new = jnp.maximum(m_run[x], m_tile[x])
            alpha = jnp.exp2(m_run[x] - m_new)
            p = jnp.exp2(s_ref[x] - m_new).astype(BF16)
            acc.append(alpha * acc_ref[x] + _dot(vta_ref[h, j], p))
            m_out.append(m_new)
            if finish_head is None:
                acc_ref[x] = acc[x]
            elif c == 1:
                finish_head(h, acc[x - 1], acc[x])
            if next_scores is not None:
                m_next.append(next_scores(x))
        return tuple(m_out), tuple(m_next)

    lam_v = lam_ref[...]
    lam = (jnp.exp(jnp.sum(lam_v[0:1] * lam_v[1:2], axis=-1, keepdims=True))
           - jnp.exp(jnp.sum(lam_v[2:3] * lam_v[3:4], axis=-1, keepdims=True)) + lambda_init)

    def finish_head(h, a0, a1):
        ot = a0[0:dv] / a0[dv:dv + 1] - lam * (a1[0:dv] / a1[dv:dv + 1])
        o = _rms(ot.T, sub_ref[...]) * (1.0 - lambda_init)
        o_ref[:, h * dv:(h + 1) * dv] = o.astype(BF16)

    first = tuple(mt_ref[x] for x in range(len(chains)))
    carry = lax.fori_loop(
        0, n_full,
        lambda j, carry: step(j, carry, lambda x: score_chain(j + 1, x, qc, n_full, tail_bias)),
        (m_init, first))

    @pl.when(i + 1 < pl.num_programs(2))
    def _():
        qn = split_heads(qn_ref)
        n_vis, tail = first_tiles(i + 1)
        _, m_first = step(n_full, carry, lambda x: score_chain(0, x, qn, n_vis, tail), finish_head)
        for x in range(len(chains)):
            mt_ref[x] = m_first[x]

    @pl.when(i + 1 == pl.num_programs(2))
    def _():
        step(n_full, carry, None, finish_head)


def _attn_bias(tq, tk):
    key = np.arange(tk)[:, None] // CHUNK
    tiles = [np.zeros((tk, tq))]
    for r in range(tk // tq):
        qry = (r * tq + np.arange(tq))[None, :] // CHUNK
        tiles.append(np.where(key <= qry, 0.0, -1e30))
    return jnp.asarray(np.stack(tiles), F32)


def _diff_attention(qt, k, vt, lam_vecs, subln, w_a, w_b, batch, seq, lambda_init):
    t, d = k.shape
    tq = ATT_TQ
    nq = seq // tq
    dv = 2 * DIFF_HD
    hp = ATT_HEADS
    n_groups = DIFF_HEADS // hp
    n_steps = batch * n_groups * nq
    bias = _attn_bias(tq, ATT_TK)
    slabs = _slab_specs((w_a, w_b), n_steps, lambda b, g, i: (b * n_groups + g) * nq + i)
    return pl.pallas_call(
        functools.partial(_attn_kernel, lambda_init=lambda_init),
        out_shape=[jax.ShapeDtypeStruct((t, d), BF16),
                   jax.ShapeDtypeStruct(w_a.shape, BF16), jax.ShapeDtypeStruct(w_b.shape, BF16)],
        grid=(batch, n_groups, nq),
        in_specs=[
            _const_spec(lam_vecs.shape),
            _const_spec(bias.shape),
            pl.BlockSpec((hp * dv, tq), lambda b, g, i: (b * n_groups + g, i)),
            pl.BlockSpec((hp * dv, tq), lambda b, g, i: (b * n_groups + g, jnp.minimum(i + 1, nq - 1))),
            pl.BlockSpec((seq, hp * dv), lambda b, g, i: (b, g)),
            pl.BlockSpec((hp * dv, seq), lambda b, g, i: (b * n_groups + g, 0)),
            _const_spec((1, dv)),
        ] + slabs,
        out_specs=[pl.BlockSpec((tq, hp * dv), lambda b, g, i: (b * nq + i, g))] + slabs,
        scratch_shapes=[pltpu.VMEM((hp, seq // ATT_TK, dv + ATT_ONES, ATT_TK), BF16),
                        pltpu.VMEM((2 * hp, ATT_TK, tq), F32),
                        pltpu.VMEM((2 * hp, 1, tq), F32),
                        pltpu.VMEM((2 * hp, dv + ATT_ONES, tq), F32)],
        compiler_params=_cparams(("parallel", "parallel", "arbitrary")),
        name="diff_attn",
    )(lam_vecs, bias, qt, qt, k, vt, subln, w_a, w_b)


ROUTE_ROWS = 2 * SUBLANES


def _attn_out_kernel(x_ref, o_ref, wo_ref, g_ref, r_ref, tri_ref, x3_ref, h3_ref, route_ref, cnt_ref, carry_ref):
    @pl.when(pl.program_id(0) == 0)
    def _():
        carry_ref[...] = jnp.zeros_like(carry_ref)

    tm = x_ref.shape[0]
    x3 = x_ref[...] + _dot(o_ref[...], wo_ref[...])
    x3_ref[...] = x3
    h3 = _rms(x3, g_ref[...])
    for s in range(h3.shape[1] // LANES):
        h3_ref[pl.ds(s, tm, stride=SUBLANES), :] = h3[:, s * LANES:(s + 1) * LANES]

    hi = h3.astype(BF16)
    lo = (h3 - hi.astype(F32)).astype(BF16)
    both = _dot(hi, r_ref[...])
    logits = both[:, :LANES] + both[:, LANES:] + _dot(lo, r_ref[:, :LANES])
    lt = logits.T[0:ROUTE_ROWS]
    sub = lax.broadcasted_iota(jnp.int32, lt.shape, 0).astype(F32)
    lt = jnp.where(sub < N_EXPERTS, lt, -jnp.inf)
    v1 = jnp.max(lt, axis=0, keepdims=True)
    i1 = jnp.min(jnp.where(lt == v1, sub, float(ROUTE_ROWS)), axis=0, keepdims=True)
    lt2 = jnp.where(sub == i1, -jnp.inf, lt)
    v2 = jnp.max(lt2, axis=0, keepdims=True)
    i2 = jnp.min(jnp.where(lt2 == v2, sub, float(ROUTE_ROWS)), axis=0, keepdims=True)
    e = jnp.exp(v2 - v1)
    w1 = 1.0 / (1.0 + e)
    w2 = e / (1.0 + e)
    oh1 = sub == i1
    oh2 = sub == i2
    assign = jnp.where(oh1 | oh2, 1.0, 0.0)
    excl = _dot(assign.astype(BF16), tri_ref[...]) + carry_ref[:, 0:1]
    r1 = jnp.sum(jnp.where(oh1, excl, 0.0), axis=0, keepdims=True)
    r2 = jnp.sum(jnp.where(oh2, excl, 0.0), axis=0, keepdims=True)
    route = jnp.zeros_like(lt)
    for row, val in enumerate((i1, i2, r1, r2, w1, w2)):
        route = jnp.where(sub == float(row), val, route)
    route_ref[...] = route[0:SUBLANES]
    total = carry_ref[:, 0:1] + jnp.sum(assign, axis=1, keepdims=True)
    carry_ref[...] = jnp.broadcast_to(total, carry_ref.shape)
    cnt_ref[...] = jnp.broadcast_to(total, cnt_ref.shape)


def _attn_out(x2d, o, w_o, gain, router_split, tri):
    t, d = x2d.shape
    tm = LIGHT_TILE
    n_sub = d // LANES
    return pl.pallas_call(
        _attn_out_kernel,
        out_shape=[jax.ShapeDtypeStruct((t, d), F32),
                   jax.ShapeDtypeStruct((t * n_sub, LANES), F32),
                   jax.ShapeDtypeStruct((SUBLANES, t), F32),
                   jax.ShapeDtypeStruct((ROUTE_ROWS, LANES), F32)],
        grid=(t // tm,),
        in_specs=[pl.BlockSpec((tm, d), lambda i: (i, 0)),
                  pl.BlockSpec((tm, d), lambda i: (i, 0)),
                  _const_spec(w_o.shape), _const_spec((1, d)), _const_spec(router_split.shape),
                  _const_spec(tri.shape)],
        out_specs=[pl.BlockSpec((tm, d), lambda i: (i, 0)),
                   pl.BlockSpec((tm * n_sub, LANES), lambda i: (i, 0)),
                   pl.BlockSpec((SUBLANES, tm), lambda i: (0, i)),
                   _const_spec((ROUTE_ROWS, LANES))],
        scratch_shapes=[pltpu.VMEM((ROUTE_ROWS, LANES), F32)],
        compiler_params=_cparams(("arbitrary",)),
        name="attn_out_route",
    )(x2d, o, w_o, gain, router_split, tri)


def _dispatch_kernel(dest_ref, zpos_ref, zflag_ref, h_ref, dst_ref, zbuf_ref, zsem, sem):
    i = pl.program_id(0)
    tm = h_ref.shape[0]

    @pl.when(i == 0)
    def _():
        zbuf_ref[...] = jnp.zeros_like(zbuf_ref)

        def zcopy(e):
            return pltpu.make_async_copy(zbuf_ref, dst_ref.at[pl.ds(zpos_ref[e], MOE_TILE)], zsem.at[e])

        for e in range(2 * N_EXPERTS):
            @pl.when(zflag_ref[e] == 1)
            def _():
                zcopy(e).start()
        for e in range(2 * N_EXPERTS):
            @pl.when(zflag_ref[e] == 1)
            def _():
                zcopy(e).wait()

    n_tok = dest_ref.shape[0] // 2

    def body(r, carry):
        for k in range(2):
            pltpu.make_async_copy(h_ref.at[r], dst_ref.at[dest_ref[k * n_tok + i * tm + r]],
                                  sem).start(priority=k)
        return carry

    lax.fori_loop(0, tm, body, 0, unroll=8)
    for k in range(2):
        pltpu.make_async_copy(h_ref, dst_ref.at[pl.ds(0, tm)], sem).wait()


def _dispatch(h3_rows, dest, zpos, zflag, n_dst):
    t = h3_rows.shape[0]
    tm = DISPATCH_TILE
    tail = h3_rows.shape[1:]
    return pl.pallas_call(
        _dispatch_kernel,
        out_shape=jax.ShapeDtypeStruct((n_dst,) + tail, F32),
        grid_spec=pltpu.PrefetchScalarGridSpec(
            num_scalar_prefetch=3, grid=(t // tm,),
            in_specs=[pl.BlockSpec((tm,) + tail, lambda i, *_: (i, 0, 0))],
            out_specs=pl.BlockSpec(memory_space=pl.ANY),
            scratch_shapes=[pltpu.VMEM((MOE_TILE,) + tail, F32),
                            pltpu.SemaphoreType.DMA((2 * N_EXPERTS,)), pltpu.SemaphoreType.DMA(())]),
        compiler_params=_cparams(("arbitrary",)),
        name="moe_dispatch",
    )(dest, zpos, zflag, h3_rows)


def _moe_kernel(te_ref, tr_ref, ts_ref, xs_ref, wg_ref, wu_ref, wd_ref, ys_ref, h_ref, acc_ref, *, nfc):
    i = pl.program_id(0)
    c = pl.program_id(1)
    last = nfc - 1
    n_sub = h_ref.shape[1] // LANES
    grp = h_ref.shape[0] // MOE_SPLIT

    def expert_mlp(h):
        gt = _dot(h, wg_ref[...])
        up = _dot(h, wu_ref[...])
        act = (gt * jax.nn.sigmoid(gt) * up).astype(BF16)
        return _dot(act, wd_ref[...])

    for part in range(MOE_SPLIT):
        r0 = part * grp
        rows = slice(r0, r0 + grp)
        live = tr_ref[i] > r0

        def load_rows():
            h = jnp.concatenate([xs_ref[pl.ds(r0 * n_sub + s, grp, stride=n_sub), :].astype(BF16)
                                 for s in range(n_sub)], axis=1)
            h_ref[rows, :] = h
            return h

        def store_rows(y):
            for s in range(n_sub):
                ys_ref[pl.ds(r0 * n_sub + s, grp, stride=n_sub), :] = y[:, s * LANES:(s + 1) * LANES]

        if nfc == 1:
            @pl.when(live)
            def _():
                store_rows(expert_mlp(load_rows()))
        else:
            @pl.when(live & (c == 0))
            def _():
                acc_ref[rows, :] = expert_mlp(load_rows())

            if nfc > 2:
                @pl.when(live & (c > 0) & (c < last))
                def _():
                    acc_ref[rows, :] += expert_mlp(h_ref[rows, :])

            @pl.when(live & (c == last))
            def _():
                store_rows(acc_ref[rows, :] + expert_mlp(h_ref[rows, :]))

        @pl.when(jnp.logical_not(live) & (c == last))
        def _():
            ys_ref[r0 * n_sub:(r0 + grp) * n_sub, :] = jnp.zeros((grp * n_sub, LANES), F32)


def _moe_experts(xs2d, w_gu, w_down, tile_expert, tile_rows, tile_src, d):
    n_sub = d // LANES
    n_tiles = xs2d.shape[0] // (MOE_TILE * n_sub)
    f = w_down.shape[1]
    nfc = f // MOE_FC
    blk = MOE_TILE * n_sub

    def ceff(i, c, tv):
        return jnp.where(tv[i] > 0, c, nfc - 1)

    return pl.pallas_call(
        functools.partial(_moe_kernel, nfc=nfc),
        out_shape=jax.ShapeDtypeStruct(xs2d.shape, F32),
        grid_spec=pltpu.PrefetchScalarGridSpec(
            num_scalar_prefetch=3, grid=(n_tiles, nfc),
            in_specs=[
                pl.BlockSpec((blk, LANES), lambda i, c, te, tv, ts: (ts[i], 0)),
                pl.BlockSpec((None, d, MOE_FC), lambda i, c, te, tv, ts: (te[i], 0, ceff(i, c, tv))),
                pl.BlockSpec((None, d, MOE_FC), lambda i, c, te, tv, ts: (te[i], 0, nfc + ceff(i, c, tv))),
                pl.BlockSpec((None, MOE_FC, d), lambda i, c, te, tv, ts: (te[i], ceff(i, c, tv), 0)),
            ],
            out_specs=pl.BlockSpec((blk, LANES), lambda i, c, te, tv, ts: (i, 0)),
            scratch_shapes=[pltpu.VMEM((MOE_TILE, d), BF16), pltpu.VMEM((MOE_TILE, d), F32)]),
        compiler_params=_cparams(("arbitrary", "arbitrary")),
        name="moe_experts",
    )(tile_expert, tile_rows, tile_src, xs2d, w_gu, w_gu, w_down)


def _combine_kernel(dest_ref, x_ref, ys_ref, gate_ref, g_ref, o_ref, ybuf_ref, sem):
    i = pl.program_id(0)
    n = pl.num_programs(0)
    tm, d = x_ref.shape
    n_sub = d // LANES

    def gather(tile, slot):
        def body(r, carry):
            for k in range(2):
                row = dest_ref[k * (dest_ref.shape[0] // 2) + tile * tm + r]
                src = ys_ref.at[pl.ds(pl.multiple_of(row * n_sub, n_sub), n_sub)]
                dst = ybuf_ref.at[slot, pl.ds(pl.multiple_of((2 * r + k) * n_sub, n_sub), n_sub)]
                pltpu.make_async_copy(src, dst, sem.at[slot]).start(priority=k)
            return carry

        lax.fori_loop(0, tm, body, 0, unroll=8)

    @pl.when(i == 0)
    def _():
        gather(0, 0)

    @pl.when(i + 1 < n)
    def _():
        gather(i + 1, (i + 1) % 2)

    slot = i % 2
    pltpu.make_async_copy(ys_ref.at[pl.ds(0, 2 * tm * n_sub)], ybuf_ref.at[slot], sem.at[slot]).wait()

    y_ref = ybuf_ref.at[slot]
    gates = gate_ref[...]
    w1 = gates[:, 0:1]
    w2 = gates[:, 1:2]
    for s in range(n_sub):
        cols = slice(s * LANES, (s + 1) * LANES)
        y1 = y_ref[pl.ds(s, tm, stride=2 * n_sub), :]
        y2 = y_ref[pl.ds(n_sub + s, tm, stride=2 * n_sub), :]
        o_ref[:, cols] = x_ref[:, cols] + w1 * y1 + w2 * y2
    o_ref[...] = _rms(o_ref[...], g_ref[...])


def _combine(x2d, ys2d, dest, gates, gain):
    t, d = x2d.shape
    tm = TOK_TILE
    n_sub = d // LANES
    return pl.pallas_call(
        _combine_kernel,
        out_shape=jax.ShapeDtypeStruct((t, d), F32),
        grid_spec=pltpu.PrefetchScalarGridSpec(
            num_scalar_prefetch=1, grid=(t // tm,),
            in_specs=[pl.BlockSpec((tm, d), lambda i, *_: (i, 0)),
                      pl.BlockSpec(memory_space=pl.ANY),
                      pl.BlockSpec((tm, gates.shape[1]), lambda i, *_: (i, 0)),
                      pl.BlockSpec((1, d), lambda i, *_: (0, 0), pipeline_mode=pl.Buffered(1))],
            out_specs=pl.BlockSpec((tm, d), lambda i, *_: (i, 0)),
            scratch_shapes=[pltpu.VMEM((2, 2 * tm * n_sub, LANES), F32), pltpu.SemaphoreType.DMA((2,))]),
        compiler_params=_cparams(("arbitrary",)),
        name="moe_combine",
    )(dest, x2d, ys2d, gates, gain)


def _rope_tables(seq, rot_dim, theta):
    inv = 1.0 / (theta ** (np.arange(0, rot_dim, 2, dtype=np.float64) / rot_dim))
    ang = np.arange(seq, dtype=np.float64)[:, None] * inv[None, :]
    return np.cos(ang), np.sin(ang)


def _retention_tables():
    c = RET_CHUNK
    log_gamma = np.log(1.0 - 2.0 ** (-5.0 - np.arange(RET_HEADS, dtype=np.float64)))
    idx = np.arange(c, dtype=np.float64)
    rel = idx[:, None] - idx[None, :]
    dmat = np.where(rel[None] >= 0, np.exp(np.maximum(rel, 0.0)[None] * log_gamma[:, None, None]), 0.0)
    qd = np.exp((idx + 1.0)[None, :] * log_gamma[:, None])[:, :, None]
    kd = np.exp((c - 1.0 - idx)[None, :] * log_gamma[:, None])[:, :, None]
    cd = np.exp(c * log_gamma)[:, None, None]
    return tuple(jnp.asarray(tb, F32) for tb in (dmat, qd, kd, cd))


def _attn_rope_tables(seq):
    cos, sin = _rope_tables(seq, ROPE_DIM, ROPE_THETA)
    half = ROPE_DIM // 2
    pad = DIFF_HD - ROPE_DIM
    ones = np.ones((seq, pad))
    zeros = np.zeros((seq, pad))
    zh = np.zeros((seq, half))
    ctab = np.concatenate([cos, cos, ones], axis=1)
    s1tab = np.concatenate([-sin, zh, zeros], axis=1)
    s2tab = np.concatenate([zh, sin, zeros], axis=1)
    rep = LANES // DIFF_HD
    tabs = [np.tile(tb, (1, rep)) for tb in (ctab, s1tab, s2tab)]
    return (tuple(jnp.asarray(tb, F32) for tb in tabs),
            tuple(jnp.asarray(tb.T, F32) for tb in tabs))


def kernel(x, ln_mix, ln_ffn, ret_w_in, ret_w_o, kv_norm, w_kv, diff_w_q, lam_q1, lam_k1, lam_q2, lam_k2,
           diff_subln, diff_w_o, ffn_w_gu, ffn_w_down, moe_router, moe_w_gu, moe_w_down, final_norm):
    batch, seq, d = x.shape
    t = batch * seq
    assert ln_mix.shape[0] == 2 and ret_w_in.shape[0] == 1 and diff_w_q.shape[0] == 1
    assert seq % TOK_TILE == 0 and TOK_TILE % RET_CHUNK == 0 and ATT_TQ % CHUNK == 0
    assert seq % LIGHT_TILE == 0 and seq % DISPATCH_TILE == 0
    assert seq % ATT_TK == 0 and ATT_TK % ATT_TQ == 0
    x2d = x.reshape(t, d)
    row = lambda g: g.reshape(1, -1)

    cos_r, sin_r = (jnp.asarray(tb, F32) for tb in _rope_tables(seq, RET_QK, RET_THETA))
    proj, w_o_bf, w_gu_d_bf, w_down_d_bf, w_q_bf, w_kv_bf, w_ao_bf = _ret_in_proj(
        x2d, row(ln_mix[0]), ret_w_in[0].astype(BF16), cos_r, sin_r, seq,
        (ret_w_o[0], ffn_w_gu[0], ffn_w_down[0], diff_w_q[0], w_kv, diff_w_o[0]))
    ret_o = _retention(proj, *_retention_tables(), batch, seq)
    x2 = _ret_out_ffn(x2d, ret_o, w_o_bf, row(ln_ffn[0]), w_gu_d_bf, w_down_d_bf)

    lambda_init = 0.8 - 0.6 * math.exp(-0.3 * 1)
    qt, k, vt = _qkv_proj(x2, row(ln_mix[1]), row(kv_norm), w_q_bf.T, w_kv_bf[:, :d], w_kv_bf[:, d:].T,
                          *_attn_rope_tables(seq), batch, seq)
    lam_vecs = jnp.stack([lam_q1[0], lam_k1[0], lam_q2[0], lam_k2[0]]).astype(F32)
    n_exp, _, two_f = moe_w_gu[0].shape
    att, w_gu_bf, w_down_bf = _diff_attention(
        qt, k, vt, lam_vecs, row(diff_subln[0]), moe_w_gu[0].reshape(n_exp * d, two_f),
        moe_w_down[0].reshape(n_exp * (two_f // 2), d), batch, seq, lambda_init)
    router_pad = jnp.pad(moe_router[0], ((0, 0), (0, LANES - N_EXPERTS)))
    router_hi = router_pad.astype(BF16)
    router_lo = (router_pad - router_hi.astype(F32)).astype(BF16)
    tri = jnp.asarray(np.arange(LIGHT_TILE)[:, None] < np.arange(LIGHT_TILE)[None, :], BF16)
    x3, h3_rows, route, cnt = _attn_out(x2, att, w_ao_bf, row(ln_ffn[1]),
                                        jnp.concatenate([router_hi, router_lo], axis=1), tri)

    n_sub = d // LANES
    n_rows = 2 * t + N_EXPERTS * MOE_TILE
    n_tiles = n_rows // MOE_TILE
    expert = route[0:2].astype(jnp.int32)
    rank = route[2:4].astype(jnp.int32)
    gates = route[4:6].T
    counts = cnt[:N_EXPERTS, 0].astype(jnp.int32)
    padded = (counts + MOE_TILE - 1) // MOE_TILE * MOE_TILE
    seg_end = jnp.cumsum(padded)
    seg_start = seg_end - padded
    dest = (jnp.sum(jnp.where(expert[..., None] == jnp.arange(N_EXPERTS), seg_start, 0), axis=-1)
            + rank).reshape(-1)
    tile_row = jnp.arange(n_tiles, dtype=jnp.int32) * MOE_TILE
    tile_valid = (tile_row < seg_end[-1]).astype(jnp.int32)
    n_valid = seg_end[-1] // MOE_TILE
    tile_src = jnp.minimum(jnp.arange(n_tiles, dtype=jnp.int32), jnp.maximum(n_valid - 1, 0))
    tile_expert = jnp.minimum(jnp.sum(tile_src[:, None] * MOE_TILE >= seg_end[None, :], axis=1),
                              N_EXPERTS - 1).astype(jnp.int32)
    token_end = jnp.sum(jnp.where(tile_expert[:, None] == jnp.arange(N_EXPERTS), seg_start + counts, 0), axis=1)
    tile_rows = (tile_valid * jnp.clip(token_end - tile_row, 0, MOE_TILE)).astype(jnp.int32)
    zflag = jnp.concatenate([(padded > 0).astype(jnp.int32), 1 - tile_valid[-N_EXPERTS:]])
    zpos = jnp.concatenate([jnp.maximum(seg_end - MOE_TILE, 0), tile_row[-N_EXPERTS:]]).astype(jnp.int32)

    xs = _dispatch(h3_rows.reshape(t, n_sub, LANES), dest, zpos, zflag, n_rows)
    ys = _moe_experts(xs.reshape(n_rows * n_sub, LANES), w_gu_bf.reshape(n_exp, d, two_f),
                      w_down_bf.reshape(n_exp, two_f // 2, d), tile_expert, tile_rows, tile_src, d)
    out = _combine(x3, ys, dest, gates, row(final_norm))
    return out.reshape(batch, seq, d)
```

```python
import functools
import math

import jax
import jax.numpy as jnp
import numpy as np
from jax import lax
from jax.experimental import pallas as pl
from jax.experimental.pallas import tpu as pltpu

F32 = jnp.float32
BF16 = jnp.bfloat16

EPS = 1e-6
CHUNK = 64
RET_QK = 256
RET_V = 512
RET_HEADS = 4
RET_THETA = 10000.0
DIFF_HD = 64
DIFF_HEADS = 8
ROPE_THETA = 500000.0
ROPE_DIM = 16
N_EXPERTS = 8

LANES = 128
SUBLANES = 8
MXU_COLS = 256
VMEM_LIMIT = 56 * 1024 * 1024

RET_CHUNK = 256
TOK_TILE = 512
LIGHT_TILE = 1024
DISPATCH_TILE = 1024
ATT_TQ = 512
ATT_TK = 512
ATT_HEADS = 4
ATT_ONES = 16
MOE_TILE = 1024
MOE_SPLIT = 2
MOE_FC = 1792


def _cparams(sem, vmem=VMEM_LIMIT):
    return pltpu.CompilerParams(dimension_semantics=sem, vmem_limit_bytes=vmem)


def _const_spec(shape):
    nd = len(shape)
    return pl.BlockSpec(shape, lambda *_: (0,) * nd, pipeline_mode=pl.Buffered(1))


def _slab_specs(weights, n_steps, step_of):
    pack = 2 * SUBLANES
    specs = []
    for w in weights:
        rows = w.shape[0]
        per = -(-rows // n_steps)
        br = next(b for b in range(-(-per // pack) * pack, rows + 1, pack) if rows % b == 0)
        last = rows // br - 1
        specs.append(pl.BlockSpec((br, w.shape[1]),
                                  lambda *g, last=last: (jnp.minimum(step_of(*g), last), 0)))
    return specs


def _round_slabs(in_refs, out_refs):
    for src, dst in zip(in_refs, out_refs):
        dst[...] = src[...].astype(BF16)


def _rms(x, g):
    return x * lax.rsqrt(jnp.mean(x * x, axis=-1, keepdims=True) + EPS) * g


def _dot(a, b):
    return jnp.dot(a, b, preferred_element_type=F32)


def _ret_in_kernel(x_ref, g_ref, w_ref, cos_ref, sin_ref, *refs):
    n_ride = (len(refs) - 1) // 2
    o_ref = refs[n_ride]
    _round_slabs(refs[:n_ride], refs[n_ride + 1:])
    h = _rms(x_ref[...], g_ref[...]).astype(BF16)
    cos = cos_ref[...]
    sin = sin_ref[...]
    d_qk = RET_HEADS * RET_QK
    half = RET_QK // 2
    for c in range(2 * RET_HEADS):
        c0 = c * RET_QK
        acc = _dot(h, w_ref[:, c0:c0 + RET_QK])
        x1 = acc[:, :half]
        x2 = acc[:, half:]
        scale = 1.0 if c < RET_HEADS else RET_QK ** -0.5
        o_ref[:, c0:c0 + half] = ((x1 * cos - x2 * sin) * scale).astype(BF16)
        o_ref[:, c0 + half:c0 + RET_QK] = ((x2 * cos + x1 * sin) * scale).astype(BF16)
    n_rest = (w_ref.shape[1] - 2 * d_qk) // RET_V
    for c in range(n_rest):
        c0 = 2 * d_qk + c * RET_V
        o_ref[:, c0:c0 + RET_V] = _dot(h, w_ref[:, c0:c0 + RET_V]).astype(BF16)


def _ret_in_proj(x2d, gain, w, cos, sin, seq, ride):
    t, d = x2d.shape
    n = w.shape[1]
    tm = TOK_TILE
    n_pos = seq // tm
    slabs = _slab_specs(ride, t // tm, lambda i: i)
    return pl.pallas_call(
        _ret_in_kernel,
        out_shape=[jax.ShapeDtypeStruct((t, n), BF16)] + [jax.ShapeDtypeStruct(r.shape, BF16) for r in ride],
        grid=(t // tm,),
        in_specs=[
            pl.BlockSpec((tm, d), lambda i: (i, 0)),
            _const_spec((1, d)),
            _const_spec((d, n)),
            pl.BlockSpec((tm, RET_QK // 2), lambda i: (i % n_pos, 0)),
            pl.BlockSpec((tm, RET_QK // 2), lambda i: (i % n_pos, 0)),
        ] + slabs,
        out_specs=[pl.BlockSpec((tm, n), lambda i: (i, 0))] + slabs,
        compiler_params=_cparams(("arbitrary",)),
        name="ret_in_proj",
    )(x2d, gain, w, cos, sin, *ride)


def _retention_kernel(q_ref, k_ref, v_ref, g_ref, dmat_ref, qd_ref, kd_ref, cd_ref, o_ref, state_ref):
    @pl.when(pl.program_id(1) == 0)
    def _():
        state_ref[...] = jnp.zeros_like(state_ref)

    n_chunks = q_ref.shape[0] // RET_CHUNK
    for h in range(RET_HEADS):
        for c in range(n_chunks):
            rows = slice(c * RET_CHUNK, (c + 1) * RET_CHUNK)
            q = q_ref[rows, h * RET_QK:(h + 1) * RET_QK]
            k = k_ref[rows, h * RET_QK:(h + 1) * RET_QK]
            v = v_ref[rows, h * RET_V:(h + 1) * RET_V]
            state = state_ref[h]
            s = lax.dot_general(q, k, (((1,), (1,)), ((), ())), preferred_element_type=F32)
            s = s * dmat_ref[h]
            o = _dot(s.astype(BF16), v)
            qs = (q.astype(F32) * qd_ref[h]).astype(BF16)
            o = o + _dot(qs, state.astype(BF16))
            ks = (k.astype(F32) * kd_ref[h]).astype(BF16)
            state_ref[h] = state * cd_ref[h] + lax.dot_general(
                ks, v, (((0,), (0,)), ((), ())), preferred_element_type=F32)
            mu = jnp.mean(o, axis=-1, keepdims=True)
            oc = o - mu
            var = jnp.mean(oc * oc, axis=-1, keepdims=True)
            on = oc * lax.rsqrt(var + EPS)
            gate = g_ref[rows, h * RET_V:(h + 1) * RET_V].astype(F32)
            o_ref[rows, h * RET_V:(h + 1) * RET_V] = (on * (gate * jax.nn.sigmoid(gate))).astype(BF16)


def _retention(proj, dmat, qd, kd, cd, batch, seq):
    t = proj.shape[0]
    tb = TOK_TILE
    nj = seq // tb
    d_qk = RET_HEADS * RET_QK
    d_v = RET_HEADS * RET_V
    row = lambda b, j: b * nj + j
    return pl.pallas_call(
        _retention_kernel,
        out_shape=jax.ShapeDtypeStruct((t, d_v), BF16),
        grid=(batch, nj),
        in_specs=[
            pl.BlockSpec((tb, d_qk), lambda b, j: (row(b, j), 0)),
            pl.BlockSpec((tb, d_qk), lambda b, j: (row(b, j), 1)),
            pl.BlockSpec((tb, d_v), lambda b, j: (row(b, j), 1)),
            pl.BlockSpec((tb, d_v), lambda b, j: (row(b, j), 2)),
            _const_spec(dmat.shape),
            _const_spec(qd.shape),
            _const_spec(kd.shape),
            _const_spec(cd.shape),
        ],
        out_specs=pl.BlockSpec((tb, d_v), lambda b, j: (row(b, j), 0)),
        scratch_shapes=[pltpu.VMEM((RET_HEADS, RET_QK, RET_V), F32)],
        compiler_params=_cparams(("parallel", "arbitrary")),
        name="retention",
    )(proj, proj, proj, proj, dmat, qd, kd, cd)


def _ret_out_ffn_kernel(x_ref, o_ref, wo_ref, g_ref, wgu_ref, wd_ref, out_ref, act_ref, *, fc):
    x1 = x_ref[...] + _dot(o_ref[...], wo_ref[...])
    h = _rms(x1, g_ref[...]).astype(BF16)
    f = wd_ref.shape[0]
    for c in range(f // fc):
        gt = _dot(h, wgu_ref[:, c * fc:(c + 1) * fc])
        up = _dot(h, wgu_ref[:, f + c * fc:f + (c + 1) * fc])
        act_ref[:, c * fc:(c + 1) * fc] = (gt * jax.nn.sigmoid(gt) * up).astype(BF16)
    out_ref[...] = x1 + _dot(act_ref[...], wd_ref[...])


def _ret_out_ffn(x2d, o, w_o, gain, w_gu, w_down):
    t, d = x2d.shape
    tm = TOK_TILE
    f = w_down.shape[0]
    return pl.pallas_call(
        functools.partial(_ret_out_ffn_kernel, fc=256),
        out_shape=jax.ShapeDtypeStruct((t, d), F32),
        grid=(t // tm,),
        in_specs=[
            pl.BlockSpec((tm, d), lambda i: (i, 0)),
            pl.BlockSpec((tm, o.shape[1]), lambda i: (i, 0)),
            _const_spec(w_o.shape),
            _const_spec((1, d)),
            _const_spec(w_gu.shape),
            _const_spec(w_down.shape),
        ],
        out_specs=pl.BlockSpec((tm, d), lambda i: (i, 0)),
        scratch_shapes=[pltpu.VMEM((tm, f), BF16)],
        compiler_params=_cparams(("parallel",)),
        name="ret_out_ffn",
    )(x2d, o, w_o, gain, w_gu, w_down)


def _rope16(x, ctab, s1tab, s2tab):
    half = ROPE_DIM // 2
    return (x * ctab + pltpu.roll(x, LANES - half, 1) * s1tab + pltpu.roll(x, half, 1) * s2tab)


def _rope16_t(x, ctab, s1tab, s2tab):
    half = ROPE_DIM // 2
    return (x * ctab + pltpu.roll(x, x.shape[0] - half, 0) * s1tab + pltpu.roll(x, half, 0) * s2tab)


def _qkv_kernel(x_ref, gq_ref, gkv_ref, wqt_ref, wk_ref, wvt_ref, c_ref, s1_ref, s2_ref,
                ct_ref, s1t_ref, s2t_ref, qt_ref, k_ref, vt_ref, qacc_ref):
    x = x_ref[...]
    xn = x * lax.rsqrt(jnp.mean(x * x, axis=-1, keepdims=True) + EPS)
    hq = (xn * gq_ref[...]).astype(BF16)
    hkv = (xn * gkv_ref[...]).astype(BF16)
    ctab, s1tab, s2tab = c_ref[...], s1_ref[...], s2_ref[...]
    d = k_ref.shape[1]
    dv = 2 * DIFF_HD
    for c in range(d // MXU_COLS):
        c0 = c * MXU_COLS
        ka = _dot(hkv, wk_ref[:, c0:c0 + MXU_COLS])
        for l0 in range(0, MXU_COLS, LANES):
            cols = slice(c0 + l0, c0 + l0 + LANES)
            k_ref[:, cols] = _rope16(ka[:, l0:l0 + LANES], ctab, s1tab, s2tab).astype(BF16)
    nt = (((1,), (1,)), ((), ()))
    vt_ref[...] = lax.dot_general(wvt_ref[...], hkv, nt, preferred_element_type=F32).astype(BF16)
    qacc_ref[...] = lax.dot_general(wqt_ref[...], hq, nt, preferred_element_type=F32)
    q_scale = DIFF_HD ** -0.5 * math.log2(math.e)
    ctab_t, s1tab_t, s2tab_t = ct_ref[...], s1t_ref[...], s2t_ref[...]
    for h in range(d // dv):
        rows = slice(h * dv, (h + 1) * dv)
        qt_ref[rows, :] = (_rope16_t(qacc_ref[rows, :], ctab_t, s1tab_t, s2tab_t) * q_scale).astype(BF16)


def _qkv_proj(x2d, gq, gkv, w_qt, w_k, w_vt, tabs, tabs_t, batch, seq):
    t, d = x2d.shape
    tm = LIGHT_TILE
    n_pos = seq // tm
    tok = pl.BlockSpec((tm, d), lambda i: (i, 0))
    tab = pl.BlockSpec((tm, LANES), lambda i: (i % n_pos, 0))
    tab_t = pl.BlockSpec((LANES, tm), lambda i: (0, i % n_pos))
    chan = pl.BlockSpec((d, tm), lambda i: (i // n_pos, i % n_pos))
    return pl.pallas_call(
        _qkv_kernel,
        out_shape=[jax.ShapeDtypeStruct((batch * d, seq), BF16), jax.ShapeDtypeStruct((t, d), BF16),
                   jax.ShapeDtypeStruct((batch * d, seq), BF16)],
        grid=(t // tm,),
        in_specs=[tok, _const_spec((1, d)), _const_spec((1, d)), _const_spec(w_qt.shape),
                  _const_spec(w_k.shape), _const_spec(w_vt.shape), tab, tab, tab, tab_t, tab_t, tab_t],
        out_specs=[chan, tok, chan],
        scratch_shapes=[pltpu.VMEM((d, tm), F32)],
        compiler_params=_cparams(("parallel",)),
        name="qkv_proj",
    )(x2d, gq, gkv, w_qt, w_k, w_vt, *tabs, *tabs_t)


def _attn_kernel(lam_ref, bias_ref, q_ref, qn_ref, k_ref, vt_ref, sub_ref, wa_ref, wb_ref,
                 o_ref, wa_out_ref, wb_out_ref, vta_ref, s_ref, mt_ref, acc_ref, *, lambda_init):
    i = pl.program_id(2)
    tq = q_ref.shape[1]
    dv = 2 * DIFF_HD
    n_heads, n_kv, _, tk = vta_ref.shape
    chains = [(h, c) for h in range(n_heads) for c in range(2)]

    @pl.when(i == 0)
    def _():
        acc_ref[...] = jnp.zeros_like(acc_ref)
        for h in range(n_heads):
            for j in range(n_kv):
                vta_ref[h, j, 0:dv, :] = vt_ref[h * dv:(h + 1) * dv, j * tk:(j + 1) * tk]
                vta_ref[h, j, dv:, :] = jnp.ones((ATT_ONES, tk), BF16)

    chan = lax.broadcasted_iota(jnp.int32, (dv, tq), 0)

    def split_heads(ref):
        out = []
        for h in range(n_heads):
            q = ref[h * dv:(h + 1) * dv, :]
            zero = jnp.zeros_like(q)
            out.append((jnp.where(chan < DIFF_HD, q, zero), jnp.where(chan >= DIFF_HD, q, zero)))
        return out

    def first_tiles(qi):
        n = (qi * tq) // tk
        return n, 1 + (qi * tq - n * tk) // tq

    n_full, tail_bias = first_tiles(i)

    def tile_max(s):
        m8 = jnp.max(s.reshape(tk // SUBLANES, SUBLANES, tq), axis=0)
        return jnp.max(m8, axis=0, keepdims=True)

    def score_chain(j, x, qsplit, n_vis, tail):
        h, c = chains[x]
        bias = bias_ref[jnp.where(j == n_vis, tail, 0)]
        kt = k_ref[pl.ds(pl.multiple_of(j * tk, tk), tk), h * dv:(h + 1) * dv]
        s = _dot(kt, qsplit[h][c]) + bias
        s_ref[x] = s
        return tile_max(s)

    @pl.when(i == 0)
    def _():
        qc0 = split_heads(q_ref)
        for x in range(len(chains)):
            mt_ref[x] = score_chain(0, x, qc0, n_full, tail_bias)

    qc = split_heads(q_ref)
    m_init = tuple(jnp.full((1, tq), -jnp.inf, F32) for _ in chains)

    def step(j, carry, next_scores, finish_head=None):
        m_run, m_tile = carry
        m_out, m_next, acc = [], [], []
        for x, (h, c) in enumerate(chains):
            m_new = jnp.maximum(m_run[x], m_tile[x])
            alpha = jnp.exp2(m_run[x] - m_new)
            p = jnp.exp2(s_ref[x] - m_new).astype(BF16)
            acc.append(alpha * acc_ref[x] + _dot(vta_ref[h, j], p))
            m_out.append(m_new)
            if finish_head is None:
                acc_ref[x] = acc[x]
            elif c == 1:
                finish_head(h, acc[x - 1], acc[x])
            if next_scores is not None:
                m_next.append(next_scores(x))
        return tuple(m_out), tuple(m_next)

    lam_v = lam_ref[...]
    lam = (jnp.exp(jnp.sum(lam_v[0:1] * lam_v[1:2], axis=-1, keepdims=True))
           - jnp.exp(jnp.sum(lam_v[2:3] * lam_v[3:4], axis=-1, keepdims=True)) + lambda_init)

    def finish_head(h, a0, a1):
        ot = a0[0:dv] / a0[dv:dv + 1] - lam * (a1[0:dv] / a1[dv:dv + 1])
        o = _rms(ot.T, sub_ref[...]) * (1.0 - lambda_init)
        o_ref[:, h * dv:(h + 1) * dv] = o.astype(BF16)
        acc_ref[2 * h:2 * h + 2] = jnp.zeros((2,) + acc_ref.shape[1:], F32)

    first = tuple(mt_ref[x] for x in range(len(chains)))
    carry = lax.fori_loop(
        0, n_full,
        lambda j, carry: step(j, carry, lambda x: score_chain(j + 1, x, qc, n_full, tail_bias)),
        (m_init, first))

    @pl.when(i + 1 < pl.num_programs(2))
    def _():
        _round_slabs((wa_ref, wb_ref), (wa_out_ref, wb_out_ref))
        qn = split_heads(qn_ref)
        n_vis, tail = first_tiles(i + 1)
        _, m_first = step(n_full, carry, lambda x: score_chain(0, x, qn, n_vis, tail), finish_head)
        for x in range(len(chains)):
            mt_ref[x] = m_first[x]

    @pl.when(i + 1 == pl.num_programs(2))
    def _():
        _round_slabs((wa_ref, wb_ref), (wa_out_ref, wb_out_ref))
        step(n_full, carry, None, finish_head)


def _attn_bias(tq, tk):
    key = np.arange(tk)[:, None] // CHUNK
    tiles = [np.zeros((tk, tq))]
    for r in range(tk // tq):
        qry = (r * tq + np.arange(tq))[None, :] // CHUNK
        tiles.append(np.where(key <= qry, 0.0, -1e30))
    return jnp.asarray(np.stack(tiles), F32)


def _diff_attention(qt, k, vt, lam_vecs, subln, w_a, w_b, batch, seq, lambda_init):
    t, d = k.shape
    tq = ATT_TQ
    nq = seq // tq
    dv = 2 * DIFF_HD
    hp = ATT_HEADS
    n_groups = DIFF_HEADS // hp
    n_steps = batch * n_groups * nq
    bias = _attn_bias(tq, ATT_TK)
    slabs = _slab_specs((w_a, w_b), n_steps, lambda b, g, i: (b * n_groups + g) * nq + i)
    return pl.pallas_call(
        functools.partial(_attn_kernel, lambda_init=lambda_init),
        out_shape=[jax.ShapeDtypeStruct((t, d), BF16),
                   jax.ShapeDtypeStruct(w_a.shape, BF16), jax.ShapeDtypeStruct(w_b.shape, BF16)],
        grid=(batch, n_groups, nq),
        in_specs=[
            _const_spec(lam_vecs.shape),
            _const_spec(bias.shape),
            pl.BlockSpec((hp * dv, tq), lambda b, g, i: (b * n_groups + g, i)),
            pl.BlockSpec((hp * dv, tq), lambda b, g, i: (b * n_groups + g, jnp.minimum(i + 1, nq - 1))),
            pl.BlockSpec((seq, hp * dv), lambda b, g, i: (b, g)),
            pl.BlockSpec((hp * dv, seq), lambda b, g, i: (b * n_groups + g, 0)),
            _const_spec((1, dv)),
        ] + slabs,
        out_specs=[pl.BlockSpec((tq, hp * dv), lambda b, g, i: (b * nq + i, g))] + slabs,
        scratch_shapes=[pltpu.VMEM((hp, seq // ATT_TK, dv + ATT_ONES, ATT_TK), BF16),
                        pltpu.VMEM((2 * hp, ATT_TK, tq), F32),
                        pltpu.VMEM((2 * hp, 1, tq), F32),
                        pltpu.VMEM((2 * hp, dv + ATT_ONES, tq), F32)],
        compiler_params=_cparams(("parallel", "parallel", "arbitrary")),
        name="diff_attn",
    )(lam_vecs, bias, qt, qt, k, vt, subln, w_a, w_b)


ROUTE_ROWS = 2 * SUBLANES


def _attn_out_kernel(x_ref, o_ref, wo_ref, g_ref, r_ref, tri_ref, x3_ref, h3_ref, route_ref, cnt_ref, carry_ref):
    @pl.when(pl.program_id(0) == 0)
    def _():
        carry_ref[...] = jnp.zeros_like(carry_ref)

    tm = x_ref.shape[0]
    x3 = x_ref[...] + _dot(o_ref[...], wo_ref[...])
    x3_ref[...] = x3
    h3 = _rms(x3, g_ref[...])
    for s in range(h3.shape[1] // LANES):
        h3_ref[pl.ds(s, tm, stride=SUBLANES), :] = h3[:, s * LANES:(s + 1) * LANES]

    hi = h3.astype(BF16)
    lo = (h3 - hi.astype(F32)).astype(BF16)
    both = _dot(hi, r_ref[...])
    logits = both[:, :LANES] + both[:, LANES:] + _dot(lo, r_ref[:, :LANES])
    lt = logits.T[0:ROUTE_ROWS]
    sub = lax.broadcasted_iota(jnp.int32, lt.shape, 0).astype(F32)
    lt = jnp.where(sub < N_EXPERTS, lt, -jnp.inf)
    v1 = jnp.max(lt, axis=0, keepdims=True)
    i1 = jnp.min(jnp.where(lt == v1, sub, float(ROUTE_ROWS)), axis=0, keepdims=True)
    lt2 = jnp.where(sub == i1, -jnp.inf, lt)
    v2 = jnp.max(lt2, axis=0, keepdims=True)
    i2 = jnp.min(jnp.where(lt2 == v2, sub, float(ROUTE_ROWS)), axis=0, keepdims=True)
    e = jnp.exp(v2 - v1)
    w1 = 1.0 / (1.0 + e)
    w2 = e / (1.0 + e)
    oh1 = sub == i1
    oh2 = sub == i2
    assign = jnp.where(oh1 | oh2, 1.0, 0.0)
    excl = _dot(assign.astype(BF16), tri_ref[...]) + carry_ref[:, 0:1]
    r1 = jnp.sum(jnp.where(oh1, excl, 0.0), axis=0, keepdims=True)
    r2 = jnp.sum(jnp.where(oh2, excl, 0.0), axis=0, keepdims=True)
    route = jnp.zeros_like(lt)
    for row, val in enumerate((i1, i2, r1, r2, w1, w2)):
        route = jnp.where(sub == float(row), val, route)
    route_ref[...] = route[0:SUBLANES]
    total = carry_ref[:, 0:1] + jnp.sum(assign, axis=1, keepdims=True)
    carry_ref[...] = jnp.broadcast_to(total, carry_ref.shape)
    cnt_ref[...] = jnp.broadcast_to(total, cnt_ref.shape)


def _attn_out(x2d, o, w_o, gain, router_split, tri):
    t, d = x2d.shape
    tm = LIGHT_TILE
    n_sub = d // LANES
    return pl.pallas_call(
        _attn_out_kernel,
        out_shape=[jax.ShapeDtypeStruct((t, d), F32),
                   jax.ShapeDtypeStruct((t * n_sub, LANES), F32),
                   jax.ShapeDtypeStruct((SUBLANES, t), F32),
                   jax.ShapeDtypeStruct((ROUTE_ROWS, LANES), F32)],
        grid=(t // tm,),
        in_specs=[pl.BlockSpec((tm, d), lambda i: (i, 0)),
                  pl.BlockSpec((tm, d), lambda i: (i, 0)),
                  _const_spec(w_o.shape), _const_spec((1, d)), _const_spec(router_split.shape),
                  _const_spec(tri.shape)],
        out_specs=[pl.BlockSpec((tm, d), lambda i: (i, 0)),
                   pl.BlockSpec((tm * n_sub, LANES), lambda i: (i, 0)),
                   pl.BlockSpec((SUBLANES, tm), lambda i: (0, i)),
                   _const_spec((ROUTE_ROWS, LANES))],
        scratch_shapes=[pltpu.VMEM((ROUTE_ROWS, LANES), F32)],
        compiler_params=_cparams(("arbitrary",)),
        name="attn_out_route",
    )(x2d, o, w_o, gain, router_split, tri)


def _dispatch_kernel(dest_ref, zpos_ref, zflag_ref, h_ref, dst_ref, zbuf_ref, zsem, sem):
    i = pl.program_id(0)
    tm = h_ref.shape[0]

    @pl.when(i == 0)
    def _():
        zbuf_ref[...] = jnp.zeros_like(zbuf_ref)

        def zcopy(e):
            return pltpu.make_async_copy(zbuf_ref, dst_ref.at[pl.ds(zpos_ref[e], MOE_TILE)], zsem.at[e])

        for e in range(2 * N_EXPERTS):
            @pl.when(zflag_ref[e] == 1)
            def _():
                zcopy(e).start()
        for e in range(2 * N_EXPERTS):
            @pl.when(zflag_ref[e] == 1)
            def _():
                zcopy(e).wait()

    n_tok = dest_ref.shape[0] // 2

    def body(r, carry):
        for k in range(2):
            pltpu.make_async_copy(h_ref.at[r], dst_ref.at[dest_ref[k * n_tok + i * tm + r]],
                                  sem).start(priority=k)
        return carry

    lax.fori_loop(0, tm, body, 0, unroll=8)
    for k in range(2):
        pltpu.make_async_copy(h_ref, dst_ref.at[pl.ds(0, tm)], sem).wait()


def _dispatch(h3_rows, dest, zpos, zflag, n_dst):
    t = h3_rows.shape[0]
    tm = DISPATCH_TILE
    tail = h3_rows.shape[1:]
    return pl.pallas_call(
        _dispatch_kernel,
        out_shape=jax.ShapeDtypeStruct((n_dst,) + tail, F32),
        grid_spec=pltpu.PrefetchScalarGridSpec(
            num_scalar_prefetch=3, grid=(t // tm,),
            in_specs=[pl.BlockSpec((tm,) + tail, lambda i, *_: (i, 0, 0))],
            out_specs=pl.BlockSpec(memory_space=pl.ANY),
            scratch_shapes=[pltpu.VMEM((MOE_TILE,) + tail, F32),
                            pltpu.SemaphoreType.DMA((2 * N_EXPERTS,)), pltpu.SemaphoreType.DMA(())]),
        compiler_params=_cparams(("arbitrary",)),
        name="moe_dispatch",
    )(dest, zpos, zflag, h3_rows)


def _moe_kernel(te_ref, tr_ref, ts_ref, xs_ref, wg_ref, wu_ref, wd_ref, ys_ref, h_ref, acc_ref, *, nfc):
    i = pl.program_id(0)
    c = pl.program_id(1)
    last = nfc - 1
    n_sub = h_ref.shape[1] // LANES
    grp = h_ref.shape[0] // MOE_SPLIT

    def expert_mlp(h):
        gt = _dot(h, wg_ref[...])
        up = _dot(h, wu_ref[...])
        act = (gt * jax.nn.sigmoid(gt) * up).astype(BF16)
        return _dot(act, wd_ref[...])

    for part in range(MOE_SPLIT):
        r0 = part * grp
        rows = slice(r0, r0 + grp)
        live = tr_ref[i] > r0

        def load_rows():
            h = jnp.concatenate([xs_ref[pl.ds(r0 * n_sub + s, grp, stride=n_sub), :].astype(BF16)
                                 for s in range(n_sub)], axis=1)
            h_ref[rows, :] = h
            return h

        def store_rows(y):
            for s in range(n_sub):
                ys_ref[pl.ds(r0 * n_sub + s, grp, stride=n_sub), :] = y[:, s * LANES:(s + 1) * LANES]

        if nfc == 1:
            @pl.when(live)
            def _():
                store_rows(expert_mlp(load_rows()))
        else:
            @pl.when(live & (c == 0))
            def _():
                acc_ref[rows, :] = expert_mlp(load_rows())

            if nfc > 2:
                @pl.when(live & (c > 0) & (c < last))
                def _():
                    acc_ref[rows, :] += expert_mlp(h_ref[rows, :])

            @pl.when(live & (c == last))
            def _():
                store_rows(acc_ref[rows, :] + expert_mlp(h_ref[rows, :]))

        @pl.when(jnp.logical_not(live) & (c == last))
        def _():
            ys_ref[r0 * n_sub:(r0 + grp) * n_sub, :] = jnp.zeros((grp * n_sub, LANES), F32)


def _moe_experts(xs2d, w_gu, w_down, tile_expert, tile_rows, tile_src, d):
    n_sub = d // LANES
    n_tiles = xs2d.shape[0] // (MOE_TILE * n_sub)
    f = w_down.shape[1]
    nfc = f // MOE_FC
    blk = MOE_TILE * n_sub

    def ceff(i, c, tv):
        return jnp.where(tv[i] > 0, c, nfc - 1)

    return pl.pallas_call(
        functools.partial(_moe_kernel, nfc=nfc),
        out_shape=jax.ShapeDtypeStruct(xs2d.shape, F32),
        grid_spec=pltpu.PrefetchScalarGridSpec(
            num_scalar_prefetch=3, grid=(n_tiles, nfc),
            in_specs=[
                pl.BlockSpec((blk, LANES), lambda i, c, te, tv, ts: (ts[i], 0)),
                pl.BlockSpec((None, d, MOE_FC), lambda i, c, te, tv, ts: (te[i], 0, ceff(i, c, tv))),
                pl.BlockSpec((None, d, MOE_FC), lambda i, c, te, tv, ts: (te[i], 0, nfc + ceff(i, c, tv))),
                pl.BlockSpec((None, MOE_FC, d), lambda i, c, te, tv, ts: (te[i], ceff(i, c, tv), 0)),
            ],
            out_specs=pl.BlockSpec((blk, LANES), lambda i, c, te, tv, ts: (i, 0)),
            scratch_shapes=[pltpu.VMEM((MOE_TILE, d), BF16), pltpu.VMEM((MOE_TILE, d), F32)]),
        compiler_params=_cparams(("arbitrary", "arbitrary")),
        name="moe_experts",
    )(tile_expert, tile_rows, tile_src, xs2d, w_gu, w_gu, w_down)


def _combine_kernel(dest_ref, x_ref, ys_ref, gate_ref, g_ref, o_ref, ybuf_ref, sem):
    i = pl.program_id(0)
    n = pl.num_programs(0)
    tm, d = x_ref.shape
    n_sub = d // LANES

    def gather(tile, slot):
        def body(r, carry):
            for k in range(2):
                row = dest_ref[k * (dest_ref.shape[0] // 2) + tile * tm + r]
                src = ys_ref.at[pl.ds(pl.multiple_of(row * n_sub, n_sub), n_sub)]
                dst = ybuf_ref.at[slot, pl.ds(pl.multiple_of((2 * r + k) * n_sub, n_sub), n_sub)]
                pltpu.make_async_copy(src, dst, sem.at[slot]).start(priority=k)
            return carry

        lax.fori_loop(0, tm, body, 0, unroll=8)

    @pl.when(i == 0)
    def _():
        gather(0, 0)

    @pl.when(i + 1 < n)
    def _():
        gather(i + 1, (i + 1) % 2)

    slot = i % 2
    pltpu.make_async_copy(ys_ref.at[pl.ds(0, 2 * tm * n_sub)], ybuf_ref.at[slot], sem.at[slot]).wait()

    y_ref = ybuf_ref.at[slot]
    gates = gate_ref[...]
    w1 = gates[:, 0:1]
    w2 = gates[:, 1:2]
    for s in range(n_sub):
        cols = slice(s * LANES, (s + 1) * LANES)
        y1 = y_ref[pl.ds(s, tm, stride=2 * n_sub), :]
        y2 = y_ref[pl.ds(n_sub + s, tm, stride=2 * n_sub), :]
        o_ref[:, cols] = x_ref[:, cols] + w1 * y1 + w2 * y2
    o_ref[...] = _rms(o_ref[...], g_ref[...])


def _combine(x2d, ys2d, dest, gates, gain):
    t, d = x2d.shape
    tm = TOK_TILE
    n_sub = d // LANES
    return pl.pallas_call(
        _combine_kernel,
        out_shape=jax.ShapeDtypeStruct((t, d), F32),
        grid_spec=pltpu.PrefetchScalarGridSpec(
            num_scalar_prefetch=1, grid=(t // tm,),
            in_specs=[pl.BlockSpec((tm, d), lambda i, *_: (i, 0)),
                      pl.BlockSpec(memory_space=pl.ANY),
                      pl.BlockSpec((tm, gates.shape[1]), lambda i, *_: (i, 0)),
                      pl.BlockSpec((1, d), lambda i, *_: (0, 0), pipeline_mode=pl.Buffered(1))],
            out_specs=pl.BlockSpec((tm, d), lambda i, *_: (i, 0)),
            scratch_shapes=[pltpu.VMEM((2, 2 * tm * n_sub, LANES), F32), pltpu.SemaphoreType.DMA((2,))]),
        compiler_params=_cparams(("arbitrary",)),
        name="moe_combine",
    )(dest, x2d, ys2d, gates, gain)


def _rope_tables(seq, rot_dim, theta):
    inv = 1.0 / (theta ** (np.arange(0, rot_dim, 2, dtype=np.float64) / rot_dim))
    ang = np.arange(seq, dtype=np.float64)[:, None] * inv[None, :]
    return np.cos(ang), np.sin(ang)


def _retention_tables():
    c = RET_CHUNK
    log_gamma = np.log(1.0 - 2.0 ** (-5.0 - np.arange(RET_HEADS, dtype=np.float64)))
    idx = np.arange(c, dtype=np.float64)
    rel = idx[:, None] - idx[None, :]
    dmat = np.where(rel[None] >= 0, np.exp(np.maximum(rel, 0.0)[None] * log_gamma[:, None, None]), 0.0)
    qd = np.exp((idx + 1.0)[None, :] * log_gamma[:, None])[:, :, None]
    kd = np.exp((c - 1.0 - idx)[None, :] * log_gamma[:, None])[:, :, None]
    cd = np.exp(c * log_gamma)[:, None, None]
    return tuple(jnp.asarray(tb, F32) for tb in (dmat, qd, kd, cd))


def _attn_rope_tables(seq):
    cos, sin = _rope_tables(seq, ROPE_DIM, ROPE_THETA)
    half = ROPE_DIM // 2
    pad = DIFF_HD - ROPE_DIM
    ones = np.ones((seq, pad))
    zeros = np.zeros((seq, pad))
    zh = np.zeros((seq, half))
    ctab = np.concatenate([cos, cos, ones], axis=1)
    s1tab = np.concatenate([-sin, zh, zeros], axis=1)
    s2tab = np.concatenate([zh, sin, zeros], axis=1)
    rep = LANES // DIFF_HD
    tabs = [np.tile(tb, (1, rep)) for tb in (ctab, s1tab, s2tab)]
    return (tuple(jnp.asarray(tb, F32) for tb in tabs),
            tuple(jnp.asarray(tb.T, F32) for tb in tabs))


def kernel(x, ln_mix, ln_ffn, ret_w_in, ret_w_o, kv_norm, w_kv, diff_w_q, lam_q1, lam_k1, lam_q2, lam_k2,
           diff_subln, diff_w_o, ffn_w_gu, ffn_w_down, moe_router, moe_w_gu, moe_w_down, final_norm):
    batch, seq, d = x.shape
    t = batch * seq
    assert ln_mix.shape[0] == 2 and ret_w_in.shape[0] == 1 and diff_w_q.shape[0] == 1
    assert seq % TOK_TILE == 0 and TOK_TILE % RET_CHUNK == 0 and ATT_TQ % CHUNK == 0
    assert seq % LIGHT_TILE == 0 and seq % DISPATCH_TILE == 0
    assert seq % ATT_TK == 0 and ATT_TK % ATT_TQ == 0
    x2d = x.reshape(t, d)
    row = lambda g: g.reshape(1, -1)

    cos_r, sin_r = (jnp.asarray(tb, F32) for tb in _rope_tables(seq, RET_QK, RET_THETA))
    proj, w_o_bf, w_gu_d_bf, w_down_d_bf, w_q_bf, w_kv_bf, w_ao_bf = _ret_in_proj(
        x2d, row(ln_mix[0]), ret_w_in[0].astype(BF16), cos_r, sin_r, seq,
        (ret_w_o[0], ffn_w_gu[0], ffn_w_down[0], diff_w_q[0], w_kv, diff_w_o[0]))
    ret_o = _retention(proj, *_retention_tables(), batch, seq)
    x2 = _ret_out_ffn(x2d, ret_o, w_o_bf, row(ln_ffn[0]), w_gu_d_bf, w_down_d_bf)

    lambda_init = 0.8 - 0.6 * math.exp(-0.3 * 1)
    qt, k, vt = _qkv_proj(x2, row(ln_mix[1]), row(kv_norm), w_q_bf.T, w_kv_bf[:, :d], w_kv_bf[:, d:].T,
                          *_attn_rope_tables(seq), batch, seq)
    lam_vecs = jnp.stack([lam_q1[0], lam_k1[0], lam_q2[0], lam_k2[0]]).astype(F32)
    n_exp, _, two_f = moe_w_gu[0].shape
    att, w_gu_bf, w_down_bf = _diff_attention(
        qt, k, vt, lam_vecs, row(diff_subln[0]), moe_w_gu[0].reshape(n_exp * d, two_f),
        moe_w_down[0].reshape(n_exp * (two_f // 2), d), batch, seq, lambda_init)
    router_pad = jnp.pad(moe_router[0], ((0, 0), (0, LANES - N_EXPERTS)))
    router_hi = router_pad.astype(BF16)
    router_lo = (router_pad - router_hi.astype(F32)).astype(BF16)
    tri = jnp.asarray(np.arange(LIGHT_TILE)[:, None] < np.arange(LIGHT_TILE)[None, :], BF16)
    x3, h3_rows, route, cnt = _attn_out(x2, att, w_ao_bf, row(ln_ffn[1]),
                                        jnp.concatenate([router_hi, router_lo], axis=1), tri)

    n_sub = d // LANES
    n_rows = 2 * t + N_EXPERTS * MOE_TILE
    n_tiles = n_rows // MOE_TILE
    expert = route[0:2].astype(jnp.int32)
    rank = route[2:4].astype(jnp.int32)
    gates = route[4:6].T
    counts = cnt[:N_EXPERTS, 0].astype(jnp.int32)
    padded = (counts + MOE_TILE - 1) // MOE_TILE * MOE_TILE
    seg_end = jnp.cumsum(padded)
    seg_start = seg_end - padded
    dest = (jnp.sum(jnp.where(expert[..., None] == jnp.arange(N_EXPERTS), seg_start, 0), axis=-1)
            + rank).reshape(-1)
    tile_row = jnp.arange(n_tiles, dtype=jnp.int32) * MOE_TILE
    tile_valid = (tile_row < seg_end[-1]).astype(jnp.int32)
    n_valid = seg_end[-1] // MOE_TILE
    tile_src = jnp.minimum(jnp.arange(n_tiles, dtype=jnp.int32), jnp.maximum(n_valid - 1, 0))
    tile_expert = jnp.minimum(jnp.sum(tile_src[:, None] * MOE_TILE >= seg_end[None, :], axis=1),
                              N_EXPERTS - 1).astype(jnp.int32)
    token_end = jnp.sum(jnp.where(tile_expert[:, None] == jnp.arange(N_EXPERTS), seg_start + counts, 0), axis=1)
    tile_rows = (tile_valid * jnp.clip(token_end - tile_row, 0, MOE_TILE)).astype(jnp.int32)
    zflag = jnp.concatenate([(padded > 0).astype(jnp.int32), 1 - tile_valid[-N_EXPERTS:]])
    zpos = jnp.concatenate([jnp.maximum(seg_end - MOE_TILE, 0), tile_row[-N_EXPERTS:]]).astype(jnp.int32)

    xs = _dispatch(h3_rows.reshape(t, n_sub, LANES), dest, zpos, zflag, n_rows)
    ys = _moe_experts(xs.reshape(n_rows * n_sub, LANES), w_gu_bf.reshape(n_exp, d, two_f),
                      w_down_bf.reshape(n_exp, two_f // 2, d), tile_expert, tile_rows, tile_src, d)
    out = _combine(x3, ys, dest, gates, row(final_norm))
    return out.reshape(batch, seq, d)
```

```python
import functools
import math

import jax
import jax.numpy as jnp
import numpy as np
from jax import lax
from jax.experimental import pallas as pl
from jax.experimental.pallas import tpu as pltpu

F32 = jnp.float32
BF16 = jnp.bfloat16

EPS = 1e-6
CHUNK = 64
RET_QK = 256
RET_V = 512
RET_HEADS = 4
RET_THETA = 10000.0
DIFF_HD = 64
DIFF_HEADS = 8
ROPE_THETA = 500000.0
ROPE_DIM = 16
N_EXPERTS = 8

LANES = 128
SUBLANES = 8
MXU_COLS = 256
VMEM_LIMIT = 56 * 1024 * 1024

RET_CHUNK = 256
TOK_TILE = 512
LIGHT_TILE = 1024
DISPATCH_TILE = 1024
ATT_TQ = 512
ATT_TK = 512
ATT_HEADS = 4
ATT_ONES = 16
MOE_TILE = 1024
MOE_SPLIT = 2
MOE_FC = 1792


def _cparams(sem, vmem=VMEM_LIMIT):
    return pltpu.CompilerParams(dimension_semantics=sem, vmem_limit_bytes=vmem)


def _const_spec(shape):
    nd = len(shape)
    return pl.BlockSpec(shape, lambda *_: (0,) * nd, pipeline_mode=pl.Buffered(1))


def _slab_specs(weights, n_steps, step_of):
    pack = 2 * SUBLANES
    specs = []
    for w in weights:
        rows = w.shape[0]
        per = -(-rows // n_steps)
        br = next(b for b in range(-(-per // pack) * pack, rows + 1, pack) if rows % b == 0)
        last = rows // br - 1
        specs.append(pl.BlockSpec((br, w.shape[1]),
                                  lambda *g, last=last: (jnp.minimum(step_of(*g), last), 0)))
    return specs


def _round_slabs(in_refs, out_refs):
    for src, dst in zip(in_refs, out_refs):
        dst[...] = src[...].astype(BF16)


def _rms(x, g):
    return x * lax.rsqrt(jnp.mean(x * x, axis=-1, keepdims=True) + EPS) * g


def _dot(a, b):
    return jnp.dot(a, b, preferred_element_type=F32)


def _ret_in_kernel(x_ref, g_ref, w_ref, cos_ref, sin_ref, *refs):
    n_ride = (len(refs) - 1) // 2
    o_ref = refs[n_ride]
    _round_slabs(refs[:n_ride], refs[n_ride + 1:])
    h = _rms(x_ref[...], g_ref[...]).astype(BF16)
    cos = cos_ref[...]
    sin = sin_ref[...]
    d_qk = RET_HEADS * RET_QK
    half = RET_QK // 2
    for c in range(2 * RET_HEADS):
        c0 = c * RET_QK
        acc = _dot(h, w_ref[:, c0:c0 + RET_QK])
        x1 = acc[:, :half]
        x2 = acc[:, half:]
        scale = 1.0 if c < RET_HEADS else RET_QK ** -0.5
        o_ref[:, c0:c0 + half] = ((x1 * cos - x2 * sin) * scale).astype(BF16)
        o_ref[:, c0 + half:c0 + RET_QK] = ((x2 * cos + x1 * sin) * scale).astype(BF16)
    n_rest = (w_ref.shape[1] - 2 * d_qk) // RET_V
    for c in range(n_rest):
        c0 = 2 * d_qk + c * RET_V
        o_ref[:, c0:c0 + RET_V] = _dot(h, w_ref[:, c0:c0 + RET_V]).astype(BF16)


def _ret_in_proj(x2d, gain, w, cos, sin, seq, ride):
    t, d = x2d.shape
    n = w.shape[1]
    tm = TOK_TILE
    n_pos = seq // tm
    slabs = _slab_specs(ride, t // tm, lambda i: i)
    return pl.pallas_call(
        _ret_in_kernel,
        out_shape=[jax.ShapeDtypeStruct((t, n), BF16)] + [jax.ShapeDtypeStruct(r.shape, BF16) for r in ride],
        grid=(t // tm,),
        in_specs=[
            pl.BlockSpec((tm, d), lambda i: (i, 0)),
            _const_spec((1, d)),
            _const_spec((d, n)),
            pl.BlockSpec((tm, RET_QK // 2), lambda i: (i % n_pos, 0)),
            pl.BlockSpec((tm, RET_QK // 2), lambda i: (i % n_pos, 0)),
        ] + slabs,
        out_specs=[pl.BlockSpec((tm, n), lambda i: (i, 0))] + slabs,
        compiler_params=_cparams(("arbitrary",)),
        name="ret_in_proj",
    )(x2d, gain, w, cos, sin, *ride)


def _retention_kernel(q_ref, k_ref, v_ref, g_ref, dmat_ref, qd_ref, kd_ref, cd_ref, o_ref, state_ref):
    @pl.when(pl.program_id(1) == 0)
    def _():
        state_ref[...] = jnp.zeros_like(state_ref)

    n_chunks = q_ref.shape[0] // RET_CHUNK
    for h in range(RET_HEADS):
        for c in range(n_chunks):
            rows = slice(c * RET_CHUNK, (c + 1) * RET_CHUNK)
            q = q_ref[rows, h * RET_QK:(h + 1) * RET_QK]
            k = k_ref[rows, h * RET_QK:(h + 1) * RET_QK]
            v = v_ref[rows, h * RET_V:(h + 1) * RET_V]
            state = state_ref[h]
            s = lax.dot_general(q, k, (((1,), (1,)), ((), ())), preferred_element_type=F32)
            s = s * dmat_ref[h]
            o = _dot(s.astype(BF16), v)
            qs = (q.astype(F32) * qd_ref[h]).astype(BF16)
            o = o + _dot(qs, state.astype(BF16))
            ks = (k.astype(F32) * kd_ref[h]).astype(BF16)
            state_ref[h] = state * cd_ref[h] + lax.dot_general(
                ks, v, (((0,), (0,)), ((), ())), preferred_element_type=F32)
            mu = jnp.mean(o, axis=-1, keepdims=True)
            oc = o - mu
            var = jnp.mean(oc * oc, axis=-1, keepdims=True)
            on = oc * lax.rsqrt(var + EPS)
            gate = g_ref[rows, h * RET_V:(h + 1) * RET_V].astype(F32)
            o_ref[rows, h * RET_V:(h + 1) * RET_V] = (on * (gate * jax.nn.sigmoid(gate))).astype(BF16)


def _retention(proj, dmat, qd, kd, cd, batch, seq):
    t = proj.shape[0]
    tb = TOK_TILE
    nj = seq // tb
    d_qk = RET_HEADS * RET_QK
    d_v = RET_HEADS * RET_V
    row = lambda b, j: b * nj + j
    return pl.pallas_call(
        _retention_kernel,
        out_shape=jax.ShapeDtypeStruct((t, d_v), BF16),
        grid=(batch, nj),
        in_specs=[
            pl.BlockSpec((tb, d_qk), lambda b, j: (row(b, j), 0)),
            pl.BlockSpec((tb, d_qk), lambda b, j: (row(b, j), 1)),
            pl.BlockSpec((tb, d_v), lambda b, j: (row(b, j), 1)),
            pl.BlockSpec((tb, d_v), lambda b, j: (row(b, j), 2)),
            _const_spec(dmat.shape),
            _const_spec(qd.shape),
            _const_spec(kd.shape),
            _const_spec(cd.shape),
        ],
        out_specs=pl.BlockSpec((tb, d_v), lambda b, j: (row(b, j), 0)),
        scratch_shapes=[pltpu.VMEM((RET_HEADS, RET_QK, RET_V), F32)],
        compiler_params=_cparams(("parallel", "arbitrary")),
        name="retention",
    )(proj, proj, proj, proj, dmat, qd, kd, cd)


def _ret_out_ffn_kernel(x_ref, o_ref, wo_ref, g_ref, wgu_ref, wd_ref, out_ref, act_ref, *, fc):
    x1 = x_ref[...] + _dot(o_ref[...], wo_ref[...])
    h = _rms(x1, g_ref[...]).astype(BF16)
    f = wd_ref.shape[0]
    for c in range(f // fc):
        gt = _dot(h, wgu_ref[:, c * fc:(c + 1) * fc])
        up = _dot(h, wgu_ref[:, f + c * fc:f + (c + 1) * fc])
        act_ref[:, c * fc:(c + 1) * fc] = (gt * jax.nn.sigmoid(gt) * up).astype(BF16)
    out_ref[...] = x1 + _dot(act_ref[...], wd_ref[...])


def _ret_out_ffn(x2d, o, w_o, gain, w_gu, w_down):
    t, d = x2d.shape
    tm = TOK_TILE
    f = w_down.shape[0]
    return pl.pallas_call(
        functools.partial(_ret_out_ffn_kernel, fc=256),
        out_shape=jax.ShapeDtypeStruct((t, d), F32),
        grid=(t // tm,),
        in_specs=[
            pl.BlockSpec((tm, d), lambda i: (i, 0)),
            pl.BlockSpec((tm, o.shape[1]), lambda i: (i, 0)),
            _const_spec(w_o.shape),
            _const_spec((1, d)),
            _const_spec(w_gu.shape),
            _const_spec(w_down.shape),
        ],
        out_specs=pl.BlockSpec((tm, d), lambda i: (i, 0)),
        scratch_shapes=[pltpu.VMEM((tm, f), BF16)],
        compiler_params=_cparams(("parallel",)),
        name="ret_out_ffn",
    )(x2d, o, w_o, gain, w_gu, w_down)


def _rope16(x, ctab, s1tab, s2tab):
    half = ROPE_DIM // 2
    return (x * ctab + pltpu.roll(x, LANES - half, 1) * s1tab + pltpu.roll(x, half, 1) * s2tab)


def _rope16_t(x, ctab, s1tab, s2tab):
    half = ROPE_DIM // 2
    return (x * ctab + pltpu.roll(x, x.shape[0] - half, 0) * s1tab + pltpu.roll(x, half, 0) * s2tab)


def _qkv_kernel(x_ref, gq_ref, gkv_ref, wqt_ref, wk_ref, wvt_ref, c_ref, s1_ref, s2_ref,
                ct_ref, s1t_ref, s2t_ref, qt_ref, k_ref, vt_ref, qacc_ref):
    x = x_ref[...]
    xn = x * lax.rsqrt(jnp.mean(x * x, axis=-1, keepdims=True) + EPS)
    hq = (xn * gq_ref[...]).astype(BF16)
    hkv = (xn * gkv_ref[...]).astype(BF16)
    ctab, s1tab, s2tab = c_ref[...], s1_ref[...], s2_ref[...]
    d = k_ref.shape[1]
    dv = 2 * DIFF_HD
    for c in range(d // MXU_COLS):
        c0 = c * MXU_COLS
        ka = _dot(hkv, wk_ref[:, c0:c0 + MXU_COLS])
        for l0 in range(0, MXU_COLS, LANES):
            cols = slice(c0 + l0, c0 + l0 + LANES)
            k_ref[:, cols] = _rope16(ka[:, l0:l0 + LANES], ctab, s1tab, s2tab).astype(BF16)
    nt = (((1,), (1,)), ((), ()))
    vt_ref[...] = lax.dot_general(wvt_ref[...], hkv, nt, preferred_element_type=F32).astype(BF16)
    qacc_ref[...] = lax.dot_general(wqt_ref[...], hq, nt, preferred_element_type=F32)
    q_scale = DIFF_HD ** -0.5 * math.log2(math.e)
    ctab_t, s1tab_t, s2tab_t = ct_ref[...], s1t_ref[...], s2t_ref[...]
    for h in range(d // dv):
        rows = slice(h * dv, (h + 1) * dv)
        qt_ref[rows, :] = (_rope16_t(qacc_ref[rows, :], ctab_t, s1tab_t, s2tab_t) * q_scale).astype(BF16)


def _qkv_proj(x2d, gq, gkv, w_qt, w_k, w_vt, tabs, tabs_t, batch, seq):
    t, d = x2d.shape
    tm = LIGHT_TILE
    n_pos = seq // tm
    tok = pl.BlockSpec((tm, d), lambda i: (i, 0))
    tab = pl.BlockSpec((tm, LANES), lambda i: (i % n_pos, 0))
    tab_t = pl.BlockSpec((LANES, tm), lambda i: (0, i % n_pos))
    chan = pl.BlockSpec((d, tm), lambda i: (i // n_pos, i % n_pos))
    return pl.pallas_call(
        _qkv_kernel,
        out_shape=[jax.ShapeDtypeStruct((batch * d, seq), BF16), jax.ShapeDtypeStruct((t, d), BF16),
                   jax.ShapeDtypeStruct((batch * d, seq), BF16)],
        grid=(t // tm,),
        in_specs=[tok, _const_spec((1, d)), _const_spec((1, d)), _const_spec(w_qt.shape),
                  _const_spec(w_k.shape), _const_spec(w_vt.shape), tab, tab, tab, tab_t, tab_t, tab_t],
        out_specs=[chan, tok, chan],
        scratch_shapes=[pltpu.VMEM((d, tm), F32)],
        compiler_params=_cparams(("parallel",)),
        name="qkv_proj",
    )(x2d, gq, gkv, w_qt, w_k, w_vt, *tabs, *tabs_t)


def _attn_kernel(lam_ref, bias_ref, q_ref, qn_ref, k_ref, vt_ref, sub_ref, wa_ref, wb_ref,
                 o_ref, wa_out_ref, wb_out_ref, vta_ref, s_ref, mt_ref, mr_ref, acc_ref, *, lambda_init):
    i = pl.program_id(2)
    tq = q_ref.shape[1]
    dv = 2 * DIFF_HD
    n_heads, n_kv, _, tk = vta_ref.shape
    chains = [(h, c) for h in range(n_heads) for c in range(2)]

    @pl.when(i == 0)
    def _():
        acc_ref[...] = jnp.zeros_like(acc_ref)
        for h in range(n_heads):
            for j in range(n_kv):
                vta_ref[h, j, 0:dv, :] = vt_ref[h * dv:(h + 1) * dv, j * tk:(j + 1) * tk]
                vta_ref[h, j, dv:, :] = jnp.ones((ATT_ONES, tk), BF16)

    chan = lax.broadcasted_iota(jnp.int32, (dv, tq), 0)

    def split_heads(ref):
        out = []
        for h in range(n_heads):
            q = ref[h * dv:(h + 1) * dv, :]
            zero = jnp.zeros_like(q)
            out.append((jnp.where(chan < DIFF_HD, q, zero), jnp.where(chan >= DIFF_HD, q, zero)))
        return out

    def first_tiles(qi):
        n = (qi * tq) // tk
        return n, 1 + (qi * tq - n * tk) // tq

    n_full, tail_bias = first_tiles(i)

    def tile_max(s):
        m8 = jnp.max(s.reshape(tk // SUBLANES, SUBLANES, tq), axis=0)
        return jnp.max(m8, axis=0, keepdims=True)

    def score_chain(j, x, qsplit, n_vis, tail):
        h, c = chains[x]
        bias = bias_ref[jnp.where(j == n_vis, tail, 0)]
        kt = k_ref[pl.ds(pl.multiple_of(j * tk, tk), tk), h * dv:(h + 1) * dv]
        s = _dot(kt, qsplit[h][c]) + bias
        s_ref[x] = s
        return tile_max(s)

    @pl.when(i == 0)
    def _():
        qc0 = split_heads(q_ref)
        for x in range(len(chains)):
            mt_ref[x] = score_chain(0, x, qc0, n_full, tail_bias)

    qc = split_heads(q_ref)
    mr_ref[...] = jnp.full(mr_ref.shape, -jnp.inf, F32)

    def step(j, next_scores, finish_head=None):
        acc = []
        for x, (h, c) in enumerate(chains):
            m_run = mr_ref[x]
            m_new = jnp.maximum(m_run, mt_ref[x])
            alpha = jnp.exp2(m_run - m_new)
            p = jnp.exp2(s_ref[x] - m_new).astype(BF16)
            acc.append(alpha * acc_ref[x] + _dot(vta_ref[h, j], p))
            mr_ref[x] = m_new
            if finish_head is None:
                acc_ref[x] = acc[x]
            elif c == 1:
                finish_head(h, acc[x - 1], acc[x])
            if next_scores is not None:
                mt_ref[x] = next_scores(x)

    lam_v = lam_ref[...]
    lam = (jnp.exp(jnp.sum(lam_v[0:1] * lam_v[1:2], axis=-1, keepdims=True))
           - jnp.exp(jnp.sum(lam_v[2:3] * lam_v[3:4], axis=-1, keepdims=True)) + lambda_init)

    def finish_head(h, a0, a1):
        ot = a0[0:dv] / a0[dv:dv + 1] - lam * (a1[0:dv] / a1[dv:dv + 1])
        o = _rms(ot.T, sub_ref[...]) * (1.0 - lambda_init)
        o_ref[:, h * dv:(h + 1) * dv] = o.astype(BF16)
        acc_ref[2 * h:2 * h + 2] = jnp.zeros((2,) + acc_ref.shape[1:], F32)

    def body(j, carry):
        step(j, lambda x: score_chain(j + 1, x, qc, n_full, tail_bias))
        return carry

    lax.fori_loop(0, n_full, body, 0)

    @pl.when(i + 1 < pl.num_programs(2))
    def _():
        _round_slabs((wa_ref, wb_ref), (wa_out_ref, wb_out_ref))
        qn = split_heads(qn_ref)
        n_vis, tail = first_tiles(i + 1)
        step(n_full, lambda x: score_chain(0, x, qn, n_vis, tail), finish_head)

    @pl.when(i + 1 == pl.num_programs(2))
    def _():
        _round_slabs((wa_ref, wb_ref), (wa_out_ref, wb_out_ref))
        step(n_full, None, finish_head)


def _attn_bias(tq, tk):
    key = np.arange(tk)[:, None] // CHUNK
    tiles = [np.zeros((tk, tq))]
    for r in range(tk // tq):
        qry = (r * tq + np.arange(tq))[None, :] // CHUNK
        tiles.append(np.where(key <= qry, 0.0, -1e30))
    return jnp.asarray(np.stack(tiles), F32)


def _diff_attention(qt, k, vt, lam_vecs, subln, w_a, w_b, batch, seq, lambda_init):
    t, d = k.shape
    tq = ATT_TQ
    nq = seq // tq
    dv = 2 * DIFF_HD
    hp = ATT_HEADS
    n_groups = DIFF_HEADS // hp
    n_steps = batch * n_groups * nq
    bias = _attn_bias(tq, ATT_TK)
    slabs = _slab_specs((w_a, w_b), n_steps, lambda b, g, i: (b * n_groups + g) * nq + i)
    return pl.pallas_call(
        functools.partial(_attn_kernel, lambda_init=lambda_init),
        out_shape=[jax.ShapeDtypeStruct((t, d), BF16),
                   jax.ShapeDtypeStruct(w_a.shape, BF16), jax.ShapeDtypeStruct(w_b.shape, BF16)],
        grid=(batch, n_groups, nq),
        in_specs=[
            _const_spec(lam_vecs.shape),
            _const_spec(bias.shape),
            pl.BlockSpec((hp * dv, tq), lambda b, g, i: (b * n_groups + g, i)),
            pl.BlockSpec((hp * dv, tq), lambda b, g, i: (b * n_groups + g, jnp.minimum(i + 1, nq - 1))),
            pl.BlockSpec((seq, hp * dv), lambda b, g, i: (b, g)),
            pl.BlockSpec((hp * dv, seq), lambda b, g, i: (b * n_groups + g, 0)),
            _const_spec((1, dv)),
        ] + slabs,
        out_specs=[pl.BlockSpec((tq, hp * dv), lambda b, g, i: (b * nq + i, g))] + slabs,
        scratch_shapes=[pltpu.VMEM((hp, seq // ATT_TK, dv + ATT_ONES, ATT_TK), BF16),
                        pltpu.VMEM((2 * hp, ATT_TK, tq), F32),
                        pltpu.VMEM((2 * hp, 1, tq), F32),
                        pltpu.VMEM((2 * hp, 1, tq), F32),
                        pltpu.VMEM((2 * hp, dv + ATT_ONES, tq), F32)],
        compiler_params=_cparams(("parallel", "parallel", "arbitrary")),
        name="diff_attn",
    )(lam_vecs, bias, qt, qt, k, vt, subln, w_a, w_b)


ROUTE_ROWS = 2 * SUBLANES


def _attn_out_kernel(x_ref, o_ref, wo_ref, g_ref, r_ref, tri_ref, x3_ref, h3_ref, route_ref, cnt_ref, carry_ref):
    @pl.when(pl.program_id(0) == 0)
    def _():
        carry_ref[...] = jnp.zeros_like(carry_ref)

    tm = x_ref.shape[0]
    x3 = x_ref[...] + _dot(o_ref[...], wo_ref[...])
    x3_ref[...] = x3
    h3 = _rms(x3, g_ref[...])
    for s in range(h3.shape[1] // LANES):
        h3_ref[pl.ds(s, tm, stride=SUBLANES), :] = h3[:, s * LANES:(s + 1) * LANES]

    hi = h3.astype(BF16)
    lo = (h3 - hi.astype(F32)).astype(BF16)
    both = _dot(hi, r_ref[...])
    logits = both[:, :LANES] + both[:, LANES:] + _dot(lo, r_ref[:, :LANES])
    lt = logits.T[0:ROUTE_ROWS]
    sub = lax.broadcasted_iota(jnp.int32, lt.shape, 0).astype(F32)
    lt = jnp.where(sub < N_EXPERTS, lt, -jnp.inf)
    v1 = jnp.max(lt, axis=0, keepdims=True)
    i1 = jnp.min(jnp.where(lt == v1, sub, float(ROUTE_ROWS)), axis=0, keepdims=True)
    lt2 = jnp.where(sub == i1, -jnp.inf, lt)
    v2 = jnp.max(lt2, axis=0, keepdims=True)
    i2 = jnp.min(jnp.where(lt2 == v2, sub, float(ROUTE_ROWS)), axis=0, keepdims=True)
    e = jnp.exp(v2 - v1)
    w1 = 1.0 / (1.0 + e)
    w2 = e / (1.0 + e)
    oh1 = sub == i1
    oh2 = sub == i2
    assign = jnp.where(oh1 | oh2, 1.0, 0.0)
    excl = _dot(assign.astype(BF16), tri_ref[...]) + carry_ref[:, 0:1]
    r1 = jnp.sum(jnp.where(oh1, excl, 0.0), axis=0, keepdims=True)
    r2 = jnp.sum(jnp.where(oh2, excl, 0.0), axis=0, keepdims=True)
    route = jnp.zeros_like(lt)
    for row, val in enumerate((i1, i2, r1, r2, w1, w2)):
        route = jnp.where(sub == float(row), val, route)
    route_ref[...] = route[0:SUBLANES]
    total = carry_ref[:, 0:1] + jnp.sum(assign, axis=1, keepdims=True)
    carry_ref[...] = jnp.broadcast_to(total, carry_ref.shape)
    cnt_ref[...] = jnp.broadcast_to(total, cnt_ref.shape)


def _attn_out(x2d, o, w_o, gain, router_split, tri):
    t, d = x2d.shape
    tm = LIGHT_TILE
    n_sub = d // LANES
    return pl.pallas_call(
        _attn_out_kernel,
        out_shape=[jax.ShapeDtypeStruct((t, d), F32),
                   jax.ShapeDtypeStruct((t * n_sub, LANES), F32),
                   jax.ShapeDtypeStruct((SUBLANES, t), F32),
                   jax.ShapeDtypeStruct((ROUTE_ROWS, LANES), F32)],
        grid=(t // tm,),
        in_specs=[pl.BlockSpec((tm, d), lambda i: (i, 0)),
                  pl.BlockSpec((tm, d), lambda i: (i, 0)),
                  _const_spec(w_o.shape), _const_spec((1, d)), _const_spec(router_split.shape),
                  _const_spec(tri.shape)],
        out_specs=[pl.BlockSpec((tm, d), lambda i: (i, 0)),
                   pl.BlockSpec((tm * n_sub, LANES), lambda i: (i, 0)),
                   pl.BlockSpec((SUBLANES, tm), lambda i: (0, i)),
                   _const_spec((ROUTE_ROWS, LANES))],
        scratch_shapes=[pltpu.VMEM((ROUTE_ROWS, LANES), F32)],
        compiler_params=_cparams(("arbitrary",)),
        name="attn_out_route",
    )(x2d, o, w_o, gain, router_split, tri)


def _dispatch_kernel(dest_ref, zpos_ref, zflag_ref, h_ref, dst_ref, zbuf_ref, zsem, sem):
    i = pl.program_id(0)
    tm = h_ref.shape[0]

    @pl.when(i == 0)
    def _():
        zbuf_ref[...] = jnp.zeros_like(zbuf_ref)

        def zcopy(e):
            return pltpu.make_async_copy(zbuf_ref, dst_ref.at[pl.ds(zpos_ref[e], MOE_TILE)], zsem.at[e])

        for e in range(2 * N_EXPERTS):
            @pl.when(zflag_ref[e] == 1)
            def _():
                zcopy(e).start()
        for e in range(2 * N_EXPERTS):
            @pl.when(zflag_ref[e] == 1)
            def _():
                zcopy(e).wait()

    n_tok = dest_ref.shape[0] // 2

    def body(r, carry):
        for k in range(2):
            pltpu.make_async_copy(h_ref.at[r], dst_ref.at[dest_ref[k * n_tok + i * tm + r]],
                                  sem).start(priority=k)
        return carry

    lax.fori_loop(0, tm, body, 0, unroll=8)
    for k in range(2):
        pltpu.make_async_copy(h_ref, dst_ref.at[pl.ds(0, tm)], sem).wait()


def _dispatch(h3_rows, dest, zpos, zflag, n_dst):
    t = h3_rows.shape[0]
    tm = DISPATCH_TILE
    tail = h3_rows.shape[1:]
    return pl.pallas_call(
        _dispatch_kernel,
        out_shape=jax.ShapeDtypeStruct((n_dst,) + tail, F32),
        grid_spec=pltpu.PrefetchScalarGridSpec(
            num_scalar_prefetch=3, grid=(t // tm,),
            in_specs=[pl.BlockSpec((tm,) + tail, lambda i, *_: (i, 0, 0))],
            out_specs=pl.BlockSpec(memory_space=pl.ANY),
            scratch_shapes=[pltpu.VMEM((MOE_TILE,) + tail, F32),
                            pltpu.SemaphoreType.DMA((2 * N_EXPERTS,)), pltpu.SemaphoreType.DMA(())]),
        compiler_params=_cparams(("arbitrary",)),
        name="moe_dispatch",
    )(dest, zpos, zflag, h3_rows)


def _moe_kernel(te_ref, tr_ref, ts_ref, xs_ref, wg_ref, wu_ref, wd_ref, ys_ref, h_ref, acc_ref, *, nfc):
    i = pl.program_id(0)
    c = pl.program_id(1)
    last = nfc - 1
    n_sub = h_ref.shape[1] // LANES
    grp = h_ref.shape[0] // MOE_SPLIT

    def expert_mlp(h):
        gt = _dot(h, wg_ref[...])
        up = _dot(h, wu_ref[...])
        act = (gt * jax.nn.sigmoid(gt) * up).astype(BF16)
        return _dot(act, wd_ref[...])

    for part in range(MOE_SPLIT):
        r0 = part * grp
        rows = slice(r0, r0 + grp)
        live = tr_ref[i] > r0

        def load_rows():
            h = jnp.concatenate([xs_ref[pl.ds(r0 * n_sub + s, grp, stride=n_sub), :].astype(BF16)
                                 for s in range(n_sub)], axis=1)
            h_ref[rows, :] = h
            return h

        def store_rows(y):
            for s in range(n_sub):
                ys_ref[pl.ds(r0 * n_sub + s, grp, stride=n_sub), :] = y[:, s * LANES:(s + 1) * LANES]

        if nfc == 1:
            @pl.when(live)
            def _():
                store_rows(expert_mlp(load_rows()))
        else:
            @pl.when(live & (c == 0))
            def _():
                acc_ref[rows, :] = expert_mlp(load_rows())

            if nfc > 2:
                @pl.when(live & (c > 0) & (c < last))
                def _():
                    acc_ref[rows, :] += expert_mlp(h_ref[rows, :])

            @pl.when(live & (c == last))
            def _():
                store_rows(acc_ref[rows, :] + expert_mlp(h_ref[rows, :]))

        @pl.when(jnp.logical_not(live) & (c == last))
        def _():
            ys_ref[r0 * n_sub:(r0 + grp) * n_sub, :] = jnp.zeros((grp * n_sub, LANES), F32)


def _moe_experts(xs2d, w_gu, w_down, tile_expert, tile_rows, tile_src, d):
    n_sub = d // LANES
    n_tiles = xs2d.shape[0] // (MOE_TILE * n_sub)
    f = w_down.shape[1]
    nfc = f // MOE_FC
    blk = MOE_TILE * n_sub

    def ceff(i, c, tv):
        return jnp.where(tv[i] > 0, c, nfc - 1)

    return pl.pallas_call(
        functools.partial(_moe_kernel, nfc=nfc),
        out_shape=jax.ShapeDtypeStruct(xs2d.shape, F32),
        grid_spec=pltpu.PrefetchScalarGridSpec(
            num_scalar_prefetch=3, grid=(n_tiles, nfc),
            in_specs=[
                pl.BlockSpec((blk, LANES), lambda i, c, te, tv, ts: (ts[i], 0)),
                pl.BlockSpec((None, d, MOE_FC), lambda i, c, te, tv, ts: (te[i], 0, ceff(i, c, tv))),
                pl.BlockSpec((None, d, MOE_FC), lambda i, c, te, tv, ts: (te[i], 0, nfc + ceff(i, c, tv))),
                pl.BlockSpec((None, MOE_FC, d), lambda i, c, te, tv, ts: (te[i], ceff(i, c, tv), 0)),
            ],
            out_specs=pl.BlockSpec((blk, LANES), lambda i, c, te, tv, ts: (i, 0)),
            scratch_shapes=[pltpu.VMEM((MOE_TILE, d), BF16), pltpu.VMEM((MOE_TILE, d), F32)]),
        compiler_params=_cparams(("arbitrary", "arbitrary")),
        name="moe_experts",
    )(tile_expert, tile_rows, tile_src, xs2d, w_gu, w_gu, w_down)


def _combine_kernel(dest_ref, x_ref, ys_ref, gate_ref, g_ref, o_ref, ybuf_ref, sem):
    i = pl.program_id(0)
    n = pl.num_programs(0)
    tm, d = x_ref.shape
    n_sub = d // LANES

    def gather(tile, slot):
        def body(r, carry):
            for k in range(2):
                row = dest_ref[k * (dest_ref.shape[0] // 2) + tile * tm + r]
                src = ys_ref.at[pl.ds(pl.multiple_of(row * n_sub, n_sub), n_sub)]
                dst = ybuf_ref.at[slot, pl.ds(pl.multiple_of((2 * r + k) * n_sub, n_sub), n_sub)]
                pltpu.make_async_copy(src, dst, sem.at[slot]).start(priority=k)
            return carry

        lax.fori_loop(0, tm, body, 0, unroll=8)

    @pl.when(i == 0)
    def _():
        gather(0, 0)

    @pl.when(i + 1 < n)
    def _():
        gather(i + 1, (i + 1) % 2)

    slot = i % 2
    pltpu.make_async_copy(ys_ref.at[pl.ds(0, 2 * tm * n_sub)], ybuf_ref.at[slot], sem.at[slot]).wait()

    y_ref = ybuf_ref.at[slot]
    gates = gate_ref[...]
    w1 = gates[:, 0:1]
    w2 = gates[:, 1:2]
    for s in range(n_sub):
        cols = slice(s * LANES, (s + 1) * LANES)
        y1 = y_ref[pl.ds(s, tm, stride=2 * n_sub), :]
        y2 = y_ref[pl.ds(n_sub + s, tm, stride=2 * n_sub), :]
        o_ref[:, cols] = x_ref[:, cols] + w1 * y1 + w2 * y2
    o_ref[...] = _rms(o_ref[...], g_ref[...])


def _combine(x2d, ys2d, dest, gates, gain):
    t, d = x2d.shape
    tm = TOK_TILE
    n_sub = d // LANES
    return pl.pallas_call(
        _combine_kernel,
        out_shape=jax.ShapeDtypeStruct((t, d), F32),
        grid_spec=pltpu.PrefetchScalarGridSpec(
            num_scalar_prefetch=1, grid=(t // tm,),
            in_specs=[pl.BlockSpec((tm, d), lambda i, *_: (i, 0)),
                      pl.BlockSpec(memory_space=pl.ANY),
                      pl.BlockSpec((tm, gates.shape[1]), lambda i, *_: (i, 0)),
                      pl.BlockSpec((1, d), lambda i, *_: (0, 0), pipeline_mode=pl.Buffered(1))],
            out_specs=pl.BlockSpec((tm, d), lambda i, *_: (i, 0)),
            scratch_shapes=[pltpu.VMEM((2, 2 * tm * n_sub, LANES), F32), pltpu.SemaphoreType.DMA((2,))]),
        compiler_params=_cparams(("arbitrary",)),
        name="moe_combine",
    )(dest, x2d, ys2d, gates, gain)


def _rope_tables(seq, rot_dim, theta):
    inv = 1.0 / (theta ** (np.arange(0, rot_dim, 2, dtype=np.float64) / rot_dim))
    ang = np.arange(seq, dtype=np.float64)[:, None] * inv[None, :]
    return np.cos(ang), np.sin(ang)


def _retention_tables():
    c = RET_CHUNK
    log_gamma = np.log(1.0 - 2.0 ** (-5.0 - np.arange(RET_HEADS, dtype=np.float64)))
    idx = np.arange(c, dtype=np.float64)
    rel = idx[:, None] - idx[None, :]
    dmat = np.where(rel[None] >= 0, np.exp(np.maximum(rel, 0.0)[None] * log_gamma[:, None, None]), 0.0)
    qd = np.exp((idx + 1.0)[None, :] * log_gamma[:, None])[:, :, None]
    kd = np.exp((c - 1.0 - idx)[None, :] * log_gamma[:, None])[:, :, None]
    cd = np.exp(c * log_gamma)[:, None, None]
    return tuple(jnp.asarray(tb, F32) for tb in (dmat, qd, kd, cd))


def _attn_rope_tables(seq):
    cos, sin = _rope_tables(seq, ROPE_DIM, ROPE_THETA)
    half = ROPE_DIM // 2
    pad = DIFF_HD - ROPE_DIM
    ones = np.ones((seq, pad))
    zeros = np.zeros((seq, pad))
    zh = np.zeros((seq, half))
    ctab = np.concatenate([cos, cos, ones], axis=1)
    s1tab = np.concatenate([-sin, zh, zeros], axis=1)
    s2tab = np.concatenate([zh, sin, zeros], axis=1)
    rep = LANES // DIFF_HD
    tabs = [np.tile(tb, (1, rep)) for tb in (ctab, s1tab, s2tab)]
    return (tuple(jnp.asarray(tb, F32) for tb in tabs),
            tuple(jnp.asarray(tb.T, F32) for tb in tabs))


def kernel(x, ln_mix, ln_ffn, ret_w_in, ret_w_o, kv_norm, w_kv, diff_w_q, lam_q1, lam_k1, lam_q2, lam_k2,
           diff_subln, diff_w_o, ffn_w_gu, ffn_w_down, moe_router, moe_w_gu, moe_w_down, final_norm):
    batch, seq, d = x.shape
    t = batch * seq
    assert ln_mix.shape[0] == 2 and ret_w_in.shape[0] == 1 and diff_w_q.shape[0] == 1
    assert seq % TOK_TILE == 0 and TOK_TILE % RET_CHUNK == 0 and ATT_TQ % CHUNK == 0
    assert seq % LIGHT_TILE == 0 and seq % DISPATCH_TILE == 0
    assert seq % ATT_TK == 0 and ATT_TK % ATT_TQ == 0
    x2d = x.reshape(t, d)
    row = lambda g: g.reshape(1, -1)

    cos_r, sin_r = (jnp.asarray(tb, F32) for tb in _rope_tables(seq, RET_QK, RET_THETA))
    proj, w_o_bf, w_gu_d_bf, w_down_d_bf, w_q_bf, w_kv_bf, w_ao_bf = _ret_in_proj(
        x2d, row(ln_mix[0]), ret_w_in[0].astype(BF16), cos_r, sin_r, seq,
        (ret_w_o[0], ffn_w_gu[0], ffn_w_down[0], diff_w_q[0], w_kv, diff_w_o[0]))
    ret_o = _retention(proj, *_retention_tables(), batch, seq)
    x2 = _ret_out_ffn(x2d, ret_o, w_o_bf, row(ln_ffn[0]), w_gu_d_bf, w_down_d_bf)

    lambda_init = 0.8 - 0.6 * math.exp(-0.3 * 1)
    qt, k, vt = _qkv_proj(x2, row(ln_mix[1]), row(kv_norm), w_q_bf.T, w_kv_bf[:, :d], w_kv_bf[:, d:].T,
                          *_attn_rope_tables(seq), batch, seq)
    lam_vecs = jnp.stack([lam_q1[0], lam_k1[0], lam_q2[0], lam_k2[0]]).astype(F32)
    n_exp, _, two_f = moe_w_gu[0].shape
    att, w_gu_bf, w_down_bf = _diff_attention(
        qt, k, vt, lam_vecs, row(diff_subln[0]), moe_w_gu[0].reshape(n_exp * d, two_f),
        moe_w_down[0].reshape(n_exp * (two_f // 2), d), batch, seq, lambda_init)
    router_pad = jnp.pad(moe_router[0], ((0, 0), (0, LANES - N_EXPERTS)))
    router_hi = router_pad.astype(BF16)
    router_lo = (router_pad - router_hi.astype(F32)).astype(BF16)
    tri = jnp.asarray(np.arange(LIGHT_TILE)[:, None] < np.arange(LIGHT_TILE)[None, :], BF16)
    x3, h3_rows, route, cnt = _attn_out(x2, att, w_ao_bf, row(ln_ffn[1]),
                                        jnp.concatenate([router_hi, router_lo], axis=1), tri)

    n_sub = d // LANES
    n_rows = 2 * t + N_EXPERTS * MOE_TILE
    n_tiles = n_rows // MOE_TILE
    expert = route[0:2].astype(jnp.int32)
    rank = route[2:4].astype(jnp.int32)
    gates = route[4:6].T
    counts = cnt[:N_EXPERTS, 0].astype(jnp.int32)
    padded = (counts + MOE_TILE - 1) // MOE_TILE * MOE_TILE
    seg_end = jnp.cumsum(padded)
    seg_start = seg_end - padded
    dest = (jnp.sum(jnp.where(expert[..., None] == jnp.arange(N_EXPERTS), seg_start, 0), axis=-1)
            + rank).reshape(-1)
    tile_row = jnp.arange(n_tiles, dtype=jnp.int32) * MOE_TILE
    tile_valid = (tile_row < seg_end[-1]).astype(jnp.int32)
    n_valid = seg_end[-1] // MOE_TILE
    tile_src = jnp.minimum(jnp.arange(n_tiles, dtype=jnp.int32), jnp.maximum(n_valid - 1, 0))
    tile_expert = jnp.minimum(jnp.sum(tile_src[:, None] * MOE_TILE >= seg_end[None, :], axis=1),
                              N_EXPERTS - 1).astype(jnp.int32)
    token_end = jnp.sum(jnp.where(tile_expert[:, None] == jnp.arange(N_EXPERTS), seg_start + counts, 0), axis=1)
    tile_rows = (tile_valid * jnp.clip(token_end - tile_row, 0, MOE_TILE)).astype(jnp.int32)
    zflag = jnp.concatenate([(padded > 0).astype(jnp.int32), 1 - tile_valid[-N_EXPERTS:]])
    zpos = jnp.concatenate([jnp.maximum(seg_end - MOE_TILE, 0), tile_row[-N_EXPERTS:]]).astype(jnp.int32)

    xs = _dispatch(h3_rows.reshape(t, n_sub, LANES), dest, zpos, zflag, n_rows)
    ys = _moe_experts(xs.reshape(n_rows * n_sub, LANES), w_gu_bf.reshape(n_exp, d, two_f),
                      w_down_bf.reshape(n_exp, two_f // 2, d), tile_expert, tile_rows, tile_src, d)
    out = _combine(x3, ys, dest, gates, row(final_norm))
    return out.reshape(batch, seq, d)
```

```python
import functools
import math

import jax
import jax.numpy as jnp
import numpy as np
from jax import lax
from jax.experimental import pallas as pl
from jax.experimental.pallas import tpu as pltpu

F32 = jnp.float32
BF16 = jnp.bfloat16

EPS = 1e-6
CHUNK = 64
RET_QK = 256
RET_V = 512
RET_HEADS = 4
RET_THETA = 10000.0
DIFF_HD = 64
DIFF_HEADS = 8
ROPE_THETA = 500000.0
ROPE_DIM = 16
N_EXPERTS = 8

LANES = 128
SUBLANES = 8
MXU_COLS = 256
VMEM_LIMIT = 56 * 1024 * 1024

RET_CHUNK = 256
TOK_TILE = 512
LIGHT_TILE = 1024
DISPATCH_TILE = 1024
ATT_TQ = 512
ATT_TK = 512
ATT_HEADS = 4
ATT_ONES = 16
MOE_TILE = 1024
MOE_SPLIT = 2
MOE_FC = 1792


def _cparams(sem, vmem=VMEM_LIMIT):
    return pltpu.CompilerParams(dimension_semantics=sem, vmem_limit_bytes=vmem)


def _const_spec(shape):
    nd = len(shape)
    return pl.BlockSpec(shape, lambda *_: (0,) * nd, pipeline_mode=pl.Buffered(1))


def _slab_specs(weights, n_steps, step_of):
    pack = 2 * SUBLANES
    specs = []
    for w in weights:
        rows = w.shape[0]
        per = -(-rows // n_steps)
        br = next(b for b in range(-(-per // pack) * pack, rows + 1, pack) if rows % b == 0)
        last = rows // br - 1
        specs.append(pl.BlockSpec((br, w.shape[1]),
                                  lambda *g, last=last: (jnp.minimum(step_of(*g), last), 0)))
    return specs


def _round_slabs(in_refs, out_refs):
    for src, dst in zip(in_refs, out_refs):
        dst[...] = src[...].astype(BF16)


def _rms(x, g):
    return x * lax.rsqrt(jnp.mean(x * x, axis=-1, keepdims=True) + EPS) * g


def _dot(a, b):
    return jnp.dot(a, b, preferred_element_type=F32)


def _ret_in_kernel(x_ref, g_ref, w_ref, cos_ref, sin_ref, *refs):
    n_ride = (len(refs) - 1) // 2
    o_ref = refs[n_ride]
    _round_slabs(refs[:n_ride], refs[n_ride + 1:])
    h = _rms(x_ref[...], g_ref[...]).astype(BF16)
    cos = cos_ref[...]
    sin = sin_ref[...]
    d_qk = RET_HEADS * RET_QK
    half = RET_QK // 2
    for c in range(2 * RET_HEADS):
        c0 = c * RET_QK
        acc = _dot(h, w_ref[:, c0:c0 + RET_QK])
        x1 = acc[:, :half]
        x2 = acc[:, half:]
        scale = 1.0 if c < RET_HEADS else RET_QK ** -0.5
        o_ref[:, c0:c0 + half] = ((x1 * cos - x2 * sin) * scale).astype(BF16)
        o_ref[:, c0 + half:c0 + RET_QK] = ((x2 * cos + x1 * sin) * scale).astype(BF16)
    n_rest = (w_ref.shape[1] - 2 * d_qk) // RET_V
    for c in range(n_rest):
        c0 = 2 * d_qk + c * RET_V
        o_ref[:, c0:c0 + RET_V] = _dot(h, w_ref[:, c0:c0 + RET_V]).astype(BF16)


def _ret_in_proj(x2d, gain, w, cos, sin, seq, ride):
    t, d = x2d.shape
    n = w.shape[1]
    tm = TOK_TILE
    n_pos = seq // tm
    slabs = _slab_specs(ride, t // tm, lambda i: i)
    return pl.pallas_call(
        _ret_in_kernel,
        out_shape=[jax.ShapeDtypeStruct((t, n), BF16)] + [jax.ShapeDtypeStruct(r.shape, BF16) for r in ride],
        grid=(t // tm,),
        in_specs=[
            pl.BlockSpec((tm, d), lambda i: (i, 0)),
            _const_spec((1, d)),
            _const_spec((d, n)),
            pl.BlockSpec((tm, RET_QK // 2), lambda i: (i % n_pos, 0)),
            pl.BlockSpec((tm, RET_QK // 2), lambda i: (i % n_pos, 0)),
        ] + slabs,
        out_specs=[pl.BlockSpec((tm, n), lambda i: (i, 0))] + slabs,
        compiler_params=_cparams(("arbitrary",)),
        name="ret_in_proj",
    )(x2d, gain, w, cos, sin, *ride)


def _retention_kernel(q_ref, k_ref, v_ref, g_ref, dmat_ref, qd_ref, kd_ref, cd_ref, o_ref, state_ref):
    @pl.when(pl.program_id(1) == 0)
    def _():
        state_ref[...] = jnp.zeros_like(state_ref)

    n_chunks = q_ref.shape[0] // RET_CHUNK
    for h in range(RET_HEADS):
        for c in range(n_chunks):
            rows = slice(c * RET_CHUNK, (c + 1) * RET_CHUNK)
            q = q_ref[rows, h * RET_QK:(h + 1) * RET_QK]
            k = k_ref[rows, h * RET_QK:(h + 1) * RET_QK]
            v = v_ref[rows, h * RET_V:(h + 1) * RET_V]
            state = state_ref[h]
            s = lax.dot_general(q, k, (((1,), (1,)), ((), ())), preferred_element_type=F32)
            s = s * dmat_ref[h]
            o = _dot(s.astype(BF16), v)
            qs = (q.astype(F32) * qd_ref[h]).astype(BF16)
            o = o + _dot(qs, state.astype(BF16))
            ks = (k.astype(F32) * kd_ref[h]).astype(BF16)
            state_ref[h] = state * cd_ref[h] + lax.dot_general(
                ks, v, (((0,), (0,)), ((), ())), preferred_element_type=F32)
            mu = jnp.mean(o, axis=-1, keepdims=True)
            oc = o - mu
            var = jnp.mean(oc * oc, axis=-1, keepdims=True)
            on = oc * lax.rsqrt(var + EPS)
            gate = g_ref[rows, h * RET_V:(h + 1) * RET_V].astype(F32)
            o_ref[rows, h * RET_V:(h + 1) * RET_V] = (on * (gate * jax.nn.sigmoid(gate))).astype(BF16)


def _retention(proj, dmat, qd, kd, cd, batch, seq):
    t = proj.shape[0]
    tb = TOK_TILE
    nj = seq // tb
    d_qk = RET_HEADS * RET_QK
    d_v = RET_HEADS * RET_V
    row = lambda b, j: b * nj + j
    return pl.pallas_call(
        _retention_kernel,
        out_shape=jax.ShapeDtypeStruct((t, d_v), BF16),
        grid=(batch, nj),
        in_specs=[
            pl.BlockSpec((tb, d_qk), lambda b, j: (row(b, j), 0)),
            pl.BlockSpec((tb, d_qk), lambda b, j: (row(b, j), 1)),
            pl.BlockSpec((tb, d_v), lambda b, j: (row(b, j), 1)),
            pl.BlockSpec((tb, d_v), lambda b, j: (row(b, j), 2)),
            _const_spec(dmat.shape),
            _const_spec(qd.shape),
            _const_spec(kd.shape),
            _const_spec(cd.shape),
        ],
        out_specs=pl.BlockSpec((tb, d_v), lambda b, j: (row(b, j), 0)),
        scratch_shapes=[pltpu.VMEM((RET_HEADS, RET_QK, RET_V), F32)],
        compiler_params=_cparams(("parallel", "arbitrary")),
        name="retention",
    )(proj, proj, proj, proj, dmat, qd, kd, cd)


def _ret_out_ffn_kernel(x_ref, o_ref, wo_ref, g_ref, wgu_ref, wd_ref, out_ref, act_ref, *, fc):
    x1 = x_ref[...] + _dot(o_ref[...], wo_ref[...])
    h = _rms(x1, g_ref[...]).astype(BF16)
    f = wd_ref.shape[0]
    for c in range(f // fc):
        gt = _dot(h, wgu_ref[:, c * fc:(c + 1) * fc])
        up = _dot(h, wgu_ref[:, f + c * fc:f + (c + 1) * fc])
        act_ref[:, c * fc:(c + 1) * fc] = (gt * jax.nn.sigmoid(gt) * up).astype(BF16)
    out_ref[...] = x1 + _dot(act_ref[...], wd_ref[...])


def _ret_out_ffn(x2d, o, w_o, gain, w_gu, w_down):
    t, d = x2d.shape
    tm = TOK_TILE
    f = w_down.shape[0]
    return pl.pallas_call(
        functools.partial(_ret_out_ffn_kernel, fc=256),
        out_shape=jax.ShapeDtypeStruct((t, d), F32),
        grid=(t // tm,),
        in_specs=[
            pl.BlockSpec((tm, d), lambda i: (i, 0)),
            pl.BlockSpec((tm, o.shape[1]), lambda i: (i, 0)),
            _const_spec(w_o.shape),
            _const_spec((1, d)),
            _const_spec(w_gu.shape),
            _const_spec(w_down.shape),
        ],
        out_specs=pl.BlockSpec((tm, d), lambda i: (i, 0)),
        scratch_shapes=[pltpu.VMEM((tm, f), BF16)],
        compiler_params=_cparams(("parallel",)),
        name="ret_out_ffn",
    )(x2d, o, w_o, gain, w_gu, w_down)


def _rope16(x, ctab, s1tab, s2tab):
    half = ROPE_DIM // 2
    return (x * ctab + pltpu.roll(x, LANES - half, 1) * s1tab + pltpu.roll(x, half, 1) * s2tab)


def _rope16_t(x, ctab, s1tab, s2tab):
    half = ROPE_DIM // 2
    return (x * ctab + pltpu.roll(x, x.shape[0] - half, 0) * s1tab + pltpu.roll(x, half, 0) * s2tab)


def _qkv_kernel(x_ref, gq_ref, gkv_ref, wqt_ref, wk_ref, wvt_ref, c_ref, s1_ref, s2_ref,
                ct_ref, s1t_ref, s2t_ref, qt_ref, k_ref, vt_ref, qacc_ref):
    x = x_ref[...]
    xn = x * lax.rsqrt(jnp.mean(x * x, axis=-1, keepdims=True) + EPS)
    hq = (xn * gq_ref[...]).astype(BF16)
    hkv = (xn * gkv_ref[...]).astype(BF16)
    ctab, s1tab, s2tab = c_ref[...], s1_ref[...], s2_ref[...]
    d = k_ref.shape[1]
    dv = 2 * DIFF_HD
    for c in range(d // MXU_COLS):
        c0 = c * MXU_COLS
        ka = _dot(hkv, wk_ref[:, c0:c0 + MXU_COLS])
        for l0 in range(0, MXU_COLS, LANES):
            cols = slice(c0 + l0, c0 + l0 + LANES)
            k_ref[:, cols] = _rope16(ka[:, l0:l0 + LANES], ctab, s1tab, s2tab).astype(BF16)
    nt = (((1,), (1,)), ((), ()))
    vt_ref[...] = lax.dot_general(wvt_ref[...], hkv, nt, preferred_element_type=F32).astype(BF16)
    qacc_ref[...] = lax.dot_general(wqt_ref[...], hq, nt, preferred_element_type=F32)
    q_scale = DIFF_HD ** -0.5 * math.log2(math.e)
    ctab_t, s1tab_t, s2tab_t = ct_ref[...], s1t_ref[...], s2t_ref[...]
    for h in range(d // dv):
        rows = slice(h * dv, (h + 1) * dv)
        qt_ref[rows, :] = (_rope16_t(qacc_ref[rows, :], ctab_t, s1tab_t, s2tab_t) * q_scale).astype(BF16)


def _qkv_proj(x2d, gq, gkv, w_qt, w_k, w_vt, tabs, tabs_t, batch, seq):
    t, d = x2d.shape
    tm = LIGHT_TILE
    n_pos = seq // tm
    tok = pl.BlockSpec((tm, d), lambda i: (i, 0))
    tab = pl.BlockSpec((tm, LANES), lambda i: (i % n_pos, 0))
    tab_t = pl.BlockSpec((LANES, tm), lambda i: (0, i % n_pos))
    chan = pl.BlockSpec((d, tm), lambda i: (i // n_pos, i % n_pos))
    return pl.pallas_call(
        _qkv_kernel,
        out_shape=[jax.ShapeDtypeStruct((batch * d, seq), BF16), jax.ShapeDtypeStruct((t, d), BF16),
                   jax.ShapeDtypeStruct((batch * d, seq), BF16)],
        grid=(t // tm,),
        in_specs=[tok, _const_spec((1, d)), _const_spec((1, d)), _const_spec(w_qt.shape),
                  _const_spec(w_k.shape), _const_spec(w_vt.shape), tab, tab, tab, tab_t, tab_t, tab_t],
        out_specs=[chan, tok, chan],
        scratch_shapes=[pltpu.VMEM((d, tm), F32)],
        compiler_params=_cparams(("parallel",)),
        name="qkv_proj",
    )(x2d, gq, gkv, w_qt, w_k, w_vt, *tabs, *tabs_t)


def _attn_kernel(lam_ref, bias_ref, q_ref, qn_ref, k_ref, vt_ref, sub_ref, wa_ref, wb_ref,
                 o_ref, wa_out_ref, wb_out_ref, vta_ref, s_ref, mt_ref, acc_ref, *, lambda_init):
    i = pl.program_id(2)
    tq = q_ref.shape[1]
    dv = 2 * DIFF_HD
    n_heads, n_kv, _, tk = vta_ref.shape
    chains = [(h, c) for h in range(n_heads) for c in range(2)]

    @pl.when(i == 0)
    def _():
        acc_ref[...] = jnp.zeros_like(acc_ref)
        for h in range(n_heads):
            for j in range(n_kv):
                vta_ref[h, j, 0:dv, :] = vt_ref[h * dv:(h + 1) * dv, j * tk:(j + 1) * tk]
                vta_ref[h, j, dv:, :] = jnp.ones((ATT_ONES, tk), BF16)

    chan = lax.broadcasted_iota(jnp.int32, (dv, tq), 0)

    def split_heads(ref):
        out = []
        for h in range(n_heads):
            q = ref[h * dv:(h + 1) * dv, :]
            zero = jnp.zeros_like(q)
            out.append((jnp.where(chan < DIFF_HD, q, zero), jnp.where(chan >= DIFF_HD, q, zero)))
        return out

    def first_tiles(qi):
        n = (qi * tq) // tk
        return n, 1 + (qi * tq - n * tk) // tq

    n_full, tail_bias = first_tiles(i)

    def tile_max(s):
        m8 = jnp.max(s.reshape(tk // SUBLANES, SUBLANES, tq), axis=0)
        return jnp.max(m8, axis=0, keepdims=True)

    def score_chain(j, x, qsplit, n_vis, tail):
        h, c = chains[x]
        bias = bias_ref[jnp.where(j == n_vis, tail, 0)]
        kt = k_ref[pl.ds(pl.multiple_of(j * tk, tk), tk), h * dv:(h + 1) * dv]
        s = _dot(kt, qsplit[h][c]) + bias
        s_ref[x] = s
        return tile_max(s)

    @pl.when(i == 0)
    def _():
        qc0 = split_heads(q_ref)
        for x in range(len(chains)):
            mt_ref[x] = score_chain(0, x, qc0, n_full, tail_bias)

    qc = split_heads(q_ref)
    m_init = tuple(jnp.full((1, tq), -jnp.inf, F32) for _ in chains)

    def step(j, carry, next_scores, finish_head=None):
        m_run, m_tile = carry
        m_out, m_next, acc = [], [], []
        for x, (h, c) in enumerate(chains):
            m_new = jnp.maximum(m_run[x], m_tile[x])
            alpha = jnp.exp2(m_run[x] - m_new)
            p = jnp.exp2(s_ref[x] - m_new).astype(BF16)
            acc.append(alpha * acc_ref[x] + _dot(vta_ref[h, j], p))
            m_out.append(m_new)
            if finish_head is None:
                acc_ref[x] = acc[x]
            elif c == 1:
                finish_head(h, acc[x - 1], acc[x])
            if next_scores is not None:
                m_next.append(next_scores(x))
        return tuple(m_out), tuple(m_next)

    lam_v = lam_ref[...]
    lam = (jnp.exp(jnp.sum(lam_v[0:1] * lam_v[1:2], axis=-1, keepdims=True))
           - jnp.exp(jnp.sum(lam_v[2:3] * lam_v[3:4], axis=-1, keepdims=True)) + lambda_init)

    def finish_head(h, a0, a1):
        ot = a0[0:dv] / a0[dv:dv + 1] - lam * (a1[0:dv] / a1[dv:dv + 1])
        o = _rms(ot.T, sub_ref[...]) * (1.0 - lambda_init)
        o_ref[:, h * dv:(h + 1) * dv] = o.astype(BF16)
        acc_ref[2 * h:2 * h + 2] = jnp.zeros((2,) + acc_ref.shape[1:], F32)

    first = tuple(mt_ref[x] for x in range(len(chains)))
    carry = lax.fori_loop(
        0, n_full,
        lambda j, carry: step(j, carry, lambda x: score_chain(j + 1, x, qc, n_full, tail_bias)),
        (m_init, first))

    @pl.when(i + 1 < pl.num_programs(2))
    def _():
        _round_slabs((wa_ref, wb_ref), (wa_out_ref, wb_out_ref))
        qn = split_heads(qn_ref)
        n_vis, tail = first_tiles(i + 1)
        _, m_first = step(n_full, carry, lambda x: score_chain(0, x, qn, n_vis, tail), finish_head)
        for x in range(len(chains)):
            mt_ref[x] = m_first[x]

    @pl.when(i + 1 == pl.num_programs(2))
    def _():
        _round_slabs((wa_ref, wb_ref), (wa_out_ref, wb_out_ref))
        step(n_full, carry, None, finish_head)


def _attn_bias(tq, tk):
    key = np.arange(tk)[:, None] // CHUNK
    tiles = [np.zeros((tk, tq))]
    for r in range(tk // tq):
        qry = (r * tq + np.arange(tq))[None, :] // CHUNK
        tiles.append(np.where(key <= qry, 0.0, -1e30))
    return jnp.asarray(np.stack(tiles), F32)


def _diff_attention(qt, k, vt, lam_vecs, subln, w_a, w_b, batch, seq, lambda_init):
    t, d = k.shape
    tq = ATT_TQ
    nq = seq // tq
    dv = 2 * DIFF_HD
    hp = ATT_HEADS
    n_groups = DIFF_HEADS // hp
    n_steps = batch * n_groups * nq
    bias = _attn_bias(tq, ATT_TK)
    slabs = _slab_specs((w_a, w_b), n_steps, lambda b, g, i: (b * n_groups + g) * nq + i)
    return pl.pallas_call(
        functools.partial(_attn_kernel, lambda_init=lambda_init),
        out_shape=[jax.ShapeDtypeStruct((t, d), BF16),
                   jax.ShapeDtypeStruct(w_a.shape, BF16), jax.ShapeDtypeStruct(w_b.shape, BF16)],
        grid=(batch, n_groups, nq),
        in_specs=[
            _const_spec(lam_vecs.shape),
            _const_spec(bias.shape),
            pl.BlockSpec((hp * dv, tq), lambda b, g, i: (b * n_groups + g, i)),
            pl.BlockSpec((hp * dv, tq), lambda b, g, i: (b * n_groups + g, jnp.minimum(i + 1, nq - 1))),
            pl.BlockSpec((seq, hp * dv), lambda b, g, i: (b, g)),
            pl.BlockSpec((hp * dv, seq), lambda b, g, i: (b * n_groups + g, 0)),
            _const_spec((1, dv)),
        ] + slabs,
        out_specs=[pl.BlockSpec((tq, hp * dv), lambda b, g, i: (b * nq + i, g))] + slabs,
        scratch_shapes=[pltpu.VMEM((hp, seq // ATT_TK, dv + ATT_ONES, ATT_TK), BF16),
                        pltpu.VMEM((2 * hp, ATT_TK, tq), F32),
                        pltpu.VMEM((2 * hp, 1, tq), F32),
                        pltpu.VMEM((2 * hp, dv + ATT_ONES, tq), F32)],
        compiler_params=_cparams(("parallel", "parallel", "arbitrary")),
        name="diff_attn",
    )(lam_vecs, bias, qt, qt, k, vt, subln, w_a, w_b)


ROUTE_ROWS = 2 * SUBLANES


def _attn_out_kernel(x_ref, o_ref, wo_ref, g_ref, r_ref, tri_ref, x3_ref, h3_ref, route_ref, cnt_ref, carry_ref):
    @pl.when(pl.program_id(0) == 0)
    def _():
        carry_ref[...] = jnp.zeros_like(carry_ref)

    tm = x_ref.shape[0]
    x3 = x_ref[...] + _dot(o_ref[...], wo_ref[...])
    x3_ref[...] = x3
    h3 = _rms(x3, g_ref[...])
    for s in range(h3.shape[1] // LANES):
        h3_ref[pl.ds(s, tm, stride=SUBLANES), :] = h3[:, s * LANES:(s + 1) * LANES]

    hi = h3.astype(BF16)
    lo = (h3 - hi.astype(F32)).astype(BF16)
    both = _dot(hi, r_ref[...])
    logits = both[:, :LANES] + both[:, LANES:] + _dot(lo, r_ref[:, :LANES])
    lt = logits.T[0:ROUTE_ROWS]
    sub = lax.broadcasted_iota(jnp.int32, lt.shape, 0).astype(F32)
    lt = jnp.where(sub < N_EXPERTS, lt, -jnp.inf)
    v1 = jnp.max(lt, axis=0, keepdims=True)
    i1 = jnp.min(jnp.where(lt == v1, sub, float(ROUTE_ROWS)), axis=0, keepdims=True)
    lt2 = jnp.where(sub == i1, -jnp.inf, lt)
    v2 = jnp.max(lt2, axis=0, keepdims=True)
    i2 = jnp.min(jnp.where(lt2 == v2, sub, float(ROUTE_ROWS)), axis=0, keepdims=True)
    e = jnp.exp(v2 - v1)
    w1 = 1.0 / (1.0 + e)
    w2 = e / (1.0 + e)
    oh1 = sub == i1
    oh2 = sub == i2
    assign = jnp.where(oh1 | oh2, 1.0, 0.0)
    excl = _dot(assign.astype(BF16), tri_ref[...]) + carry_ref[:, 0:1]
    r1 = jnp.sum(jnp.where(oh1, excl, 0.0), axis=0, keepdims=True)
    r2 = jnp.sum(jnp.where(oh2, excl, 0.0), axis=0, keepdims=True)
    route = jnp.zeros_like(lt)
    for row, val in enumerate((i1, i2, r1, r2, w1, w2)):
        route = jnp.where(sub == float(row), val, route)
    route_ref[...] = route[0:SUBLANES]
    total = carry_ref[:, 0:1] + jnp.sum(assign, axis=1, keepdims=True)
    carry_ref[...] = jnp.broadcast_to(total, carry_ref.shape)
    cnt_ref[...] = jnp.broadcast_to(total, cnt_ref.shape)


def _attn_out(x2d, o, w_o, gain, router_split, tri):
    t, d = x2d.shape
    tm = LIGHT_TILE
    n_sub = d // LANES
    return pl.pallas_call(
        _attn_out_kernel,
        out_shape=[jax.ShapeDtypeStruct((t, d), F32),
                   jax.ShapeDtypeStruct((t * n_sub, LANES), F32),
                   jax.ShapeDtypeStruct((SUBLANES, t), F32),
                   jax.ShapeDtypeStruct((ROUTE_ROWS, LANES), F32)],
        grid=(t // tm,),
        in_specs=[pl.BlockSpec((tm, d), lambda i: (i, 0)),
                  pl.BlockSpec((tm, d), lambda i: (i, 0)),
                  _const_spec(w_o.shape), _const_spec((1, d)), _const_spec(router_split.shape),
                  _const_spec(tri.shape)],
        out_specs=[pl.BlockSpec((tm, d), lambda i: (i, 0)),
                   pl.BlockSpec((tm * n_sub, LANES), lambda i: (i, 0)),
                   pl.BlockSpec((SUBLANES, tm), lambda i: (0, i)),
                   _const_spec((ROUTE_ROWS, LANES))],
        scratch_shapes=[pltpu.VMEM((ROUTE_ROWS, LANES), F32)],
        compiler_params=_cparams(("arbitrary",)),
        name="attn_out_route",
    )(x2d, o, w_o, gain, router_split, tri)


def _dispatch_kernel(dest_ref, zpos_ref, zflag_ref, h_ref, dst_ref, zbuf_ref, zsem, sem):
    i = pl.program_id(0)
    tm = h_ref.shape[0]

    @pl.when(i == 0)
    def _():
        zbuf_ref[...] = jnp.zeros_like(zbuf_ref)

        def zcopy(e):
            return pltpu.make_async_copy(zbuf_ref, dst_ref.at[pl.ds(zpos_ref[e], MOE_TILE)], zsem.at[e])

        for e in range(2 * N_EXPERTS):
            @pl.when(zflag_ref[e] == 1)
            def _():
                zcopy(e).start()
        for e in range(2 * N_EXPERTS):
            @pl.when(zflag_ref[e] == 1)
            def _():
                zcopy(e).wait()

    n_tok = dest_ref.shape[0] // 2

    def body(r, carry):
        for k in range(2):
            pltpu.make_async_copy(h_ref.at[r], dst_ref.at[dest_ref[k * n_tok + i * tm + r]],
                                  sem).start(priority=k)
        return carry

    lax.fori_loop(0, tm, body, 0, unroll=8)
    for k in range(2):
        pltpu.make_async_copy(h_ref, dst_ref.at[pl.ds(0, tm)], sem).wait()


def _dispatch(h3_rows, dest, zpos, zflag, n_dst):
    t = h3_rows.shape[0]
    tm = DISPATCH_TILE
    tail = h3_rows.shape[1:]
    return pl.pallas_call(
        _dispatch_kernel,
        out_shape=jax.ShapeDtypeStruct((n_dst,) + tail, F32),
        grid_spec=pltpu.PrefetchScalarGridSpec(
            num_scalar_prefetch=3, grid=(t // tm,),
            in_specs=[pl.BlockSpec((tm,) + tail, lambda i, *_: (i, 0, 0))],
            out_specs=pl.BlockSpec(memory_space=pl.ANY),
            scratch_shapes=[pltpu.VMEM((MOE_TILE,) + tail, F32),
                            pltpu.SemaphoreType.DMA((2 * N_EXPERTS,)), pltpu.SemaphoreType.DMA(())]),
        compiler_params=_cparams(("arbitrary",)),
        name="moe_dispatch",
    )(dest, zpos, zflag, h3_rows)


def _moe_kernel(te_ref, tr_ref, ts_ref, xs_ref, wg_ref, wu_ref, wd_ref, ys_ref, h_ref, acc_ref, *, nfc):
    i = pl.program_id(0)
    c = pl.program_id(1)
    last = nfc - 1
    n_sub = h_ref.shape[1] // LANES
    grp = h_ref.shape[0] // MOE_SPLIT

    def expert_mlp(h):
        gt = _dot(h, wg_ref[...])
        up = _dot(h, wu_ref[...])
        act = (gt * jax.nn.sigmoid(gt) * up).astype(BF16)
        return _dot(act, wd_ref[...])

    for part in range(MOE_SPLIT):
        r0 = part * grp
        rows = slice(r0, r0 + grp)
        live = tr_ref[i] > r0

        def load_rows():
            h = jnp.concatenate([xs_ref[pl.ds(r0 * n_sub + s, grp, stride=n_sub), :].astype(BF16)
                                 for s in range(n_sub)], axis=1)
            h_ref[rows, :] = h
            return h

        def store_rows(y):
            for s in range(n_sub):
                ys_ref[pl.ds(r0 * n_sub + s, grp, stride=n_sub), :] = y[:, s * LANES:(s + 1) * LANES]

        if nfc == 1:
            @pl.when(live)
            def _():
                store_rows(expert_mlp(load_rows()))
        else:
            @pl.when(live & (c == 0))
            def _():
                acc_ref[rows, :] = expert_mlp(load_rows())

            if nfc > 2:
                @pl.when(live & (c > 0) & (c < last))
                def _():
                    acc_ref[rows, :] += expert_mlp(h_ref[rows, :])

            @pl.when(live & (c == last))
            def _():
                store_rows(acc_ref[rows, :] + expert_mlp(h_ref[rows, :]))

        @pl.when(jnp.logical_not(live) & (c == last))
        def _():
            ys_ref[r0 * n_sub:(r0 + grp) * n_sub, :] = jnp.zeros((grp * n_sub, LANES), F32)


def _moe_experts(xs2d, w_gu, w_down, tile_expert, tile_rows, tile_src, d):
    n_sub = d // LANES
    n_tiles = xs2d.shape[0] // (MOE_TILE * n_sub)
    f = w_down.shape[1]
    nfc = f // MOE_FC
    blk = MOE_TILE * n_sub

    def ceff(i, c, tv):
        return jnp.where(tv[i] > 0, c, nfc - 1)

    return pl.pallas_call(
        functools.partial(_moe_kernel, nfc=nfc),
        out_shape=jax.ShapeDtypeStruct(xs2d.shape, F32),
        grid_spec=pltpu.PrefetchScalarGridSpec(
            num_scalar_prefetch=3, grid=(n_tiles, nfc),
            in_specs=[
                pl.BlockSpec((blk, LANES), lambda i, c, te, tv, ts: (ts[i], 0)),
                pl.BlockSpec((None, d, MOE_FC), lambda i, c, te, tv, ts: (te[i], 0, ceff(i, c, tv))),
                pl.BlockSpec((None, d, MOE_FC), lambda i, c, te, tv, ts: (te[i], 0, nfc + ceff(i, c, tv))),
                pl.BlockSpec((None, MOE_FC, d), lambda i, c, te, tv, ts: (te[i], ceff(i, c, tv), 0)),
            ],
            out_specs=pl.BlockSpec((blk, LANES), lambda i, c, te, tv, ts: (i, 0)),
            scratch_shapes=[pltpu.VMEM((MOE_TILE, d), BF16), pltpu.VMEM((MOE_TILE, d), F32)]),
        compiler_params=_cparams(("arbitrary", "arbitrary")),
        name="moe_experts",
    )(tile_expert, tile_rows, tile_src, xs2d, w_gu, w_gu, w_down)


def _combine_kernel(dest_ref, x_ref, ys_ref, gate_ref, g_ref, o_ref, ybuf_ref, sem):
    i = pl.program_id(0)
    n = pl.num_programs(0)
    tm, d = x_ref.shape
    n_sub = d // LANES

    def gather(tile, slot):
        def body(r, carry):
            for k in range(2):
                row = dest_ref[k * (dest_ref.shape[0] // 2) + tile * tm + r]
                src = ys_ref.at[pl.ds(pl.multiple_of(row * n_sub, n_sub), n_sub)]
                dst = ybuf_ref.at[slot, pl.ds(pl.multiple_of((k * tm + r) * n_sub, n_sub), n_sub)]
                pltpu.make_async_copy(src, dst, sem.at[slot]).start(priority=k)
            return carry

        lax.fori_loop(0, tm, body, 0, unroll=8)

    @pl.when(i == 0)
    def _():
        gather(0, 0)

    @pl.when(i + 1 < n)
    def _():
        gather(i + 1, (i + 1) % 2)

    slot = i % 2
    pltpu.make_async_copy(ys_ref.at[pl.ds(0, 2 * tm * n_sub)], ybuf_ref.at[slot], sem.at[slot]).wait()

    y_ref = ybuf_ref.at[slot]
    gates = gate_ref[...]
    w1 = gates[:, 0:1]
    w2 = gates[:, 1:2]
    for s in range(n_sub):
        cols = slice(s * LANES, (s + 1) * LANES)
        y1 = y_ref[pl.ds(s, tm, stride=n_sub), :]
        y2 = y_ref[pl.ds(tm * n_sub + s, tm, stride=n_sub), :]
        o_ref[:, cols] = x_ref[:, cols] + w1 * y1 + w2 * y2
    o_ref[...] = _rms(o_ref[...], g_ref[...])


def _combine(x2d, ys2d, dest, gates, gain):
    t, d = x2d.shape
    tm = TOK_TILE
    n_sub = d // LANES
    return pl.pallas_call(
        _combine_kernel,
        out_shape=jax.ShapeDtypeStruct((t, d), F32),
        grid_spec=pltpu.PrefetchScalarGridSpec(
            num_scalar_prefetch=1, grid=(t // tm,),
            in_specs=[pl.BlockSpec((tm, d), lambda i, *_: (i, 0)),
                      pl.BlockSpec(memory_space=pl.ANY),
                      pl.BlockSpec((tm, gates.shape[1]), lambda i, *_: (i, 0)),
                      pl.BlockSpec((1, d), lambda i, *_: (0, 0), pipeline_mode=pl.Buffered(1))],
            out_specs=pl.BlockSpec((tm, d), lambda i, *_: (i, 0)),
            scratch_shapes=[pltpu.VMEM((2, 2 * tm * n_sub, LANES), F32), pltpu.SemaphoreType.DMA((2,))]),
        compiler_params=_cparams(("arbitrary",)),
        name="moe_combine",
    )(dest, x2d, ys2d, gates, gain)


def _rope_tables(seq, rot_dim, theta):
    inv = 1.0 / (theta ** (np.arange(0, rot_dim, 2, dtype=np.float64) / rot_dim))
    ang = np.arange(seq, dtype=np.float64)[:, None] * inv[None, :]
    return np.cos(ang), np.sin(ang)


def _retention_tables():
    c = RET_CHUNK
    log_gamma = np.log(1.0 - 2.0 ** (-5.0 - np.arange(RET_HEADS, dtype=np.float64)))
    idx = np.arange(c, dtype=np.float64)
    rel = idx[:, None] - idx[None, :]
    dmat = np.where(rel[None] >= 0, np.exp(np.maximum(rel, 0.0)[None] * log_gamma[:, None, None]), 0.0)
    qd = np.exp((idx + 1.0)[None, :] * log_gamma[:, None])[:, :, None]
    kd = np.exp((c - 1.0 - idx)[None, :] * log_gamma[:, None])[:, :, None]
    cd = np.exp(c * log_gamma)[:, None, None]
    return tuple(jnp.asarray(tb, F32) for tb in (dmat, qd, kd, cd))


def _attn_rope_tables(seq):
    cos, sin = _rope_tables(seq, ROPE_DIM, ROPE_THETA)
    half = ROPE_DIM // 2
    pad = DIFF_HD - ROPE_DIM
    ones = np.ones((seq, pad))
    zeros = np.zeros((seq, pad))
    zh = np.zeros((seq, half))
    ctab = np.concatenate([cos, cos, ones], axis=1)
    s1tab = np.concatenate([-sin, zh, zeros], axis=1)
    s2tab = np.concatenate([zh, sin, zeros], axis=1)
    rep = LANES // DIFF_HD
    tabs = [np.tile(tb, (1, rep)) for tb in (ctab, s1tab, s2tab)]
    return (tuple(jnp.asarray(tb, F32) for tb in tabs),
            tuple(jnp.asarray(tb.T, F32) for tb in tabs))


def kernel(x, ln_mix, ln_ffn, ret_w_in, ret_w_o, kv_norm, w_kv, diff_w_q, lam_q1, lam_k1, lam_q2, lam_k2,
           diff_subln, diff_w_o, ffn_w_gu, ffn_w_down, moe_router, moe_w_gu, moe_w_down, final_norm):
    batch, seq, d = x.shape
    t = batch * seq
    assert ln_mix.shape[0] == 2 and ret_w_in.shape[0] == 1 and diff_w_q.shape[0] == 1
    assert seq % TOK_TILE == 0 and TOK_TILE % RET_CHUNK == 0 and ATT_TQ % CHUNK == 0
    assert seq % LIGHT_TILE == 0 and seq % DISPATCH_TILE == 0
    assert seq % ATT_TK == 0 and ATT_TK % ATT_TQ == 0
    x2d = x.reshape(t, d)
    row = lambda g: g.reshape(1, -1)

    cos_r, sin_r = (jnp.asarray(tb, F32) for tb in _rope_tables(seq, RET_QK, RET_THETA))
    proj, w_o_bf, w_gu_d_bf, w_down_d_bf, w_q_bf, w_kv_bf, w_ao_bf = _ret_in_proj(
        x2d, row(ln_mix[0]), ret_w_in[0].astype(BF16), cos_r, sin_r, seq,
        (ret_w_o[0], ffn_w_gu[0], ffn_w_down[0], diff_w_q[0], w_kv, diff_w_o[0]))
    ret_o = _retention(proj, *_retention_tables(), batch, seq)
    x2 = _ret_out_ffn(x2d, ret_o, w_o_bf, row(ln_ffn[0]), w_gu_d_bf, w_down_d_bf)

    lambda_init = 0.8 - 0.6 * math.exp(-0.3 * 1)
    qt, k, vt = _qkv_proj(x2, row(ln_mix[1]), row(kv_norm), w_q_bf.T, w_kv_bf[:, :d], w_kv_bf[:, d:].T,
                          *_attn_rope_tables(seq), batch, seq)
    lam_vecs = jnp.stack([lam_q1[0], lam_k1[0], lam_q2[0], lam_k2[0]]).astype(F32)
    n_exp, _, two_f = moe_w_gu[0].shape
    att, w_gu_bf, w_down_bf = _diff_attention(
        qt, k, vt, lam_vecs, row(diff_subln[0]), moe_w_gu[0].reshape(n_exp * d, two_f),
        moe_w_down[0].reshape(n_exp * (two_f // 2), d), batch, seq, lambda_init)
    router_pad = jnp.pad(moe_router[0], ((0, 0), (0, LANES - N_EXPERTS)))
    router_hi = router_pad.astype(BF16)
    router_lo = (router_pad - router_hi.astype(F32)).astype(BF16)
    tri = jnp.asarray(np.arange(LIGHT_TILE)[:, None] < np.arange(LIGHT_TILE)[None, :], BF16)
    x3, h3_rows, route, cnt = _attn_out(x2, att, w_ao_bf, row(ln_ffn[1]),
                                        jnp.concatenate([router_hi, router_lo], axis=1), tri)

    n_sub = d // LANES
    n_rows = 2 * t + N_EXPERTS * MOE_TILE
    n_tiles = n_rows // MOE_TILE
    expert = route[0:2].astype(jnp.int32)
    rank = route[2:4].astype(jnp.int32)
    gates = route[4:6].T
    counts = cnt[:N_EXPERTS, 0].astype(jnp.int32)
    padded = (counts + MOE_TILE - 1) // MOE_TILE * MOE_TILE
    seg_end = jnp.cumsum(padded)
    seg_start = seg_end - padded
    dest = (jnp.sum(jnp.where(expert[..., None] == jnp.arange(N_EXPERTS), seg_start, 0), axis=-1)
            + rank).reshape(-1)
    tile_row = jnp.arange(n_tiles, dtype=jnp.int32) * MOE_TILE
    tile_valid = (tile_row < seg_end[-1]).astype(jnp.int32)
    n_valid = seg_end[-1] // MOE_TILE
    tile_src = jnp.minimum(jnp.arange(n_tiles, dtype=jnp.int32), jnp.maximum(n_valid - 1, 0))
    tile_expert = jnp.minimum(jnp.sum(tile_src[:, None] * MOE_TILE >= seg_end[None, :], axis=1),
                              N_EXPERTS - 1).astype(jnp.int32)
    token_end = jnp.sum(jnp.where(tile_expert[:, None] == jnp.arange(N_EXPERTS), seg_start + counts, 0), axis=1)
    tile_rows = (tile_valid * jnp.clip(token_end - tile_row, 0, MOE_TILE)).astype(jnp.int32)
    zflag = jnp.concatenate([(padded > 0).astype(jnp.int32), 1 - tile_valid[-N_EXPERTS:]])
    zpos = jnp.concatenate([jnp.maximum(seg_end - MOE_TILE, 0), tile_row[-N_EXPERTS:]]).astype(jnp.int32)

    xs = _dispatch(h3_rows.reshape(t, n_sub, LANES), dest, zpos, zflag, n_rows)
    ys = _moe_experts(xs.reshape(n_rows * n_sub, LANES), w_gu_bf.reshape(n_exp, d, two_f),
                      w_down_bf.reshape(n_exp, two_f // 2, d), tile_expert, tile_rows, tile_src, d)
    out = _combine(x3, ys, dest, gates, row(final_norm))
    return out.reshape(batch, seq, d)
```
